```python
import jax, jax.numpy as jnp
from jax import lax
import numpy as np

D_MODEL = 2048
BATCH = 8
SEQ = 8192
DEPTH = 1

N_HEADS = 16
N_KV_HEADS = 4
HEAD_DIM = D_MODEL // N_HEADS
GROUP = N_HEADS // N_KV_HEADS
Q_BLOCK = 128
ROPE_THETA = 10000.0
AXIS_DIM = HEAD_DIM // 2
GRID_W = 64
CONV_WIDTH = D_MODEL // 2
CONV_KERNEL = 31
D_FF = 4 * D_MODEL
PLE_DIM = 256
N_BRANCHES = 2
EPS = 1e-6

Q_W = N_HEADS * HEAD_DIM
KV_W = N_KV_HEADS * HEAD_DIM
IN_W = 2 * CONV_WIDTH + Q_W + 2 * KV_W + N_BRANCHES * D_MODEL

kernel_name = "hybrid_conformer_gqa_axial_gated_encoder_block"


def rms_norm(x, g):
    xf = x.astype(jnp.float32)
    y = xf * lax.rsqrt(jnp.mean(xf * xf, axis=-1, keepdims=True) + EPS)
    return (y * g.astype(jnp.float32)).astype(x.dtype)


def layer_norm(x, g, b):
    xf = x.astype(jnp.float32)
    mu = jnp.mean(xf, axis=-1, keepdims=True)
    xc = xf - mu
    y = xc * lax.rsqrt(jnp.mean(xc * xc, axis=-1, keepdims=True) + EPS)
    return (y * g.astype(jnp.float32) + b.astype(jnp.float32)).astype(x.dtype)


def rope_half(x, ang):
    n = ang.shape[-1]
    cos = jnp.cos(ang)[None, :, None, :].astype(x.dtype)
    sin = jnp.sin(ang)[None, :, None, :].astype(x.dtype)
    x1, x2 = x[..., :n], x[..., n:]
    return jnp.concatenate([x1 * cos - x2 * sin, x2 * cos + x1 * sin], axis=-1)


def axial_rope(x, ang_row, ang_col):
    return jnp.concatenate([rope_half(x[..., :AXIS_DIM], ang_row),
                            rope_half(x[..., AXIS_DIM:], ang_col)], axis=-1)


def conformer_branch(u_a, u_b, w_dw, ln_g, ln_b, w_proj):
    u = u_a * jax.nn.sigmoid(u_b)
    u = lax.conv_general_dilated(
        u, w_dw[:, None, :].astype(u.dtype), window_strides=(1,),
        padding=[(CONV_KERNEL // 2, CONV_KERNEL // 2)],
        dimension_numbers=("NWC", "WIO", "NWC"),
        feature_group_count=CONV_WIDTH)
    u = jax.nn.silu(layer_norm(u, ln_g, ln_b))
    return u @ w_proj


def attention_branch(q, k, v, q_g, k_g, w_proj):
    B, S, _ = q.shape
    rows = S // GRID_W
    row = jnp.repeat(jnp.arange(rows, dtype=jnp.int32), GRID_W)
    col = jnp.tile(jnp.arange(GRID_W, dtype=jnp.int32), rows)
    inv_freq = ROPE_THETA ** (-jnp.arange(0, AXIS_DIM, 2, dtype=jnp.float32) / AXIS_DIM)
    ang_row = row.astype(jnp.float32)[:, None] * inv_freq[None, :]
    ang_col = col.astype(jnp.float32)[:, None] * inv_freq[None, :]

    q = q.reshape(B, S, N_HEADS, HEAD_DIM)
    k = k.reshape(B, S, N_KV_HEADS, HEAD_DIM)
    v = v.reshape(B, S, N_KV_HEADS, HEAD_DIM)
    q = axial_rope(rms_norm(q, q_g), ang_row, ang_col)
    k = axial_rope(rms_norm(k, k_g), ang_row, ang_col)

    n_blk = S // Q_BLOCK
    qb = q.reshape(B, n_blk, Q_BLOCK, N_KV_HEADS, GROUP, HEAD_DIM)
    qb = qb.transpose(1, 0, 3, 4, 2, 5)
    kt = k.transpose(0, 2, 1, 3)
    vt = v.transpose(0, 2, 1, 3)
    scale = HEAD_DIM ** -0.5

    def one_block(qblk):
        s = jnp.einsum("bkgqd,bksd->bkgqs", qblk, kt).astype(jnp.float32) * scale
        pr = jax.nn.softmax(s, axis=-1).astype(vt.dtype)
        return jnp.einsum("bkgqs,bksd->bkgqd", pr, vt)

    o = lax.map(one_block, qb)
    o = o.transpose(1, 0, 4, 2, 3, 5).reshape(B, S, Q_W)
    return o @ w_proj


def _fwd_setup_inputs(seed: int = 0) -> dict:
    key = jax.random.key(seed)
    ks = jax.random.split(key, 24)
    f32 = jnp.float32

    def nrm(k, shape, scale):
        return jax.random.normal(k, shape, f32) * scale

    def gain(k, shape):
        return 1.0 + 0.02 * jax.random.normal(k, shape, f32)

    return {
        "x": nrm(ks[0], (BATCH, SEQ, D_MODEL), 1.0),
        "p": nrm(ks[1], (DEPTH, BATCH, SEQ, PLE_DIM), 1.0),
        "norm_mix": gain(ks[2], (DEPTH, D_MODEL)),
        "w_in": nrm(ks[3], (DEPTH, D_MODEL, IN_W), D_MODEL ** -0.5),
        "w_dw": nrm(ks[4], (DEPTH, CONV_KERNEL, CONV_WIDTH), CONV_KERNEL ** -0.5),
        "conv_ln_g": gain(ks[5], (DEPTH, CONV_WIDTH)),
        "conv_ln_b": nrm(ks[6], (DEPTH, CONV_WIDTH), 0.02),
        "w_conv_proj": nrm(ks[7], (DEPTH, CONV_WIDTH, D_MODEL), CONV_WIDTH ** -0.5),
        "q_norm": gain(ks[8], (DEPTH, HEAD_DIM)),
        "k_norm": gain(ks[9], (DEPTH, HEAD_DIM)),
        "w_attn_proj": nrm(ks[10], (DEPTH, Q_W, D_MODEL), Q_W ** -0.5),
        "w_out": nrm(ks[11], (DEPTH, D_MODEL, D_MODEL), D_MODEL ** -0.5),
        "norm_ffn": gain(ks[12], (DEPTH, D_MODEL)),
        "w_ff1": nrm(ks[13], (DEPTH, D_MODEL, D_FF), D_MODEL ** -0.5),
        "w_ff2": nrm(ks[14], (DEPTH, D_FF, D_MODEL), D_FF ** -0.5),
        "norm_ple": gain(ks[15], (DEPTH, D_MODEL)),
        "w_ple_gate": nrm(ks[16], (DEPTH, D_MODEL, D_MODEL), D_MODEL ** -0.5),
        "w_ple_proj": nrm(ks[17], (DEPTH, PLE_DIM, D_MODEL), PLE_DIM ** -0.5),
        "norm_final": gain(ks[18], (D_MODEL,)),
    }


def _fwd_reference(x, p, norm_mix, w_in, w_dw, conv_ln_g, conv_ln_b, w_conv_proj,
              q_norm, k_norm, w_attn_proj, w_out, norm_ffn, w_ff1, w_ff2,
              norm_ple, w_ple_gate, w_ple_proj, norm_final):
    split_at = list(np.cumsum([CONV_WIDTH, CONV_WIDTH, Q_W, KV_W, KV_W, D_MODEL]))
    for i in range(DEPTH):
        h = rms_norm(x, norm_mix[i])
        z = h @ w_in[i]
        c_a, c_b, q, k, v, g_c, g_a = jnp.split(z, split_at, axis=-1)
        y_c = conformer_branch(c_a, c_b, w_dw[i], conv_ln_g[i], conv_ln_b[i], w_conv_proj[i])
        y_a = attention_branch(q, k, v, q_norm[i], k_norm[i], w_attn_proj[i])
        merged = jax.nn.sigmoid(g_c) * y_c + jax.nn.sigmoid(g_a) * y_a
        x = x + merged @ w_out[i]
        h = rms_norm(x, norm_ffn[i])
        x = x + jnp.square(jax.nn.relu(h @ w_ff1[i])) @ w_ff2[i]
        gate = jax.nn.sigmoid(rms_norm(x, norm_ple[i]) @ w_ple_gate[i])
        x = x + gate * (p[i] @ w_ple_proj[i])
    return rms_norm(x, norm_final)


import jax as _jax
import jax.numpy as _jnp

TWIN_FORMAT = 'train_step'
FWD_PARAMS = ['x', 'p', 'norm_mix', 'w_in', 'w_dw', 'conv_ln_g', 'conv_ln_b', 'w_conv_proj', 'q_norm', 'k_norm', 'w_attn_proj', 'w_out', 'norm_ffn', 'w_ff1', 'w_ff2', 'norm_ple', 'w_ple_gate', 'w_ple_proj', 'norm_final']
TWIN_WEIGHTS = ['norm_mix', 'w_in', 'w_dw', 'conv_ln_g', 'conv_ln_b', 'w_conv_proj', 'q_norm', 'k_norm', 'w_attn_proj', 'w_out', 'norm_ffn', 'w_ff1', 'w_ff2', 'norm_ple', 'w_ple_gate', 'w_ple_proj', 'norm_final']
TWIN_DIFF_INPUT = 'x'
TWIN_INPUTS = ['x', 'p', 'norm_mix', 'w_in', 'w_dw', 'conv_ln_g', 'conv_ln_b', 'w_conv_proj', 'q_norm', 'k_norm', 'w_attn_proj', 'w_out', 'norm_ffn', 'w_ff1', 'w_ff2', 'norm_ple', 'w_ple_gate', 'w_ple_proj', 'norm_final', 'loss_target', 'm_norm_mix', 'm_w_in', 'm_w_dw', 'm_conv_ln_g', 'm_conv_ln_b', 'm_w_conv_proj', 'm_q_norm', 'm_k_norm', 'm_w_attn_proj', 'm_w_out', 'm_norm_ffn', 'm_w_ff1', 'm_w_ff2', 'm_norm_ple', 'm_w_ple_gate', 'm_w_ple_proj', 'm_norm_final', 'v_norm_mix', 'v_w_in', 'v_w_dw', 'v_conv_ln_g', 'v_conv_ln_b', 'v_w_conv_proj', 'v_q_norm', 'v_k_norm', 'v_w_attn_proj', 'v_w_out', 'v_norm_ffn', 'v_w_ff1', 'v_w_ff2', 'v_norm_ple', 'v_w_ple_gate', 'v_w_ple_proj', 'v_norm_final']
TWIN_OUTPUTS = ['loss', 'grad_x', 'grad_norm_mix', 'grad_w_in', 'grad_w_dw', 'grad_conv_ln_g', 'grad_conv_ln_b', 'grad_w_conv_proj', 'grad_q_norm', 'grad_k_norm', 'grad_w_attn_proj', 'grad_w_out', 'grad_norm_ffn', 'grad_w_ff1', 'grad_w_ff2', 'grad_norm_ple', 'grad_w_ple_gate', 'grad_w_ple_proj', 'grad_norm_final', 'delta_norm_mix', 'delta_w_in', 'delta_w_dw', 'delta_conv_ln_g', 'delta_conv_ln_b', 'delta_w_conv_proj', 'delta_q_norm', 'delta_k_norm', 'delta_w_attn_proj', 'delta_w_out', 'delta_norm_ffn', 'delta_w_ff1', 'delta_w_ff2', 'delta_norm_ple', 'delta_w_ple_gate', 'delta_w_ple_proj', 'delta_norm_final', 'new_m_norm_mix', 'new_m_w_in', 'new_m_w_dw', 'new_m_conv_ln_g', 'new_m_conv_ln_b', 'new_m_w_conv_proj', 'new_m_q_norm', 'new_m_k_norm', 'new_m_w_attn_proj', 'new_m_w_out', 'new_m_norm_ffn', 'new_m_w_ff1', 'new_m_w_ff2', 'new_m_norm_ple', 'new_m_w_ple_gate', 'new_m_w_ple_proj', 'new_m_norm_final', 'new_v_norm_mix', 'new_v_w_in', 'new_v_w_dw', 'new_v_conv_ln_g', 'new_v_conv_ln_b', 'new_v_w_conv_proj', 'new_v_q_norm', 'new_v_k_norm', 'new_v_w_attn_proj', 'new_v_w_out', 'new_v_norm_ffn', 'new_v_w_ff1', 'new_v_w_ff2', 'new_v_norm_ple', 'new_v_w_ple_gate', 'new_v_w_ple_proj', 'new_v_norm_final']
TWIN_LEAF_KINDS = {'loss': 'loss', 'grad_x': 'grad_x', 'grad_norm_mix': 'grad_w', 'grad_w_in': 'grad_w', 'grad_w_dw': 'grad_w', 'grad_conv_ln_g': 'grad_w', 'grad_conv_ln_b': 'grad_w', 'grad_w_conv_proj': 'grad_w', 'grad_q_norm': 'grad_w', 'grad_k_norm': 'grad_w', 'grad_w_attn_proj': 'grad_w', 'grad_w_out': 'grad_w', 'grad_norm_ffn': 'grad_w', 'grad_w_ff1': 'grad_w', 'grad_w_ff2': 'grad_w', 'grad_norm_ple': 'grad_w', 'grad_w_ple_gate': 'grad_w', 'grad_w_ple_proj': 'grad_w', 'grad_norm_final': 'grad_w', 'delta_norm_mix': 'delta_w', 'delta_w_in': 'delta_w', 'delta_w_dw': 'delta_w', 'delta_conv_ln_g': 'delta_w', 'delta_conv_ln_b': 'delta_w', 'delta_w_conv_proj': 'delta_w', 'delta_q_norm': 'delta_w', 'delta_k_norm': 'delta_w', 'delta_w_attn_proj': 'delta_w', 'delta_w_out': 'delta_w', 'delta_norm_ffn': 'delta_w', 'delta_w_ff1': 'delta_w', 'delta_w_ff2': 'delta_w', 'delta_norm_ple': 'delta_w', 'delta_w_ple_gate': 'delta_w', 'delta_w_ple_proj': 'delta_w', 'delta_norm_final': 'delta_w', 'new_m_norm_mix': 'new_m', 'new_m_w_in': 'new_m', 'new_m_w_dw': 'new_m', 'new_m_conv_ln_g': 'new_m', 'new_m_conv_ln_b': 'new_m', 'new_m_w_conv_proj': 'new_m', 'new_m_q_norm': 'new_m', 'new_m_k_norm': 'new_m', 'new_m_w_attn_proj': 'new_m', 'new_m_w_out': 'new_m', 'new_m_norm_ffn': 'new_m', 'new_m_w_ff1': 'new_m', 'new_m_w_ff2': 'new_m', 'new_m_norm_ple': 'new_m', 'new_m_w_ple_gate': 'new_m', 'new_m_w_ple_proj': 'new_m', 'new_m_norm_final': 'new_m', 'new_v_norm_mix': 'new_v', 'new_v_w_in': 'new_v', 'new_v_w_dw': 'new_v', 'new_v_conv_ln_g': 'new_v', 'new_v_conv_ln_b': 'new_v', 'new_v_w_conv_proj': 'new_v', 'new_v_q_norm': 'new_v', 'new_v_k_norm': 'new_v', 'new_v_w_attn_proj': 'new_v', 'new_v_w_out': 'new_v', 'new_v_norm_ffn': 'new_v', 'new_v_w_ff1': 'new_v', 'new_v_w_ff2': 'new_v', 'new_v_norm_ple': 'new_v', 'new_v_w_ple_gate': 'new_v', 'new_v_w_ple_proj': 'new_v', 'new_v_norm_final': 'new_v'}


def _forward(args):
    return _fwd_reference(*[args[k] for k in FWD_PARAMS])


def _output_shape():
    def fwd():
        inp = _fwd_setup_inputs(0)
        return _fwd_reference(*[inp[k] for k in FWD_PARAMS])
    out = _jax.eval_shape(fwd)
    return out.shape, out.dtype

N_MICROBATCH = 1
ADAM_LR = 0.001
ADAM_B1 = 0.9
ADAM_B2 = 0.999
ADAM_EPS = 1e-08
ADAM_WD = 0.01
ADAM_STEP = 10
PER_EXAMPLE_BATCH_AXIS = {'x': 0, 'p': 1, 'loss_target': 0}
SHARED_INPUTS = []
_WEIGHT_DTYPES = {'norm_mix': _jnp.float32, 'w_in': _jnp.float32, 'w_dw': _jnp.float32, 'conv_ln_g': _jnp.float32, 'conv_ln_b': _jnp.float32, 'w_conv_proj': _jnp.float32, 'q_norm': _jnp.float32, 'k_norm': _jnp.float32, 'w_attn_proj': _jnp.float32, 'w_out': _jnp.float32, 'norm_ffn': _jnp.float32, 'w_ff1': _jnp.float32, 'w_ff2': _jnp.float32, 'norm_ple': _jnp.float32, 'w_ple_gate': _jnp.float32, 'w_ple_proj': _jnp.float32, 'norm_final': _jnp.float32}
MOMENT_SCALE = {'norm_mix': 4.714480e-02, 'w_in': 2.307816e-02, 'w_dw': 5.902798e-02, 'conv_ln_g': 7.881891e-02, 'conv_ln_b': 8.691570e-02, 'w_conv_proj': 4.339057e-02, 'q_norm': 2.723852e-02, 'k_norm': 2.847287e-02, 'w_attn_proj': 6.926912e-03, 'w_out': 4.268847e-02, 'norm_ffn': 1.085847e-01, 'w_ff1': 5.271002e-02, 'w_ff2': 1.121523e-01, 'norm_ple': 1.834143e-02, 'w_ple_gate': 1.744267e-02, 'w_ple_proj': 4.015897e-02, 'norm_final': 3.231134e+01}


def _to_microbatches(a, axis):
    t = _jnp.moveaxis(a, axis, 0)
    t = t.reshape((N_MICROBATCH, t.shape[0] // N_MICROBATCH) + t.shape[1:])
    return _jnp.moveaxis(t, 1, axis + 1)


def setup_inputs(seed: int = 0) -> dict:
    inp = _fwd_setup_inputs(seed)
    key = _jax.random.fold_in(_jax.random.key(seed), 7919)
    shape, _ = _output_shape()
    out = dict(inp)
    out["loss_target"] = _jax.random.normal(_jax.random.fold_in(key, 0), shape, _jnp.float32)
    for i, name in enumerate(TWIN_WEIGHTS):
        w = inp[name].astype(_jnp.float32)
        if MOMENT_SCALE is None:
            s = _jnp.sqrt(_jnp.mean(_jnp.square(w)) + 1e-30)
        else:
            s = MOMENT_SCALE[name]
        km, kv = _jax.random.split(_jax.random.fold_in(key, i + 1))
        out[name] = w
        out["m_" + name] = s * _jax.random.normal(km, w.shape, _jnp.float32)
        out["v_" + name] = (s * s) * _jax.random.uniform(kv, w.shape, _jnp.float32, 0.5, 1.5)
    if N_MICROBATCH > 1:
        for name, axis in PER_EXAMPLE_BATCH_AXIS.items():
            out[name] = _to_microbatches(out[name], axis)
    return {'x': out['x'], 'p': out['p'], 'norm_mix': out['norm_mix'], 'w_in': out['w_in'], 'w_dw': out['w_dw'], 'conv_ln_g': out['conv_ln_g'], 'conv_ln_b': out['conv_ln_b'], 'w_conv_proj': out['w_conv_proj'], 'q_norm': out['q_norm'], 'k_norm': out['k_norm'], 'w_attn_proj': out['w_attn_proj'], 'w_out': out['w_out'], 'norm_ffn': out['norm_ffn'], 'w_ff1': out['w_ff1'], 'w_ff2': out['w_ff2'], 'norm_ple': out['norm_ple'], 'w_ple_gate': out['w_ple_gate'], 'w_ple_proj': out['w_ple_proj'], 'norm_final': out['norm_final'], 'loss_target': out['loss_target'], 'm_norm_mix': out['m_norm_mix'], 'm_w_in': out['m_w_in'], 'm_w_dw': out['m_w_dw'], 'm_conv_ln_g': out['m_conv_ln_g'], 'm_conv_ln_b': out['m_conv_ln_b'], 'm_w_conv_proj': out['m_w_conv_proj'], 'm_q_norm': out['m_q_norm'], 'm_k_norm': out['m_k_norm'], 'm_w_attn_proj': out['m_w_attn_proj'], 'm_w_out': out['m_w_out'], 'm_norm_ffn': out['m_norm_ffn'], 'm_w_ff1': out['m_w_ff1'], 'm_w_ff2': out['m_w_ff2'], 'm_norm_ple': out['m_norm_ple'], 'm_w_ple_gate': out['m_w_ple_gate'], 'm_w_ple_proj': out['m_w_ple_proj'], 'm_norm_final': out['m_norm_final'], 'v_norm_mix': out['v_norm_mix'], 'v_w_in': out['v_w_in'], 'v_w_dw': out['v_w_dw'], 'v_conv_ln_g': out['v_conv_ln_g'], 'v_conv_ln_b': out['v_conv_ln_b'], 'v_w_conv_proj': out['v_w_conv_proj'], 'v_q_norm': out['v_q_norm'], 'v_k_norm': out['v_k_norm'], 'v_w_attn_proj': out['v_w_attn_proj'], 'v_w_out': out['v_w_out'], 'v_norm_ffn': out['v_norm_ffn'], 'v_w_ff1': out['v_w_ff1'], 'v_w_ff2': out['v_w_ff2'], 'v_norm_ple': out['v_norm_ple'], 'v_w_ple_gate': out['v_w_ple_gate'], 'v_w_ple_proj': out['v_w_ple_proj'], 'v_norm_final': out['v_norm_final']}


def _loss(weights, diff, rest, loss_target):
    with _jax.named_scope("forward"):
        args = {**rest, TWIN_DIFF_INPUT: diff, **{k: w.astype(_WEIGHT_DTYPES[k]) for k, w in weights.items()}}
        y = _forward(args)
    with _jax.named_scope("loss_head"):
        err = _jnp.square(y.astype(_jnp.float32) - loss_target)
        return 0.5 * _jnp.sum(_jnp.mean(err, axis=-1)) if err.ndim else 0.5 * err


def _adamw(w, g, m, v):
    m = ADAM_B1 * m + (1.0 - ADAM_B1) * g
    v = ADAM_B2 * v + (1.0 - ADAM_B2) * _jnp.square(g)
    m_hat = m / (1.0 - ADAM_B1 ** ADAM_STEP)
    v_hat = v / (1.0 - ADAM_B2 ** ADAM_STEP)
    delta = -ADAM_LR * (m_hat / (_jnp.sqrt(v_hat) + ADAM_EPS) + ADAM_WD * w)
    return delta, m, v


def reference(x, p, norm_mix, w_in, w_dw, conv_ln_g, conv_ln_b, w_conv_proj, q_norm, k_norm, w_attn_proj, w_out, norm_ffn, w_ff1, w_ff2, norm_ple, w_ple_gate, w_ple_proj, norm_final, loss_target, m_norm_mix, m_w_in, m_w_dw, m_conv_ln_g, m_conv_ln_b, m_w_conv_proj, m_q_norm, m_k_norm, m_w_attn_proj, m_w_out, m_norm_ffn, m_w_ff1, m_w_ff2, m_norm_ple, m_w_ple_gate, m_w_ple_proj, m_norm_final, v_norm_mix, v_w_in, v_w_dw, v_conv_ln_g, v_conv_ln_b, v_w_conv_proj, v_q_norm, v_k_norm, v_w_attn_proj, v_w_out, v_norm_ffn, v_w_ff1, v_w_ff2, v_norm_ple, v_w_ple_gate, v_w_ple_proj, v_norm_final):
    given = dict(x=x, p=p, norm_mix=norm_mix, w_in=w_in, w_dw=w_dw, conv_ln_g=conv_ln_g, conv_ln_b=conv_ln_b, w_conv_proj=w_conv_proj, q_norm=q_norm, k_norm=k_norm, w_attn_proj=w_attn_proj, w_out=w_out, norm_ffn=norm_ffn, w_ff1=w_ff1, w_ff2=w_ff2, norm_ple=norm_ple, w_ple_gate=w_ple_gate, w_ple_proj=w_ple_proj, norm_final=norm_final, loss_target=loss_target, m_norm_mix=m_norm_mix, m_w_in=m_w_in, m_w_dw=m_w_dw, m_conv_ln_g=m_conv_ln_g, m_conv_ln_b=m_conv_ln_b, m_w_conv_proj=m_w_conv_proj, m_q_norm=m_q_norm, m_k_norm=m_k_norm, m_w_attn_proj=m_w_attn_proj, m_w_out=m_w_out, m_norm_ffn=m_norm_ffn, m_w_ff1=m_w_ff1, m_w_ff2=m_w_ff2, m_norm_ple=m_norm_ple, m_w_ple_gate=m_w_ple_gate, m_w_ple_proj=m_w_ple_proj, m_norm_final=m_norm_final, v_norm_mix=v_norm_mix, v_w_in=v_w_in, v_w_dw=v_w_dw, v_conv_ln_g=v_conv_ln_g, v_conv_ln_b=v_conv_ln_b, v_w_conv_proj=v_w_conv_proj, v_q_norm=v_q_norm, v_k_norm=v_k_norm, v_w_attn_proj=v_w_attn_proj, v_w_out=v_w_out, v_norm_ffn=v_norm_ffn, v_w_ff1=v_w_ff1, v_w_ff2=v_w_ff2, v_norm_ple=v_norm_ple, v_w_ple_gate=v_w_ple_gate, v_w_ple_proj=v_w_ple_proj, v_norm_final=v_norm_final)
    weights = {n: given[n] for n in TWIN_WEIGHTS}
    shared = {n: given[n] for n in SHARED_INPUTS}
    per_example = {n: given[n] for n in ['x', 'p']}
    grad_fn = _jax.value_and_grad(_loss, argnums=(0, 1))

    def one_microbatch(ex, loss_target):
        ex = dict(ex)
        diff = ex.pop(TWIN_DIFF_INPUT)
        return grad_fn(weights, diff, {**shared, **ex}, loss_target)

    if N_MICROBATCH == 1:
        loss, (grad_w, grad_x) = one_microbatch(per_example, given["loss_target"])
    else:
        def body(carry, xs):
            loss_sum, grad_sum = carry
            l_k, (gw_k, gx_k) = one_microbatch(xs[0], xs[1])
            with _jax.named_scope("update"):
                return (loss_sum + l_k, _jax.tree.map(_jnp.add, grad_sum, gw_k)), gx_k

        init = (_jnp.zeros((), _jnp.float32), _jax.tree.map(_jnp.zeros_like, weights))
        (loss, grad_w), grad_x = _jax.lax.scan(body, init, (per_example, given["loss_target"]))
    with _jax.named_scope("update"):
        delta_w, new_m, new_v = {}, {}, {}
        for n in TWIN_WEIGHTS:
            delta_w[n], new_m[n], new_v[n] = _adamw(weights[n], grad_w[n], given["m_" + n], given["v_" + n])
    return (loss, grad_x, *[grad_w[n] for n in TWIN_WEIGHTS], *[delta_w[n] for n in TWIN_WEIGHTS],
            *[new_m[n] for n in TWIN_WEIGHTS], *[new_v[n] for n in TWIN_WEIGHTS])
```

```python
import functools

import jax
import jax.numpy as jnp
from jax import lax
from jax.experimental import pallas as pl
from jax.experimental.pallas import tpu as pltpu

F32 = jnp.float32
BF16 = jnp.bfloat16

EPS = 1e-6
HEAD_DIM = 128
GROUP = 4
CONV_KERNEL = 31
CONV_HALO = 16
CONV_TAPS_PAD = 32
GRID_W = 64
ROPE_THETA = 10000.0
ADAM_LR, ADAM_B1, ADAM_B2, ADAM_EPS, ADAM_WD, ADAM_STEP = 0.001, 0.9, 0.999, 1e-08, 0.01, 10

N_DEV = 8
MESH = pl.DeviceIdType.MESH
SUBLANES = 8
LANES = 128
VMEM_LIMIT = 48 * 1024 * 1024
ROW_TILE_ELEMS = 512 * 1024


def _div(n, pref, mult=1):
    if n <= pref:
        return n
    for d in range(pref, 0, -1):
        if n % d == 0 and d % mult == 0:
            return d
    return n


def _params(sem):
    return pltpu.CompilerParams(dimension_semantics=sem, vmem_limit_bytes=VMEM_LIMIT)


def _sum8(a):
    tm, w = a.shape
    return a.reshape(tm // SUBLANES, SUBLANES, w).sum(axis=0)


def _sigmoid(x):
    return jax.nn.sigmoid(x)


def _rowwise(name, fn, n_rows, tm, ins, outs, accs=(), ncol=1, refs_to_fn=False):
    nrow = n_rows // tm
    in_specs, arrays = [], []
    for spec in ins:
        kind, arr, w, c0 = spec[:4]
        if kind == "t":
            cs, r0 = spec[4], spec[5]
            in_specs.append(pl.BlockSpec((tm, w), lambda j, i, c0=c0, cs=cs, r0=r0: (r0 + i, c0 + cs * j)))
        elif kind == "p":
            h = spec[4]
            r = tm // h
            in_specs.append(pl.BlockSpec((h, w), lambda j, i, c0=c0, r=r: (jnp.maximum(i * r - 1, 0), c0 + j)))
        elif kind == "n":
            h = spec[4]
            r = tm // h
            last = n_rows // h - 1
            in_specs.append(
                pl.BlockSpec((h, w), lambda j, i, c0=c0, r=r, last=last: (jnp.minimum((i + 1) * r, last), c0 + j)))
        else:
            cs = spec[4]
            in_specs.append(pl.BlockSpec((arr.shape[0], w), lambda j, i, c0=c0, cs=cs: (0, c0 + cs * j)))
        arrays.append(arr)
    out_shape, out_specs = [], []
    for dtype, total, w in outs:
        out_shape.append(jax.ShapeDtypeStruct((n_rows, total), dtype))
        out_specs.append(pl.BlockSpec((tm, w), lambda j, i: (i, j)))
    for rows, total, w, follow in accs:
        out_shape.append(jax.ShapeDtypeStruct((rows, total), F32))
        out_specs.append(pl.BlockSpec((rows, w), (lambda j, i: (0, j)) if follow else (lambda j, i: (0, 0))))
    n_in, n_out, n_acc = len(ins), len(outs), len(accs)

    def body(*refs):
        j, i = pl.program_id(0), pl.program_id(1)
        res = fn(i, nrow, *(refs[:n_in] if refs_to_fn else [r[...] for r in refs[:n_in]]))
        if not isinstance(res, (tuple, list)):
            res = (res,)
        for k in range(n_out):
            refs[n_in + k][...] = res[k].astype(outs[k][0])
        for k in range(n_acc):
            ref, term = refs[n_in + n_out + k], res[n_out + k]
            first = (i == 0) if accs[k][3] else jnp.logical_and(i == 0, j == 0)

            @pl.when(first)
            def _():
                ref[...] = term

            @pl.when(jnp.logical_not(first))
            def _():
                ref[...] += term

    res = pl.pallas_call(
        body, name=name, grid=(ncol, nrow), in_specs=in_specs, out_specs=out_specs, out_shape=out_shape,
        compiler_params=_params(("arbitrary", "arbitrary")))(*arrays)
    return res


def _row_tile(n_rows, width, mult=SUBLANES):
    return _div(n_rows, max(mult, ROW_TILE_ELEMS // width), mult)


def _t(arr, w=None, col0=0, cstride=1, row0=0):
    return ("t", arr, arr.shape[1] if w is None else w, col0, cstride, row0)


def _b(arr, w=None, col0=0, cstride=0):
    return ("b", arr, arr.shape[1] if w is None else w, col0, cstride)


def _rms_fwd(name, x, g):
    n, d = x.shape

    def fn(i, nrow, x, g):
        r = lax.rsqrt(jnp.mean(x * x, axis=-1, keepdims=True) + EPS)
        return x * r * g

    return _rowwise(name, fn, n, _row_tile(n, d), [_t(x), _b(g)], [(BF16, d, d)])[0]


def _rms_bwd(name, x, g, dh, dres):
    n, d = x.shape

    def fn(i, nrow, x, g, dh, dres):
        r = lax.rsqrt(jnp.mean(x * x, axis=-1, keepdims=True) + EPS)
        w = dh * g
        dx = dres + r * w - x * (r * r * r) * jnp.mean(x * w, axis=-1, keepdims=True)
        return dx, _sum8(dh * x * r)

    return _rowwise(name, fn, n, _row_tile(n, 2 * d), [_t(x), _b(g), _t(dh), _t(dres)], [(F32, d, d)],
                    [(SUBLANES, d, d, True)])


def _swap_halves(x):
    lane = lax.broadcasted_iota(jnp.int32, x.shape, 1)
    return jnp.where(lane % 64 < 32, pltpu.roll(x, 96, axis=1), pltpu.roll(x, 32, axis=1))


def _head_norm_rope(name, z, col0, n_heads, g, cos, sin):
    n = z.shape[0]

    def fn(i, nrow, x, g, cos, sin):
        r = lax.rsqrt(jnp.mean(x * x, axis=-1, keepdims=True) + EPS)
        y = x * r * g
        return y * cos + _swap_halves(y) * sin

    tm = _div(n, 1024, SUBLANES)
    return _rowwise(name, fn, n, tm, [_t(z, HEAD_DIM, col0), _b(g), _t(cos, cstride=0), _t(sin, cstride=0)],
                    [(BF16, n_heads * HEAD_DIM, HEAD_DIM)], ncol=n_heads)[0]


def _head_norm_rope_bwd(name, douts, z, col0, n_heads, g, cos, sin):
    n = z.shape[0]
    n_d = len(douts)

    def fn(i, nrow, *a):
        dy = a[0]
        for k in range(1, n_d):
            dy = dy + a[k]
        x, g, cos, sin = a[n_d:]
        dn = dy * cos + _swap_halves(dy * sin)
        r = lax.rsqrt(jnp.mean(x * x, axis=-1, keepdims=True) + EPS)
        w = dn * g
        dx = r * w - x * (r * r * r) * jnp.mean(x * w, axis=-1, keepdims=True)
        return dx, _sum8(dn * x * r)

    tm = _div(n, 1024, SUBLANES)
    ins = [_t(arr, HEAD_DIM, c0, cs) for arr, c0, cs in douts]
    ins += [_t(z, HEAD_DIM, col0), _b(g), _t(cos, cstride=0), _t(sin, cstride=0)]
    return _rowwise(name, fn, n, tm, ins, [(BF16, n_heads * HEAD_DIM, HEAD_DIM)],
                    [(SUBLANES, HEAD_DIM, HEAD_DIM, False)], ncol=n_heads)


def _halo(arr, w, col0):
    return [("p", arr, w, col0, CONV_HALO), _t(arr, w, col0), ("n", arr, w, col0, CONV_HALO)]


def _extend(i, nrow, prev, cur, nxt):
    prev = jnp.where(i > 0, prev, 0.0)
    nxt = jnp.where(i < nrow - 1, nxt, 0.0)
    return jnp.concatenate([prev, cur, nxt], axis=0)


def _shifted(ext, offset, tm):
    n = ext.shape[0]
    rolled = ext if offset == 0 else pltpu.roll(ext, (-offset) % n, axis=0)
    return rolled[CONV_HALO:CONV_HALO + tm]


def _conv_fwd(name, z, cw, w_dw, ln_g, ln_b):
    n = z.shape[0]
    tm = _row_tile(n, 4 * cw, CONV_HALO)

    def fn(i, nrow, ap, a, an, bp, b, bn, w, g, beta):
        a = a[...]
        ext = _extend(i, nrow, ap[...] * _sigmoid(bp[...]), a * _sigmoid(b[...]), an[...] * _sigmoid(bn[...]))
        conv = jnp.zeros_like(a)
        for k in range(CONV_KERNEL):
            conv = conv + _shifted(ext, k - CONV_KERNEL // 2, tm) * w[k:k + 1, :]
        xc = conv - jnp.mean(conv, axis=-1, keepdims=True)
        ln = xc * lax.rsqrt(jnp.mean(xc * xc, axis=-1, keepdims=True) + EPS) * g[...] + beta[...]
        return conv, ln * _sigmoid(ln)

    ins = _halo(z, cw, 0) + _halo(z, cw, 1) + [_b(w_dw), _b(ln_g), _b(ln_b)]
    return _rowwise(name, fn, n, tm, ins, [(F32, cw, cw), (BF16, cw, cw)], refs_to_fn=True)


def _conv_ln_bwd(name, ds, conv, ln_g, ln_b):
    n, cw = conv.shape

    def fn(i, nrow, ds, conv, g, beta):
        xc = conv - jnp.mean(conv, axis=-1, keepdims=True)
        rstd = lax.rsqrt(jnp.mean(xc * xc, axis=-1, keepdims=True) + EPS)
        xhat = xc * rstd
        ln = xhat * g + beta
        sg = _sigmoid(ln)
        dln = ds * (sg * (1.0 + ln * (1.0 - sg)))
        dxh = dln * g
        dconv = rstd * (dxh - jnp.mean(dxh, axis=-1, keepdims=True)
                        - xhat * jnp.mean(dxh * xhat, axis=-1, keepdims=True))
        return dconv, _sum8(dln * xhat), _sum8(dln)

    return _rowwise(name, fn, n, _row_tile(n, 2 * cw), [_t(ds), _t(conv), _b(ln_g), _b(ln_b)], [(F32, cw, cw)],
                    [(SUBLANES, cw, cw, True), (SUBLANES, cw, cw, True)])


def _conv_glu_bwd(name, dconv, z, cw, w_dw):
    n = z.shape[0]
    tm = _row_tile(n, 4 * cw, CONV_HALO)

    def fn(i, nrow, dp, d, dn, ap, a, an, bp, b, bn, w):
        d, a = d[...], a[...]
        dext = _extend(i, nrow, dp[...], d, dn[...])
        sg = _sigmoid(b[...])
        uext = _extend(i, nrow, ap[...] * _sigmoid(bp[...]), a * sg, an[...] * _sigmoid(bn[...]))
        tap = lax.broadcasted_iota(jnp.int32, (CONV_TAPS_PAD, cw), 0)
        du = jnp.zeros_like(d)
        dw = jnp.zeros((CONV_TAPS_PAD, cw), F32)
        for k in range(CONV_KERNEL):
            off = k - CONV_KERNEL // 2
            du = du + _shifted(dext, -off, tm) * w[k:k + 1, :]
            dw_k = jnp.sum(d * _shifted(uext, off, tm), axis=0, keepdims=True)
            dw = dw + jnp.where(tap == k, dw_k, 0.0)
        return du * sg, du * a * sg * (1.0 - sg), dw

    ins = _halo(dconv, cw, 0) + _halo(z, cw, 0) + _halo(z, cw, 1) + [_b(w_dw)]
    return _rowwise(name, fn, n, tm, ins, [(BF16, cw, cw), (BF16, cw, cw)], [(CONV_TAPS_PAD, cw, cw, True)],
                    refs_to_fn=True)


def _merge_fwd(name, z, d, y_c, y_a):
    n, hw = z.shape[0], d // 2

    def fn(i, nrow, gc, ga, yc, ya):
        return _sigmoid(gc) * yc + _sigmoid(ga) * ya

    ins = [_t(z, hw, 5), _t(z, hw, 7), _t(y_c, hw), _t(y_a, hw)]
    return _rowwise(name, fn, n, _row_tile(n, 2 * hw), ins, [(BF16, d, hw)], ncol=2)[0]


def _merge_bwd(name, dm, z, d, y_c, y_a):
    n, hw = z.shape[0], d // 2

    def fn(i, nrow, dm, gc, ga, yc, ya):
        sc, sa = _sigmoid(gc), _sigmoid(ga)
        return dm * yc * sc * (1.0 - sc), dm * ya * sa * (1.0 - sa), dm * sc, dm * sa

    ins = [_t(dm, hw), _t(z, hw, 5), _t(z, hw, 7), _t(y_c, hw), _t(y_a, hw)]
    return _rowwise(name, fn, n, _row_tile(n, 2 * hw), ins, [(BF16, d, hw)] * 4, ncol=2)


def _relu2(name, f):
    n, w = f.shape
    tw = _div(w, 2048, LANES)

    def fn(i, nrow, f):
        r = jnp.maximum(f, 0.0)
        return r * r

    return _rowwise(name, fn, n, _row_tile(n, tw), [_t(f, tw)], [(BF16, w, tw)], ncol=w // tw)[0]


def _relu2_bwd(name, da, f):
    n, w = f.shape
    tw = _div(w, 2048, LANES)

    def fn(i, nrow, da, f):
        return da * (2.0 * jnp.maximum(f, 0.0))

    return _rowwise(name, fn, n, _row_tile(n, tw), [_t(da, tw), _t(f, tw)], [(BF16, w, tw)], ncol=w // tw)[0]


def _head_loss(name, x2, pre, pp, target, g):
    n, d = x2.shape

    def fn(i, nrow, x2, pre, pp, t, g):
        x3 = x2 + _sigmoid(pre) * pp
        r = lax.rsqrt(jnp.mean(x3 * x3, axis=-1, keepdims=True) + EPS)
        xn = x3 * r
        e = xn * g - t
        part = 0.5 * jnp.sum(jnp.mean(e * e, axis=-1, keepdims=True), axis=0, keepdims=True)
        dy = e * (1.0 / d)
        w = dy * g
        dx3 = r * w - x3 * (r * r * r) * jnp.mean(x3 * w, axis=-1, keepdims=True)
        return dx3, _sum8(dy * xn), jnp.broadcast_to(part, (SUBLANES, LANES))

    ins = [_t(x2), _t(pre), _t(pp), _t(target), _b(g)]
    return _rowwise(name, fn, n, _row_tile(n, 2 * d), ins, [(F32, d, d)],
                    [(SUBLANES, d, d, True), (SUBLANES, LANES, LANES, True)])


def _ple_bwd(name, dx3, pre, pp):
    n, d = dx3.shape

    def fn(i, nrow, dx3, pre, pp):
        gate = _sigmoid(pre)
        return dx3 * gate, dx3 * pp * gate * (1.0 - gate)

    return _rowwise(name, fn, n, _row_tile(n, 2 * d), [_t(dx3), _t(pre), _t(pp)], [(BF16, d, d)] * 2)


def _group_sum(name, dvh, n_kv):
    n = dvh.shape[0]

    def fn(i, nrow, *a):
        s = a[0]
        for k in range(1, GROUP):
            s = s + a[k]
        return s

    ins = [_t(dvh, HEAD_DIM, k, GROUP) for k in range(GROUP)]
    return _rowwise(name, fn, n, _div(n, 1024, SUBLANES), ins, [(BF16, n_kv * HEAD_DIM, HEAD_DIM)], ncol=n_kv)[0]


def _mm_tiles(m, n_free, n_red):
    return _div(m, 1024, SUBLANES), _div(n_free, 1152, LANES), _div(n_red, 512, LANES)


def _mm_nn(name, a, w3, out_dtype, residual=None):
    m, k = a.shape
    nb, _, ns = w3.shape
    tm, tn, tk = _mm_tiles(m, ns, k)
    per, nk = ns // tn, k // tk
    has_res = residual is not None

    def body(*refs):
        a_ref, w_ref = refs[:2]
        o_ref, acc = refs[-2:]
        kk = pl.program_id(2)
        part = jnp.dot(a_ref[...].astype(BF16), w_ref[...].astype(BF16), preferred_element_type=F32)

        @pl.when(kk == 0)
        def _():
            acc[...] = part

        @pl.when(kk > 0)
        def _():
            acc[...] += part

        @pl.when(kk == nk - 1)
        def _():
            r = acc[...]
            if has_res:
                r = r + refs[2][...]
            o_ref[...] = r.astype(out_dtype)

    in_specs = [pl.BlockSpec((tm, tk), lambda i, n, kk: (i, kk)),
                pl.BlockSpec((None, tk, tn), lambda i, n, kk: (n // per, kk, n % per))]
    args = [a, w3]
    if has_res:
        in_specs.append(pl.BlockSpec((tm, tn), lambda i, n, kk: (i, n)))
        args.append(residual)
    return pl.pallas_call(
        body, name=name, grid=(m // tm, nb * per, nk), in_specs=in_specs,
        out_specs=pl.BlockSpec((tm, tn), lambda i, n, kk: (i, n)),
        out_shape=jax.ShapeDtypeStruct((m, nb * ns), out_dtype),
        scratch_shapes=[pltpu.VMEM((tm, tn), F32)],
        compiler_params=_params(("parallel", "parallel", "arbitrary")))(*args)


def _mm_nt(name, dc, w3, out_dtype):
    m = dc.shape[0]
    nb, k, ns = w3.shape
    tm, to, tr = _mm_tiles(m, k, ns)
    per, nr = ns // tr, nb * (ns // tr)

    def body(dc_ref, w_ref, o_ref, acc):
        r = pl.program_id(2)
        part = lax.dot_general(dc_ref[...].astype(BF16), w_ref[...].astype(BF16), (((1,), (1,)), ((), ())),
                               preferred_element_type=F32)

        @pl.when(r == 0)
        def _():
            acc[...] = part

        @pl.when(r > 0)
        def _():
            acc[...] += part

        @pl.when(r == nr - 1)
        def _():
            o_ref[...] = acc[...].astype(out_dtype)

    return pl.pallas_call(
        body, name=name, grid=(m // tm, k // to, nr),
        in_specs=[pl.BlockSpec((tm, tr), lambda i, o, r: (i, r)),
                  pl.BlockSpec((None, to, tr), lambda i, o, r: (r // per, o, r % per))],
        out_specs=pl.BlockSpec((tm, to), lambda i, o, r: (i, o)),
        out_shape=jax.ShapeDtypeStruct((m, k), out_dtype),
        scratch_shapes=[pltpu.VMEM((tm, to), F32)],
        compiler_params=_params(("parallel", "parallel", "arbitrary")))(dc, w3)


def _mm_tn(name, a, dc, nb):
    m, k = a.shape
    ns = dc.shape[1] // nb
    tr, tn, to = _div(m, 512, LANES), _div(ns, 1152, LANES), _div(k, 1024, LANES)
    per, nr = ns // tn, m // tr

    def body(a_ref, dc_ref, o_ref, acc):
        r = pl.program_id(2)
        part = lax.dot_general(a_ref[...].astype(BF16), dc_ref[...].astype(BF16), (((0,), (0,)), ((), ())),
                               preferred_element_type=F32)

        @pl.when(r == 0)
        def _():
            acc[...] = part

        @pl.when(r > 0)
        def _():
            acc[...] += part

        @pl.when(r == nr - 1)
        def _():
            o_ref[...] = acc[...].astype(BF16)

    return pl.pallas_call(
        body, name=name, grid=(k // to, nb * per, nr),
        in_specs=[pl.BlockSpec((tr, to), lambda o, n, r: (r, o)),
                  pl.BlockSpec((tr, tn), lambda o, n, r: (r, n))],
        out_specs=pl.BlockSpec((None, to, tn), lambda o, n, r: (n // per, o, n % per)),
        out_shape=jax.ShapeDtypeStruct((nb, k, ns), BF16),
        scratch_shapes=[pltpu.VMEM((to, tn), F32)],
        compiler_params=_params(("parallel", "parallel", "arbitrary")))(a, dc)


def _flash_fwd(qr, kr, z, v_col0, n_heads):
    t = qr.shape[0]
    tq, tk = _div(t, 1024, LANES), _div(t, 1024, LANES)
    nk = t // tk
    scale = HEAD_DIM ** -0.5

    def body(q_ref, k_ref, v_ref, o_ref, lse_ref, m_scr, l_scr, acc_scr):
        kb = pl.program_id(2)

        @pl.when(kb == 0)
        def _():
            m_scr[...] = jnp.full_like(m_scr, -1e30)
            l_scr[...] = jnp.zeros_like(l_scr)
            acc_scr[...] = jnp.zeros_like(acc_scr)

        s = lax.dot_general(q_ref[...], k_ref[...], (((1,), (1,)), ((), ())), preferred_element_type=F32) * scale
        m_prev = m_scr[...]
        m_new = jnp.maximum(m_prev, jnp.max(s, axis=1, keepdims=True))
        alpha = jnp.exp(m_prev - m_new)
        p = jnp.exp(s - m_new)
        l_scr[...] = alpha * l_scr[...] + jnp.sum(p, axis=1, keepdims=True)
        acc_scr[...] = alpha * acc_scr[...] + jnp.dot(p.astype(BF16), v_ref[...].astype(BF16),
                                                      preferred_element_type=F32)
        m_scr[...] = m_new

        @pl.when(kb == nk - 1)
        def _():
            l = l_scr[...]
            o_ref[...] = acc_scr[...] / l
            lse = jnp.broadcast_to(m_scr[...] + jnp.log(l), (tq, LANES))
            lse_ref[...] = jnp.transpose(lse)[0:SUBLANES, :]

    return pl.pallas_call(
        body, name="flash_fwd", grid=(n_heads, t // tq, nk),
        in_specs=[pl.BlockSpec((tq, HEAD_DIM), lambda h, qi, ki: (qi, h)),
                  pl.BlockSpec((tk, HEAD_DIM), lambda h, qi, ki: (ki, h // GROUP)),
                  pl.BlockSpec((tk, HEAD_DIM), lambda h, qi, ki: (ki, v_col0 + h // GROUP))],
        out_specs=[pl.BlockSpec((tq, HEAD_DIM), lambda h, qi, ki: (qi, h)),
                   pl.BlockSpec((None, SUBLANES, tq), lambda h, qi, ki: (h, 0, qi))],
        out_shape=[jax.ShapeDtypeStruct((t, n_heads * HEAD_DIM), F32),
                   jax.ShapeDtypeStruct((n_heads, SUBLANES, t), F32)],
        scratch_shapes=[pltpu.VMEM((tq, 1), F32), pltpu.VMEM((tq, 1), F32), pltpu.VMEM((tq, HEAD_DIM), F32)],
        compiler_params=_params(("parallel", "parallel", "arbitrary")))(qr, kr, z)


def _attn_delta(do, o, n_heads):
    t = o.shape[0]
    tm = _div(t, 1024, LANES)

    def body(do_ref, o_ref, d_ref):
        prod = do_ref[...].astype(F32) * o_ref[...]
        row = jnp.sum(jnp.transpose(prod), axis=0, keepdims=True)
        d_ref[...] = jnp.broadcast_to(row, (SUBLANES, tm))

    return pl.pallas_call(
        body, name="attn_delta", grid=(n_heads, t // tm),
        in_specs=[pl.BlockSpec((tm, HEAD_DIM), lambda h, i: (i, h)),
                  pl.BlockSpec((tm, HEAD_DIM), lambda h, i: (i, h))],
        out_specs=pl.BlockSpec((None, SUBLANES, tm), lambda h, i: (h, 0, i)),
        out_shape=jax.ShapeDtypeStruct((n_heads, SUBLANES, t), F32),
        compiler_params=_params(("parallel", "parallel")))(do, o)


def _flash_bwd(qr, kr, z, v_col0, do, lse, delta, n_heads):
    t = qr.shape[0]
    tq, tk = _div(t, 1024, LANES), _div(t, 1024, LANES)
    nq = t // tq
    scale = HEAD_DIM ** -0.5

    def body(q_ref, k_ref, v_ref, do_ref, lse_ref, dl_ref, dq_ref, dk_ref, dv_ref, dk_acc, dv_acc):
        kb, qb = pl.program_id(1), pl.program_id(2)
        q, k, do_ = q_ref[...], k_ref[...], do_ref[...]
        v = v_ref[...].astype(BF16)
        s_t = lax.dot_general(k, q, (((1,), (1,)), ((), ())), preferred_element_type=F32) * scale
        p_t = jnp.exp(s_t - lse_ref[0:1, :])
        dp_t = lax.dot_general(v, do_, (((1,), (1,)), ((), ())), preferred_element_type=F32)
        ds_t = (p_t * (dp_t - dl_ref[0:1, :]) * scale).astype(BF16)
        dv_c = jnp.dot(p_t.astype(BF16), do_, preferred_element_type=F32)
        dk_c = jnp.dot(ds_t, q, preferred_element_type=F32)
        dq_c = lax.dot_general(ds_t, k, (((0,), (0,)), ((), ())), preferred_element_type=F32)

        @pl.when(qb == 0)
        def _():
            dk_acc[...] = dk_c
            dv_acc[...] = dv_c

        @pl.when(qb > 0)
        def _():
            dk_acc[...] += dk_c
            dv_acc[...] += dv_c

        @pl.when(qb == nq - 1)
        def _():
            dk_ref[...] = dk_acc[...]
            dv_ref[...] = dv_acc[...]

        rows = pl.ds(pl.multiple_of(qb * tq, tq), tq)

        @pl.when(kb == 0)
        def _():
            dq_ref[rows, :] = dq_c

        @pl.when(kb > 0)
        def _():
            dq_ref[rows, :] += dq_c

    wide = jax.ShapeDtypeStruct((t, n_heads * HEAD_DIM), F32)
    return pl.pallas_call(
        body, name="flash_bwd", grid=(n_heads, t // tk, nq),
        in_specs=[pl.BlockSpec((tq, HEAD_DIM), lambda h, kb, qb: (qb, h)),
                  pl.BlockSpec((tk, HEAD_DIM), lambda h, kb, qb: (kb, h // GROUP)),
                  pl.BlockSpec((tk, HEAD_DIM), lambda h, kb, qb: (kb, v_col0 + h // GROUP)),
                  pl.BlockSpec((tq, HEAD_DIM), lambda h, kb, qb: (qb, h)),
                  pl.BlockSpec((None, SUBLANES, tq), lambda h, kb, qb: (h, 0, qb)),
                  pl.BlockSpec((None, SUBLANES, tq), lambda h, kb, qb: (h, 0, qb))],
        out_specs=[pl.BlockSpec((t, HEAD_DIM), lambda h, kb, qb: (0, h)),
                   pl.BlockSpec((tk, HEAD_DIM), lambda h, kb, qb: (kb, h)),
                   pl.BlockSpec((tk, HEAD_DIM), lambda h, kb, qb: (kb, h))],
        out_shape=[wide, wide, wide],
        scratch_shapes=[pltpu.VMEM((tk, HEAD_DIM), F32), pltpu.VMEM((tk, HEAD_DIM), F32)],
        compiler_params=_params(("parallel", "arbitrary", "arbitrary")))(qr, kr, z, do, lse, delta)


def _position():
    x, y, c = lax.axis_index("x"), lax.axis_index("y"), lax.axis_index("c")
    return x, y, c


def _index(x, y, c):
    return 4 * x + 2 * y + c


def _all_gather(name, shard):
    a, b = shard.shape

    def body(x_ref, out_ref, send_sems, recv_sems, local_sem):
        x, y, c = _position()
        me, sibling = (x, y, c), (x, y, 1 - c)
        chips = [(1 - x, y), (x, 1 - y), (1 - x, 1 - y)]

        def block(px, py, pc):
            return out_ref.at[_index(px, py, pc)]

        def copy(k, blk, to, src=None):
            return pltpu.make_async_remote_copy(
                src_ref=block(*blk) if src is None else src, dst_ref=block(*blk),
                send_sem=send_sems.at[k], recv_sem=recv_sems.at[k], device_id=to, device_id_type=MESH)

        mine = pltpu.make_async_copy(x_ref, block(*me), local_sem)
        mine.start()
        first = [copy(0, me, sibling, src=x_ref)]
        first += [copy(1 + j, me, (*chip, c), src=x_ref) for j, chip in enumerate(chips)]
        for cp in first:
            cp.start()
        passed = [copy(4 + j, (*chip, c), sibling) for j, chip in enumerate(chips)]
        for j, chip in enumerate(chips):
            copy(1 + j, (*chip, c), me).wait_recv()
            passed[j].start()
        copy(0, sibling, me).wait_recv()
        for j, chip in enumerate(chips):
            copy(4 + j, (*chip, 1 - c), me).wait_recv()
        for cp in first + passed:
            cp.wait_send()
        mine.wait()

    return pl.pallas_call(
        body, name=name, out_shape=jax.ShapeDtypeStruct((N_DEV, a, b), shard.dtype),
        in_specs=[pl.BlockSpec(memory_space=pltpu.HBM)], out_specs=pl.BlockSpec(memory_space=pltpu.HBM),
        scratch_shapes=[pltpu.SemaphoreType.DMA((7,)), pltpu.SemaphoreType.DMA((7,)), pltpu.SemaphoreType.DMA],
    )(shard)


def _exchange_blocks(name, g3):
    _, a, b = g3.shape
    flips = [(fx, fy, fc) for fx in (0, 1) for fy in (0, 1) for fc in (0, 1)][1:]

    def body(g_ref, land_ref, send_sems, recv_sems, local_sem):
        x, y, c = _position()
        me = _index(x, y, c)
        own = pltpu.make_async_copy(g_ref.at[me], land_ref.at[me], local_sem)
        own.start()
        peers = [((1 - x) if fx else x, (1 - y) if fy else y, (1 - c) if fc else c) for fx, fy, fc in flips]

        def copy(k, src_block, dst_block):
            return pltpu.make_async_remote_copy(
                src_ref=g_ref.at[src_block], dst_ref=land_ref.at[dst_block],
                send_sem=send_sems.at[k], recv_sem=recv_sems.at[k], device_id=peers[k], device_id_type=MESH)

        sends = [copy(k, _index(*peer), me) for k, peer in enumerate(peers)]
        for cp in sends:
            cp.start()
        for k, peer in enumerate(peers):
            copy(k, me, _index(*peer)).wait_recv()
        for cp in sends:
            cp.wait_send()
        own.wait()

    return pl.pallas_call(
        body, name=name, out_shape=jax.ShapeDtypeStruct(g3.shape, g3.dtype),
        in_specs=[pl.BlockSpec(memory_space=pltpu.HBM)], out_specs=pl.BlockSpec(memory_space=pltpu.HBM),
        scratch_shapes=[pltpu.SemaphoreType.DMA((7,)), pltpu.SemaphoreType.DMA((7,)), pltpu.SemaphoreType.DMA],
    )(g3)


def _adamw(w, g, m, v):
    m = ADAM_B1 * m + (1.0 - ADAM_B1) * g
    v = ADAM_B2 * v + (1.0 - ADAM_B2) * (g * g)
    m_hat = m / (1.0 - ADAM_B1 ** ADAM_STEP)
    v_hat = v / (1.0 - ADAM_B2 ** ADAM_STEP)
    delta = -ADAM_LR * (m_hat / (jnp.sqrt(v_hat) + ADAM_EPS) + ADAM_WD * w)
    return delta, m, v


def _adamw_shard(name, land, w, m, v):
    _, a, b = land.shape
    tm = _div(a, max(16, ROW_TILE_ELEMS // (2 * b)), 16)
    per = a // tm

    def fn(i, nrow, *t):
        g = t[0].astype(F32)
        for s in range(1, N_DEV):
            g = g + t[s].astype(F32)
        w, m, v = t[N_DEV:]
        return (g,) + _adamw(w, g, m, v)

    land2 = land.reshape(N_DEV * a, b)
    ins = [_t(land2, row0=s * per) for s in range(N_DEV)] + [_t(w), _t(m), _t(v)]
    return _rowwise(name, fn, a, tm, ins, [(F32, b, b)] * 4)


def _adamw_small(name, parts, w, m, v):
    rows, n = w.shape

    def body(p_ref, w_ref, m_ref, v_ref, g_out, d_out, m_out, v_out):
        g = p_ref[0]
        for s in range(1, parts.shape[0]):
            g = g + p_ref[s]
        g = jnp.sum(g, axis=0, keepdims=True) if rows == 1 else g[0:rows, :]
        delta, m_new, v_new = _adamw(w_ref[...], g, m_ref[...], v_ref[...])
        g_out[...] = g
        d_out[...] = delta
        m_out[...] = m_new
        v_out[...] = v_new

    return pl.pallas_call(body, name=name, out_shape=[jax.ShapeDtypeStruct((rows, n), F32)] * 4)(parts, w, m, v)


def _rope_tables(t):
    pos = jnp.arange(t, dtype=jnp.int32)
    half = HEAD_DIM // 4
    inv_freq = ROPE_THETA ** (-jnp.arange(0, 2 * half, 2, dtype=F32) / (2 * half))
    ang_r = (pos // GRID_W).astype(F32)[:, None] * inv_freq[None, :]
    ang_c = (pos % GRID_W).astype(F32)[:, None] * inv_freq[None, :]
    cos = jnp.concatenate([jnp.cos(ang_r)] * 2 + [jnp.cos(ang_c)] * 2, axis=-1)
    sin = jnp.concatenate([-jnp.sin(ang_r), jnp.sin(ang_r), -jnp.sin(ang_c), jnp.sin(ang_c)], axis=-1)
    return cos, sin


def _gather_weight(name, w, cols):
    g = _all_gather(name, w[0].astype(BF16))
    return g if cols else g.reshape(1, N_DEV * g.shape[1], g.shape[2])


def kernel(x, p, norm_mix, w_in, w_dw, conv_ln_g, conv_ln_b, w_conv_proj, q_norm, k_norm, w_attn_proj, w_out, norm_ffn, w_ff1, w_ff2, norm_ple, w_ple_gate, w_ple_proj, norm_final, loss_target, m_norm_mix, m_w_in, m_w_dw, m_conv_ln_g, m_conv_ln_b, m_w_conv_proj, m_q_norm, m_k_norm, m_w_attn_proj, m_w_out, m_norm_ffn, m_w_ff1, m_w_ff2, m_norm_ple, m_w_ple_gate, m_w_ple_proj, m_norm_final, v_norm_mix, v_w_in, v_w_dw, v_conv_ln_g, v_conv_ln_b, v_w_conv_proj, v_q_norm, v_k_norm, v_w_attn_proj, v_w_out, v_norm_ffn, v_w_ff1, v_w_ff2, v_norm_ple, v_w_ple_gate, v_w_ple_proj, v_norm_final):
    t, d = x.shape[1], x.shape[2]
    cw = d // 2
    n_heads = d // HEAD_DIM
    n_kv = n_heads // GROUP
    col_q, col_k, col_v = d // HEAD_DIM, 2 * d // HEAD_DIM, (2 * d + d // 4) // HEAD_DIM
    x0, pe, tgt = x[0], p[0, 0], loss_target[0]
    g_final = norm_final.reshape(1, d)
    me = _index(*_position())

    wg_in = _gather_weight("ag_w_in", w_in, True)
    wg_cp = _gather_weight("ag_w_conv_proj", w_conv_proj, True)
    wg_ap = _gather_weight("ag_w_attn_proj", w_attn_proj, False)
    wg_out = _gather_weight("ag_w_out", w_out, False)
    wg_ff1 = _gather_weight("ag_w_ff1", w_ff1, True)
    wg_ff2 = _gather_weight("ag_w_ff2", w_ff2, False)
    wg_pg = _gather_weight("ag_w_ple_gate", w_ple_gate, False)
    wg_pp = _gather_weight("ag_w_ple_proj", w_ple_proj, True)
    dw_pad = jnp.pad(w_dw[0], ((0, CONV_TAPS_PAD - CONV_KERNEL), (0, 0)))
    dw_all = _all_gather("ag_w_dw", dw_pad)
    dw_full = dw_all.transpose(1, 0, 2).reshape(CONV_TAPS_PAD, cw)
    cos, sin = _rope_tables(t)

    h1 = _rms_fwd("rms_mix", x0, norm_mix)
    z = _mm_nn("mm_in", h1, wg_in, F32)
    conv, s_c = _conv_fwd("conv_fwd", z, cw, dw_full, conv_ln_g, conv_ln_b)
    y_c = _mm_nn("mm_conv_proj", s_c, wg_cp, F32)
    qr = _head_norm_rope("q_prep", z, col_q, n_heads, q_norm, cos, sin)
    kr = _head_norm_rope("k_prep", z, col_k, n_kv, k_norm, cos, sin)
    o, lse = _flash_fwd(qr, kr, z, col_v, n_heads)
    y_a = _mm_nn("mm_attn_proj", o, wg_ap, F32)
    merged = _merge_fwd("merge_fwd", z, d, y_c, y_a)
    x1 = _mm_nn("mm_out", merged, wg_out, F32, residual=x0)
    h2 = _rms_fwd("rms_ffn", x1, norm_ffn)
    f = _mm_nn("mm_ff1", h2, wg_ff1, F32)
    act = _relu2("relu2", f)
    x2 = _mm_nn("mm_ff2", act, wg_ff2, F32, residual=x1)
    hp = _rms_fwd("rms_ple", x2, norm_ple)
    pre = _mm_nn("mm_ple_gate", hp, wg_pg, F32)
    pp = _mm_nn("mm_ple_proj", pe, wg_pp, F32)

    dx3, dg_final, loss_part = _head_loss("head_loss", x2, pre, pp, tgt, g_final)
    loss = lax.psum(loss_part[0, 0], ("x", "y", "c"))
    dpp, dpre = _ple_bwd("ple_bwd", dx3, pre, pp)
    gw_pp = _mm_tn("mm_d_ple_proj", pe, dpp, N_DEV)
    gw_pg = _mm_tn("mm_d_ple_gate", hp, dpre, 1)
    dhp = _mm_nt("mm_dhp", dpre, wg_pg, F32)
    dx2, dg_ple = _rms_bwd("rms_ple_bwd", x2, norm_ple, dhp, dx3)
    gw_ff2 = _mm_tn("mm_d_ff2", act, dx2, 1)
    dact = _mm_nt("mm_dact", dx2, wg_ff2, F32)
    df = _relu2_bwd("relu2_bwd", dact, f)
    gw_ff1 = _mm_tn("mm_d_ff1", h2, df, N_DEV)
    dh2 = _mm_nt("mm_dh2", df, wg_ff1, F32)
    dx1, dg_ffn = _rms_bwd("rms_ffn_bwd", x1, norm_ffn, dh2, dx2)
    gw_out = _mm_tn("mm_d_out", merged, dx1, 1)
    dmerged = _mm_nt("mm_dmerged", dx1, wg_out, F32)
    dgc, dga, dyc, dya = _merge_bwd("merge_bwd", dmerged, z, d, y_c, y_a)
    gw_cp = _mm_tn("mm_d_conv_proj", s_c, dyc, N_DEV)
    ds_c = _mm_nt("mm_ds_c", dyc, wg_cp, F32)
    dconv, dg_ln, db_ln = _conv_ln_bwd("conv_ln_bwd", ds_c, conv, conv_ln_g, conv_ln_b)
    dca, dcb, dg_dw = _conv_glu_bwd("conv_glu_bwd", dconv, z, cw, dw_full)
    gw_ap = _mm_tn("mm_d_attn_proj", o, dya, 1)
    do = _mm_nt("mm_do", dya, wg_ap, BF16)
    delta = _attn_delta(do, o, n_heads)
    dqr, dkh, dvh = _flash_bwd(qr, kr, z, col_v, do, lse, delta, n_heads)
    dq, dg_q = _head_norm_rope_bwd("q_prep_bwd", [(dqr, 0, 1)], z, col_q, n_heads, q_norm, cos, sin)
    dk, dg_k = _head_norm_rope_bwd("k_prep_bwd", [(dkh, g, GROUP) for g in range(GROUP)], z, col_k, n_kv, k_norm,
                                   cos, sin)
    dv = _group_sum("dv_sum", dvh, n_kv)
    dz = jnp.concatenate([dca, dcb, dq, dk, dv, dgc, dga], axis=1)
    gw_in = _mm_tn("mm_d_in", h1, dz, N_DEV)
    dh1 = _mm_nt("mm_dh1", dz, wg_in, F32)
    grad_x, dg_mix = _rms_bwd("rms_mix_bwd", x0, norm_mix, dh1, dx1)

    def pad(a):
        return jnp.pad(a, ((0, 0), (0, d - a.shape[1])))

    small = [dg_mix, dg_ffn, dg_ple, dg_final, dg_ln, db_ln, dg_q, dg_k, dg_dw]
    packed = _all_gather("ag_small", jnp.concatenate([pad(a) for a in small], axis=0))
    parts, row = [], 0
    for a in small:
        parts.append(packed[:, row:row + a.shape[0], :a.shape[1]])
        row += a.shape[0]
    p_mix, p_ffn, p_ple, p_final, p_lng, p_lnb, p_q, p_k, p_dw = parts
    p_dw = lax.dynamic_slice_in_dim(p_dw, me * (cw // N_DEV), cw // N_DEV, axis=2)

    def big(name, gw, w, m, v):
        blocks = gw if gw.shape[0] == N_DEV else gw.reshape(N_DEV, gw.shape[1] // N_DEV, gw.shape[2])
        land = _exchange_blocks("rs_" + name, blocks)
        return [r.reshape(w.shape) for r in _adamw_shard("adamw_" + name, land, w[0], m[0], v[0])]

    def little(name, part, w, m, v):
        rows, n = (1 if w.ndim < 3 else w.shape[1]), w.shape[-1]
        padded = rows if rows == 1 else part.shape[1]

        def two_d(a):
            return jnp.pad(a.reshape(rows, n), ((0, padded - rows), (0, 0)))

        res = _adamw_small("adamw_" + name, part, two_d(w), two_d(m), two_d(v))
        return [r[:rows].reshape(w.shape) for r in res]

    results = [
        little("norm_mix", p_mix, norm_mix, m_norm_mix, v_norm_mix),
        big("w_in", gw_in, w_in, m_w_in, v_w_in),
        little("w_dw", p_dw, w_dw, m_w_dw, v_w_dw),
        little("conv_ln_g", p_lng, conv_ln_g, m_conv_ln_g, v_conv_ln_g),
        little("conv_ln_b", p_lnb, conv_ln_b, m_conv_ln_b, v_conv_ln_b),
        big("w_conv_proj", gw_cp, w_conv_proj, m_w_conv_proj, v_w_conv_proj),
        little("q_norm", p_q, q_norm, m_q_norm, v_q_norm),
        little("k_norm", p_k, k_norm, m_k_norm, v_k_norm),
        big("w_attn_proj", gw_ap, w_attn_proj, m_w_attn_proj, v_w_attn_proj),
        big("w_out", gw_out, w_out, m_w_out, v_w_out),
        little("norm_ffn", p_ffn, norm_ffn, m_norm_ffn, v_norm_ffn),
        big("w_ff1", gw_ff1, w_ff1, m_w_ff1, v_w_ff1),
        big("w_ff2", gw_ff2, w_ff2, m_w_ff2, v_w_ff2),
        little("norm_ple", p_ple, norm_ple, m_norm_ple, v_norm_ple),
        big("w_ple_gate", gw_pg, w_ple_gate, m_w_ple_gate, v_w_ple_gate),
        big("w_ple_proj", gw_pp, w_ple_proj, m_w_ple_proj, v_w_ple_proj),
        little("norm_final", p_final, norm_final, m_norm_final, v_norm_final),
    ]
    grads, deltas, new_m, new_v = zip(*results)
    return (loss, grad_x[None], *grads, *deltas, *new_m, *new_v)
```

```python
import functools

import jax
import jax.numpy as jnp
from jax import lax
from jax.experimental import pallas as pl
from jax.experimental.pallas import tpu as pltpu

F32 = jnp.float32
BF16 = jnp.bfloat16

EPS = 1e-6
HEAD_DIM = 128
GROUP = 4
CONV_KERNEL = 31
CONV_HALO = 16
CONV_TAPS_PAD = 32
GRID_W = 64
ROPE_THETA = 10000.0
ADAM_LR, ADAM_B1, ADAM_B2, ADAM_EPS, ADAM_WD, ADAM_STEP = 0.001, 0.9, 0.999, 1e-08, 0.01, 10

N_DEV = 8
MESH = pl.DeviceIdType.MESH
SUBLANES = 8
LANES = 128
VMEM_LIMIT = 48 * 1024 * 1024
ROW_TILE_ELEMS = 512 * 1024


def _div(n, pref, mult=1):
    if n <= pref:
        return n
    for d in range(pref, 0, -1):
        if n % d == 0 and d % mult == 0:
            return d
    return n


def _params(sem):
    return pltpu.CompilerParams(dimension_semantics=sem, vmem_limit_bytes=VMEM_LIMIT)


def _sum8(a):
    tm, w = a.shape
    return a.reshape(tm // SUBLANES, SUBLANES, w).sum(axis=0)


def _sigmoid(x):
    return jax.nn.sigmoid(x)


def _rowwise(name, fn, n_rows, tm, ins, outs, accs=(), ncol=1, refs_to_fn=False):
    nrow = n_rows // tm
    in_specs, arrays = [], []
    for spec in ins:
        kind, arr, w, c0 = spec[:4]
        if kind == "t":
            cs, r0 = spec[4], spec[5]
            in_specs.append(pl.BlockSpec((tm, w), lambda j, i, c0=c0, cs=cs, r0=r0: (r0 + i, c0 + cs * j)))
        elif kind == "p":
            h = spec[4]
            r = tm // h
            in_specs.append(pl.BlockSpec((h, w), lambda j, i, c0=c0, r=r: (jnp.maximum(i * r - 1, 0), c0 + j)))
        elif kind == "n":
            h = spec[4]
            r = tm // h
            last = n_rows // h - 1
            in_specs.append(
                pl.BlockSpec((h, w), lambda j, i, c0=c0, r=r, last=last: (jnp.minimum((i + 1) * r, last), c0 + j)))
        else:
            cs = spec[4]
            in_specs.append(pl.BlockSpec((arr.shape[0], w), lambda j, i, c0=c0, cs=cs: (0, c0 + cs * j)))
        arrays.append(arr)
    out_shape, out_specs = [], []
    for dtype, total, w in outs:
        out_shape.append(jax.ShapeDtypeStruct((n_rows, total), dtype))
        out_specs.append(pl.BlockSpec((tm, w), lambda j, i: (i, j)))
    for rows, total, w, follow in accs:
        out_shape.append(jax.ShapeDtypeStruct((rows, total), F32))
        out_specs.append(pl.BlockSpec((rows, w), (lambda j, i: (0, j)) if follow else (lambda j, i: (0, 0))))
    n_in, n_out, n_acc = len(ins), len(outs), len(accs)

    def body(*refs):
        j, i = pl.program_id(0), pl.program_id(1)
        res = fn(i, nrow, *(refs[:n_in] if refs_to_fn else [r[...] for r in refs[:n_in]]))
        if not isinstance(res, (tuple, list)):
            res = (res,)
        for k in range(n_out):
            refs[n_in + k][...] = res[k].astype(outs[k][0])
        for k in range(n_acc):
            ref, term = refs[n_in + n_out + k], res[n_out + k]
            first = (i == 0) if accs[k][3] else jnp.logical_and(i == 0, j == 0)

            @pl.when(first)
            def _():
                ref[...] = term

            @pl.when(jnp.logical_not(first))
            def _():
                ref[...] += term

    res = pl.pallas_call(
        body, name=name, grid=(ncol, nrow), in_specs=in_specs, out_specs=out_specs, out_shape=out_shape,
        compiler_params=_params(("arbitrary", "arbitrary")))(*arrays)
    return res


def _row_tile(n_rows, width, mult=SUBLANES):
    return _div(n_rows, max(mult, ROW_TILE_ELEMS // width), mult)


def _t(arr, w=None, col0=0, cstride=1, row0=0):
    return ("t", arr, arr.shape[1] if w is None else w, col0, cstride, row0)


def _b(arr, w=None, col0=0, cstride=0):
    return ("b", arr, arr.shape[1] if w is None else w, col0, cstride)


def _rms_fwd(name, x, g):
    n, d = x.shape

    def fn(i, nrow, x, g):
        r = lax.rsqrt(jnp.mean(x * x, axis=-1, keepdims=True) + EPS)
        return x * r * g

    return _rowwise(name, fn, n, _row_tile(n, d), [_t(x), _b(g)], [(BF16, d, d)])[0]


def _rms_bwd(name, x, g, dh, dres):
    n, d = x.shape

    def fn(i, nrow, x, g, dh, dres):
        r = lax.rsqrt(jnp.mean(x * x, axis=-1, keepdims=True) + EPS)
        w = dh * g
        dx = dres + r * w - x * (r * r * r) * jnp.mean(x * w, axis=-1, keepdims=True)
        return dx, _sum8(dh * x * r)

    return _rowwise(name, fn, n, _row_tile(n, 2 * d), [_t(x), _b(g), _t(dh), _t(dres)], [(F32, d, d)],
                    [(SUBLANES, d, d, True)])


def _swap_halves(x):
    lane = lax.broadcasted_iota(jnp.int32, x.shape, 1)
    return jnp.where(lane % 64 < 32, pltpu.roll(x, 96, axis=1), pltpu.roll(x, 32, axis=1))


def _head_norm_rope(name, z, col0, n_heads, g, cos, sin, out_scale=None):
    n = z.shape[0]

    def fn(i, nrow, x, g, cos, sin):
        r = lax.rsqrt(jnp.mean(x * x, axis=-1, keepdims=True) + EPS)
        y = x * r * g
        y = y * cos + _swap_halves(y) * sin
        return y if out_scale is None else y * out_scale

    tm = _div(n, 1024, SUBLANES)
    return _rowwise(name, fn, n, tm, [_t(z, HEAD_DIM, col0), _b(g), _t(cos, cstride=0), _t(sin, cstride=0)],
                    [(BF16, n_heads * HEAD_DIM, HEAD_DIM)], ncol=n_heads)[0]


def _head_norm_rope_bwd(name, douts, z, col0, n_heads, g, cos, sin):
    n = z.shape[0]
    n_d = len(douts)

    def fn(i, nrow, *a):
        dy = a[0]
        for k in range(1, n_d):
            dy = dy + a[k]
        x, g, cos, sin = a[n_d:]
        dn = dy * cos + _swap_halves(dy * sin)
        r = lax.rsqrt(jnp.mean(x * x, axis=-1, keepdims=True) + EPS)
        w = dn * g
        dx = r * w - x * (r * r * r) * jnp.mean(x * w, axis=-1, keepdims=True)
        return dx, _sum8(dn * x * r)

    tm = _div(n, 1024, SUBLANES)
    ins = [_t(arr, HEAD_DIM, c0, cs) for arr, c0, cs in douts]
    ins += [_t(z, HEAD_DIM, col0), _b(g), _t(cos, cstride=0), _t(sin, cstride=0)]
    return _rowwise(name, fn, n, tm, ins, [(BF16, n_heads * HEAD_DIM, HEAD_DIM)],
                    [(SUBLANES, HEAD_DIM, HEAD_DIM, False)], ncol=n_heads)


def _halo(arr, w, col0):
    return [("p", arr, w, col0, CONV_HALO), _t(arr, w, col0), ("n", arr, w, col0, CONV_HALO)]


def _extend(i, nrow, prev, cur, nxt):
    prev = jnp.where(i > 0, prev, 0.0)
    nxt = jnp.where(i < nrow - 1, nxt, 0.0)
    return jnp.concatenate([prev, cur, nxt], axis=0)


def _shifted(ext, offset, tm):
    n = ext.shape[0]
    rolled = ext if offset == 0 else pltpu.roll(ext, (-offset) % n, axis=0)
    return rolled[CONV_HALO:CONV_HALO + tm]


def _conv_fwd(name, z, cw, w_dw, ln_g, ln_b):
    n = z.shape[0]
    tm = _row_tile(n, 4 * cw, CONV_HALO)

    def fn(i, nrow, ap, a, an, bp, b, bn, w, g, beta):
        a = a[...]
        ext = _extend(i, nrow, ap[...] * _sigmoid(bp[...]), a * _sigmoid(b[...]), an[...] * _sigmoid(bn[...]))
        conv = jnp.zeros_like(a)
        for k in range(CONV_KERNEL):
            conv = conv + _shifted(ext, k - CONV_KERNEL // 2, tm) * w[k:k + 1, :]
        xc = conv - jnp.mean(conv, axis=-1, keepdims=True)
        ln = xc * lax.rsqrt(jnp.mean(xc * xc, axis=-1, keepdims=True) + EPS) * g[...] + beta[...]
        return conv, ln * _sigmoid(ln)

    ins = _halo(z, cw, 0) + _halo(z, cw, 1) + [_b(w_dw), _b(ln_g), _b(ln_b)]
    return _rowwise(name, fn, n, tm, ins, [(F32, cw, cw), (BF16, cw, cw)], refs_to_fn=True)


def _conv_ln_bwd(name, ds, conv, ln_g, ln_b):
    n, cw = conv.shape

    def fn(i, nrow, ds, conv, g, beta):
        xc = conv - jnp.mean(conv, axis=-1, keepdims=True)
        rstd = lax.rsqrt(jnp.mean(xc * xc, axis=-1, keepdims=True) + EPS)
        xhat = xc * rstd
        ln = xhat * g + beta
        sg = _sigmoid(ln)
        dln = ds * (sg * (1.0 + ln * (1.0 - sg)))
        dxh = dln * g
        dconv = rstd * (dxh - jnp.mean(dxh, axis=-1, keepdims=True)
                        - xhat * jnp.mean(dxh * xhat, axis=-1, keepdims=True))
        return dconv, _sum8(dln * xhat), _sum8(dln)

    return _rowwise(name, fn, n, _row_tile(n, 2 * cw), [_t(ds), _t(conv), _b(ln_g), _b(ln_b)], [(F32, cw, cw)],
                    [(SUBLANES, cw, cw, True), (SUBLANES, cw, cw, True)])


def _conv_glu_bwd(name, dconv, z, cw, w_dw):
    n = z.shape[0]
    tm = _row_tile(n, 4 * cw, CONV_HALO)

    def fn(i, nrow, dp, d, dn, ap, a, an, bp, b, bn, w):
        d, a = d[...], a[...]
        dext = _extend(i, nrow, dp[...], d, dn[...])
        sg = _sigmoid(b[...])
        uext = _extend(i, nrow, ap[...] * _sigmoid(bp[...]), a * sg, an[...] * _sigmoid(bn[...]))
        tap = lax.broadcasted_iota(jnp.int32, (CONV_TAPS_PAD, cw), 0)
        du = jnp.zeros_like(d)
        dw = jnp.zeros((CONV_TAPS_PAD, cw), F32)
        for k in range(CONV_KERNEL):
            off = k - CONV_KERNEL // 2
            du = du + _shifted(dext, -off, tm) * w[k:k + 1, :]
            dw_k = jnp.sum(d * _shifted(uext, off, tm), axis=0, keepdims=True)
            dw = dw + jnp.where(tap == k, dw_k, 0.0)
        return du * sg, du * a * sg * (1.0 - sg), dw

    ins = _halo(dconv, cw, 0) + _halo(z, cw, 0) + _halo(z, cw, 1) + [_b(w_dw)]
    return _rowwise(name, fn, n, tm, ins, [(BF16, cw, cw), (BF16, cw, cw)], [(CONV_TAPS_PAD, cw, cw, True)],
                    refs_to_fn=True)


def _merge_fwd(name, z, d, y_c, y_a):
    n, hw = z.shape[0], d // 2

    def fn(i, nrow, gc, ga, yc, ya):
        return _sigmoid(gc) * yc + _sigmoid(ga) * ya

    ins = [_t(z, hw, 5), _t(z, hw, 7), _t(y_c, hw), _t(y_a, hw)]
    return _rowwise(name, fn, n, _row_tile(n, 2 * hw), ins, [(BF16, d, hw)], ncol=2)[0]


def _merge_bwd(name, dm, z, d, y_c, y_a):
    n, hw = z.shape[0], d // 2

    def fn(i, nrow, dm, gc, ga, yc, ya):
        sc, sa = _sigmoid(gc), _sigmoid(ga)
        return dm * yc * sc * (1.0 - sc), dm * ya * sa * (1.0 - sa), dm * sc, dm * sa

    ins = [_t(dm, hw), _t(z, hw, 5), _t(z, hw, 7), _t(y_c, hw), _t(y_a, hw)]
    return _rowwise(name, fn, n, _row_tile(n, 2 * hw), ins, [(BF16, d, hw)] * 4, ncol=2)


def _relu2(name, f):
    n, w = f.shape
    tw = _div(w, 2048, LANES)

    def fn(i, nrow, f):
        r = jnp.maximum(f, 0.0)
        return r * r

    return _rowwise(name, fn, n, _row_tile(n, tw), [_t(f, tw)], [(BF16, w, tw)], ncol=w // tw)[0]


def _relu2_bwd(name, da, f):
    n, w = f.shape
    tw = _div(w, 2048, LANES)

    def fn(i, nrow, da, f):
        return da * (2.0 * jnp.maximum(f, 0.0))

    return _rowwise(name, fn, n, _row_tile(n, tw), [_t(da, tw), _t(f, tw)], [(BF16, w, tw)], ncol=w // tw)[0]


def _head_loss(name, x2, pre, pp, target, g):
    n, d = x2.shape

    def fn(i, nrow, x2, pre, pp, t, g):
        x3 = x2 + _sigmoid(pre) * pp
        r = lax.rsqrt(jnp.mean(x3 * x3, axis=-1, keepdims=True) + EPS)
        xn = x3 * r
        e = xn * g - t
        part = 0.5 * jnp.sum(jnp.mean(e * e, axis=-1, keepdims=True), axis=0, keepdims=True)
        dy = e * (1.0 / d)
        w = dy * g
        dx3 = r * w - x3 * (r * r * r) * jnp.mean(x3 * w, axis=-1, keepdims=True)
        return dx3, _sum8(dy * xn), jnp.broadcast_to(part, (SUBLANES, LANES))

    ins = [_t(x2), _t(pre), _t(pp), _t(target), _b(g)]
    return _rowwise(name, fn, n, _row_tile(n, 2 * d), ins, [(F32, d, d)],
                    [(SUBLANES, d, d, True), (SUBLANES, LANES, LANES, True)])


def _ple_bwd(name, dx3, pre, pp):
    n, d = dx3.shape

    def fn(i, nrow, dx3, pre, pp):
        gate = _sigmoid(pre)
        return dx3 * gate, dx3 * pp * gate * (1.0 - gate)

    return _rowwise(name, fn, n, _row_tile(n, 2 * d), [_t(dx3), _t(pre), _t(pp)], [(BF16, d, d)] * 2)


def _group_sum(name, dvh, n_kv):
    n = dvh.shape[0]

    def fn(i, nrow, *a):
        s = a[0]
        for k in range(1, GROUP):
            s = s + a[k]
        return s

    ins = [_t(dvh, HEAD_DIM, k, GROUP) for k in range(GROUP)]
    return _rowwise(name, fn, n, _div(n, 1024, SUBLANES), ins, [(BF16, n_kv * HEAD_DIM, HEAD_DIM)], ncol=n_kv)[0]


MM_ROWS = 1024
MM_COLS = 1152
MM_REDUCE = 2048
MM_REDUCE_BLOCKS = 2304


def _accumulate(step, n_steps, part, acc, finish):
    if n_steps == 1:
        finish(part)
        return

    @pl.when(step == 0)
    def _():
        acc[...] = part

    @pl.when(jnp.logical_and(step > 0, step < n_steps - 1))
    def _():
        acc[...] += part

    @pl.when(step == n_steps - 1)
    def _():
        finish(acc[...] + part)


def _mm_nn(name, a, w3, out_dtype, residual=None):
    m, k = a.shape
    nb, _, ns = w3.shape
    tm, tn, tk = _div(m, MM_ROWS, SUBLANES), _div(ns, MM_COLS, LANES), _div(k, MM_REDUCE, LANES)
    per, nk = ns // tn, k // tk
    has_res = residual is not None

    def body(*refs):
        a_ref, w_ref = refs[:2]
        o_ref, acc = refs[-2:]
        part = jnp.dot(a_ref[...].astype(BF16), w_ref[...].astype(BF16), preferred_element_type=F32)

        def finish(total):
            if has_res:
                total = total + refs[2][...]
            o_ref[...] = total.astype(out_dtype)

        _accumulate(pl.program_id(2), nk, part, acc, finish)

    in_specs = [pl.BlockSpec((tm, tk), lambda i, n, kk: (i, kk)),
                pl.BlockSpec((None, tk, tn), lambda i, n, kk: (n // per, kk, n % per))]
    args = [a, w3]
    if has_res:
        in_specs.append(pl.BlockSpec((tm, tn), lambda i, n, kk: (i, n)))
        args.append(residual)
    return pl.pallas_call(
        body, name=name, grid=(m // tm, nb * per, nk), in_specs=in_specs,
        out_specs=pl.BlockSpec((tm, tn), lambda i, n, kk: (i, n)),
        out_shape=jax.ShapeDtypeStruct((m, nb * ns), out_dtype),
        scratch_shapes=[pltpu.VMEM((tm, tn) if nk > 1 else (SUBLANES, LANES), F32)],
        compiler_params=_params(("parallel", "parallel", "arbitrary")))(*args)


def _mm_nt(name, dc, w3, out_dtype):
    m = dc.shape[0]
    nb, k, ns = w3.shape
    tm, to = _div(m, MM_ROWS, SUBLANES), _div(k, MM_ROWS, LANES)
    if ns > MM_REDUCE_BLOCKS // 2:
        jb, tr = 1, _div(ns, MM_REDUCE, LANES)
    else:
        jb, tr = max(j for j in range(1, nb + 1) if nb % j == 0 and j * ns <= MM_REDUCE_BLOCKS), ns
    per = ns // tr
    nr = (nb // jb) * per

    def body(dc_ref, w_ref, o_ref, acc):
        part = None
        for j in range(jb):
            term = lax.dot_general(dc_ref[:, j * tr:(j + 1) * tr].astype(BF16), w_ref[j].astype(BF16),
                                   (((1,), (1,)), ((), ())), preferred_element_type=F32)
            part = term if part is None else part + term

        def finish(total):
            o_ref[...] = total.astype(out_dtype)

        _accumulate(pl.program_id(2), nr, part, acc, finish)

    return pl.pallas_call(
        body, name=name, grid=(m // tm, k // to, nr),
        in_specs=[pl.BlockSpec((tm, jb * tr), lambda i, o, r: (i, r)),
                  pl.BlockSpec((jb, to, tr), lambda i, o, r: (r // per, o, r % per))],
        out_specs=pl.BlockSpec((tm, to), lambda i, o, r: (i, o)),
        out_shape=jax.ShapeDtypeStruct((m, k), out_dtype),
        scratch_shapes=[pltpu.VMEM((tm, to) if nr > 1 else (SUBLANES, LANES), F32)],
        compiler_params=_params(("parallel", "parallel", "arbitrary")))(dc, w3)


def _mm_tn(name, a, dc, nb):
    m, k = a.shape
    ns = dc.shape[1] // nb
    tr, tn, to = _div(m, MM_REDUCE, LANES), _div(ns, MM_COLS, LANES), _div(k, MM_ROWS, LANES)
    per, nr = ns // tn, m // tr

    def body(a_ref, dc_ref, o_ref, acc):
        part = lax.dot_general(a_ref[...].astype(BF16), dc_ref[...].astype(BF16), (((0,), (0,)), ((), ())),
                               preferred_element_type=F32)

        def finish(total):
            o_ref[...] = total.astype(BF16)

        _accumulate(pl.program_id(2), nr, part, acc, finish)

    return pl.pallas_call(
        body, name=name, grid=(k // to, nb * per, nr),
        in_specs=[pl.BlockSpec((tr, to), lambda o, n, r: (r, o)),
                  pl.BlockSpec((tr, tn), lambda o, n, r: (r, n))],
        out_specs=pl.BlockSpec((None, to, tn), lambda o, n, r: (n // per, o, n % per)),
        out_shape=jax.ShapeDtypeStruct((nb, k, ns), BF16),
        scratch_shapes=[pltpu.VMEM((to, tn) if nr > 1 else (SUBLANES, LANES), F32)],
        compiler_params=_params(("parallel", "parallel", "arbitrary")))(a, dc)


QK_SCALE = HEAD_DIM ** -0.5
LOG2_E = 1.4426950408889634
LN_2 = 0.6931471805599453


def _columns(row, n):
    return jnp.transpose(jnp.broadcast_to(row, (LANES, n)))


def _flash_fwd(qr, kr, z, v_col0, n_heads):
    t = qr.shape[0]
    tq, tk = _div(t, 1024, LANES), _div(t, 1024, LANES)
    nk = t // tk

    def body(q_ref, k_ref, v_ref, o_ref, lse_ref, m_scr, l_scr, acc_scr):
        kb = pl.program_id(2)

        @pl.when(kb == 0)
        def _():
            m_scr[...] = jnp.full_like(m_scr, -1e30)
            l_scr[...] = jnp.zeros_like(l_scr)
            acc_scr[...] = jnp.zeros_like(acc_scr)

        s_t = lax.dot_general(k_ref[...], q_ref[...], (((1,), (1,)), ((), ())), preferred_element_type=F32)
        m_prev = m_scr[...]
        m_new = jnp.maximum(m_prev, jnp.max(s_t, axis=0, keepdims=True))
        alpha = jnp.exp2(m_prev - m_new)
        p_t = jnp.exp2(s_t - m_new)
        l_scr[...] = alpha * l_scr[...] + jnp.sum(p_t, axis=0, keepdims=True)
        pv = lax.dot_general(p_t.astype(BF16), v_ref[...].astype(BF16), (((0,), (0,)), ((), ())),
                             preferred_element_type=F32)
        acc_scr[...] = _columns(alpha, tq) * acc_scr[...] + pv
        m_scr[...] = m_new

        @pl.when(kb == nk - 1)
        def _():
            l = l_scr[...]
            o_ref[...] = acc_scr[...] / _columns(l, tq)
            lse_ref[...] = jnp.broadcast_to(m_scr[...] + jnp.log(l) * LOG2_E, (SUBLANES, tq))

    return pl.pallas_call(
        body, name="flash_fwd", grid=(n_heads, t // tq, nk),
        in_specs=[pl.BlockSpec((tq, HEAD_DIM), lambda h, qi, ki: (qi, h)),
                  pl.BlockSpec((tk, HEAD_DIM), lambda h, qi, ki: (ki, h // GROUP)),
                  pl.BlockSpec((tk, HEAD_DIM), lambda h, qi, ki: (ki, v_col0 + h // GROUP))],
        out_specs=[pl.BlockSpec((tq, HEAD_DIM), lambda h, qi, ki: (qi, h)),
                   pl.BlockSpec((None, SUBLANES, tq), lambda h, qi, ki: (h, 0, qi))],
        out_shape=[jax.ShapeDtypeStruct((t, n_heads * HEAD_DIM), F32),
                   jax.ShapeDtypeStruct((n_heads, SUBLANES, t), F32)],
        scratch_shapes=[pltpu.VMEM((1, tq), F32), pltpu.VMEM((1, tq), F32), pltpu.VMEM((tq, HEAD_DIM), F32)],
        compiler_params=_params(("parallel", "parallel", "arbitrary")))(qr, kr, z)


def _attn_delta(do, o, n_heads):
    t = o.shape[0]
    tm = _div(t, 1024, LANES)

    def body(do_ref, o_ref, d_ref):
        prod = do_ref[...].astype(F32) * o_ref[...]
        row = jnp.sum(jnp.transpose(prod), axis=0, keepdims=True)
        d_ref[...] = jnp.broadcast_to(row, (SUBLANES, tm))

    return pl.pallas_call(
        body, name="attn_delta", grid=(n_heads, t // tm),
        in_specs=[pl.BlockSpec((tm, HEAD_DIM), lambda h, i: (i, h)),
                  pl.BlockSpec((tm, HEAD_DIM), lambda h, i: (i, h))],
        out_specs=pl.BlockSpec((None, SUBLANES, tm), lambda h, i: (h, 0, i)),
        out_shape=jax.ShapeDtypeStruct((n_heads, SUBLANES, t), F32),
        compiler_params=_params(("parallel", "parallel")))(do, o)


def _flash_bwd(qr, kr, z, v_col0, do, lse, delta, n_heads):
    t = qr.shape[0]
    tq, tk = _div(t, 1024, LANES), _div(t, 1024, LANES)
    nq = t // tq

    def body(q_ref, k_ref, v_ref, do_ref, lse_ref, dl_ref, dq_ref, dk_ref, dv_ref, dk_acc, dv_acc):
        kb, qb = pl.program_id(1), pl.program_id(2)
        q, k, do_ = q_ref[...], k_ref[...], do_ref[...]
        v = v_ref[...].astype(BF16)
        s_t = lax.dot_general(k, q, (((1,), (1,)), ((), ())), preferred_element_type=F32)
        p_t = jnp.exp2(s_t - lse_ref[0:1, :])
        dp_t = lax.dot_general(v, do_, (((1,), (1,)), ((), ())), preferred_element_type=F32)
        ds_t = (p_t * (dp_t - dl_ref[0:1, :])).astype(BF16)
        dv_c = jnp.dot(p_t.astype(BF16), do_, preferred_element_type=F32)
        dk_c = jnp.dot(ds_t, q, preferred_element_type=F32) * LN_2
        dq_c = lax.dot_general(ds_t, k, (((0,), (0,)), ((), ())), preferred_element_type=F32) * QK_SCALE

        @pl.when(qb == 0)
        def _():
            dk_acc[...] = dk_c
            dv_acc[...] = dv_c

        @pl.when(qb > 0)
        def _():
            dk_acc[...] += dk_c
            dv_acc[...] += dv_c

        @pl.when(qb == nq - 1)
        def _():
            dk_ref[...] = dk_acc[...]
            dv_ref[...] = dv_acc[...]

        rows = pl.ds(pl.multiple_of(qb * tq, tq), tq)

        @pl.when(kb == 0)
        def _():
            dq_ref[rows, :] = dq_c

        @pl.when(kb > 0)
        def _():
            dq_ref[rows, :] += dq_c

    wide = jax.ShapeDtypeStruct((t, n_heads * HEAD_DIM), F32)
    return pl.pallas_call(
        body, name="flash_bwd", grid=(n_heads, t // tk, nq),
        in_specs=[pl.BlockSpec((tq, HEAD_DIM), lambda h, kb, qb: (qb, h)),
                  pl.BlockSpec((tk, HEAD_DIM), lambda h, kb, qb: (kb, h // GROUP)),
                  pl.BlockSpec((tk, HEAD_DIM), lambda h, kb, qb: (kb, v_col0 + h // GROUP)),
                  pl.BlockSpec((tq, HEAD_DIM), lambda h, kb, qb: (qb, h)),
                  pl.BlockSpec((None, SUBLANES, tq), lambda h, kb, qb: (h, 0, qb)),
                  pl.BlockSpec((None, SUBLANES, tq), lambda h, kb, qb: (h, 0, qb))],
        out_specs=[pl.BlockSpec((t, HEAD_DIM), lambda h, kb, qb: (0, h)),
                   pl.BlockSpec((tk, HEAD_DIM), lambda h, kb, qb: (kb, h)),
                   pl.BlockSpec((tk, HEAD_DIM), lambda h, kb, qb: (kb, h))],
        out_shape=[wide, wide, wide],
        scratch_shapes=[pltpu.VMEM((tk, HEAD_DIM), F32), pltpu.VMEM((tk, HEAD_DIM), F32)],
        compiler_params=_params(("parallel", "arbitrary", "arbitrary")))(qr, kr, z, do, lse, delta)


def _position():
    x, y, c = lax.axis_index("x"), lax.axis_index("y"), lax.axis_index("c")
    return x, y, c


def _index(x, y, c):
    return 4 * x + 2 * y + c


def _all_gather(name, shard):
    a, b = shard.shape

    def body(x_ref, out_ref, send_sems, recv_sems, local_sem):
        x, y, c = _position()
        me, sibling = (x, y, c), (x, y, 1 - c)
        chips = [(1 - x, y), (x, 1 - y), (1 - x, 1 - y)]

        def block(px, py, pc):
            return out_ref.at[_index(px, py, pc)]

        def copy(k, blk, to, src=None):
            return pltpu.make_async_remote_copy(
                src_ref=block(*blk) if src is None else src, dst_ref=block(*blk),
                send_sem=send_sems.at[k], recv_sem=recv_sems.at[k], device_id=to, device_id_type=MESH)

        mine = pltpu.make_async_copy(x_ref, block(*me), local_sem)
        mine.start()
        first = [copy(0, me, sibling, src=x_ref)]
        first += [copy(1 + j, me, (*chip, c), src=x_ref) for j, chip in enumerate(chips)]
        for cp in first:
            cp.start()
        passed = [copy(4 + j, (*chip, c), sibling) for j, chip in enumerate(chips)]
        for j, chip in enumerate(chips):
            copy(1 + j, (*chip, c), me).wait_recv()
            passed[j].start()
        copy(0, sibling, me).wait_recv()
        for j, chip in enumerate(chips):
            copy(4 + j, (*chip, 1 - c), me).wait_recv()
        for cp in first + passed:
            cp.wait_send()
        mine.wait()

    return pl.pallas_call(
        body, name=name, out_shape=jax.ShapeDtypeStruct((N_DEV, a, b), shard.dtype),
        in_specs=[pl.BlockSpec(memory_space=pltpu.HBM)], out_specs=pl.BlockSpec(memory_space=pltpu.HBM),
        scratch_shapes=[pltpu.SemaphoreType.DMA((7,)), pltpu.SemaphoreType.DMA((7,)), pltpu.SemaphoreType.DMA],
    )(shard)


def _exchange_blocks(name, g3):
    _, a, b = g3.shape
    flips = [(fx, fy, fc) for fx in (0, 1) for fy in (0, 1) for fc in (0, 1)][1:]

    def body(g_ref, land_ref, send_sems, recv_sems, local_sem):
        x, y, c = _position()
        me = _index(x, y, c)
        own = pltpu.make_async_copy(g_ref.at[me], land_ref.at[me], local_sem)
        own.start()
        peers = [((1 - x) if fx else x, (1 - y) if fy else y, (1 - c) if fc else c) for fx, fy, fc in flips]

        def copy(k, src_block, dst_block):
            return pltpu.make_async_remote_copy(
                src_ref=g_ref.at[src_block], dst_ref=land_ref.at[dst_block],
                send_sem=send_sems.at[k], recv_sem=recv_sems.at[k], device_id=peers[k], device_id_type=MESH)

        sends = [copy(k, _index(*peer), me) for k, peer in enumerate(peers)]
        for cp in sends:
            cp.start()
        for k, peer in enumerate(peers):
            copy(k, me, _index(*peer)).wait_recv()
        for cp in sends:
            cp.wait_send()
        own.wait()

    return pl.pallas_call(
        body, name=name, out_shape=jax.ShapeDtypeStruct(g3.shape, g3.dtype),
        in_specs=[pl.BlockSpec(memory_space=pltpu.HBM)], out_specs=pl.BlockSpec(memory_space=pltpu.HBM),
        scratch_shapes=[pltpu.SemaphoreType.DMA((7,)), pltpu.SemaphoreType.DMA((7,)), pltpu.SemaphoreType.DMA],
    )(g3)


def _adamw(w, g, m, v):
    m = ADAM_B1 * m + (1.0 - ADAM_B1) * g
    v = ADAM_B2 * v + (1.0 - ADAM_B2) * (g * g)
    m_hat = m / (1.0 - ADAM_B1 ** ADAM_STEP)
    v_hat = v / (1.0 - ADAM_B2 ** ADAM_STEP)
    delta = -ADAM_LR * (m_hat / (jnp.sqrt(v_hat) + ADAM_EPS) + ADAM_WD * w)
    return delta, m, v


def _adamw_shard(name, land, w, m, v):
    _, a, b = land.shape
    tm = _div(a, max(16, ROW_TILE_ELEMS // (2 * b)), 16)
    per = a // tm

    def fn(i, nrow, *t):
        g = t[0].astype(F32)
        for s in range(1, N_DEV):
            g = g + t[s].astype(F32)
        w, m, v = t[N_DEV:]
        return (g,) + _adamw(w, g, m, v)

    land2 = land.reshape(N_DEV * a, b)
    ins = [_t(land2, row0=s * per) for s in range(N_DEV)] + [_t(w), _t(m), _t(v)]
    return _rowwise(name, fn, a, tm, ins, [(F32, b, b)] * 4)


def _adamw_small(name, parts, w, m, v):
    rows, n = w.shape

    def body(p_ref, w_ref, m_ref, v_ref, g_out, d_out, m_out, v_out):
        g = p_ref[0]
        for s in range(1, parts.shape[0]):
            g = g + p_ref[s]
        g = jnp.sum(g, axis=0, keepdims=True) if rows == 1 else g[0:rows, :]
        delta, m_new, v_new = _adamw(w_ref[...], g, m_ref[...], v_ref[...])
        g_out[...] = g
        d_out[...] = delta
        m_out[...] = m_new
        v_out[...] = v_new

    return pl.pallas_call(body, name=name, out_shape=[jax.ShapeDtypeStruct((rows, n), F32)] * 4)(parts, w, m, v)


def _rope_tables(t):
    pos = jnp.arange(t, dtype=jnp.int32)
    half = HEAD_DIM // 4
    inv_freq = ROPE_THETA ** (-jnp.arange(0, 2 * half, 2, dtype=F32) / (2 * half))
    ang_r = (pos // GRID_W).astype(F32)[:, None] * inv_freq[None, :]
    ang_c = (pos % GRID_W).astype(F32)[:, None] * inv_freq[None, :]
    cos = jnp.concatenate([jnp.cos(ang_r)] * 2 + [jnp.cos(ang_c)] * 2, axis=-1)
    sin = jnp.concatenate([-jnp.sin(ang_r), jnp.sin(ang_r), -jnp.sin(ang_c), jnp.sin(ang_c)], axis=-1)
    return cos, sin


def _gather_weight(name, w, cols):
    g = _all_gather(name, w[0].astype(BF16))
    return g if cols else g.reshape(1, N_DEV * g.shape[1], g.shape[2])


def kernel(x, p, norm_mix, w_in, w_dw, conv_ln_g, conv_ln_b, w_conv_proj, q_norm, k_norm, w_attn_proj, w_out, norm_ffn, w_ff1, w_ff2, norm_ple, w_ple_gate, w_ple_proj, norm_final, loss_target, m_norm_mix, m_w_in, m_w_dw, m_conv_ln_g, m_conv_ln_b, m_w_conv_proj, m_q_norm, m_k_norm, m_w_attn_proj, m_w_out, m_norm_ffn, m_w_ff1, m_w_ff2, m_norm_ple, m_w_ple_gate, m_w_ple_proj, m_norm_final, v_norm_mix, v_w_in, v_w_dw, v_conv_ln_g, v_conv_ln_b, v_w_conv_proj, v_q_norm, v_k_norm, v_w_attn_proj, v_w_out, v_norm_ffn, v_w_ff1, v_w_ff2, v_norm_ple, v_w_ple_gate, v_w_ple_proj, v_norm_final):
    t, d = x.shape[1], x.shape[2]
    cw = d // 2
    n_heads = d // HEAD_DIM
    n_kv = n_heads // GROUP
    col_q, col_k, col_v = d // HEAD_DIM, 2 * d // HEAD_DIM, (2 * d + d // 4) // HEAD_DIM
    x0, pe, tgt = x[0], p[0, 0], loss_target[0]
    g_final = norm_final.reshape(1, d)
    me = _index(*_position())

    wg_in = _gather_weight("ag_w_in", w_in, True)
    wg_cp = _gather_weight("ag_w_conv_proj", w_conv_proj, True)
    wg_ap = _gather_weight("ag_w_attn_proj", w_attn_proj, False)
    wg_out = _gather_weight("ag_w_out", w_out, False)
    wg_ff1 = _gather_weight("ag_w_ff1", w_ff1, True)
    wg_ff2 = _gather_weight("ag_w_ff2", w_ff2, False)
    wg_pg = _gather_weight("ag_w_ple_gate", w_ple_gate, False)
    wg_pp = _gather_weight("ag_w_ple_proj", w_ple_proj, True)
    dw_pad = jnp.pad(w_dw[0], ((0, CONV_TAPS_PAD - CONV_KERNEL), (0, 0)))
    dw_all = _all_gather("ag_w_dw", dw_pad)
    dw_full = dw_all.transpose(1, 0, 2).reshape(CONV_TAPS_PAD, cw)
    cos, sin = _rope_tables(t)

    h1 = _rms_fwd("rms_mix", x0, norm_mix)
    z = _mm_nn("mm_in", h1, wg_in, F32)
    conv, s_c = _conv_fwd("conv_fwd", z, cw, dw_full, conv_ln_g, conv_ln_b)
    y_c = _mm_nn("mm_conv_proj", s_c, wg_cp, F32)
    qr = _head_norm_rope("q_prep", z, col_q, n_heads, q_norm, cos, sin, out_scale=QK_SCALE * LOG2_E)
    kr = _head_norm_rope("k_prep", z, col_k, n_kv, k_norm, cos, sin)
    o, lse = _flash_fwd(qr, kr, z, col_v, n_heads)
    y_a = _mm_nn("mm_attn_proj", o, wg_ap, F32)
    merged = _merge_fwd("merge_fwd", z, d, y_c, y_a)
    x1 = _mm_nn("mm_out", merged, wg_out, F32, residual=x0)
    h2 = _rms_fwd("rms_ffn", x1, norm_ffn)
    f = _mm_nn("mm_ff1", h2, wg_ff1, F32)
    act = _relu2("relu2", f)
    x2 = _mm_nn("mm_ff2", act, wg_ff2, F32, residual=x1)
    hp = _rms_fwd("rms_ple", x2, norm_ple)
    pre = _mm_nn("mm_ple_gate", hp, wg_pg, F32)
    pp = _mm_nn("mm_ple_proj", pe, wg_pp, F32)

    dx3, dg_final, loss_part = _head_loss("head_loss", x2, pre, pp, tgt, g_final)
    loss = lax.psum(loss_part[0, 0], ("x", "y", "c"))
    dpp, dpre = _ple_bwd("ple_bwd", dx3, pre, pp)
    gw_pp = _mm_tn("mm_d_ple_proj", pe, dpp, N_DEV)
    gw_pg = _mm_tn("mm_d_ple_gate", hp, dpre, 1)
    dhp = _mm_nt("mm_dhp", dpre, wg_pg, F32)
    dx2, dg_ple = _rms_bwd("rms_ple_bwd", x2, norm_ple, dhp, dx3)
    gw_ff2 = _mm_tn("mm_d_ff2", act, dx2, 1)
    dact = _mm_nt("mm_dact", dx2, wg_ff2, F32)
    df = _relu2_bwd("relu2_bwd", dact, f)
    gw_ff1 = _mm_tn("mm_d_ff1", h2, df, N_DEV)
    dh2 = _mm_nt("mm_dh2", df, wg_ff1, F32)
    dx1, dg_ffn = _rms_bwd("rms_ffn_bwd", x1, norm_ffn, dh2, dx2)
    gw_out = _mm_tn("mm_d_out", merged, dx1, 1)
    dmerged = _mm_nt("mm_dmerged", dx1, wg_out, F32)
    dgc, dga, dyc, dya = _merge_bwd("merge_bwd", dmerged, z, d, y_c, y_a)
    gw_cp = _mm_tn("mm_d_conv_proj", s_c, dyc, N_DEV)
    ds_c = _mm_nt("mm_ds_c", dyc, wg_cp, F32)
    dconv, dg_ln, db_ln = _conv_ln_bwd("conv_ln_bwd", ds_c, conv, conv_ln_g, conv_ln_b)
    dca, dcb, dg_dw = _conv_glu_bwd("conv_glu_bwd", dconv, z, cw, dw_full)
    gw_ap = _mm_tn("mm_d_attn_proj", o, dya, 1)
    do = _mm_nt("mm_do", dya, wg_ap, BF16)
    delta = _attn_delta(do, o, n_heads)
    dqr, dkh, dvh = _flash_bwd(qr, kr, z, col_v, do, lse, delta, n_heads)
    dq, dg_q = _head_norm_rope_bwd("q_prep_bwd", [(dqr, 0, 1)], z, col_q, n_heads, q_norm, cos, sin)
    dk, dg_k = _head_norm_rope_bwd("k_prep_bwd", [(dkh, g, GROUP) for g in range(GROUP)], z, col_k, n_kv, k_norm,
                                   cos, sin)
    dv = _group_sum("dv_sum", dvh, n_kv)
    dz = jnp.concatenate([dca, dcb, dq, dk, dv, dgc, dga], axis=1)
    gw_in = _mm_tn("mm_d_in", h1, dz, N_DEV)
    dh1 = _mm_nt("mm_dh1", dz, wg_in, F32)
    grad_x, dg_mix = _rms_bwd("rms_mix_bwd", x0, norm_mix, dh1, dx1)

    def pad(a):
        return jnp.pad(a, ((0, 0), (0, d - a.shape[1])))

    small = [dg_mix, dg_ffn, dg_ple, dg_final, dg_ln, db_ln, dg_q, dg_k, dg_dw]
    packed = _all_gather("ag_small", jnp.concatenate([pad(a) for a in small], axis=0))
    parts, row = [], 0
    for a in small:
        parts.append(packed[:, row:row + a.shape[0], :a.shape[1]])
        row += a.shape[0]
    p_mix, p_ffn, p_ple, p_final, p_lng, p_lnb, p_q, p_k, p_dw = parts
    p_dw = lax.dynamic_slice_in_dim(p_dw, me * (cw // N_DEV), cw // N_DEV, axis=2)

    def big(name, gw, w, m, v):
        blocks = gw if gw.shape[0] == N_DEV else gw.reshape(N_DEV, gw.shape[1] // N_DEV, gw.shape[2])
        land = _exchange_blocks("rs_" + name, blocks)
        return [r.reshape(w.shape) for r in _adamw_shard("adamw_" + name, land, w[0], m[0], v[0])]

    def little(name, part, w, m, v):
        rows, n = (1 if w.ndim < 3 else w.shape[1]), w.shape[-1]
        padded = rows if rows == 1 else part.shape[1]

        def two_d(a):
            return jnp.pad(a.reshape(rows, n), ((0, padded - rows), (0, 0)))

        res = _adamw_small("adamw_" + name, part, two_d(w), two_d(m), two_d(v))
        return [r[:rows].reshape(w.shape) for r in res]

    results = [
        little("norm_mix", p_mix, norm_mix, m_norm_mix, v_norm_mix),
        big("w_in", gw_in, w_in, m_w_in, v_w_in),
        little("w_dw", p_dw, w_dw, m_w_dw, v_w_dw),
        little("conv_ln_g", p_lng, conv_ln_g, m_conv_ln_g, v_conv_ln_g),
        little("conv_ln_b", p_lnb, conv_ln_b, m_conv_ln_b, v_conv_ln_b),
        big("w_conv_proj", gw_cp, w_conv_proj, m_w_conv_proj, v_w_conv_proj),
        little("q_norm", p_q, q_norm, m_q_norm, v_q_norm),
        little("k_norm", p_k, k_norm, m_k_norm, v_k_norm),
        big("w_attn_proj", gw_ap, w_attn_proj, m_w_attn_proj, v_w_attn_proj),
        big("w_out", gw_out, w_out, m_w_out, v_w_out),
        little("norm_ffn", p_ffn, norm_ffn, m_norm_ffn, v_norm_ffn),
        big("w_ff1", gw_ff1, w_ff1, m_w_ff1, v_w_ff1),
        big("w_ff2", gw_ff2, w_ff2, m_w_ff2, v_w_ff2),
        little("norm_ple", p_ple, norm_ple, m_norm_ple, v_norm_ple),
        big("w_ple_gate", gw_pg, w_ple_gate, m_w_ple_gate, v_w_ple_gate),
        big("w_ple_proj", gw_pp, w_ple_proj, m_w_ple_proj, v_w_ple_proj),
        little("norm_final", p_final, norm_final, m_norm_final, v_norm_final),
    ]
    grads, deltas, new_m, new_v = zip(*results)
    return (loss, grad_x[None], *grads, *deltas, *new_m, *new_v)
```

```python
import functools

import jax
import jax.numpy as jnp
from jax import lax
from jax.experimental import pallas as pl
from jax.experimental.pallas import tpu as pltpu

F32 = jnp.float32
BF16 = jnp.bfloat16

EPS = 1e-6
HEAD_DIM = 128
GROUP = 4
CONV_KERNEL = 31
CONV_HALO = 16
CONV_TAPS_PAD = 32
GRID_W = 64
ROPE_THETA = 10000.0
ADAM_LR, ADAM_B1, ADAM_B2, ADAM_EPS, ADAM_WD, ADAM_STEP = 0.001, 0.9, 0.999, 1e-08, 0.01, 10

N_DEV = 8
MESH = pl.DeviceIdType.MESH
SUBLANES = 8
LANES = 128
VMEM_LIMIT = 48 * 1024 * 1024
ROW_TILE_ELEMS = 512 * 1024


def _div(n, pref, mult=1):
    if n <= pref:
        return n
    for d in range(pref, 0, -1):
        if n % d == 0 and d % mult == 0:
            return d
    return n


def _params(sem):
    return pltpu.CompilerParams(dimension_semantics=sem, vmem_limit_bytes=VMEM_LIMIT)


def _sum8(a):
    tm, w = a.shape
    return a.reshape(tm // SUBLANES, SUBLANES, w).sum(axis=0)


def _sigmoid(x):
    return jax.nn.sigmoid(x)


def _rowwise(name, fn, n_rows, tm, ins, outs, accs=(), ncol=1, refs_to_fn=False, after=()):
    nrow = n_rows // tm
    in_specs, arrays = [], []
    for spec in ins:
        kind, arr, w, c0 = spec[:4]
        if kind == "t":
            cs, r0 = spec[4], spec[5]
            in_specs.append(pl.BlockSpec((tm, w), lambda j, i, c0=c0, cs=cs, r0=r0: (r0 + i, c0 + cs * j)))
        elif kind == "p":
            h = spec[4]
            r = tm // h
            in_specs.append(pl.BlockSpec((h, w), lambda j, i, c0=c0, r=r: (jnp.maximum(i * r - 1, 0), c0 + j)))
        elif kind == "n":
            h = spec[4]
            r = tm // h
            last = n_rows // h - 1
            in_specs.append(
                pl.BlockSpec((h, w), lambda j, i, c0=c0, r=r, last=last: (jnp.minimum((i + 1) * r, last), c0 + j)))
        else:
            cs = spec[4]
            in_specs.append(pl.BlockSpec((arr.shape[0], w), lambda j, i, c0=c0, cs=cs: (0, c0 + cs * j)))
        arrays.append(arr)
    out_shape, out_specs = [], []
    for dtype, total, w in outs:
        out_shape.append(jax.ShapeDtypeStruct((n_rows, total), dtype))
        out_specs.append(pl.BlockSpec((tm, w), lambda j, i: (i, j)))
    for rows, total, w, follow in accs:
        out_shape.append(jax.ShapeDtypeStruct((rows, total), F32))
        out_specs.append(pl.BlockSpec((rows, w), (lambda j, i: (0, j)) if follow else (lambda j, i: (0, 0))))
    n_in, n_out, n_acc = len(ins), len(outs), len(accs)
    for token in after:
        in_specs.append(pl.BlockSpec(memory_space=pl.ANY))
        arrays.append(token)
    out0 = n_in + len(after)

    def body(*refs):
        j, i = pl.program_id(0), pl.program_id(1)
        res = fn(i, nrow, *(refs[:n_in] if refs_to_fn else [r[...] for r in refs[:n_in]]))
        if not isinstance(res, (tuple, list)):
            res = (res,)
        for k in range(n_out):
            refs[out0 + k][...] = res[k].astype(outs[k][0])
        for k in range(n_acc):
            ref, term = refs[out0 + n_out + k], res[n_out + k]
            first = (i == 0) if accs[k][3] else jnp.logical_and(i == 0, j == 0)

            @pl.when(first)
            def _():
                ref[...] = term

            @pl.when(jnp.logical_not(first))
            def _():
                ref[...] += term

    res = pl.pallas_call(
        body, name=name, grid=(ncol, nrow), in_specs=in_specs, out_specs=out_specs, out_shape=out_shape,
        compiler_params=_params(("arbitrary", "arbitrary")))(*arrays)
    return res


def _row_tile(n_rows, width, mult=SUBLANES):
    return _div(n_rows, max(mult, ROW_TILE_ELEMS // width), mult)


def _t(arr, w=None, col0=0, cstride=1, row0=0):
    return ("t", arr, arr.shape[1] if w is None else w, col0, cstride, row0)


def _b(arr, w=None, col0=0, cstride=0):
    return ("b", arr, arr.shape[1] if w is None else w, col0, cstride)


def _rms_fwd(name, x, g, after=()):
    n, d = x.shape

    def fn(i, nrow, x, g):
        r = lax.rsqrt(jnp.mean(x * x, axis=-1, keepdims=True) + EPS)
        return x * r * g

    return _rowwise(name, fn, n, _row_tile(n, d), [_t(x), _b(g)], [(BF16, d, d)], after=after)[0]


def _rms_bwd(name, x, g, dh, dres):
    n, d = x.shape

    def fn(i, nrow, x, g, dh, dres):
        r = lax.rsqrt(jnp.mean(x * x, axis=-1, keepdims=True) + EPS)
        w = dh * g
        dx = dres + r * w - x * (r * r * r) * jnp.mean(x * w, axis=-1, keepdims=True)
        return dx, _sum8(dh * x * r)

    return _rowwise(name, fn, n, _row_tile(n, 2 * d), [_t(x), _b(g), _t(dh), _t(dres)], [(F32, d, d)],
                    [(SUBLANES, d, d, True)])


def _swap_halves(x):
    lane = lax.broadcasted_iota(jnp.int32, x.shape, 1)
    return jnp.where(lane % 64 < 32, pltpu.roll(x, 96, axis=1), pltpu.roll(x, 32, axis=1))


def _head_norm_rope(name, z, col0, n_heads, g, cos, sin, out_scale=None):
    n = z.shape[0]

    def fn(i, nrow, x, g, cos, sin):
        r = lax.rsqrt(jnp.mean(x * x, axis=-1, keepdims=True) + EPS)
        y = x * r * g
        y = y * cos + _swap_halves(y) * sin
        return y if out_scale is None else y * out_scale

    tm = _div(n, 1024, SUBLANES)
    return _rowwise(name, fn, n, tm, [_t(z, HEAD_DIM, col0), _b(g), _t(cos, cstride=0), _t(sin, cstride=0)],
                    [(BF16, n_heads * HEAD_DIM, HEAD_DIM)], ncol=n_heads)[0]


def _head_norm_rope_bwd(name, douts, z, col0, n_heads, g, cos, sin):
    n = z.shape[0]
    n_d = len(douts)

    def fn(i, nrow, *a):
        dy = a[0]
        for k in range(1, n_d):
            dy = dy + a[k]
        x, g, cos, sin = a[n_d:]
        dn = dy * cos + _swap_halves(dy * sin)
        r = lax.rsqrt(jnp.mean(x * x, axis=-1, keepdims=True) + EPS)
        w = dn * g
        dx = r * w - x * (r * r * r) * jnp.mean(x * w, axis=-1, keepdims=True)
        return dx, _sum8(dn * x * r)

    tm = _div(n, 1024, SUBLANES)
    ins = [_t(arr, HEAD_DIM, c0, cs) for arr, c0, cs in douts]
    ins += [_t(z, HEAD_DIM, col0), _b(g), _t(cos, cstride=0), _t(sin, cstride=0)]
    return _rowwise(name, fn, n, tm, ins, [(BF16, n_heads * HEAD_DIM, HEAD_DIM)],
                    [(SUBLANES, HEAD_DIM, HEAD_DIM, False)], ncol=n_heads)


def _halo(arr, w, col0):
    return [("p", arr, w, col0, CONV_HALO), _t(arr, w, col0), ("n", arr, w, col0, CONV_HALO)]


def _extend(i, nrow, prev, cur, nxt):
    prev = jnp.where(i > 0, prev, 0.0)
    nxt = jnp.where(i < nrow - 1, nxt, 0.0)
    return jnp.concatenate([prev, cur, nxt], axis=0)


def _shifted(ext, offset, tm):
    n = ext.shape[0]
    rolled = ext if offset == 0 else pltpu.roll(ext, (-offset) % n, axis=0)
    return rolled[CONV_HALO:CONV_HALO + tm]


def _conv_fwd(name, z, cw, w_dw, ln_g, ln_b):
    n = z.shape[0]
    tm = _row_tile(n, 4 * cw, CONV_HALO)

    def fn(i, nrow, ap, a, an, bp, b, bn, w, g, beta):
        a = a[...]
        ext = _extend(i, nrow, ap[...] * _sigmoid(bp[...]), a * _sigmoid(b[...]), an[...] * _sigmoid(bn[...]))
        conv = jnp.zeros_like(a)
        for k in range(CONV_KERNEL):
            conv = conv + _shifted(ext, k - CONV_KERNEL // 2, tm) * w[k:k + 1, :]
        xc = conv - jnp.mean(conv, axis=-1, keepdims=True)
        ln = xc * lax.rsqrt(jnp.mean(xc * xc, axis=-1, keepdims=True) + EPS) * g[...] + beta[...]
        return conv, ln * _sigmoid(ln)

    ins = _halo(z, cw, 0) + _halo(z, cw, 1) + [_b(w_dw), _b(ln_g), _b(ln_b)]
    return _rowwise(name, fn, n, tm, ins, [(F32, cw, cw), (BF16, cw, cw)], refs_to_fn=True)


def _conv_ln_bwd(name, ds, conv, ln_g, ln_b):
    n, cw = conv.shape

    def fn(i, nrow, ds, conv, g, beta):
        xc = conv - jnp.mean(conv, axis=-1, keepdims=True)
        rstd = lax.rsqrt(jnp.mean(xc * xc, axis=-1, keepdims=True) + EPS)
        xhat = xc * rstd
        ln = xhat * g + beta
        sg = _sigmoid(ln)
        dln = ds * (sg * (1.0 + ln * (1.0 - sg)))
        dxh = dln * g
        dconv = rstd * (dxh - jnp.mean(dxh, axis=-1, keepdims=True)
                        - xhat * jnp.mean(dxh * xhat, axis=-1, keepdims=True))
        return dconv, _sum8(dln * xhat), _sum8(dln)

    return _rowwise(name, fn, n, _row_tile(n, 2 * cw), [_t(ds), _t(conv), _b(ln_g), _b(ln_b)], [(F32, cw, cw)],
                    [(SUBLANES, cw, cw, True), (SUBLANES, cw, cw, True)])


def _conv_glu_bwd(name, dconv, z, cw, w_dw):
    n = z.shape[0]
    tm = _row_tile(n, 4 * cw, CONV_HALO)

    def fn(i, nrow, dp, d, dn, ap, a, an, bp, b, bn, w):
        d, a = d[...], a[...]
        dext = _extend(i, nrow, dp[...], d, dn[...])
        sg = _sigmoid(b[...])
        uext = _extend(i, nrow, ap[...] * _sigmoid(bp[...]), a * sg, an[...] * _sigmoid(bn[...]))
        tap = lax.broadcasted_iota(jnp.int32, (CONV_TAPS_PAD, cw), 0)
        du = jnp.zeros_like(d)
        dw = jnp.zeros((CONV_TAPS_PAD, cw), F32)
        for k in range(CONV_KERNEL):
            off = k - CONV_KERNEL // 2
            du = du + _shifted(dext, -off, tm) * w[k:k + 1, :]
            dw_k = jnp.sum(d * _shifted(uext, off, tm), axis=0, keepdims=True)
            dw = dw + jnp.where(tap == k, dw_k, 0.0)
        return du * sg, du * a * sg * (1.0 - sg), dw

    ins = _halo(dconv, cw, 0) + _halo(z, cw, 0) + _halo(z, cw, 1) + [_b(w_dw)]
    return _rowwise(name, fn, n, tm, ins, [(BF16, cw, cw), (BF16, cw, cw)], [(CONV_TAPS_PAD, cw, cw, True)],
                    refs_to_fn=True)


def _merge_fwd(name, z, d, y_c, y_a):
    n, hw = z.shape[0], d // 2

    def fn(i, nrow, gc, ga, yc, ya):
        return _sigmoid(gc) * yc + _sigmoid(ga) * ya

    ins = [_t(z, hw, 5), _t(z, hw, 7), _t(y_c, hw), _t(y_a, hw)]
    return _rowwise(name, fn, n, _row_tile(n, 2 * hw), ins, [(BF16, d, hw)], ncol=2)[0]


def _merge_bwd(name, dm, z, d, y_c, y_a):
    n, hw = z.shape[0], d // 2

    def fn(i, nrow, dm, gc, ga, yc, ya):
        sc, sa = _sigmoid(gc), _sigmoid(ga)
        return dm * yc * sc * (1.0 - sc), dm * ya * sa * (1.0 - sa), dm * sc, dm * sa

    ins = [_t(dm, hw), _t(z, hw, 5), _t(z, hw, 7), _t(y_c, hw), _t(y_a, hw)]
    return _rowwise(name, fn, n, _row_tile(n, 2 * hw), ins, [(BF16, d, hw)] * 4, ncol=2)


def _relu2(name, f):
    n, w = f.shape
    tw = _div(w, 2048, LANES)

    def fn(i, nrow, f):
        r = jnp.maximum(f, 0.0)
        return r * r

    return _rowwise(name, fn, n, _row_tile(n, tw), [_t(f, tw)], [(BF16, w, tw)], ncol=w // tw)[0]


def _relu2_bwd(name, da, f):
    n, w = f.shape
    tw = _div(w, 2048, LANES)

    def fn(i, nrow, da, f):
        return da * (2.0 * jnp.maximum(f, 0.0))

    return _rowwise(name, fn, n, _row_tile(n, tw), [_t(da, tw), _t(f, tw)], [(BF16, w, tw)], ncol=w // tw)[0]


def _head_loss(name, x2, pre, pp, target, g):
    n, d = x2.shape

    def fn(i, nrow, x2, pre, pp, t, g):
        x3 = x2 + _sigmoid(pre) * pp
        r = lax.rsqrt(jnp.mean(x3 * x3, axis=-1, keepdims=True) + EPS)
        xn = x3 * r
        e = xn * g - t
        part = 0.5 * jnp.sum(jnp.mean(e * e, axis=-1, keepdims=True), axis=0, keepdims=True)
        dy = e * (1.0 / d)
        w = dy * g
        dx3 = r * w - x3 * (r * r * r) * jnp.mean(x3 * w, axis=-1, keepdims=True)
        return dx3, _sum8(dy * xn), jnp.broadcast_to(part, (SUBLANES, LANES))

    ins = [_t(x2), _t(pre), _t(pp), _t(target), _b(g)]
    return _rowwise(name, fn, n, _row_tile(n, 2 * d), ins, [(F32, d, d)],
                    [(SUBLANES, d, d, True), (SUBLANES, LANES, LANES, True)])


def _ple_bwd(name, dx3, pre, pp):
    n, d = dx3.shape

    def fn(i, nrow, dx3, pre, pp):
        gate = _sigmoid(pre)
        return dx3 * gate, dx3 * pp * gate * (1.0 - gate)

    return _rowwise(name, fn, n, _row_tile(n, 2 * d), [_t(dx3), _t(pre), _t(pp)], [(BF16, d, d)] * 2)


def _group_sum(name, dvh, n_kv):
    n = dvh.shape[0]

    def fn(i, nrow, *a):
        s = a[0]
        for k in range(1, GROUP):
            s = s + a[k]
        return s

    ins = [_t(dvh, HEAD_DIM, k, GROUP) for k in range(GROUP)]
    return _rowwise(name, fn, n, _div(n, 1024, SUBLANES), ins, [(BF16, n_kv * HEAD_DIM, HEAD_DIM)], ncol=n_kv)[0]


MM_ROWS = 1024
MM_COLS = 1152
MM_REDUCE = 2048
MM_REDUCE_BLOCKS = 2304


def _accumulate(step, n_steps, part, acc, finish):
    if n_steps == 1:
        finish(part)
        return

    @pl.when(step == 0)
    def _():
        acc[...] = part

    @pl.when(jnp.logical_and(step > 0, step < n_steps - 1))
    def _():
        acc[...] += part

    @pl.when(step == n_steps - 1)
    def _():
        finish(acc[...] + part)


def _mm_nn(name, a, w3, out_dtype, residual=None):
    m, k = a.shape
    nb, _, ns = w3.shape
    tm, tn, tk = _div(m, MM_ROWS, SUBLANES), _div(ns, MM_COLS, LANES), _div(k, MM_REDUCE, LANES)
    per, nk = ns // tn, k // tk
    has_res = residual is not None

    def body(*refs):
        a_ref, w_ref = refs[:2]
        o_ref, acc = refs[-2:]
        part = jnp.dot(a_ref[...].astype(BF16), w_ref[...].astype(BF16), preferred_element_type=F32)

        def finish(total):
            if has_res:
                total = total + refs[2][...]
            o_ref[...] = total.astype(out_dtype)

        _accumulate(pl.program_id(2), nk, part, acc, finish)

    in_specs = [pl.BlockSpec((tm, tk), lambda i, n, kk: (i, kk)),
                pl.BlockSpec((None, tk, tn), lambda i, n, kk: (n // per, kk, n % per))]
    args = [a, w3]
    if has_res:
        in_specs.append(pl.BlockSpec((tm, tn), lambda i, n, kk: (i, n)))
        args.append(residual)
    return pl.pallas_call(
        body, name=name, grid=(m // tm, nb * per, nk), in_specs=in_specs,
        out_specs=pl.BlockSpec((tm, tn), lambda i, n, kk: (i, n)),
        out_shape=jax.ShapeDtypeStruct((m, nb * ns), out_dtype),
        scratch_shapes=[pltpu.VMEM((tm, tn) if nk > 1 else (SUBLANES, LANES), F32)],
        compiler_params=_params(("parallel", "parallel", "arbitrary")))(*args)


def _mm_nt(name, dc, w3, out_dtype, after=()):
    m = dc.shape[0]
    nb, k, ns = w3.shape
    tm, to = _div(m, MM_ROWS, SUBLANES), _div(k, MM_ROWS, LANES)
    if ns > MM_REDUCE_BLOCKS // 2:
        jb, tr = 1, _div(ns, MM_REDUCE, LANES)
    else:
        jb, tr = max(j for j in range(1, nb + 1) if nb % j == 0 and j * ns <= MM_REDUCE_BLOCKS), ns
    per = ns // tr
    nr = (nb // jb) * per

    def body(dc_ref, w_ref, *rest):
        o_ref, acc = rest[-2:]
        part = None
        for j in range(jb):
            term = lax.dot_general(dc_ref[:, j * tr:(j + 1) * tr].astype(BF16), w_ref[j].astype(BF16),
                                   (((1,), (1,)), ((), ())), preferred_element_type=F32)
            part = term if part is None else part + term

        def finish(total):
            o_ref[...] = total.astype(out_dtype)

        _accumulate(pl.program_id(2), nr, part, acc, finish)

    in_specs = [pl.BlockSpec((tm, jb * tr), lambda i, o, r: (i, r)),
                pl.BlockSpec((jb, to, tr), lambda i, o, r: (r // per, o, r % per))]
    args = [dc, w3]
    for token in after:
        in_specs.append(pl.BlockSpec(memory_space=pl.ANY))
        args.append(token)
    return pl.pallas_call(
        body, name=name, grid=(m // tm, k // to, nr), in_specs=in_specs,
        out_specs=pl.BlockSpec((tm, to), lambda i, o, r: (i, o)),
        out_shape=jax.ShapeDtypeStruct((m, k), out_dtype),
        scratch_shapes=[pltpu.VMEM((tm, to) if nr > 1 else (SUBLANES, LANES), F32)],
        compiler_params=_params(("parallel", "parallel", "arbitrary")))(*args)


def _mm_tn(name, a, dc, nb):
    m, k = a.shape
    ns = dc.shape[1] // nb
    tr, tn, to = _div(m, MM_REDUCE, LANES), _div(ns, MM_COLS, LANES), _div(k, MM_ROWS, LANES)
    per, nr = ns // tn, m // tr

    def body(a_ref, dc_ref, o_ref, acc):
        part = lax.dot_general(a_ref[...].astype(BF16), dc_ref[...].astype(BF16), (((0,), (0,)), ((), ())),
                               preferred_element_type=F32)

        def finish(total):
            o_ref[...] = total.astype(BF16)

        _accumulate(pl.program_id(2), nr, part, acc, finish)

    return pl.pallas_call(
        body, name=name, grid=(k // to, nb * per, nr),
        in_specs=[pl.BlockSpec((tr, to), lambda o, n, r: (r, o)),
                  pl.BlockSpec((tr, tn), lambda o, n, r: (r, n))],
        out_specs=pl.BlockSpec((None, to, tn), lambda o, n, r: (n // per, o, n % per)),
        out_shape=jax.ShapeDtypeStruct((nb, k, ns), BF16),
        scratch_shapes=[pltpu.VMEM((to, tn) if nr > 1 else (SUBLANES, LANES), F32)],
        compiler_params=_params(("parallel", "parallel", "arbitrary")))(a, dc)


QK_SCALE = HEAD_DIM ** -0.5
LOG2_E = 1.4426950408889634
LN_2 = 0.6931471805599453


def _columns(row, n):
    return jnp.transpose(jnp.broadcast_to(row, (LANES, n)))


def _flash_fwd(qr, kr, z, v_col0, n_heads):
    t = qr.shape[0]
    tq, tk = _div(t, 1024, LANES), _div(t, 1024, LANES)
    nk = t // tk

    def body(q_ref, k_ref, v_ref, o_ref, lse_ref, m_scr, l_scr, acc_scr):
        kb = pl.program_id(2)

        @pl.when(kb == 0)
        def _():
            m_scr[...] = jnp.full_like(m_scr, -1e30)
            l_scr[...] = jnp.zeros_like(l_scr)
            acc_scr[...] = jnp.zeros_like(acc_scr)

        s_t = lax.dot_general(k_ref[...], q_ref[...], (((1,), (1,)), ((), ())), preferred_element_type=F32)
        m_prev = m_scr[...]
        m_new = jnp.maximum(m_prev, jnp.max(s_t, axis=0, keepdims=True))
        alpha = jnp.exp2(m_prev - m_new)
        p_t = jnp.exp2(s_t - m_new)
        l_scr[...] = alpha * l_scr[...] + jnp.sum(p_t, axis=0, keepdims=True)
        pv = lax.dot_general(p_t.astype(BF16), v_ref[...].astype(BF16), (((0,), (0,)), ((), ())),
                             preferred_element_type=F32)
        acc_scr[...] = _columns(alpha, tq) * acc_scr[...] + pv
        m_scr[...] = m_new

        @pl.when(kb == nk - 1)
        def _():
            l = l_scr[...]
            o_ref[...] = acc_scr[...] / _columns(l, tq)
            lse_ref[...] = jnp.broadcast_to(m_scr[...] + jnp.log(l) * LOG2_E, (SUBLANES, tq))

    return pl.pallas_call(
        body, name="flash_fwd", grid=(n_heads, t // tq, nk),
        in_specs=[pl.BlockSpec((tq, HEAD_DIM), lambda h, qi, ki: (qi, h)),
                  pl.BlockSpec((tk, HEAD_DIM), lambda h, qi, ki: (ki, h // GROUP)),
                  pl.BlockSpec((tk, HEAD_DIM), lambda h, qi, ki: (ki, v_col0 + h // GROUP))],
        out_specs=[pl.BlockSpec((tq, HEAD_DIM), lambda h, qi, ki: (qi, h)),
                   pl.BlockSpec((None, SUBLANES, tq), lambda h, qi, ki: (h, 0, qi))],
        out_shape=[jax.ShapeDtypeStruct((t, n_heads * HEAD_DIM), F32),
                   jax.ShapeDtypeStruct((n_heads, SUBLANES, t), F32)],
        scratch_shapes=[pltpu.VMEM((1, tq), F32), pltpu.VMEM((1, tq), F32), pltpu.VMEM((tq, HEAD_DIM), F32)],
        compiler_params=_params(("parallel", "parallel", "arbitrary")))(qr, kr, z)


def _attn_delta(do, o, n_heads):
    t = o.shape[0]
    tm = _div(t, 1024, LANES)

    def body(do_ref, o_ref, d_ref):
        prod = do_ref[...].astype(F32) * o_ref[...]
        row = jnp.sum(jnp.transpose(prod), axis=0, keepdims=True)
        d_ref[...] = jnp.broadcast_to(row, (SUBLANES, tm))

    return pl.pallas_call(
        body, name="attn_delta", grid=(n_heads, t // tm),
        in_specs=[pl.BlockSpec((tm, HEAD_DIM), lambda h, i: (i, h)),
                  pl.BlockSpec((tm, HEAD_DIM), lambda h, i: (i, h))],
        out_specs=pl.BlockSpec((None, SUBLANES, tm), lambda h, i: (h, 0, i)),
        out_shape=jax.ShapeDtypeStruct((n_heads, SUBLANES, t), F32),
        compiler_params=_params(("parallel", "parallel")))(do, o)


def _flash_bwd(qr, kr, z, v_col0, do, lse, delta, n_heads):
    t = qr.shape[0]
    tq, tk = _div(t, 1024, LANES), _div(t, 1024, LANES)
    nq = t // tq

    def body(q_ref, k_ref, v_ref, do_ref, lse_ref, dl_ref, dq_ref, dk_ref, dv_ref, dk_acc, dv_acc):
        kb, qb = pl.program_id(1), pl.program_id(2)
        q, k, do_ = q_ref[...], k_ref[...], do_ref[...]
        v = v_ref[...].astype(BF16)
        s_t = lax.dot_general(k, q, (((1,), (1,)), ((), ())), preferred_element_type=F32)
        p_t = jnp.exp2(s_t - lse_ref[0:1, :])
        dp_t = lax.dot_general(v, do_, (((1,), (1,)), ((), ())), preferred_element_type=F32)
        ds_t = (p_t * (dp_t - dl_ref[0:1, :])).astype(BF16)
        dv_c = jnp.dot(p_t.astype(BF16), do_, preferred_element_type=F32)
        dk_c = jnp.dot(ds_t, q, preferred_element_type=F32) * LN_2
        dq_c = lax.dot_general(ds_t, k, (((0,), (0,)), ((), ())), preferred_element_type=F32) * QK_SCALE

        @pl.when(qb == 0)
        def _():
            dk_acc[...] = dk_c
            dv_acc[...] = dv_c

        @pl.when(qb > 0)
        def _():
            dk_acc[...] += dk_c
            dv_acc[...] += dv_c

        @pl.when(qb == nq - 1)
        def _():
            dk_ref[...] = dk_acc[...]
            dv_ref[...] = dv_acc[...]

        rows = pl.ds(pl.multiple_of(qb * tq, tq), tq)

        @pl.when(kb == 0)
        def _():
            dq_ref[rows, :] = dq_c

        @pl.when(kb > 0)
        def _():
            dq_ref[rows, :] += dq_c

    wide = jax.ShapeDtypeStruct((t, n_heads * HEAD_DIM), F32)
    return pl.pallas_call(
        body, name="flash_bwd", grid=(n_heads, t // tk, nq),
        in_specs=[pl.BlockSpec((tq, HEAD_DIM), lambda h, kb, qb: (qb, h)),
                  pl.BlockSpec((tk, HEAD_DIM), lambda h, kb, qb: (kb, h // GROUP)),
                  pl.BlockSpec((tk, HEAD_DIM), lambda h, kb, qb: (kb, v_col0 + h // GROUP)),
                  pl.BlockSpec((tq, HEAD_DIM), lambda h, kb, qb: (qb, h)),
                  pl.BlockSpec((None, SUBLANES, tq), lambda h, kb, qb: (h, 0, qb)),
                  pl.BlockSpec((None, SUBLANES, tq), lambda h, kb, qb: (h, 0, qb))],
        out_specs=[pl.BlockSpec((t, HEAD_DIM), lambda h, kb, qb: (0, h)),
                   pl.BlockSpec((tk, HEAD_DIM), lambda h, kb, qb: (kb, h)),
                   pl.BlockSpec((tk, HEAD_DIM), lambda h, kb, qb: (kb, h))],
        out_shape=[wide, wide, wide],
        scratch_shapes=[pltpu.VMEM((tk, HEAD_DIM), F32), pltpu.VMEM((tk, HEAD_DIM), F32)],
        compiler_params=_params(("parallel", "arbitrary", "arbitrary")))(qr, kr, z, do, lse, delta)


def _position():
    x, y, c = lax.axis_index("x"), lax.axis_index("y"), lax.axis_index("c")
    return x, y, c


def _index(x, y, c):
    return 4 * x + 2 * y + c


def _all_gather(name, shard):
    a, b = shard.shape

    def body(x_ref, out_ref, send_sems, recv_sems, local_sem):
        x, y, c = _position()
        me, sibling = (x, y, c), (x, y, 1 - c)
        chips = [(1 - x, y), (x, 1 - y), (1 - x, 1 - y)]

        def block(px, py, pc):
            return out_ref.at[_index(px, py, pc)]

        def copy(k, blk, to, src=None):
            return pltpu.make_async_remote_copy(
                src_ref=block(*blk) if src is None else src, dst_ref=block(*blk),
                send_sem=send_sems.at[k], recv_sem=recv_sems.at[k], device_id=to, device_id_type=MESH)

        mine = pltpu.make_async_copy(x_ref, block(*me), local_sem)
        mine.start()
        first = [copy(0, me, sibling, src=x_ref)]
        first += [copy(1 + j, me, (*chip, c), src=x_ref) for j, chip in enumerate(chips)]
        for cp in first:
            cp.start()
        passed = [copy(4 + j, (*chip, c), sibling) for j, chip in enumerate(chips)]
        for j, chip in enumerate(chips):
            copy(1 + j, (*chip, c), me).wait_recv()
            passed[j].start()
        copy(0, sibling, me).wait_recv()
        for j, chip in enumerate(chips):
            copy(4 + j, (*chip, 1 - c), me).wait_recv()
        for cp in first + passed:
            cp.wait_send()
        mine.wait()

    return pl.pallas_call(
        body, name=name, out_shape=jax.ShapeDtypeStruct((N_DEV, a, b), shard.dtype),
        in_specs=[pl.BlockSpec(memory_space=pltpu.HBM)], out_specs=pl.BlockSpec(memory_space=pltpu.HBM),
        scratch_shapes=[pltpu.SemaphoreType.DMA((7,)), pltpu.SemaphoreType.DMA((7,)), pltpu.SemaphoreType.DMA],
    )(shard)


HBM_SPEC = pl.BlockSpec(memory_space=pltpu.HBM)
SEM_SPEC = pl.BlockSpec(memory_space=pltpu.SEMAPHORE)
N_PEERS = N_DEV - 1
PEER_FLIPS = [(fx, fy, fc) for fx in (0, 1) for fy in (0, 1) for fc in (0, 1)][1:]


def _peers(x, y, c):
    return [((1 - x) if fx else x, (1 - y) if fy else y, (1 - c) if fc else c) for fx, fy, fc in PEER_FLIPS]


def _exchange_copies(g_ref, land_ref, send_sems, recv_sems):
    x, y, c = _position()
    me = _index(x, y, c)
    peers = _peers(x, y, c)
    blocked = len(g_ref.shape) == 3

    def copy(k, src_block, dst_block):
        return pltpu.make_async_remote_copy(
            src_ref=g_ref.at[src_block] if blocked else g_ref, dst_ref=land_ref.at[dst_block],
            send_sem=send_sems.at[k], recv_sem=recv_sems.at[k], device_id=peers[k], device_id_type=MESH)

    sent = [copy(k, _index(*peer), me) for k, peer in enumerate(peers)]
    arriving = [copy(k, me, _index(*peer)) for k, peer in enumerate(peers)]
    own = pltpu.make_async_copy(g_ref.at[me] if blocked else g_ref, land_ref.at[me], send_sems.at[N_PEERS])
    return sent, arriving, own


def _exchange_start(name, g3, after=()):
    def body(g_ref, land_ref, *rest):
        send_sems, recv_sems, _, _, token = rest[len(after):]
        sent, _, own = _exchange_copies(g_ref, land_ref, send_sems, recv_sems)
        for cp in sent:
            cp.start()
        own.start()
        token[...] = jnp.zeros_like(token)

    land_shape = g3.shape if g3.ndim == 3 else (N_DEV,) + g3.shape
    land = lax.empty(land_shape, g3.dtype)
    send_sems, recv_sems, g_thru, land_thru, token = pl.pallas_call(
        body, name=name,
        out_shape=(pltpu.SemaphoreType.DMA((N_PEERS + 1,)), pltpu.SemaphoreType.DMA((N_PEERS,)),
                   pltpu.HBM(g3.shape, g3.dtype), pltpu.HBM(land_shape, g3.dtype),
                   jax.ShapeDtypeStruct((SUBLANES, LANES), F32)),
        in_specs=(HBM_SPEC, HBM_SPEC) + (pl.BlockSpec(memory_space=pl.ANY),) * len(after),
        out_specs=(SEM_SPEC, SEM_SPEC, HBM_SPEC, HBM_SPEC, pl.BlockSpec(memory_space=pltpu.VMEM)),
        input_output_aliases={0: 2, 1: 3},
        compiler_params=pltpu.CompilerParams(has_side_effects=pltpu.SideEffectType.DATAFLOW_SIDE_EFFECTING),
    )(pltpu.with_memory_space_constraint(g3, pltpu.HBM), pltpu.with_memory_space_constraint(land, pltpu.HBM), *after)
    return (send_sems, recv_sems, g_thru, land_thru), token


def _exchange_wait(name, handle, after):
    send_sems, recv_sems, g_thru, land_thru = handle

    def body(g_ref, land_ref, send_sems, recv_sems, after_ref, g_dead, land_out):
        sent, arriving, own = _exchange_copies(g_ref, land_ref, send_sems, recv_sems)
        for cp in sent:
            cp.wait_send()
        for cp in arriving:
            cp.wait_recv()
        own.wait()

    return pl.pallas_call(
        body, name=name,
        out_shape=(pltpu.HBM(g_thru.shape, g_thru.dtype), pltpu.HBM(land_thru.shape, land_thru.dtype)),
        in_specs=(HBM_SPEC, HBM_SPEC, SEM_SPEC, SEM_SPEC, pl.BlockSpec(memory_space=pl.ANY)),
        out_specs=(HBM_SPEC, HBM_SPEC), input_output_aliases={0: 0, 1: 1},
        compiler_params=pltpu.CompilerParams(has_side_effects=pltpu.SideEffectType.DATAFLOW_SIDE_EFFECTING),
    )(g_thru, land_thru, send_sems, recv_sems, after)[1]


def _adamw(w, g, m, v):
    m = ADAM_B1 * m + (1.0 - ADAM_B1) * g
    v = ADAM_B2 * v + (1.0 - ADAM_B2) * (g * g)
    m_hat = m / (1.0 - ADAM_B1 ** ADAM_STEP)
    v_hat = v / (1.0 - ADAM_B2 ** ADAM_STEP)
    delta = -ADAM_LR * (m_hat / (jnp.sqrt(v_hat) + ADAM_EPS) + ADAM_WD * w)
    return delta, m, v


def _adamw_shard(name, land, w, m, v):
    _, a, b = land.shape
    tm = _div(a, max(16, ROW_TILE_ELEMS // (2 * b)), 16)
    per = a // tm

    def fn(i, nrow, *t):
        g = t[0].astype(F32)
        for s in range(1, N_DEV):
            g = g + t[s].astype(F32)
        w, m, v = t[N_DEV:]
        return (g,) + _adamw(w, g, m, v)

    land2 = land.reshape(N_DEV * a, b)
    ins = [_t(land2, row0=s * per) for s in range(N_DEV)] + [_t(w), _t(m), _t(v)]
    return _rowwise(name, fn, a, tm, ins, [(F32, b, b)] * 4)


def _adamw_small(name, parts, w, m, v):
    rows, n = w.shape

    def body(p_ref, w_ref, m_ref, v_ref, g_out, d_out, m_out, v_out):
        g = p_ref[0]
        for s in range(1, parts.shape[0]):
            g = g + p_ref[s]
        g = jnp.sum(g, axis=0, keepdims=True) if rows == 1 else g[0:rows, :]
        delta, m_new, v_new = _adamw(w_ref[...], g, m_ref[...], v_ref[...])
        g_out[...] = g
        d_out[...] = delta
        m_out[...] = m_new
        v_out[...] = v_new

    return pl.pallas_call(body, name=name, out_shape=[jax.ShapeDtypeStruct((rows, n), F32)] * 4)(parts, w, m, v)


def _rope_tables(t):
    pos = jnp.arange(t, dtype=jnp.int32)
    half = HEAD_DIM // 4
    inv_freq = ROPE_THETA ** (-jnp.arange(0, 2 * half, 2, dtype=F32) / (2 * half))
    ang_r = (pos // GRID_W).astype(F32)[:, None] * inv_freq[None, :]
    ang_c = (pos % GRID_W).astype(F32)[:, None] * inv_freq[None, :]
    cos = jnp.concatenate([jnp.cos(ang_r)] * 2 + [jnp.cos(ang_c)] * 2, axis=-1)
    sin = jnp.concatenate([-jnp.sin(ang_r), jnp.sin(ang_r), -jnp.sin(ang_c), jnp.sin(ang_c)], axis=-1)
    return cos, sin


def _gather_weight(name, w, cols):
    g = _all_gather(name, w[0].astype(BF16))
    return g if cols else g.reshape(1, N_DEV * g.shape[1], g.shape[2])


def kernel(x, p, norm_mix, w_in, w_dw, conv_ln_g, conv_ln_b, w_conv_proj, q_norm, k_norm, w_attn_proj, w_out, norm_ffn, w_ff1, w_ff2, norm_ple, w_ple_gate, w_ple_proj, norm_final, loss_target, m_norm_mix, m_w_in, m_w_dw, m_conv_ln_g, m_conv_ln_b, m_w_conv_proj, m_q_norm, m_k_norm, m_w_attn_proj, m_w_out, m_norm_ffn, m_w_ff1, m_w_ff2, m_norm_ple, m_w_ple_gate, m_w_ple_proj, m_norm_final, v_norm_mix, v_w_in, v_w_dw, v_conv_ln_g, v_conv_ln_b, v_w_conv_proj, v_q_norm, v_k_norm, v_w_attn_proj, v_w_out, v_norm_ffn, v_w_ff1, v_w_ff2, v_norm_ple, v_w_ple_gate, v_w_ple_proj, v_norm_final):
    t, d = x.shape[1], x.shape[2]
    cw = d // 2
    n_heads = d // HEAD_DIM
    n_kv = n_heads // GROUP
    col_q, col_k, col_v = d // HEAD_DIM, 2 * d // HEAD_DIM, (2 * d + d // 4) // HEAD_DIM
    x0, pe, tgt = x[0], p[0, 0], loss_target[0]
    g_final = norm_final.reshape(1, d)
    me = _index(*_position())

    wg_in = _gather_weight("ag_w_in", w_in, True)
    dw_pad = jnp.pad(w_dw[0], ((0, CONV_TAPS_PAD - CONV_KERNEL), (0, 0)))
    dw_all = _all_gather("ag_w_dw", dw_pad)
    dw_full = dw_all.transpose(1, 0, 2).reshape(CONV_TAPS_PAD, cw)
    cos, sin = _rope_tables(t)

    def fetch(name, w):
        return _exchange_start("ag_start_" + name, w[0].astype(BF16), after=(wg_in,))

    def arrived(name, handle, cols, after):
        g = _exchange_wait("ag_wait_" + name, handle, after)
        return g if cols else g.reshape(1, N_DEV * g.shape[1], g.shape[2])

    on_cp, t_cp = fetch("w_conv_proj", w_conv_proj)
    on_ap, t_ap = fetch("w_attn_proj", w_attn_proj)
    on_out, t_out = fetch("w_out", w_out)
    on_ff1, t_ff1 = fetch("w_ff1", w_ff1)
    on_ff2, t_ff2 = fetch("w_ff2", w_ff2)
    on_pg, t_pg = fetch("w_ple_gate", w_ple_gate)
    on_pp, t_pp = fetch("w_ple_proj", w_ple_proj)

    h1 = _rms_fwd("rms_mix", x0, norm_mix, after=(t_cp, t_ap, t_out, t_ff1, t_ff2, t_pg, t_pp))
    z = _mm_nn("mm_in", h1, wg_in, F32)
    conv, s_c = _conv_fwd("conv_fwd", z, cw, dw_full, conv_ln_g, conv_ln_b)
    wg_cp = arrived("w_conv_proj", on_cp, True, s_c)
    y_c = _mm_nn("mm_conv_proj", s_c, wg_cp, F32)
    qr = _head_norm_rope("q_prep", z, col_q, n_heads, q_norm, cos, sin, out_scale=QK_SCALE * LOG2_E)
    kr = _head_norm_rope("k_prep", z, col_k, n_kv, k_norm, cos, sin)
    o, lse = _flash_fwd(qr, kr, z, col_v, n_heads)
    wg_ap = arrived("w_attn_proj", on_ap, False, o)
    y_a = _mm_nn("mm_attn_proj", o, wg_ap, F32)
    merged = _merge_fwd("merge_fwd", z, d, y_c, y_a)
    wg_out = arrived("w_out", on_out, False, merged)
    x1 = _mm_nn("mm_out", merged, wg_out, F32, residual=x0)
    h2 = _rms_fwd("rms_ffn", x1, norm_ffn)
    wg_ff1 = arrived("w_ff1", on_ff1, True, h2)
    f = _mm_nn("mm_ff1", h2, wg_ff1, F32)
    act = _relu2("relu2", f)
    wg_ff2 = arrived("w_ff2", on_ff2, False, act)
    x2 = _mm_nn("mm_ff2", act, wg_ff2, F32, residual=x1)
    hp = _rms_fwd("rms_ple", x2, norm_ple)
    wg_pg = arrived("w_ple_gate", on_pg, False, hp)
    pre = _mm_nn("mm_ple_gate", hp, wg_pg, F32)
    wg_pp = arrived("w_ple_proj", on_pp, True, pre)
    pp = _mm_nn("mm_ple_proj", pe, wg_pp, F32)

    dx3, dg_final, loss_part = _head_loss("head_loss", x2, pre, pp, tgt, g_final)
    loss = lax.psum(loss_part[0, 0], ("x", "y", "c"))
    def send(name, gw):
        blocks = gw if gw.shape[0] == N_DEV else gw.reshape(N_DEV, gw.shape[1] // N_DEV, gw.shape[2])
        return _exchange_start("rs_start_" + name, blocks)

    dpp, dpre = _ple_bwd("ple_bwd", dx3, pre, pp)
    sent_pp, tok_pp = send("w_ple_proj", _mm_tn("mm_d_ple_proj", pe, dpp, N_DEV))
    sent_pg, tok = send("w_ple_gate", _mm_tn("mm_d_ple_gate", hp, dpre, 1))
    dhp = _mm_nt("mm_dhp", dpre, wg_pg, F32, after=(tok_pp, tok))
    dx2, dg_ple = _rms_bwd("rms_ple_bwd", x2, norm_ple, dhp, dx3)
    sent_ff2, tok = send("w_ff2", _mm_tn("mm_d_ff2", act, dx2, 1))
    dact = _mm_nt("mm_dact", dx2, wg_ff2, F32, after=(tok,))
    df = _relu2_bwd("relu2_bwd", dact, f)
    sent_ff1, tok = send("w_ff1", _mm_tn("mm_d_ff1", h2, df, N_DEV))
    dh2 = _mm_nt("mm_dh2", df, wg_ff1, F32, after=(tok,))
    dx1, dg_ffn = _rms_bwd("rms_ffn_bwd", x1, norm_ffn, dh2, dx2)
    sent_out, tok = send("w_out", _mm_tn("mm_d_out", merged, dx1, 1))
    dmerged = _mm_nt("mm_dmerged", dx1, wg_out, F32, after=(tok,))
    dgc, dga, dyc, dya = _merge_bwd("merge_bwd", dmerged, z, d, y_c, y_a)
    sent_cp, tok = send("w_conv_proj", _mm_tn("mm_d_conv_proj", s_c, dyc, N_DEV))
    ds_c = _mm_nt("mm_ds_c", dyc, wg_cp, F32, after=(tok,))
    dconv, dg_ln, db_ln = _conv_ln_bwd("conv_ln_bwd", ds_c, conv, conv_ln_g, conv_ln_b)
    dca, dcb, dg_dw = _conv_glu_bwd("conv_glu_bwd", dconv, z, cw, dw_full)
    sent_ap, tok = send("w_attn_proj", _mm_tn("mm_d_attn_proj", o, dya, 1))
    do = _mm_nt("mm_do", dya, wg_ap, BF16, after=(tok,))
    delta = _attn_delta(do, o, n_heads)
    dqr, dkh, dvh = _flash_bwd(qr, kr, z, col_v, do, lse, delta, n_heads)
    dq, dg_q = _head_norm_rope_bwd("q_prep_bwd", [(dqr, 0, 1)], z, col_q, n_heads, q_norm, cos, sin)
    dk, dg_k = _head_norm_rope_bwd("k_prep_bwd", [(dkh, g, GROUP) for g in range(GROUP)], z, col_k, n_kv, k_norm,
                                   cos, sin)
    dv = _group_sum("dv_sum", dvh, n_kv)
    dz = jnp.concatenate([dca, dcb, dq, dk, dv, dgc, dga], axis=1)
    sent_in, tok = send("w_in", _mm_tn("mm_d_in", h1, dz, N_DEV))
    dh1 = _mm_nt("mm_dh1", dz, wg_in, F32, after=(tok,))
    grad_x, dg_mix = _rms_bwd("rms_mix_bwd", x0, norm_mix, dh1, dx1)

    def pad(a):
        return jnp.pad(a, ((0, 0), (0, d - a.shape[1])))

    small = [dg_mix, dg_ffn, dg_ple, dg_final, dg_ln, db_ln, dg_q, dg_k, dg_dw]
    packed = _all_gather("ag_small", jnp.concatenate([pad(a) for a in small], axis=0))
    parts, row = [], 0
    for a in small:
        parts.append(packed[:, row:row + a.shape[0], :a.shape[1]])
        row += a.shape[0]
    p_mix, p_ffn, p_ple, p_final, p_lng, p_lnb, p_q, p_k, p_dw = parts
    p_dw = lax.dynamic_slice_in_dim(p_dw, me * (cw // N_DEV), cw // N_DEV, axis=2)

    def little(name, part, w, m, v):
        rows, n = (1 if w.ndim < 3 else w.shape[1]), w.shape[-1]
        padded = rows if rows == 1 else part.shape[1]

        def two_d(a):
            return jnp.pad(a.reshape(rows, n), ((0, padded - rows), (0, 0)))

        res = _adamw_small("adamw_" + name, part, two_d(w), two_d(m), two_d(v))
        return [r[:rows].reshape(w.shape) for r in res]

    done = {}
    after = packed
    for name, sent, w, m, v in [
            ("w_ple_proj", sent_pp, w_ple_proj, m_w_ple_proj, v_w_ple_proj),
            ("w_ple_gate", sent_pg, w_ple_gate, m_w_ple_gate, v_w_ple_gate),
            ("w_ff2", sent_ff2, w_ff2, m_w_ff2, v_w_ff2),
            ("w_ff1", sent_ff1, w_ff1, m_w_ff1, v_w_ff1),
            ("w_out", sent_out, w_out, m_w_out, v_w_out),
            ("w_conv_proj", sent_cp, w_conv_proj, m_w_conv_proj, v_w_conv_proj),
            ("w_attn_proj", sent_ap, w_attn_proj, m_w_attn_proj, v_w_attn_proj),
            ("w_in", sent_in, w_in, m_w_in, v_w_in)]:
        land = _exchange_wait("rs_wait_" + name, sent, after)
        res = _adamw_shard("adamw_" + name, land, w[0], m[0], v[0])
        after = res[0]
        done[name] = [r.reshape(w.shape) for r in res]

    results = [
        little("norm_mix", p_mix, norm_mix, m_norm_mix, v_norm_mix),
        done["w_in"],
        little("w_dw", p_dw, w_dw, m_w_dw, v_w_dw),
        little("conv_ln_g", p_lng, conv_ln_g, m_conv_ln_g, v_conv_ln_g),
        little("conv_ln_b", p_lnb, conv_ln_b, m_conv_ln_b, v_conv_ln_b),
        done["w_conv_proj"],
        little("q_norm", p_q, q_norm, m_q_norm, v_q_norm),
        little("k_norm", p_k, k_norm, m_k_norm, v_k_norm),
        done["w_attn_proj"],
        done["w_out"],
        little("norm_ffn", p_ffn, norm_ffn, m_norm_ffn, v_norm_ffn),
        done["w_ff1"],
        done["w_ff2"],
        little("norm_ple", p_ple, norm_ple, m_norm_ple, v_norm_ple),
        done["w_ple_gate"],
        done["w_ple_proj"],
        little("norm_final", p_final, norm_final, m_norm_final, v_norm_final),
    ]
    grads, deltas, new_m, new_v = zip(*results)
    return (loss, grad_x[None], *grads, *deltas, *new_m, *new_v)
```

```python
import functools

import jax
import jax.numpy as jnp
from jax import lax
from jax.experimental import pallas as pl
from jax.experimental.pallas import tpu as pltpu

F32 = jnp.float32
BF16 = jnp.bfloat16

EPS = 1e-6
HEAD_DIM = 128
GROUP = 4
CONV_KERNEL = 31
CONV_HALO = 16
CONV_TAPS_PAD = 32
GRID_W = 64
ROPE_THETA = 10000.0
ADAM_LR, ADAM_B1, ADAM_B2, ADAM_EPS, ADAM_WD, ADAM_STEP = 0.001, 0.9, 0.999, 1e-08, 0.01, 10

N_DEV = 8
MESH = pl.DeviceIdType.MESH
SUBLANES = 8
LANES = 128
VMEM_LIMIT = 48 * 1024 * 1024
ROW_TILE_ELEMS = 512 * 1024


def _div(n, pref, mult=1):
    if n <= pref:
        return n
    for d in range(pref, 0, -1):
        if n % d == 0 and d % mult == 0:
            return d
    return n


def _params(sem):
    return pltpu.CompilerParams(dimension_semantics=sem, vmem_limit_bytes=VMEM_LIMIT)


def _sum8(a):
    tm, w = a.shape
    return a.reshape(tm // SUBLANES, SUBLANES, w).sum(axis=0)


def _sigmoid(x):
    return jax.nn.sigmoid(x)


def _rowwise(name, fn, n_rows, tm, ins, outs, accs=(), ncol=1, refs_to_fn=False, after=()):
    nrow = n_rows // tm
    in_specs, arrays = [], []
    for spec in ins:
        kind, arr, w, c0 = spec[:4]
        if kind == "t":
            cs, r0 = spec[4], spec[5]
            in_specs.append(pl.BlockSpec((tm, w), lambda j, i, c0=c0, cs=cs, r0=r0: (r0 + i, c0 + cs * j)))
        elif kind == "p":
            h = spec[4]
            r = tm // h
            in_specs.append(pl.BlockSpec((h, w), lambda j, i, c0=c0, r=r: (jnp.maximum(i * r - 1, 0), c0 + j)))
        elif kind == "n":
            h = spec[4]
            r = tm // h
            last = n_rows // h - 1
            in_specs.append(
                pl.BlockSpec((h, w), lambda j, i, c0=c0, r=r, last=last: (jnp.minimum((i + 1) * r, last), c0 + j)))
        else:
            cs = spec[4]
            in_specs.append(pl.BlockSpec((arr.shape[0], w), lambda j, i, c0=c0, cs=cs: (0, c0 + cs * j)))
        arrays.append(arr)
    out_shape, out_specs = [], []
    for dtype, total, w in outs:
        out_shape.append(jax.ShapeDtypeStruct((n_rows, total), dtype))
        out_specs.append(pl.BlockSpec((tm, w), lambda j, i: (i, j)))
    for rows, total, w, follow in accs:
        out_shape.append(jax.ShapeDtypeStruct((rows, total), F32))
        out_specs.append(pl.BlockSpec((rows, w), (lambda j, i: (0, j)) if follow else (lambda j, i: (0, 0))))
    n_in, n_out, n_acc = len(ins), len(outs), len(accs)
    for token in after:
        in_specs.append(pl.BlockSpec(memory_space=pl.ANY))
        arrays.append(token)
    out0 = n_in + len(after)

    def body(*refs):
        j, i = pl.program_id(0), pl.program_id(1)
        res = fn(i, nrow, *(refs[:n_in] if refs_to_fn else [r[...] for r in refs[:n_in]]))
        if not isinstance(res, (tuple, list)):
            res = (res,)
        for k in range(n_out):
            refs[out0 + k][...] = res[k].astype(outs[k][0])
        for k in range(n_acc):
            ref, term = refs[out0 + n_out + k], res[n_out + k]
            first = (i == 0) if accs[k][3] else jnp.logical_and(i == 0, j == 0)

            @pl.when(first)
            def _():
                ref[...] = term

            @pl.when(jnp.logical_not(first))
            def _():
                ref[...] += term

    res = pl.pallas_call(
        body, name=name, grid=(ncol, nrow), in_specs=in_specs, out_specs=out_specs, out_shape=out_shape,
        compiler_params=_params(("arbitrary", "arbitrary")))(*arrays)
    return res


def _row_tile(n_rows, width, mult=SUBLANES):
    return _div(n_rows, max(mult, ROW_TILE_ELEMS // width), mult)


def _t(arr, w=None, col0=0, cstride=1, row0=0):
    return ("t", arr, arr.shape[1] if w is None else w, col0, cstride, row0)


def _b(arr, w=None, col0=0, cstride=0):
    return ("b", arr, arr.shape[1] if w is None else w, col0, cstride)


def _rms_fwd(name, x, g, after=()):
    n, d = x.shape

    def fn(i, nrow, x, g):
        r = lax.rsqrt(jnp.mean(x * x, axis=-1, keepdims=True) + EPS)
        return x * r * g

    return _rowwise(name, fn, n, _row_tile(n, d), [_t(x), _b(g)], [(BF16, d, d)], after=after)[0]


def _rms_bwd(name, x, g, dh, dres):
    n, d = x.shape

    def fn(i, nrow, x, g, dh, dres):
        r = lax.rsqrt(jnp.mean(x * x, axis=-1, keepdims=True) + EPS)
        w = dh * g
        dx = dres + r * w - x * (r * r * r) * jnp.mean(x * w, axis=-1, keepdims=True)
        return dx, _sum8(dh * x * r)

    return _rowwise(name, fn, n, _row_tile(n, 2 * d), [_t(x), _b(g), _t(dh), _t(dres)], [(F32, d, d)],
                    [(SUBLANES, d, d, True)])


def _swap_halves(x):
    lane = lax.broadcasted_iota(jnp.int32, x.shape, 1)
    return jnp.where(lane % 64 < 32, pltpu.roll(x, 96, axis=1), pltpu.roll(x, 32, axis=1))


def _head_norm_rope(name, z, col0, n_heads, g, cos, sin, out_scale=None):
    n = z.shape[0]

    def fn(i, nrow, x, g, cos, sin):
        r = lax.rsqrt(jnp.mean(x * x, axis=-1, keepdims=True) + EPS)
        y = x * r * g
        y = y * cos + _swap_halves(y) * sin
        return y if out_scale is None else y * out_scale

    tm = _div(n, 1024, SUBLANES)
    return _rowwise(name, fn, n, tm, [_t(z, HEAD_DIM, col0), _b(g), _t(cos, cstride=0), _t(sin, cstride=0)],
                    [(BF16, n_heads * HEAD_DIM, HEAD_DIM)], ncol=n_heads)[0]


def _head_norm_rope_bwd(name, douts, z, col0, n_heads, g, cos, sin):
    n = z.shape[0]
    n_d = len(douts)

    def fn(i, nrow, *a):
        dy = a[0]
        for k in range(1, n_d):
            dy = dy + a[k]
        x, g, cos, sin = a[n_d:]
        dn = dy * cos + _swap_halves(dy * sin)
        r = lax.rsqrt(jnp.mean(x * x, axis=-1, keepdims=True) + EPS)
        w = dn * g
        dx = r * w - x * (r * r * r) * jnp.mean(x * w, axis=-1, keepdims=True)
        return dx, _sum8(dn * x * r)

    tm = _div(n, 1024, SUBLANES)
    ins = [_t(arr, HEAD_DIM, c0, cs) for arr, c0, cs in douts]
    ins += [_t(z, HEAD_DIM, col0), _b(g), _t(cos, cstride=0), _t(sin, cstride=0)]
    return _rowwise(name, fn, n, tm, ins, [(BF16, n_heads * HEAD_DIM, HEAD_DIM)],
                    [(SUBLANES, HEAD_DIM, HEAD_DIM, False)], ncol=n_heads)


def _halo(arr, w, col0):
    return [("p", arr, w, col0, CONV_HALO), _t(arr, w, col0), ("n", arr, w, col0, CONV_HALO)]


def _extend(i, nrow, prev, cur, nxt):
    prev = jnp.where(i > 0, prev, 0.0)
    nxt = jnp.where(i < nrow - 1, nxt, 0.0)
    return jnp.concatenate([prev, cur, nxt], axis=0)


def _shifted(ext, offset, tm):
    n = ext.shape[0]
    rolled = ext if offset == 0 else pltpu.roll(ext, (-offset) % n, axis=0)
    return rolled[CONV_HALO:CONV_HALO + tm]


def _conv_fwd(name, z, cw, w_dw, ln_g, ln_b):
    n = z.shape[0]
    tm = _row_tile(n, 4 * cw, CONV_HALO)

    def fn(i, nrow, ap, a, an, bp, b, bn, w, g, beta):
        a = a[...]
        ext = _extend(i, nrow, ap[...] * _sigmoid(bp[...]), a * _sigmoid(b[...]), an[...] * _sigmoid(bn[...]))
        conv = jnp.zeros_like(a)
        for k in range(CONV_KERNEL):
            conv = conv + _shifted(ext, k - CONV_KERNEL // 2, tm) * w[k:k + 1, :]
        xc = conv - jnp.mean(conv, axis=-1, keepdims=True)
        ln = xc * lax.rsqrt(jnp.mean(xc * xc, axis=-1, keepdims=True) + EPS) * g[...] + beta[...]
        return conv, ln * _sigmoid(ln)

    ins = _halo(z, cw, 0) + _halo(z, cw, 1) + [_b(w_dw), _b(ln_g), _b(ln_b)]
    return _rowwise(name, fn, n, tm, ins, [(F32, cw, cw), (BF16, cw, cw)], refs_to_fn=True)


def _conv_ln_bwd(name, ds, conv, ln_g, ln_b):
    n, cw = conv.shape

    def fn(i, nrow, ds, conv, g, beta):
        xc = conv - jnp.mean(conv, axis=-1, keepdims=True)
        rstd = lax.rsqrt(jnp.mean(xc * xc, axis=-1, keepdims=True) + EPS)
        xhat = xc * rstd
        ln = xhat * g + beta
        sg = _sigmoid(ln)
        dln = ds * (sg * (1.0 + ln * (1.0 - sg)))
        dxh = dln * g
        dconv = rstd * (dxh - jnp.mean(dxh, axis=-1, keepdims=True)
                        - xhat * jnp.mean(dxh * xhat, axis=-1, keepdims=True))
        return dconv, _sum8(dln * xhat), _sum8(dln)

    return _rowwise(name, fn, n, _row_tile(n, 2 * cw), [_t(ds), _t(conv), _b(ln_g), _b(ln_b)], [(F32, cw, cw)],
                    [(SUBLANES, cw, cw, True), (SUBLANES, cw, cw, True)])


def _conv_glu_bwd(name, dconv, z, cw, w_dw):
    n = z.shape[0]
    tm = _row_tile(n, 4 * cw, CONV_HALO)

    def fn(i, nrow, dp, d, dn, ap, a, an, bp, b, bn, w):
        d, a = d[...], a[...]
        dext = _extend(i, nrow, dp[...], d, dn[...])
        sg = _sigmoid(b[...])
        uext = _extend(i, nrow, ap[...] * _sigmoid(bp[...]), a * sg, an[...] * _sigmoid(bn[...]))
        tap = lax.broadcasted_iota(jnp.int32, (CONV_TAPS_PAD, cw), 0)
        du = jnp.zeros_like(d)
        dw = jnp.zeros((CONV_TAPS_PAD, cw), F32)
        for k in range(CONV_KERNEL):
            off = k - CONV_KERNEL // 2
            du = du + _shifted(dext, -off, tm) * w[k:k + 1, :]
            dw_k = jnp.sum(d * _shifted(uext, off, tm), axis=0, keepdims=True)
            dw = dw + jnp.where(tap == k, dw_k, 0.0)
        return du * sg, du * a * sg * (1.0 - sg), dw

    ins = _halo(dconv, cw, 0) + _halo(z, cw, 0) + _halo(z, cw, 1) + [_b(w_dw)]
    return _rowwise(name, fn, n, tm, ins, [(BF16, cw, cw), (BF16, cw, cw)], [(CONV_TAPS_PAD, cw, cw, True)],
                    refs_to_fn=True)


def _merge_fwd(name, z, d, y_c, y_a):
    n, hw = z.shape[0], d // 2

    def fn(i, nrow, gc, ga, yc, ya):
        return _sigmoid(gc) * yc + _sigmoid(ga) * ya

    ins = [_t(z, hw, 5), _t(z, hw, 7), _t(y_c, hw), _t(y_a, hw)]
    return _rowwise(name, fn, n, _row_tile(n, 2 * hw), ins, [(BF16, d, hw)], ncol=2)[0]


def _merge_bwd(name, dm, z, d, y_c, y_a):
    n, hw = z.shape[0], d // 2

    def fn(i, nrow, dm, gc, ga, yc, ya):
        sc, sa = _sigmoid(gc), _sigmoid(ga)
        return dm * yc * sc * (1.0 - sc), dm * ya * sa * (1.0 - sa), dm * sc, dm * sa

    ins = [_t(dm, hw), _t(z, hw, 5), _t(z, hw, 7), _t(y_c, hw), _t(y_a, hw)]
    return _rowwise(name, fn, n, _row_tile(n, 2 * hw), ins, [(BF16, d, hw)] * 4, ncol=2)


def _relu2(f):
    r = jnp.maximum(f, 0.0)
    return r * r


def _relu2_bwd(da, f):
    return da * (2.0 * jnp.maximum(f, 0.0))


def _head_loss(name, x2, pre, pp, target, g):
    n, d = x2.shape

    def fn(i, nrow, x2, pre, pp, t, g):
        x3 = x2 + _sigmoid(pre) * pp
        r = lax.rsqrt(jnp.mean(x3 * x3, axis=-1, keepdims=True) + EPS)
        xn = x3 * r
        e = xn * g - t
        part = 0.5 * jnp.sum(jnp.mean(e * e, axis=-1, keepdims=True), axis=0, keepdims=True)
        dy = e * (1.0 / d)
        w = dy * g
        dx3 = r * w - x3 * (r * r * r) * jnp.mean(x3 * w, axis=-1, keepdims=True)
        return dx3, _sum8(dy * xn), jnp.broadcast_to(part, (SUBLANES, LANES))

    ins = [_t(x2), _t(pre), _t(pp), _t(target), _b(g)]
    return _rowwise(name, fn, n, _row_tile(n, 2 * d), ins, [(F32, d, d)],
                    [(SUBLANES, d, d, True), (SUBLANES, LANES, LANES, True)])


def _ple_bwd(name, dx3, pre, pp):
    n, d = dx3.shape

    def fn(i, nrow, dx3, pre, pp):
        gate = _sigmoid(pre)
        return dx3 * gate, dx3 * pp * gate * (1.0 - gate)

    return _rowwise(name, fn, n, _row_tile(n, 2 * d), [_t(dx3), _t(pre), _t(pp)], [(BF16, d, d)] * 2)


def _group_sum(name, dvh, n_kv):
    n = dvh.shape[0]

    def fn(i, nrow, *a):
        s = a[0]
        for k in range(1, GROUP):
            s = s + a[k]
        return s

    ins = [_t(dvh, HEAD_DIM, k, GROUP) for k in range(GROUP)]
    return _rowwise(name, fn, n, _div(n, 1024, SUBLANES), ins, [(BF16, n_kv * HEAD_DIM, HEAD_DIM)], ncol=n_kv)[0]


MM_ROWS = 1024
MM_COLS = 1152
MM_REDUCE = 2048
MM_REDUCE_BLOCKS = 2304


def _accumulate(step, n_steps, part, acc, finish):
    if n_steps == 1:
        finish(part)
        return

    @pl.when(step == 0)
    def _():
        acc[...] = part

    @pl.when(jnp.logical_and(step > 0, step < n_steps - 1))
    def _():
        acc[...] += part

    @pl.when(step == n_steps - 1)
    def _():
        finish(acc[...] + part)


def _mm_nn(name, a, w3, out_dtype, residual=None, activation=None):
    m, k = a.shape
    nb, _, ns = w3.shape
    tm, tn, tk = _div(m, MM_ROWS, SUBLANES), _div(ns, MM_COLS, LANES), _div(k, MM_REDUCE, LANES)
    per, nk = ns // tn, k // tk
    has_res = residual is not None
    n_out = 1 if activation is None else 2

    def body(*refs):
        a_ref, w_ref = refs[:2]
        acc = refs[-1]
        outs = refs[-1 - n_out:-1]
        part = jnp.dot(a_ref[...].astype(BF16), w_ref[...].astype(BF16), preferred_element_type=F32)

        def finish(total):
            if has_res:
                total = total + refs[2][...]
            outs[0][...] = total.astype(out_dtype)
            if activation is not None:
                outs[1][...] = activation(total).astype(BF16)

        _accumulate(pl.program_id(2), nk, part, acc, finish)

    in_specs = [pl.BlockSpec((tm, tk), lambda i, n, kk: (i, kk)),
                pl.BlockSpec((None, tk, tn), lambda i, n, kk: (n // per, kk, n % per))]
    args = [a, w3]
    if has_res:
        in_specs.append(pl.BlockSpec((tm, tn), lambda i, n, kk: (i, n)))
        args.append(residual)
    out_dtypes = [out_dtype] + ([BF16] if activation is not None else [])
    res = pl.pallas_call(
        body, name=name, grid=(m // tm, nb * per, nk), in_specs=in_specs,
        out_specs=[pl.BlockSpec((tm, tn), lambda i, n, kk: (i, n)) for _ in out_dtypes],
        out_shape=[jax.ShapeDtypeStruct((m, nb * ns), dt) for dt in out_dtypes],
        scratch_shapes=[pltpu.VMEM((tm, tn) if nk > 1 else (SUBLANES, LANES), F32)],
        compiler_params=_params(("parallel", "parallel", "arbitrary")))(*args)
    return res[0] if activation is None else res


def _mm_nt(name, dc, w3, out_dtype, after=(), through=None):
    m = dc.shape[0]
    nb, k, ns = w3.shape
    tm, to = _div(m, MM_ROWS, SUBLANES), _div(k, MM_ROWS, LANES)
    if ns > MM_REDUCE_BLOCKS // 2:
        jb, tr = 1, _div(ns, MM_REDUCE, LANES)
    else:
        jb, tr = max(j for j in range(1, nb + 1) if nb % j == 0 and j * ns <= MM_REDUCE_BLOCKS), ns
    per = ns // tr
    nr = (nb // jb) * per

    def body(dc_ref, w_ref, *rest):
        o_ref, acc = rest[-2:]
        part = None
        for j in range(jb):
            term = lax.dot_general(dc_ref[:, j * tr:(j + 1) * tr].astype(BF16), w_ref[j].astype(BF16),
                                   (((1,), (1,)), ((), ())), preferred_element_type=F32)
            part = term if part is None else part + term

        def finish(total):
            if through is not None:
                total = through[0](total, rest[0][...])
            o_ref[...] = total.astype(out_dtype)

        _accumulate(pl.program_id(2), nr, part, acc, finish)

    in_specs = [pl.BlockSpec((tm, jb * tr), lambda i, o, r: (i, r)),
                pl.BlockSpec((jb, to, tr), lambda i, o, r: (r // per, o, r % per))]
    args = [dc, w3]
    if through is not None:
        in_specs.append(pl.BlockSpec((tm, to), lambda i, o, r: (i, o)))
        args.append(through[1])
    for token in after:
        in_specs.append(pl.BlockSpec(memory_space=pl.ANY))
        args.append(token)
    return pl.pallas_call(
        body, name=name, grid=(m // tm, k // to, nr), in_specs=in_specs,
        out_specs=pl.BlockSpec((tm, to), lambda i, o, r: (i, o)),
        out_shape=jax.ShapeDtypeStruct((m, k), out_dtype),
        scratch_shapes=[pltpu.VMEM((tm, to) if nr > 1 else (SUBLANES, LANES), F32)],
        compiler_params=_params(("parallel", "parallel", "arbitrary")))(*args)


def _mm_tn(name, a, dc, nb):
    m, k = a.shape
    ns = dc.shape[1] // nb
    tr, tn, to = _div(m, MM_REDUCE, LANES), _div(ns, MM_COLS, LANES), _div(k, MM_ROWS, LANES)
    per, nr = ns // tn, m // tr

    def body(a_ref, dc_ref, o_ref, acc):
        part = lax.dot_general(a_ref[...].astype(BF16), dc_ref[...].astype(BF16), (((0,), (0,)), ((), ())),
                               preferred_element_type=F32)

        def finish(total):
            o_ref[...] = total.astype(BF16)

        _accumulate(pl.program_id(2), nr, part, acc, finish)

    return pl.pallas_call(
        body, name=name, grid=(k // to, nb * per, nr),
        in_specs=[pl.BlockSpec((tr, to), lambda o, n, r: (r, o)),
                  pl.BlockSpec((tr, tn), lambda o, n, r: (r, n))],
        out_specs=pl.BlockSpec((None, to, tn), lambda o, n, r: (n // per, o, n % per)),
        out_shape=jax.ShapeDtypeStruct((nb, k, ns), BF16),
        scratch_shapes=[pltpu.VMEM((to, tn) if nr > 1 else (SUBLANES, LANES), F32)],
        compiler_params=_params(("parallel", "parallel", "arbitrary")))(a, dc)


QK_SCALE = HEAD_DIM ** -0.5
LOG2_E = 1.4426950408889634
LN_2 = 0.6931471805599453


def _columns(row, n):
    return jnp.transpose(jnp.broadcast_to(row, (LANES, n)))


FLASH_TILE = 1024


def _flash_fwd(qr, kr, vt, n_heads):
    t = qr.shape[0]
    tq, tk = _div(t, FLASH_TILE, LANES), _div(t, 2 * FLASH_TILE, LANES)
    nk = t // tk

    def body(q_ref, k_ref, vt_ref, o_ref, lse_ref, m_scr, l_scr, acc_scr):
        kb = pl.program_id(2)

        @pl.when(kb == 0)
        def _():
            m_scr[...] = jnp.full_like(m_scr, -1e30)
            l_scr[...] = jnp.zeros_like(l_scr)
            acc_scr[...] = jnp.zeros_like(acc_scr)

        s_t = lax.dot_general(k_ref[...], q_ref[...], (((1,), (1,)), ((), ())), preferred_element_type=F32)
        m_prev = m_scr[...]
        m_new = jnp.maximum(m_prev, jnp.max(s_t, axis=0, keepdims=True))
        alpha = jnp.exp2(m_prev - m_new)
        p_t = jnp.exp2(s_t - m_new)
        l_scr[...] = alpha * l_scr[...] + jnp.sum(p_t, axis=0, keepdims=True)
        acc_scr[...] = alpha * acc_scr[...] + jnp.dot(vt_ref[...], p_t.astype(BF16), preferred_element_type=F32)
        m_scr[...] = m_new

        @pl.when(kb == nk - 1)
        def _():
            l = l_scr[...]
            o_ref[...] = jnp.transpose(acc_scr[...] / l)
            lse_ref[...] = jnp.broadcast_to(m_scr[...] + jnp.log(l) * LOG2_E, (SUBLANES, tq))

    return pl.pallas_call(
        body, name="flash_fwd", grid=(n_heads, t // tq, nk),
        in_specs=[pl.BlockSpec((tq, HEAD_DIM), lambda h, qi, ki: (qi, h)),
                  pl.BlockSpec((tk, HEAD_DIM), lambda h, qi, ki: (ki, h // GROUP)),
                  pl.BlockSpec((HEAD_DIM, tk), lambda h, qi, ki: (h // GROUP, ki))],
        out_specs=[pl.BlockSpec((tq, HEAD_DIM), lambda h, qi, ki: (qi, h)),
                   pl.BlockSpec((None, SUBLANES, tq), lambda h, qi, ki: (h, 0, qi))],
        out_shape=[jax.ShapeDtypeStruct((t, n_heads * HEAD_DIM), F32),
                   jax.ShapeDtypeStruct((n_heads, SUBLANES, t), F32)],
        scratch_shapes=[pltpu.VMEM((1, tq), F32), pltpu.VMEM((1, tq), F32), pltpu.VMEM((HEAD_DIM, tq), F32)],
        compiler_params=_params(("parallel", "parallel", "arbitrary")))(qr, kr, vt)


def _attn_delta(do, o, n_heads):
    t = o.shape[0]
    tm = _div(t, 1024, LANES)

    def body(do_ref, o_ref, d_ref):
        prod = do_ref[...].astype(F32) * o_ref[...]
        row = jnp.sum(jnp.transpose(prod), axis=0, keepdims=True)
        d_ref[...] = jnp.broadcast_to(row, (SUBLANES, tm))

    return pl.pallas_call(
        body, name="attn_delta", grid=(n_heads, t // tm),
        in_specs=[pl.BlockSpec((tm, HEAD_DIM), lambda h, i: (i, h)),
                  pl.BlockSpec((tm, HEAD_DIM), lambda h, i: (i, h))],
        out_specs=pl.BlockSpec((None, SUBLANES, tm), lambda h, i: (h, 0, i)),
        out_shape=jax.ShapeDtypeStruct((n_heads, SUBLANES, t), F32),
        compiler_params=_params(("parallel", "parallel")))(do, o)


def _flash_bwd(qr, kr, z, v_col0, do, lse, delta, n_heads):
    t = qr.shape[0]
    tq, tk = _div(t, 1024, LANES), _div(t, 1024, LANES)
    nq = t // tq

    def body(q_ref, k_ref, v_ref, do_ref, lse_ref, dl_ref, dq_ref, dk_ref, dv_ref, dk_acc, dv_acc):
        kb, qb = pl.program_id(1), pl.program_id(2)
        q, k, do_ = q_ref[...], k_ref[...], do_ref[...]
        v = v_ref[...].astype(BF16)
        s_t = lax.dot_general(k, q, (((1,), (1,)), ((), ())), preferred_element_type=F32)
        p_t = jnp.exp2(s_t - lse_ref[0:1, :])
        dp_t = lax.dot_general(v, do_, (((1,), (1,)), ((), ())), preferred_element_type=F32)
        ds_t = (p_t * (dp_t - dl_ref[0:1, :])).astype(BF16)
        dv_c = jnp.dot(p_t.astype(BF16), do_, preferred_element_type=F32)
        dk_c = jnp.dot(ds_t, q, preferred_element_type=F32) * LN_2
        dq_c = lax.dot_general(ds_t, k, (((0,), (0,)), ((), ())), preferred_element_type=F32) * QK_SCALE

        @pl.when(qb == 0)
        def _():
            dk_acc[...] = dk_c
            dv_acc[...] = dv_c

        @pl.when(qb > 0)
        def _():
            dk_acc[...] += dk_c
            dv_acc[...] += dv_c

        @pl.when(qb == nq - 1)
        def _():
            dk_ref[...] = dk_acc[...]
            dv_ref[...] = dv_acc[...]

        rows = pl.ds(pl.multiple_of(qb * tq, tq), tq)

        @pl.when(kb == 0)
        def _():
            dq_ref[rows, :] = dq_c

        @pl.when(kb > 0)
        def _():
            dq_ref[rows, :] += dq_c

    wide = jax.ShapeDtypeStruct((t, n_heads * HEAD_DIM), F32)
    return pl.pallas_call(
        body, name="flash_bwd", grid=(n_heads, t // tk, nq),
        in_specs=[pl.BlockSpec((tq, HEAD_DIM), lambda h, kb, qb: (qb, h)),
                  pl.BlockSpec((tk, HEAD_DIM), lambda h, kb, qb: (kb, h // GROUP)),
                  pl.BlockSpec((tk, HEAD_DIM), lambda h, kb, qb: (kb, v_col0 + h // GROUP)),
                  pl.BlockSpec((tq, HEAD_DIM), lambda h, kb, qb: (qb, h)),
                  pl.BlockSpec((None, SUBLANES, tq), lambda h, kb, qb: (h, 0, qb)),
                  pl.BlockSpec((None, SUBLANES, tq), lambda h, kb, qb: (h, 0, qb))],
        out_specs=[pl.BlockSpec((t, HEAD_DIM), lambda h, kb, qb: (0, h)),
                   pl.BlockSpec((tk, HEAD_DIM), lambda h, kb, qb: (kb, h)),
                   pl.BlockSpec((tk, HEAD_DIM), lambda h, kb, qb: (kb, h))],
        out_shape=[wide, wide, wide],
        scratch_shapes=[pltpu.VMEM((tk, HEAD_DIM), F32), pltpu.VMEM((tk, HEAD_DIM), F32)],
        compiler_params=_params(("parallel", "arbitrary", "arbitrary")))(qr, kr, z, do, lse, delta)


def _position():
    x, y, c = lax.axis_index("x"), lax.axis_index("y"), lax.axis_index("c")
    return x, y, c


def _index(x, y, c):
    return 4 * x + 2 * y + c


def _all_gather(name, shard):
    a, b = shard.shape

    def body(x_ref, out_ref, send_sems, recv_sems, local_sem):
        x, y, c = _position()
        me, sibling = (x, y, c), (x, y, 1 - c)
        chips = [(1 - x, y), (x, 1 - y), (1 - x, 1 - y)]

        def block(px, py, pc):
            return out_ref.at[_index(px, py, pc)]

        def copy(k, blk, to, src=None):
            return pltpu.make_async_remote_copy(
                src_ref=block(*blk) if src is None else src, dst_ref=block(*blk),
                send_sem=send_sems.at[k], recv_sem=recv_sems.at[k], device_id=to, device_id_type=MESH)

        mine = pltpu.make_async_copy(x_ref, block(*me), local_sem)
        mine.start()
        first = [copy(0, me, sibling, src=x_ref)]
        first += [copy(1 + j, me, (*chip, c), src=x_ref) for j, chip in enumerate(chips)]
        for cp in first:
            cp.start()
        passed = [copy(4 + j, (*chip, c), sibling) for j, chip in enumerate(chips)]
        for j, chip in enumerate(chips):
            copy(1 + j, (*chip, c), me).wait_recv()
            passed[j].start()
        copy(0, sibling, me).wait_recv()
        for j, chip in enumerate(chips):
            copy(4 + j, (*chip, 1 - c), me).wait_recv()
        for cp in first + passed:
            cp.wait_send()
        mine.wait()

    return pl.pallas_call(
        body, name=name, out_shape=jax.ShapeDtypeStruct((N_DEV, a, b), shard.dtype),
        in_specs=[pl.BlockSpec(memory_space=pltpu.HBM)], out_specs=pl.BlockSpec(memory_space=pltpu.HBM),
        scratch_shapes=[pltpu.SemaphoreType.DMA((7,)), pltpu.SemaphoreType.DMA((7,)), pltpu.SemaphoreType.DMA],
    )(shard)


HBM_SPEC = pl.BlockSpec(memory_space=pltpu.HBM)
SEM_SPEC = pl.BlockSpec(memory_space=pltpu.SEMAPHORE)
N_PEERS = N_DEV - 1
PEER_FLIPS = [(fx, fy, fc) for fx in (0, 1) for fy in (0, 1) for fc in (0, 1)][1:]


def _peers(x, y, c):
    return [((1 - x) if fx else x, (1 - y) if fy else y, (1 - c) if fc else c) for fx, fy, fc in PEER_FLIPS]


def _exchange_copies(g_ref, land_ref, send_sems, recv_sems):
    x, y, c = _position()
    me = _index(x, y, c)
    peers = _peers(x, y, c)
    blocked = len(g_ref.shape) == 3

    def copy(k, src_block, dst_block):
        return pltpu.make_async_remote_copy(
            src_ref=g_ref.at[src_block] if blocked else g_ref, dst_ref=land_ref.at[dst_block],
            send_sem=send_sems.at[k], recv_sem=recv_sems.at[k], device_id=peers[k], device_id_type=MESH)

    sent = [copy(k, _index(*peer), me) for k, peer in enumerate(peers)]
    arriving = [copy(k, me, _index(*peer)) for k, peer in enumerate(peers)]
    own = pltpu.make_async_copy(g_ref.at[me] if blocked else g_ref, land_ref.at[me], send_sems.at[N_PEERS])
    return sent, arriving, own


def _exchange_start(name, g3, after=()):
    def body(g_ref, land_ref, *rest):
        send_sems, recv_sems, _, _, token = rest[len(after):]
        sent, _, own = _exchange_copies(g_ref, land_ref, send_sems, recv_sems)
        for cp in sent:
            cp.start()
        own.start()
        token[...] = jnp.zeros_like(token)

    land_shape = g3.shape if g3.ndim == 3 else (N_DEV,) + g3.shape
    land = lax.empty(land_shape, g3.dtype)
    send_sems, recv_sems, g_thru, land_thru, token = pl.pallas_call(
        body, name=name,
        out_shape=(pltpu.SemaphoreType.DMA((N_PEERS + 1,)), pltpu.SemaphoreType.DMA((N_PEERS,)),
                   pltpu.HBM(g3.shape, g3.dtype), pltpu.HBM(land_shape, g3.dtype),
                   jax.ShapeDtypeStruct((SUBLANES, LANES), F32)),
        in_specs=(HBM_SPEC, HBM_SPEC) + (pl.BlockSpec(memory_space=pl.ANY),) * len(after),
        out_specs=(SEM_SPEC, SEM_SPEC, HBM_SPEC, HBM_SPEC, pl.BlockSpec(memory_space=pltpu.VMEM)),
        input_output_aliases={0: 2, 1: 3},
        compiler_params=pltpu.CompilerParams(has_side_effects=pltpu.SideEffectType.DATAFLOW_SIDE_EFFECTING),
    )(pltpu.with_memory_space_constraint(g3, pltpu.HBM), pltpu.with_memory_space_constraint(land, pltpu.HBM), *after)
    return (send_sems, recv_sems, g_thru, land_thru), token


def _exchange_wait(name, handle, after):
    send_sems, recv_sems, g_thru, land_thru = handle

    def body(g_ref, land_ref, send_sems, recv_sems, after_ref, g_dead, land_out):
        sent, arriving, own = _exchange_copies(g_ref, land_ref, send_sems, recv_sems)
        for cp in sent:
            cp.wait_send()
        for cp in arriving:
            cp.wait_recv()
        own.wait()

    return pl.pallas_call(
        body, name=name,
        out_shape=(pltpu.HBM(g_thru.shape, g_thru.dtype), pltpu.HBM(land_thru.shape, land_thru.dtype)),
        in_specs=(HBM_SPEC, HBM_SPEC, SEM_SPEC, SEM_SPEC, pl.BlockSpec(memory_space=pl.ANY)),
        out_specs=(HBM_SPEC, HBM_SPEC), input_output_aliases={0: 0, 1: 1},
        compiler_params=pltpu.CompilerParams(has_side_effects=pltpu.SideEffectType.DATAFLOW_SIDE_EFFECTING),
    )(g_thru, land_thru, send_sems, recv_sems, after)[1]


def _adamw(w, g, m, v):
    m = ADAM_B1 * m + (1.0 - ADAM_B1) * g
    v = ADAM_B2 * v + (1.0 - ADAM_B2) * (g * g)
    m_hat = m / (1.0 - ADAM_B1 ** ADAM_STEP)
    v_hat = v / (1.0 - ADAM_B2 ** ADAM_STEP)
    delta = -ADAM_LR * (m_hat / (jnp.sqrt(v_hat) + ADAM_EPS) + ADAM_WD * w)
    return delta, m, v


def _adamw_shard(name, land, w, m, v):
    _, a, b = land.shape
    tm = _div(a, max(16, ROW_TILE_ELEMS // (2 * b)), 16)
    per = a // tm

    def fn(i, nrow, *t):
        g = t[0].astype(F32)
        for s in range(1, N_DEV):
            g = g + t[s].astype(F32)
        w, m, v = t[N_DEV:]
        return (g,) + _adamw(w, g, m, v)

    land2 = land.reshape(N_DEV * a, b)
    ins = [_t(land2, row0=s * per) for s in range(N_DEV)] + [_t(w), _t(m), _t(v)]
    return _rowwise(name, fn, a, tm, ins, [(F32, b, b)] * 4)


def _adamw_small(name, parts, w, m, v):
    rows, n = w.shape

    def body(p_ref, w_ref, m_ref, v_ref, g_out, d_out, m_out, v_out):
        g = p_ref[0]
        for s in range(1, parts.shape[0]):
            g = g + p_ref[s]
        g = jnp.sum(g, axis=0, keepdims=True) if rows == 1 else g[0:rows, :]
        delta, m_new, v_new = _adamw(w_ref[...], g, m_ref[...], v_ref[...])
        g_out[...] = g
        d_out[...] = delta
        m_out[...] = m_new
        v_out[...] = v_new

    return pl.pallas_call(body, name=name, out_shape=[jax.ShapeDtypeStruct((rows, n), F32)] * 4)(parts, w, m, v)


def _rope_tables(t):
    pos = jnp.arange(t, dtype=jnp.int32)
    half = HEAD_DIM // 4
    inv_freq = ROPE_THETA ** (-jnp.arange(0, 2 * half, 2, dtype=F32) / (2 * half))
    ang_r = (pos // GRID_W).astype(F32)[:, None] * inv_freq[None, :]
    ang_c = (pos % GRID_W).astype(F32)[:, None] * inv_freq[None, :]
    cos = jnp.concatenate([jnp.cos(ang_r)] * 2 + [jnp.cos(ang_c)] * 2, axis=-1)
    sin = jnp.concatenate([-jnp.sin(ang_r), jnp.sin(ang_r), -jnp.sin(ang_c), jnp.sin(ang_c)], axis=-1)
    return cos, sin


def _gather_weight(name, w, cols):
    g = _all_gather(name, w[0].astype(BF16))
    return g if cols else g.reshape(1, N_DEV * g.shape[1], g.shape[2])


def kernel(x, p, norm_mix, w_in, w_dw, conv_ln_g, conv_ln_b, w_conv_proj, q_norm, k_norm, w_attn_proj, w_out, norm_ffn, w_ff1, w_ff2, norm_ple, w_ple_gate, w_ple_proj, norm_final, loss_target, m_norm_mix, m_w_in, m_w_dw, m_conv_ln_g, m_conv_ln_b, m_w_conv_proj, m_q_norm, m_k_norm, m_w_attn_proj, m_w_out, m_norm_ffn, m_w_ff1, m_w_ff2, m_norm_ple, m_w_ple_gate, m_w_ple_proj, m_norm_final, v_norm_mix, v_w_in, v_w_dw, v_conv_ln_g, v_conv_ln_b, v_w_conv_proj, v_q_norm, v_k_norm, v_w_attn_proj, v_w_out, v_norm_ffn, v_w_ff1, v_w_ff2, v_norm_ple, v_w_ple_gate, v_w_ple_proj, v_norm_final):
    t, d = x.shape[1], x.shape[2]
    cw = d // 2
    n_heads = d // HEAD_DIM
    n_kv = n_heads // GROUP
    col_q, col_k, col_v = d // HEAD_DIM, 2 * d // HEAD_DIM, (2 * d + d // 4) // HEAD_DIM
    x0, pe, tgt = x[0], p[0, 0], loss_target[0]
    g_final = norm_final.reshape(1, d)
    me = _index(*_position())

    wg_in = _gather_weight("ag_w_in", w_in, True)
    dw_pad = jnp.pad(w_dw[0], ((0, CONV_TAPS_PAD - CONV_KERNEL), (0, 0)))
    dw_all = _all_gather("ag_w_dw", dw_pad)
    dw_full = dw_all.transpose(1, 0, 2).reshape(CONV_TAPS_PAD, cw)
    cos, sin = _rope_tables(t)

    def fetch(name, w):
        return _exchange_start("ag_start_" + name, w[0].astype(BF16), after=(wg_in, dw_all))

    def arrived(name, handle, cols, after):
        g = _exchange_wait("ag_wait_" + name, handle, after)
        return g if cols else g.reshape(1, N_DEV * g.shape[1], g.shape[2])

    on_cp, t_cp = fetch("w_conv_proj", w_conv_proj)
    on_ap, t_ap = fetch("w_attn_proj", w_attn_proj)
    on_out, t_out = fetch("w_out", w_out)
    on_ff1, t_ff1 = fetch("w_ff1", w_ff1)
    on_ff2, t_ff2 = fetch("w_ff2", w_ff2)
    on_pg, t_pg = fetch("w_ple_gate", w_ple_gate)
    on_pp, t_pp = fetch("w_ple_proj", w_ple_proj)

    h1 = _rms_fwd("rms_mix", x0, norm_mix, after=(t_cp, t_ap, t_out, t_ff1, t_ff2, t_pg, t_pp))
    z = _mm_nn("mm_in", h1, wg_in, F32)
    conv, s_c = _conv_fwd("conv_fwd", z, cw, dw_full, conv_ln_g, conv_ln_b)
    wg_cp = arrived("w_conv_proj", on_cp, True, s_c)
    y_c = _mm_nn("mm_conv_proj", s_c, wg_cp, F32)
    qr = _head_norm_rope("q_prep", z, col_q, n_heads, q_norm, cos, sin, out_scale=QK_SCALE * LOG2_E)
    kr = _head_norm_rope("k_prep", z, col_k, n_kv, k_norm, cos, sin)
    vt = z[:, col_v * HEAD_DIM:(col_v + n_kv) * HEAD_DIM].T.astype(BF16)
    o, lse = _flash_fwd(qr, kr, vt, n_heads)
    wg_ap = arrived("w_attn_proj", on_ap, False, o)
    y_a = _mm_nn("mm_attn_proj", o, wg_ap, F32)
    merged = _merge_fwd("merge_fwd", z, d, y_c, y_a)
    wg_out = arrived("w_out", on_out, False, merged)
    x1 = _mm_nn("mm_out", merged, wg_out, F32, residual=x0)
    h2 = _rms_fwd("rms_ffn", x1, norm_ffn)
    wg_ff1 = arrived("w_ff1", on_ff1, True, h2)
    f, act = _mm_nn("mm_ff1", h2, wg_ff1, F32, activation=_relu2)
    wg_ff2 = arrived("w_ff2", on_ff2, False, act)
    x2 = _mm_nn("mm_ff2", act, wg_ff2, F32, residual=x1)
    hp = _rms_fwd("rms_ple", x2, norm_ple)
    wg_pg = arrived("w_ple_gate", on_pg, False, hp)
    pre = _mm_nn("mm_ple_gate", hp, wg_pg, F32)
    wg_pp = arrived("w_ple_proj", on_pp, True, pre)
    pp = _mm_nn("mm_ple_proj", pe, wg_pp, F32)

    dx3, dg_final, loss_part = _head_loss("head_loss", x2, pre, pp, tgt, g_final)
    loss = lax.psum(loss_part[0, 0], ("x", "y", "c"))
    def send(name, gw):
        blocks = gw if gw.shape[0] == N_DEV else gw.reshape(N_DEV, gw.shape[1] // N_DEV, gw.shape[2])
        return _exchange_start("rs_start_" + name, blocks)

    dpp, dpre = _ple_bwd("ple_bwd", dx3, pre, pp)
    sent_pp, tok_pp = send("w_ple_proj", _mm_tn("mm_d_ple_proj", pe, dpp, N_DEV))
    sent_pg, tok = send("w_ple_gate", _mm_tn("mm_d_ple_gate", hp, dpre, 1))
    dhp = _mm_nt("mm_dhp", dpre, wg_pg, F32, after=(tok_pp, tok))
    dx2, dg_ple = _rms_bwd("rms_ple_bwd", x2, norm_ple, dhp, dx3)
    sent_ff2, tok = send("w_ff2", _mm_tn("mm_d_ff2", act, dx2, 1))
    df = _mm_nt("mm_dact", dx2, wg_ff2, BF16, after=(tok,), through=(_relu2_bwd, f))
    sent_ff1, tok = send("w_ff1", _mm_tn("mm_d_ff1", h2, df, N_DEV))
    dh2 = _mm_nt("mm_dh2", df, wg_ff1, F32, after=(tok,))
    dx1, dg_ffn = _rms_bwd("rms_ffn_bwd", x1, norm_ffn, dh2, dx2)
    sent_out, tok = send("w_out", _mm_tn("mm_d_out", merged, dx1, 1))
    dmerged = _mm_nt("mm_dmerged", dx1, wg_out, F32, after=(tok,))
    dgc, dga, dyc, dya = _merge_bwd("merge_bwd", dmerged, z, d, y_c, y_a)
    sent_cp, tok = send("w_conv_proj", _mm_tn("mm_d_conv_proj", s_c, dyc, N_DEV))
    ds_c = _mm_nt("mm_ds_c", dyc, wg_cp, F32, after=(tok,))
    dconv, dg_ln, db_ln = _conv_ln_bwd("conv_ln_bwd", ds_c, conv, conv_ln_g, conv_ln_b)
    dca, dcb, dg_dw = _conv_glu_bwd("conv_glu_bwd", dconv, z, cw, dw_full)
    sent_ap, tok = send("w_attn_proj", _mm_tn("mm_d_attn_proj", o, dya, 1))
    do = _mm_nt("mm_do", dya, wg_ap, BF16, after=(tok,))
    delta = _attn_delta(do, o, n_heads)
    dqr, dkh, dvh = _flash_bwd(qr, kr, z, col_v, do, lse, delta, n_heads)
    dq, dg_q = _head_norm_rope_bwd("q_prep_bwd", [(dqr, 0, 1)], z, col_q, n_heads, q_norm, cos, sin)
    dk, dg_k = _head_norm_rope_bwd("k_prep_bwd", [(dkh, g, GROUP) for g in range(GROUP)], z, col_k, n_kv, k_norm,
                                   cos, sin)
    dv = _group_sum("dv_sum", dvh, n_kv)
    dz = jnp.concatenate([dca, dcb, dq, dk, dv, dgc, dga], axis=1)
    sent_in, tok = send("w_in", _mm_tn("mm_d_in", h1, dz, N_DEV))
    dh1 = _mm_nt("mm_dh1", dz, wg_in, F32, after=(tok,))
    grad_x, dg_mix = _rms_bwd("rms_mix_bwd", x0, norm_mix, dh1, dx1)

    def pad(a):
        return jnp.pad(a, ((0, 0), (0, d - a.shape[1])))

    small = [dg_mix, dg_ffn, dg_ple, dg_final, dg_ln, db_ln, dg_q, dg_k, dg_dw]
    packed = _all_gather("ag_small", jnp.concatenate([pad(a) for a in small], axis=0))
    parts, row = [], 0
    for a in small:
        parts.append(packed[:, row:row + a.shape[0], :a.shape[1]])
        row += a.shape[0]
    p_mix, p_ffn, p_ple, p_final, p_lng, p_lnb, p_q, p_k, p_dw = parts
    p_dw = lax.dynamic_slice_in_dim(p_dw, me * (cw // N_DEV), cw // N_DEV, axis=2)

    def little(name, part, w, m, v):
        rows, n = (1 if w.ndim < 3 else w.shape[1]), w.shape[-1]
        padded = rows if rows == 1 else part.shape[1]

        def two_d(a):
            return jnp.pad(a.reshape(rows, n), ((0, padded - rows), (0, 0)))

        res = _adamw_small("adamw_" + name, part, two_d(w), two_d(m), two_d(v))
        return [r[:rows].reshape(w.shape) for r in res]

    done = {}
    after = packed
    for name, sent, w, m, v in [
            ("w_ple_proj", sent_pp, w_ple_proj, m_w_ple_proj, v_w_ple_proj),
            ("w_ple_gate", sent_pg, w_ple_gate, m_w_ple_gate, v_w_ple_gate),
            ("w_ff2", sent_ff2, w_ff2, m_w_ff2, v_w_ff2),
            ("w_ff1", sent_ff1, w_ff1, m_w_ff1, v_w_ff1),
            ("w_out", sent_out, w_out, m_w_out, v_w_out),
            ("w_conv_proj", sent_cp, w_conv_proj, m_w_conv_proj, v_w_conv_proj),
            ("w_attn_proj", sent_ap, w_attn_proj, m_w_attn_proj, v_w_attn_proj),
            ("w_in", sent_in, w_in, m_w_in, v_w_in)]:
        land = _exchange_wait("rs_wait_" + name, sent, after)
        res = _adamw_shard("adamw_" + name, land, w[0], m[0], v[0])
        after = res[0]
        done[name] = [r.reshape(w.shape) for r in res]

    results = [
        little("norm_mix", p_mix, norm_mix, m_norm_mix, v_norm_mix),
        done["w_in"],
        little("w_dw", p_dw, w_dw, m_w_dw, v_w_dw),
        little("conv_ln_g", p_lng, conv_ln_g, m_conv_ln_g, v_conv_ln_g),
        little("conv_ln_b", p_lnb, conv_ln_b, m_conv_ln_b, v_conv_ln_b),
        done["w_conv_proj"],
        little("q_norm", p_q, q_norm, m_q_norm, v_q_norm),
        little("k_norm", p_k, k_norm, m_k_norm, v_k_norm),
        done["w_attn_proj"],
        done["w_out"],
        little("norm_ffn", p_ffn, norm_ffn, m_norm_ffn, v_norm_ffn),
        done["w_ff1"],
        done["w_ff2"],
        little("norm_ple", p_ple, norm_ple, m_norm_ple, v_norm_ple),
        done["w_ple_gate"],
        done["w_ple_proj"],
        little("norm_final", p_final, norm_final, m_norm_final, v_norm_final),
    ]
    grads, deltas, new_m, new_v = zip(*results)
    return (loss, grad_x[None], *grads, *deltas, *new_m, *new_v)
```

```python
import functools

import jax
import jax.numpy as jnp
from jax import lax
from jax.experimental import pallas as pl
from jax.experimental.pallas import tpu as pltpu

F32 = jnp.float32
BF16 = jnp.bfloat16

EPS = 1e-6
HEAD_DIM = 128
GROUP = 4
CONV_KERNEL = 31
CONV_HALO = 16
CONV_TAPS_PAD = 32
GRID_W = 64
ROPE_THETA = 10000.0
ADAM_LR, ADAM_B1, ADAM_B2, ADAM_EPS, ADAM_WD, ADAM_STEP = 0.001, 0.9, 0.999, 1e-08, 0.01, 10

N_DEV = 8
MESH = pl.DeviceIdType.MESH
SUBLANES = 8
LANES = 128
VMEM_LIMIT = 48 * 1024 * 1024
ROW_TILE_ELEMS = 512 * 1024


def _div(n, pref, mult=1):
    if n <= pref:
        return n
    for d in range(pref, 0, -1):
        if n % d == 0 and d % mult == 0:
            return d
    return n


def _params(sem):
    return pltpu.CompilerParams(dimension_semantics=sem, vmem_limit_bytes=VMEM_LIMIT)


def _sum8(a):
    tm, w = a.shape
    return a.reshape(tm // SUBLANES, SUBLANES, w).sum(axis=0)


def _sigmoid(x):
    return jax.nn.sigmoid(x)


def _rowwise(name, fn, n_rows, tm, ins, outs, accs=(), ncol=1, refs_to_fn=False, after=()):
    nrow = n_rows // tm
    in_specs, arrays = [], []
    for spec in ins:
        kind, arr, w, c0 = spec[:4]
        if kind == "t":
            cs, r0 = spec[4], spec[5]
            in_specs.append(pl.BlockSpec((tm, w), lambda j, i, c0=c0, cs=cs, r0=r0: (r0 + i, c0 + cs * j)))
        elif kind == "p":
            h = spec[4]
            r = tm // h
            in_specs.append(pl.BlockSpec((h, w), lambda j, i, c0=c0, r=r: (jnp.maximum(i * r - 1, 0), c0 + j)))
        elif kind == "n":
            h = spec[4]
            r = tm // h
            last = n_rows // h - 1
            in_specs.append(
                pl.BlockSpec((h, w), lambda j, i, c0=c0, r=r, last=last: (jnp.minimum((i + 1) * r, last), c0 + j)))
        else:
            cs = spec[4]
            in_specs.append(pl.BlockSpec((arr.shape[0], w), lambda j, i, c0=c0, cs=cs: (0, c0 + cs * j)))
        arrays.append(arr)
    out_shape, out_specs = [], []
    for dtype, total, w in outs:
        out_shape.append(jax.ShapeDtypeStruct((n_rows, total), dtype))
        out_specs.append(pl.BlockSpec((tm, w), lambda j, i: (i, j)))
    for rows, total, w, follow in accs:
        out_shape.append(jax.ShapeDtypeStruct((rows, total), F32))
        out_specs.append(pl.BlockSpec((rows, w), (lambda j, i: (0, j)) if follow else (lambda j, i: (0, 0))))
    n_in, n_out, n_acc = len(ins), len(outs), len(accs)
    for token in after:
        in_specs.append(pl.BlockSpec(memory_space=pl.ANY))
        arrays.append(token)
    out0 = n_in + len(after)

    def body(*refs):
        j, i = pl.program_id(0), pl.program_id(1)
        res = fn(i, nrow, *(refs[:n_in] if refs_to_fn else [r[...] for r in refs[:n_in]]))
        if not isinstance(res, (tuple, list)):
            res = (res,)
        for k in range(n_out):
            refs[out0 + k][...] = res[k].astype(outs[k][0])
        for k in range(n_acc):
            ref, term = refs[out0 + n_out + k], res[n_out + k]
            first = (i == 0) if accs[k][3] else jnp.logical_and(i == 0, j == 0)

            @pl.when(first)
            def _():
                ref[...] = term

            @pl.when(jnp.logical_not(first))
            def _():
                ref[...] += term

    res = pl.pallas_call(
        body, name=name, grid=(ncol, nrow), in_specs=in_specs, out_specs=out_specs, out_shape=out_shape,
        compiler_params=_params(("arbitrary", "arbitrary")))(*arrays)
    return res


def _row_tile(n_rows, width, mult=SUBLANES):
    return _div(n_rows, max(mult, ROW_TILE_ELEMS // width), mult)


def _t(arr, w=None, col0=0, cstride=1, row0=0):
    return ("t", arr, arr.shape[1] if w is None else w, col0, cstride, row0)


def _b(arr, w=None, col0=0, cstride=0):
    return ("b", arr, arr.shape[1] if w is None else w, col0, cstride)


def _rms_fwd(name, x, g, after=()):
    n, d = x.shape

    def fn(i, nrow, x, g):
        r = lax.rsqrt(jnp.mean(x * x, axis=-1, keepdims=True) + EPS)
        return x * r * g

    return _rowwise(name, fn, n, _row_tile(n, d), [_t(x), _b(g)], [(BF16, d, d)], after=after)[0]


def _rms_bwd(name, x, g, dh, dres):
    n, d = x.shape

    def fn(i, nrow, x, g, dh, dres):
        r = lax.rsqrt(jnp.mean(x * x, axis=-1, keepdims=True) + EPS)
        w = dh * g
        dx = dres + r * w - x * (r * r * r) * jnp.mean(x * w, axis=-1, keepdims=True)
        return dx, _sum8(dh * x * r)

    return _rowwise(name, fn, n, _row_tile(n, 2 * d), [_t(x), _b(g), _t(dh), _t(dres)], [(F32, d, d)],
                    [(SUBLANES, d, d, True)])


def _swap_halves(x):
    lane = lax.broadcasted_iota(jnp.int32, x.shape, 1)
    return jnp.where(lane % 64 < 32, pltpu.roll(x, 96, axis=1), pltpu.roll(x, 32, axis=1))


def _head_norm_rope(name, z, col0, n_heads, g, cos, sin, out_scale=None, heads_per_step=1):
    n, hp = z.shape[0], heads_per_step
    assert col0 % hp == 0 and n_heads % hp == 0

    def fn(i, nrow, x, g, cos, sin):
        ys = []
        for h in range(hp):
            xh = x[:, h * HEAD_DIM:(h + 1) * HEAD_DIM]
            r = lax.rsqrt(jnp.mean(xh * xh, axis=-1, keepdims=True) + EPS)
            y = xh * r * g
            y = y * cos + _swap_halves(y) * sin
            ys.append(y if out_scale is None else y * out_scale)
        return ys[0] if hp == 1 else jnp.concatenate(ys, axis=1)

    tm = _div(n, 1024, SUBLANES)
    w = hp * HEAD_DIM
    return _rowwise(name, fn, n, tm, [_t(z, w, col0 // hp), _b(g), _t(cos, cstride=0), _t(sin, cstride=0)],
                    [(BF16, n_heads * HEAD_DIM, w)], ncol=n_heads // hp)[0]


def _head_norm_rope_bwd(name, douts, z, col0, n_heads, g, cos, sin, heads_per_step=1):
    n, hp = z.shape[0], heads_per_step
    n_d = len(douts)
    assert col0 % hp == 0 and n_heads % hp == 0 and (hp == 1 or all(c0 % hp == 0 and cs == 1 for _, c0, cs in douts))

    def fn(i, nrow, *a):
        dy = a[0]
        for k in range(1, n_d):
            dy = dy + a[k]
        x, g, cos, sin = a[n_d:]
        dxs, dg = [], None
        for h in range(hp):
            xh, dyh = x[:, h * HEAD_DIM:(h + 1) * HEAD_DIM], dy[:, h * HEAD_DIM:(h + 1) * HEAD_DIM]
            dn = dyh * cos + _swap_halves(dyh * sin)
            r = lax.rsqrt(jnp.mean(xh * xh, axis=-1, keepdims=True) + EPS)
            w = dn * g
            dxs.append(r * w - xh * (r * r * r) * jnp.mean(xh * w, axis=-1, keepdims=True))
            term = _sum8(dn * xh * r)
            dg = term if dg is None else dg + term
        return (dxs[0] if hp == 1 else jnp.concatenate(dxs, axis=1)), dg

    tm = _div(n, 1024, SUBLANES)
    w = hp * HEAD_DIM
    ins = [_t(arr, w, c0 // hp, cs) for arr, c0, cs in douts]
    ins += [_t(z, w, col0 // hp), _b(g), _t(cos, cstride=0), _t(sin, cstride=0)]
    return _rowwise(name, fn, n, tm, ins, [(BF16, n_heads * HEAD_DIM, w)],
                    [(SUBLANES, HEAD_DIM, HEAD_DIM, False)], ncol=n_heads // hp)


def _halo(arr, w, col0):
    return [("p", arr, w, col0, CONV_HALO), _t(arr, w, col0), ("n", arr, w, col0, CONV_HALO)]


def _extend(i, nrow, prev, cur, nxt):
    prev = jnp.where(i > 0, prev, 0.0)
    nxt = jnp.where(i < nrow - 1, nxt, 0.0)
    return jnp.concatenate([prev, cur, nxt], axis=0)


def _shifted(ext, offset, tm):
    n = ext.shape[0]
    rolled = ext if offset == 0 else pltpu.roll(ext, (-offset) % n, axis=0)
    return rolled[CONV_HALO:CONV_HALO + tm]


def _conv_fwd(name, z, cw, w_dw, ln_g, ln_b):
    n = z.shape[0]
    tm = _row_tile(n, 4 * cw, CONV_HALO)

    def fn(i, nrow, ap, a, an, bp, b, bn, w, g, beta):
        a = a[...]
        ext = _extend(i, nrow, ap[...] * _sigmoid(bp[...]), a * _sigmoid(b[...]), an[...] * _sigmoid(bn[...]))
        conv = jnp.zeros_like(a)
        for k in range(CONV_KERNEL):
            conv = conv + _shifted(ext, k - CONV_KERNEL // 2, tm) * w[k:k + 1, :]
        xc = conv - jnp.mean(conv, axis=-1, keepdims=True)
        ln = xc * lax.rsqrt(jnp.mean(xc * xc, axis=-1, keepdims=True) + EPS) * g[...] + beta[...]
        return conv, ln * _sigmoid(ln)

    ins = _halo(z, cw, 0) + _halo(z, cw, 1) + [_b(w_dw), _b(ln_g), _b(ln_b)]
    return _rowwise(name, fn, n, tm, ins, [(F32, cw, cw), (BF16, cw, cw)], refs_to_fn=True)


def _conv_ln_bwd(name, ds, conv, ln_g, ln_b):
    n, cw = conv.shape

    def fn(i, nrow, ds, conv, g, beta):
        xc = conv - jnp.mean(conv, axis=-1, keepdims=True)
        rstd = lax.rsqrt(jnp.mean(xc * xc, axis=-1, keepdims=True) + EPS)
        xhat = xc * rstd
        ln = xhat * g + beta
        sg = _sigmoid(ln)
        dln = ds * (sg * (1.0 + ln * (1.0 - sg)))
        dxh = dln * g
        dconv = rstd * (dxh - jnp.mean(dxh, axis=-1, keepdims=True)
                        - xhat * jnp.mean(dxh * xhat, axis=-1, keepdims=True))
        return dconv, _sum8(dln * xhat), _sum8(dln)

    return _rowwise(name, fn, n, _row_tile(n, 2 * cw), [_t(ds), _t(conv), _b(ln_g), _b(ln_b)], [(F32, cw, cw)],
                    [(SUBLANES, cw, cw, True), (SUBLANES, cw, cw, True)])


def _conv_glu_bwd(name, dconv, z, cw, w_dw):
    n = z.shape[0]
    tm = _row_tile(n, 4 * cw, CONV_HALO)

    def fn(i, nrow, dp, d, dn, ap, a, an, bp, b, bn, w):
        d, a = d[...], a[...]
        dext = _extend(i, nrow, dp[...], d, dn[...])
        sg = _sigmoid(b[...])
        uext = _extend(i, nrow, ap[...] * _sigmoid(bp[...]), a * sg, an[...] * _sigmoid(bn[...]))
        tap = lax.broadcasted_iota(jnp.int32, (CONV_TAPS_PAD, cw), 0)
        du = jnp.zeros_like(d)
        dw = jnp.zeros((CONV_TAPS_PAD, cw), F32)
        for k in range(CONV_KERNEL):
            off = k - CONV_KERNEL // 2
            du = du + _shifted(dext, -off, tm) * w[k:k + 1, :]
            dw_k = jnp.sum(d * _shifted(uext, off, tm), axis=0, keepdims=True)
            dw = dw + jnp.where(tap == k, dw_k, 0.0)
        return du * sg, du * a * sg * (1.0 - sg), dw

    ins = _halo(dconv, cw, 0) + _halo(z, cw, 0) + _halo(z, cw, 1) + [_b(w_dw)]
    return _rowwise(name, fn, n, tm, ins, [(BF16, cw, cw), (BF16, cw, cw)], [(CONV_TAPS_PAD, cw, cw, True)],
                    refs_to_fn=True)


def _merge_fwd(name, z, d, y_c, y_a):
    n, hw = z.shape[0], d // 2

    def fn(i, nrow, gc, ga, yc, ya):
        return _sigmoid(gc) * yc + _sigmoid(ga) * ya

    ins = [_t(z, hw, 5), _t(z, hw, 7), _t(y_c, hw), _t(y_a, hw)]
    return _rowwise(name, fn, n, _row_tile(n, 2 * hw), ins, [(BF16, d, hw)], ncol=2)[0]


def _merge_bwd(name, dm, z, d, y_c, y_a):
    n, hw = z.shape[0], d // 2

    def fn(i, nrow, dm, gc, ga, yc, ya):
        sc, sa = _sigmoid(gc), _sigmoid(ga)
        return dm * yc * sc * (1.0 - sc), dm * ya * sa * (1.0 - sa), dm * sc, dm * sa

    ins = [_t(dm, hw), _t(z, hw, 5), _t(z, hw, 7), _t(y_c, hw), _t(y_a, hw)]
    return _rowwise(name, fn, n, _row_tile(n, 2 * hw), ins, [(BF16, d, hw)] * 4, ncol=2)


def _relu2(f):
    r = jnp.maximum(f, 0.0)
    return r * r


def _relu2_bwd(da, f):
    return da * (2.0 * jnp.maximum(f, 0.0))


def _head_loss(name, x2, pre, pp, target, g):
    n, d = x2.shape

    def fn(i, nrow, x2, pre, pp, t, g):
        gate = _sigmoid(pre)
        x3 = x2 + gate * pp
        r = lax.rsqrt(jnp.mean(x3 * x3, axis=-1, keepdims=True) + EPS)
        xn = x3 * r
        e = xn * g - t
        part = 0.5 * jnp.sum(jnp.mean(e * e, axis=-1, keepdims=True), axis=0, keepdims=True)
        dy = e * (1.0 / d)
        w = dy * g
        dx3 = r * w - x3 * (r * r * r) * jnp.mean(x3 * w, axis=-1, keepdims=True)
        return (dx3, dx3 * gate, dx3 * pp * gate * (1.0 - gate), _sum8(dy * xn),
                jnp.broadcast_to(part, (SUBLANES, LANES)))

    ins = [_t(x2), _t(pre), _t(pp), _t(target), _b(g)]
    return _rowwise(name, fn, n, _row_tile(n, 2 * d), ins, [(F32, d, d), (BF16, d, d), (BF16, d, d)],
                    [(SUBLANES, d, d, True), (SUBLANES, LANES, LANES, True)])


def _group_sum(name, dvh, n_kv):
    n = dvh.shape[0]

    def fn(i, nrow, *a):
        s = a[0]
        for k in range(1, GROUP):
            s = s + a[k]
        return s

    ins = [_t(dvh, HEAD_DIM, k, GROUP) for k in range(GROUP)]
    return _rowwise(name, fn, n, _div(n, 1024, SUBLANES), ins, [(BF16, n_kv * HEAD_DIM, HEAD_DIM)], ncol=n_kv)[0]


MM_ROWS = 1024
MM_COLS = 1152
MM_REDUCE = 2048
MM_REDUCE_BLOCKS = 2304


def _accumulate(step, n_steps, part, acc, finish):
    if n_steps == 1:
        finish(part)
        return

    @pl.when(step == 0)
    def _():
        acc[...] = part

    @pl.when(jnp.logical_and(step > 0, step < n_steps - 1))
    def _():
        acc[...] += part

    @pl.when(step == n_steps - 1)
    def _():
        finish(acc[...] + part)


def _mm_nn(name, a, w3, out_dtype, residual=None, activation=None):
    m, k = a.shape
    nb, _, ns = w3.shape
    tm, tn, tk = _div(m, MM_ROWS, SUBLANES), _div(ns, MM_COLS, LANES), _div(k, MM_REDUCE, LANES)
    per, nk = ns // tn, k // tk
    has_res = residual is not None
    n_out = 1 if activation is None else 2

    def body(*refs):
        a_ref, w_ref = refs[:2]
        acc = refs[-1]
        outs = refs[-1 - n_out:-1]
        part = jnp.dot(a_ref[...].astype(BF16), w_ref[...].astype(BF16), preferred_element_type=F32)

        def finish(total):
            if has_res:
                total = total + refs[2][...]
            outs[0][...] = total.astype(out_dtype)
            if activation is not None:
                outs[1][...] = activation(total).astype(BF16)

        _accumulate(pl.program_id(2), nk, part, acc, finish)

    in_specs = [pl.BlockSpec((tm, tk), lambda i, n, kk: (i, kk)),
                pl.BlockSpec((None, tk, tn), lambda i, n, kk: (n // per, kk, n % per))]
    args = [a, w3]
    if has_res:
        in_specs.append(pl.BlockSpec((tm, tn), lambda i, n, kk: (i, n)))
        args.append(residual)
    out_dtypes = [out_dtype] + ([BF16] if activation is not None else [])
    res = pl.pallas_call(
        body, name=name, grid=(m // tm, nb * per, nk), in_specs=in_specs,
        out_specs=[pl.BlockSpec((tm, tn), lambda i, n, kk: (i, n)) for _ in out_dtypes],
        out_shape=[jax.ShapeDtypeStruct((m, nb * ns), dt) for dt in out_dtypes],
        scratch_shapes=[pltpu.VMEM((tm, tn) if nk > 1 else (SUBLANES, LANES), F32)],
        compiler_params=_params(("parallel", "parallel", "arbitrary")))(*args)
    return res[0] if activation is None else res


def _mm_nt(name, dc, w3, out_dtype, after=(), through=None):
    m = dc.shape[0]
    nb, k, ns = w3.shape
    tm, to = _div(m, MM_ROWS, SUBLANES), _div(k, MM_ROWS, LANES)
    if ns > MM_REDUCE_BLOCKS // 2:
        jb, tr = 1, _div(ns, MM_REDUCE, LANES)
    else:
        jb, tr = max(j for j in range(1, nb + 1) if nb % j == 0 and j * ns <= MM_REDUCE_BLOCKS), ns
    per = ns // tr
    nr = (nb // jb) * per

    def body(dc_ref, w_ref, *rest):
        o_ref, acc = rest[-2:]
        part = None
        for j in range(jb):
            term = lax.dot_general(dc_ref[:, j * tr:(j + 1) * tr].astype(BF16), w_ref[j].astype(BF16),
                                   (((1,), (1,)), ((), ())), preferred_element_type=F32)
            part = term if part is None else part + term

        def finish(total):
            if through is not None:
                total = through[0](total, rest[0][...])
            o_ref[...] = total.astype(out_dtype)

        _accumulate(pl.program_id(2), nr, part, acc, finish)

    in_specs = [pl.BlockSpec((tm, jb * tr), lambda i, o, r: (i, r)),
                pl.BlockSpec((jb, to, tr), lambda i, o, r: (r // per, o, r % per))]
    args = [dc, w3]
    if through is not None:
        in_specs.append(pl.BlockSpec((tm, to), lambda i, o, r: (i, o)))
        args.append(through[1])
    for token in after:
        in_specs.append(pl.BlockSpec(memory_space=pl.ANY))
        args.append(token)
    return pl.pallas_call(
        body, name=name, grid=(m // tm, k // to, nr), in_specs=in_specs,
        out_specs=pl.BlockSpec((tm, to), lambda i, o, r: (i, o)),
        out_shape=jax.ShapeDtypeStruct((m, k), out_dtype),
        scratch_shapes=[pltpu.VMEM((tm, to) if nr > 1 else (SUBLANES, LANES), F32)],
        compiler_params=_params(("parallel", "parallel", "arbitrary")))(*args)


def _mm_tn(name, a, dc, nb):
    m, k = a.shape
    ns = dc.shape[1] // nb
    tr, tn, to = _div(m, MM_REDUCE, LANES), _div(ns, MM_COLS, LANES), _div(k, MM_ROWS, LANES)
    per, nr = ns // tn, m // tr

    def body(a_ref, dc_ref, o_ref, acc):
        part = lax.dot_general(a_ref[...].astype(BF16), dc_ref[...].astype(BF16), (((0,), (0,)), ((), ())),
                               preferred_element_type=F32)

        def finish(total):
            o_ref[...] = total.astype(BF16)

        _accumulate(pl.program_id(2), nr, part, acc, finish)

    return pl.pallas_call(
        body, name=name, grid=(k // to, nb * per, nr),
        in_specs=[pl.BlockSpec((tr, to), lambda o, n, r: (r, o)),
                  pl.BlockSpec((tr, tn), lambda o, n, r: (r, n))],
        out_specs=pl.BlockSpec((None, to, tn), lambda o, n, r: (n // per, o, n % per)),
        out_shape=jax.ShapeDtypeStruct((nb, k, ns), BF16),
        scratch_shapes=[pltpu.VMEM((to, tn) if nr > 1 else (SUBLANES, LANES), F32)],
        compiler_params=_params(("parallel", "parallel", "arbitrary")))(a, dc)


QK_SCALE = HEAD_DIM ** -0.5
LOG2_E = 1.4426950408889634
LN_2 = 0.6931471805599453


def _columns(row, n):
    return jnp.transpose(jnp.broadcast_to(row, (LANES, n)))


FLASH_TILE = 1024


def _flash_fwd(qr, kr, vt, n_heads):
    t = qr.shape[0]
    tq, tk = _div(t, FLASH_TILE, LANES), _div(t, 2 * FLASH_TILE, LANES)
    nk = t // tk

    def body(q_ref, k_ref, vt_ref, o_ref, lse_ref, m_scr, l_scr, acc_scr):
        kb = pl.program_id(2)

        @pl.when(kb == 0)
        def _():
            m_scr[...] = jnp.full_like(m_scr, -1e30)
            l_scr[...] = jnp.zeros_like(l_scr)
            acc_scr[...] = jnp.zeros_like(acc_scr)

        s_t = lax.dot_general(k_ref[...], q_ref[...], (((1,), (1,)), ((), ())), preferred_element_type=F32)
        m_prev = m_scr[...]
        m_new = jnp.maximum(m_prev, jnp.max(s_t, axis=0, keepdims=True))
        alpha = jnp.exp2(m_prev - m_new)
        p_t = jnp.exp2(s_t - m_new)
        l_scr[...] = alpha * l_scr[...] + jnp.sum(p_t, axis=0, keepdims=True)
        acc_scr[...] = alpha * acc_scr[...] + jnp.dot(vt_ref[...], p_t.astype(BF16), preferred_element_type=F32)
        m_scr[...] = m_new

        @pl.when(kb == nk - 1)
        def _():
            l = l_scr[...]
            o_ref[...] = jnp.transpose(acc_scr[...] / l)
            lse_ref[...] = jnp.broadcast_to(m_scr[...] + jnp.log(l) * LOG2_E, (SUBLANES, tq))

    return pl.pallas_call(
        body, name="flash_fwd", grid=(n_heads, t // tq, nk),
        in_specs=[pl.BlockSpec((tq, HEAD_DIM), lambda h, qi, ki: (qi, h)),
                  pl.BlockSpec((tk, HEAD_DIM), lambda h, qi, ki: (ki, h // GROUP)),
                  pl.BlockSpec((HEAD_DIM, tk), lambda h, qi, ki: (h // GROUP, ki))],
        out_specs=[pl.BlockSpec((tq, HEAD_DIM), lambda h, qi, ki: (qi, h)),
                   pl.BlockSpec((None, SUBLANES, tq), lambda h, qi, ki: (h, 0, qi))],
        out_shape=[jax.ShapeDtypeStruct((t, n_heads * HEAD_DIM), F32),
                   jax.ShapeDtypeStruct((n_heads, SUBLANES, t), F32)],
        scratch_shapes=[pltpu.VMEM((1, tq), F32), pltpu.VMEM((1, tq), F32), pltpu.VMEM((HEAD_DIM, tq), F32)],
        compiler_params=_params(("parallel", "parallel", "arbitrary")))(qr, kr, vt)


def _attn_delta(do, o, n_heads):
    t = o.shape[0]
    tm = _div(t, 1024, LANES)

    def body(do_ref, o_ref, d_ref):
        prod = do_ref[...].astype(F32) * o_ref[...]
        row = jnp.sum(jnp.transpose(prod), axis=0, keepdims=True)
        d_ref[...] = jnp.broadcast_to(row, (SUBLANES, tm))

    return pl.pallas_call(
        body, name="attn_delta", grid=(n_heads, t // tm),
        in_specs=[pl.BlockSpec((tm, HEAD_DIM), lambda h, i: (i, h)),
                  pl.BlockSpec((tm, HEAD_DIM), lambda h, i: (i, h))],
        out_specs=pl.BlockSpec((None, SUBLANES, tm), lambda h, i: (h, 0, i)),
        out_shape=jax.ShapeDtypeStruct((n_heads, SUBLANES, t), F32),
        compiler_params=_params(("parallel", "parallel")))(do, o)


def _flash_bwd(qr, kr, z, v_col0, do, lse, delta, n_heads):
    t = qr.shape[0]
    tq, tk = _div(t, 2 * FLASH_TILE, LANES), _div(t, FLASH_TILE, LANES)
    nq = t // tq

    def body(q_ref, k_ref, v_ref, do_ref, lse_ref, dl_ref, dq_ref, dk_ref, dv_ref, dk_acc, dv_acc):
        kb, qb = pl.program_id(1), pl.program_id(2)
        q, k, do_ = q_ref[...], k_ref[...], do_ref[...]
        v = v_ref[...].astype(BF16)
        s_t = lax.dot_general(k, q, (((1,), (1,)), ((), ())), preferred_element_type=F32)
        p_t = jnp.exp2(s_t - lse_ref[0:1, :])
        dp_t = lax.dot_general(v, do_, (((1,), (1,)), ((), ())), preferred_element_type=F32)
        ds_t = (p_t * (dp_t - dl_ref[0:1, :])).astype(BF16)
        dv_c = jnp.dot(p_t.astype(BF16), do_, preferred_element_type=F32)
        dk_c = jnp.dot(ds_t, q, preferred_element_type=F32) * LN_2
        dq_c = lax.dot_general(ds_t, k, (((0,), (0,)), ((), ())), preferred_element_type=F32) * QK_SCALE

        @pl.when(qb == 0)
        def _():
            dk_acc[...] = dk_c
            dv_acc[...] = dv_c

        @pl.when(qb > 0)
        def _():
            dk_acc[...] += dk_c
            dv_acc[...] += dv_c

        @pl.when(qb == nq - 1)
        def _():
            dk_ref[...] = dk_acc[...]
            dv_ref[...] = dv_acc[...]

        rows = pl.ds(pl.multiple_of(qb * tq, tq), tq)

        @pl.when(kb == 0)
        def _():
            dq_ref[rows, :] = dq_c

        @pl.when(kb > 0)
        def _():
            dq_ref[rows, :] += dq_c

    wide = jax.ShapeDtypeStruct((t, n_heads * HEAD_DIM), F32)
    return pl.pallas_call(
        body, name="flash_bwd", grid=(n_heads, t // tk, nq),
        in_specs=[pl.BlockSpec((tq, HEAD_DIM), lambda h, kb, qb: (qb, h)),
                  pl.BlockSpec((tk, HEAD_DIM), lambda h, kb, qb: (kb, h // GROUP)),
                  pl.BlockSpec((tk, HEAD_DIM), lambda h, kb, qb: (kb, v_col0 + h // GROUP)),
                  pl.BlockSpec((tq, HEAD_DIM), lambda h, kb, qb: (qb, h)),
                  pl.BlockSpec((None, SUBLANES, tq), lambda h, kb, qb: (h, 0, qb)),
                  pl.BlockSpec((None, SUBLANES, tq), lambda h, kb, qb: (h, 0, qb))],
        out_specs=[pl.BlockSpec((t, HEAD_DIM), lambda h, kb, qb: (0, h)),
                   pl.BlockSpec((tk, HEAD_DIM), lambda h, kb, qb: (kb, h)),
                   pl.BlockSpec((tk, HEAD_DIM), lambda h, kb, qb: (kb, h))],
        out_shape=[wide, wide, wide],
        scratch_shapes=[pltpu.VMEM((tk, HEAD_DIM), F32), pltpu.VMEM((tk, HEAD_DIM), F32)],
        compiler_params=_params(("parallel", "arbitrary", "arbitrary")))(qr, kr, z, do, lse, delta)


def _position():
    x, y, c = lax.axis_index("x"), lax.axis_index("y"), lax.axis_index("c")
    return x, y, c


def _index(x, y, c):
    return 4 * x + 2 * y + c


def _all_gather(name, shard):
    a, b = shard.shape

    def body(x_ref, out_ref, send_sems, recv_sems, local_sem):
        x, y, c = _position()
        me, sibling = (x, y, c), (x, y, 1 - c)
        chips = [(1 - x, y), (x, 1 - y), (1 - x, 1 - y)]

        def block(px, py, pc):
            return out_ref.at[_index(px, py, pc)]

        def copy(k, blk, to, src=None):
            return pltpu.make_async_remote_copy(
                src_ref=block(*blk) if src is None else src, dst_ref=block(*blk),
                send_sem=send_sems.at[k], recv_sem=recv_sems.at[k], device_id=to, device_id_type=MESH)

        mine = pltpu.make_async_copy(x_ref, block(*me), local_sem)
        mine.start()
        first = [copy(0, me, sibling, src=x_ref)]
        first += [copy(1 + j, me, (*chip, c), src=x_ref) for j, chip in enumerate(chips)]
        for cp in first:
            cp.start()
        passed = [copy(4 + j, (*chip, c), sibling) for j, chip in enumerate(chips)]
        for j, chip in enumerate(chips):
            copy(1 + j, (*chip, c), me).wait_recv()
            passed[j].start()
        copy(0, sibling, me).wait_recv()
        for j, chip in enumerate(chips):
            copy(4 + j, (*chip, 1 - c), me).wait_recv()
        for cp in first + passed:
            cp.wait_send()
        mine.wait()

    return pl.pallas_call(
        body, name=name, out_shape=jax.ShapeDtypeStruct((N_DEV, a, b), shard.dtype),
        in_specs=[pl.BlockSpec(memory_space=pltpu.HBM)], out_specs=pl.BlockSpec(memory_space=pltpu.HBM),
        scratch_shapes=[pltpu.SemaphoreType.DMA((7,)), pltpu.SemaphoreType.DMA((7,)), pltpu.SemaphoreType.DMA],
    )(shard)


HBM_SPEC = pl.BlockSpec(memory_space=pltpu.HBM)
SEM_SPEC = pl.BlockSpec(memory_space=pltpu.SEMAPHORE)
N_PEERS = N_DEV - 1
PEER_FLIPS = [(fx, fy, fc) for fx in (0, 1) for fy in (0, 1) for fc in (0, 1)][1:]


def _peers(x, y, c):
    return [((1 - x) if fx else x, (1 - y) if fy else y, (1 - c) if fc else c) for fx, fy, fc in PEER_FLIPS]


def _exchange_copies(g_ref, land_ref, send_sems, recv_sems):
    x, y, c = _position()
    me = _index(x, y, c)
    peers = _peers(x, y, c)
    blocked = len(g_ref.shape) == 3

    def copy(k, src_block, dst_block):
        return pltpu.make_async_remote_copy(
            src_ref=g_ref.at[src_block] if blocked else g_ref, dst_ref=land_ref.at[dst_block],
            send_sem=send_sems.at[k], recv_sem=recv_sems.at[k], device_id=peers[k], device_id_type=MESH)

    sent = [copy(k, _index(*peer), me) for k, peer in enumerate(peers)]
    arriving = [copy(k, me, _index(*peer)) for k, peer in enumerate(peers)]
    own = pltpu.make_async_copy(g_ref.at[me] if blocked else g_ref, land_ref.at[me], send_sems.at[N_PEERS])
    return sent, arriving, own


def _exchange_start(name, g3, after=()):
    def body(g_ref, land_ref, *rest):
        send_sems, recv_sems, _, _, token = rest[len(after):]
        sent, _, own = _exchange_copies(g_ref, land_ref, send_sems, recv_sems)
        for cp in sent:
            cp.start()
        own.start()
        token[...] = jnp.zeros_like(token)

    land_shape = g3.shape if g3.ndim == 3 else (N_DEV,) + g3.shape
    land = lax.empty(land_shape, g3.dtype)
    send_sems, recv_sems, g_thru, land_thru, token = pl.pallas_call(
        body, name=name,
        out_shape=(pltpu.SemaphoreType.DMA((N_PEERS + 1,)), pltpu.SemaphoreType.DMA((N_PEERS,)),
                   pltpu.HBM(g3.shape, g3.dtype), pltpu.HBM(land_shape, g3.dtype),
                   jax.ShapeDtypeStruct((SUBLANES, LANES), F32)),
        in_specs=(HBM_SPEC, HBM_SPEC) + (pl.BlockSpec(memory_space=pl.ANY),) * len(after),
        out_specs=(SEM_SPEC, SEM_SPEC, HBM_SPEC, HBM_SPEC, pl.BlockSpec(memory_space=pltpu.VMEM)),
        input_output_aliases={0: 2, 1: 3},
        compiler_params=pltpu.CompilerParams(has_side_effects=pltpu.SideEffectType.DATAFLOW_SIDE_EFFECTING),
    )(pltpu.with_memory_space_constraint(g3, pltpu.HBM), pltpu.with_memory_space_constraint(land, pltpu.HBM), *after)
    return (send_sems, recv_sems, g_thru, land_thru), token


def _exchange_wait(name, handle, after):
    send_sems, recv_sems, g_thru, land_thru = handle

    def body(g_ref, land_ref, send_sems, recv_sems, after_ref, g_dead, land_out):
        sent, arriving, own = _exchange_copies(g_ref, land_ref, send_sems, recv_sems)
        for cp in sent:
            cp.wait_send()
        for cp in arriving:
            cp.wait_recv()
        own.wait()

    return pl.pallas_call(
        body, name=name,
        out_shape=(pltpu.HBM(g_thru.shape, g_thru.dtype), pltpu.HBM(land_thru.shape, land_thru.dtype)),
        in_specs=(HBM_SPEC, HBM_SPEC, SEM_SPEC, SEM_SPEC, pl.BlockSpec(memory_space=pl.ANY)),
        out_specs=(HBM_SPEC, HBM_SPEC), input_output_aliases={0: 0, 1: 1},
        compiler_params=pltpu.CompilerParams(has_side_effects=pltpu.SideEffectType.DATAFLOW_SIDE_EFFECTING),
    )(g_thru, land_thru, send_sems, recv_sems, after)[1]


def _adamw(w, g, m, v):
    m = ADAM_B1 * m + (1.0 - ADAM_B1) * g
    v = ADAM_B2 * v + (1.0 - ADAM_B2) * (g * g)
    m_hat = m / (1.0 - ADAM_B1 ** ADAM_STEP)
    v_hat = v / (1.0 - ADAM_B2 ** ADAM_STEP)
    delta = -ADAM_LR * (m_hat / (jnp.sqrt(v_hat) + ADAM_EPS) + ADAM_WD * w)
    return delta, m, v


def _adamw_shard(name, land, w, m, v):
    _, a, b = land.shape
    tm = _div(a, max(16, ROW_TILE_ELEMS // (2 * b)), 16)
    per = a // tm

    def fn(i, nrow, *t):
        g = t[0].astype(F32)
        for s in range(1, N_DEV):
            g = g + t[s].astype(F32)
        w, m, v = t[N_DEV:]
        return (g,) + _adamw(w, g, m, v)

    land2 = land.reshape(N_DEV * a, b)
    ins = [_t(land2, row0=s * per) for s in range(N_DEV)] + [_t(w), _t(m), _t(v)]
    return _rowwise(name, fn, a, tm, ins, [(F32, b, b)] * 4)


def _adamw_small(name, parts, w, m, v):
    rows, n = w.shape

    def body(p_ref, w_ref, m_ref, v_ref, g_out, d_out, m_out, v_out):
        g = p_ref[0]
        for s in range(1, parts.shape[0]):
            g = g + p_ref[s]
        g = jnp.sum(g, axis=0, keepdims=True) if rows == 1 else g[0:rows, :]
        delta, m_new, v_new = _adamw(w_ref[...], g, m_ref[...], v_ref[...])
        g_out[...] = g
        d_out[...] = delta
        m_out[...] = m_new
        v_out[...] = v_new

    return pl.pallas_call(body, name=name, out_shape=[jax.ShapeDtypeStruct((rows, n), F32)] * 4)(parts, w, m, v)


def _rope_tables(t):
    pos = jnp.arange(t, dtype=jnp.int32)
    half = HEAD_DIM // 4
    inv_freq = ROPE_THETA ** (-jnp.arange(0, 2 * half, 2, dtype=F32) / (2 * half))
    ang_r = (pos // GRID_W).astype(F32)[:, None] * inv_freq[None, :]
    ang_c = (pos % GRID_W).astype(F32)[:, None] * inv_freq[None, :]
    cos = jnp.concatenate([jnp.cos(ang_r)] * 2 + [jnp.cos(ang_c)] * 2, axis=-1)
    sin = jnp.concatenate([-jnp.sin(ang_r), jnp.sin(ang_r), -jnp.sin(ang_c), jnp.sin(ang_c)], axis=-1)
    return cos, sin


def _gather_weight(name, w, cols):
    g = _all_gather(name, w[0].astype(BF16))
    return g if cols else g.reshape(1, N_DEV * g.shape[1], g.shape[2])


def kernel(x, p, norm_mix, w_in, w_dw, conv_ln_g, conv_ln_b, w_conv_proj, q_norm, k_norm, w_attn_proj, w_out, norm_ffn, w_ff1, w_ff2, norm_ple, w_ple_gate, w_ple_proj, norm_final, loss_target, m_norm_mix, m_w_in, m_w_dw, m_conv_ln_g, m_conv_ln_b, m_w_conv_proj, m_q_norm, m_k_norm, m_w_attn_proj, m_w_out, m_norm_ffn, m_w_ff1, m_w_ff2, m_norm_ple, m_w_ple_gate, m_w_ple_proj, m_norm_final, v_norm_mix, v_w_in, v_w_dw, v_conv_ln_g, v_conv_ln_b, v_w_conv_proj, v_q_norm, v_k_norm, v_w_attn_proj, v_w_out, v_norm_ffn, v_w_ff1, v_w_ff2, v_norm_ple, v_w_ple_gate, v_w_ple_proj, v_norm_final):
    t, d = x.shape[1], x.shape[2]
    cw = d // 2
    n_heads = d // HEAD_DIM
    n_kv = n_heads // GROUP
    col_q, col_k, col_v = d // HEAD_DIM, 2 * d // HEAD_DIM, (2 * d + d // 4) // HEAD_DIM
    x0, pe, tgt = x[0], p[0, 0], loss_target[0]
    g_final = norm_final.reshape(1, d)
    me = _index(*_position())

    wg_in = _gather_weight("ag_w_in", w_in, True)
    dw_pad = jnp.pad(w_dw[0], ((0, CONV_TAPS_PAD - CONV_KERNEL), (0, 0)))
    dw_all = _all_gather("ag_w_dw", dw_pad)
    dw_full = dw_all.transpose(1, 0, 2).reshape(CONV_TAPS_PAD, cw)
    cos, sin = _rope_tables(t)

    def fetch(name, w):
        return _exchange_start("ag_start_" + name, w[0].astype(BF16), after=(wg_in, dw_all))

    def arrived(name, handle, cols, after):
        g = _exchange_wait("ag_wait_" + name, handle, after)
        return g if cols else g.reshape(1, N_DEV * g.shape[1], g.shape[2])

    on_cp, t_cp = fetch("w_conv_proj", w_conv_proj)
    on_ap, t_ap = fetch("w_attn_proj", w_attn_proj)
    on_out, t_out = fetch("w_out", w_out)
    on_ff1, t_ff1 = fetch("w_ff1", w_ff1)
    on_ff2, t_ff2 = fetch("w_ff2", w_ff2)
    on_pg, t_pg = fetch("w_ple_gate", w_ple_gate)
    on_pp, t_pp = fetch("w_ple_proj", w_ple_proj)

    h1 = _rms_fwd("rms_mix", x0, norm_mix, after=(t_cp, t_ap, t_out, t_ff1, t_ff2, t_pg, t_pp))
    z = _mm_nn("mm_in", h1, wg_in, F32)
    conv, s_c = _conv_fwd("conv_fwd", z, cw, dw_full, conv_ln_g, conv_ln_b)
    wg_cp = arrived("w_conv_proj", on_cp, True, s_c)
    y_c = _mm_nn("mm_conv_proj", s_c, wg_cp, F32)
    qr = _head_norm_rope("q_prep", z, col_q, n_heads, q_norm, cos, sin, out_scale=QK_SCALE * LOG2_E,
                         heads_per_step=GROUP)
    kr = _head_norm_rope("k_prep", z, col_k, n_kv, k_norm, cos, sin)
    vt = z[:, col_v * HEAD_DIM:(col_v + n_kv) * HEAD_DIM].T.astype(BF16)
    o, lse = _flash_fwd(qr, kr, vt, n_heads)
    wg_ap = arrived("w_attn_proj", on_ap, False, o)
    y_a = _mm_nn("mm_attn_proj", o, wg_ap, F32)
    merged = _merge_fwd("merge_fwd", z, d, y_c, y_a)
    wg_out = arrived("w_out", on_out, False, merged)
    x1 = _mm_nn("mm_out", merged, wg_out, F32, residual=x0)
    h2 = _rms_fwd("rms_ffn", x1, norm_ffn)
    wg_ff1 = arrived("w_ff1", on_ff1, True, h2)
    f, act = _mm_nn("mm_ff1", h2, wg_ff1, F32, activation=_relu2)
    wg_ff2 = arrived("w_ff2", on_ff2, False, act)
    x2 = _mm_nn("mm_ff2", act, wg_ff2, F32, residual=x1)
    hp = _rms_fwd("rms_ple", x2, norm_ple)
    wg_pg = arrived("w_ple_gate", on_pg, False, hp)
    pre = _mm_nn("mm_ple_gate", hp, wg_pg, F32)
    wg_pp = arrived("w_ple_proj", on_pp, True, pre)
    pp = _mm_nn("mm_ple_proj", pe, wg_pp, F32)

    dx3, dpp, dpre, dg_final, loss_part = _head_loss("head_loss", x2, pre, pp, tgt, g_final)
    loss = lax.psum(loss_part[0, 0], ("x", "y", "c"))
    def send(name, gw):
        blocks = gw if gw.shape[0] == N_DEV else gw.reshape(N_DEV, gw.shape[1] // N_DEV, gw.shape[2])
        return _exchange_start("rs_start_" + name, blocks)

    sent_pp, tok_pp = send("w_ple_proj", _mm_tn("mm_d_ple_proj", pe, dpp, N_DEV))
    sent_pg, tok = send("w_ple_gate", _mm_tn("mm_d_ple_gate", hp, dpre, 1))
    dhp = _mm_nt("mm_dhp", dpre, wg_pg, F32, after=(tok_pp, tok))
    dx2, dg_ple = _rms_bwd("rms_ple_bwd", x2, norm_ple, dhp, dx3)
    sent_ff2, tok = send("w_ff2", _mm_tn("mm_d_ff2", act, dx2, 1))
    df = _mm_nt("mm_dact", dx2, wg_ff2, BF16, after=(tok,), through=(_relu2_bwd, f))
    sent_ff1, tok = send("w_ff1", _mm_tn("mm_d_ff1", h2, df, N_DEV))
    dh2 = _mm_nt("mm_dh2", df, wg_ff1, F32, after=(tok,))
    dx1, dg_ffn = _rms_bwd("rms_ffn_bwd", x1, norm_ffn, dh2, dx2)
    sent_out, tok = send("w_out", _mm_tn("mm_d_out", merged, dx1, 1))
    dmerged = _mm_nt("mm_dmerged", dx1, wg_out, F32, after=(tok,))
    dgc, dga, dyc, dya = _merge_bwd("merge_bwd", dmerged, z, d, y_c, y_a)
    sent_cp, tok = send("w_conv_proj", _mm_tn("mm_d_conv_proj", s_c, dyc, N_DEV))
    ds_c = _mm_nt("mm_ds_c", dyc, wg_cp, F32, after=(tok,))
    dconv, dg_ln, db_ln = _conv_ln_bwd("conv_ln_bwd", ds_c, conv, conv_ln_g, conv_ln_b)
    dca, dcb, dg_dw = _conv_glu_bwd("conv_glu_bwd", dconv, z, cw, dw_full)
    sent_ap, tok = send("w_attn_proj", _mm_tn("mm_d_attn_proj", o, dya, 1))
    do = _mm_nt("mm_do", dya, wg_ap, BF16, after=(tok,))
    delta = _attn_delta(do, o, n_heads)
    dqr, dkh, dvh = _flash_bwd(qr, kr, z, col_v, do, lse, delta, n_heads)
    dq, dg_q = _head_norm_rope_bwd("q_prep_bwd", [(dqr, 0, 1)], z, col_q, n_heads, q_norm, cos, sin,
                                   heads_per_step=GROUP)
    dk, dg_k = _head_norm_rope_bwd("k_prep_bwd", [(dkh, g, GROUP) for g in range(GROUP)], z, col_k, n_kv, k_norm,
                                   cos, sin)
    dv = _group_sum("dv_sum", dvh, n_kv)
    dz = jnp.concatenate([dca, dcb, dq, dk, dv, dgc, dga], axis=1)
    sent_in, tok = send("w_in", _mm_tn("mm_d_in", h1, dz, N_DEV))
    dh1 = _mm_nt("mm_dh1", dz, wg_in, F32, after=(tok,))
    grad_x, dg_mix = _rms_bwd("rms_mix_bwd", x0, norm_mix, dh1, dx1)

    def pad(a):
        return jnp.pad(a, ((0, 0), (0, d - a.shape[1])))

    small = [dg_mix, dg_ffn, dg_ple, dg_final, dg_ln, db_ln, dg_q, dg_k, dg_dw]
    packed = _all_gather("ag_small", jnp.concatenate([pad(a) for a in small], axis=0))
    parts, row = [], 0
    for a in small:
        parts.append(packed[:, row:row + a.shape[0], :a.shape[1]])
        row += a.shape[0]
    p_mix, p_ffn, p_ple, p_final, p_lng, p_lnb, p_q, p_k, p_dw = parts
    p_dw = lax.dynamic_slice_in_dim(p_dw, me * (cw // N_DEV), cw // N_DEV, axis=2)

    def little(name, part, w, m, v):
        rows, n = (1 if w.ndim < 3 else w.shape[1]), w.shape[-1]
        padded = rows if rows == 1 else part.shape[1]

        def two_d(a):
            return jnp.pad(a.reshape(rows, n), ((0, padded - rows), (0, 0)))

        res = _adamw_small("adamw_" + name, part, two_d(w), two_d(m), two_d(v))
        return [r[:rows].reshape(w.shape) for r in res]

    done = {}
    after = packed
    for name, sent, w, m, v in [
            ("w_ple_proj", sent_pp, w_ple_proj, m_w_ple_proj, v_w_ple_proj),
            ("w_ple_gate", sent_pg, w_ple_gate, m_w_ple_gate, v_w_ple_gate),
            ("w_ff2", sent_ff2, w_ff2, m_w_ff2, v_w_ff2),
            ("w_ff1", sent_ff1, w_ff1, m_w_ff1, v_w_ff1),
            ("w_out", sent_out, w_out, m_w_out, v_w_out),
            ("w_conv_proj", sent_cp, w_conv_proj, m_w_conv_proj, v_w_conv_proj),
            ("w_attn_proj", sent_ap, w_attn_proj, m_w_attn_proj, v_w_attn_proj),
            ("w_in", sent_in, w_in, m_w_in, v_w_in)]:
        land = _exchange_wait("rs_wait_" + name, sent, after)
        res = _adamw_shard("adamw_" + name, land, w[0], m[0], v[0])
        after = res[0]
        done[name] = [r.reshape(w.shape) for r in res]

    results = [
        little("norm_mix", p_mix, norm_mix, m_norm_mix, v_norm_mix),
        done["w_in"],
        little("w_dw", p_dw, w_dw, m_w_dw, v_w_dw),
        little("conv_ln_g", p_lng, conv_ln_g, m_conv_ln_g, v_conv_ln_g),
        little("conv_ln_b", p_lnb, conv_ln_b, m_conv_ln_b, v_conv_ln_b),
        done["w_conv_proj"],
        little("q_norm", p_q, q_norm, m_q_norm, v_q_norm),
        little("k_norm", p_k, k_norm, m_k_norm, v_k_norm),
        done["w_attn_proj"],
        done["w_out"],
        little("norm_ffn", p_ffn, norm_ffn, m_norm_ffn, v_norm_ffn),
        done["w_ff1"],
        done["w_ff2"],
        little("norm_ple", p_ple, norm_ple, m_norm_ple, v_norm_ple),
        done["w_ple_gate"],
        done["w_ple_proj"],
        little("norm_final", p_final, norm_final, m_norm_final, v_norm_final),
    ]
    grads, deltas, new_m, new_v = zip(*results)
    return (loss, grad_x[None], *grads, *deltas, *new_m, *new_v)
```

```python
import functools

import jax
import jax.numpy as jnp
from jax import lax
from jax.experimental import pallas as pl
from jax.experimental.pallas import tpu as pltpu

F32 = jnp.float32
BF16 = jnp.bfloat16

EPS = 1e-6
HEAD_DIM = 128
GROUP = 4
CONV_KERNEL = 31
CONV_HALO = 16
CONV_TAPS_PAD = 32
GRID_W = 64
ROPE_THETA = 10000.0
ADAM_LR, ADAM_B1, ADAM_B2, ADAM_EPS, ADAM_WD, ADAM_STEP = 0.001, 0.9, 0.999, 1e-08, 0.01, 10

N_DEV = 8
MESH = pl.DeviceIdType.MESH
SUBLANES = 8
LANES = 128
VMEM_LIMIT = 48 * 1024 * 1024
ROW_TILE_ELEMS = 512 * 1024


def _div(n, pref, mult=1):
    if n <= pref:
        return n
    for d in range(pref, 0, -1):
        if n % d == 0 and d % mult == 0:
            return d
    return n


def _params(sem):
    return pltpu.CompilerParams(dimension_semantics=sem, vmem_limit_bytes=VMEM_LIMIT)


def _sum8(a):
    tm, w = a.shape
    return a.reshape(tm // SUBLANES, SUBLANES, w).sum(axis=0)


def _sigmoid(x):
    return jax.nn.sigmoid(x)


def _rowwise(name, fn, n_rows, tm, ins, outs, accs=(), ncol=1, refs_to_fn=False, after=()):
    nrow = n_rows // tm
    in_specs, arrays = [], []
    for spec in ins:
        kind, arr, w, c0 = spec[:4]
        if kind == "t":
            cs, r0 = spec[4], spec[5]
            in_specs.append(pl.BlockSpec((tm, w), lambda j, i, c0=c0, cs=cs, r0=r0: (r0 + i, c0 + cs * j)))
        elif kind == "p":
            h = spec[4]
            r = tm // h
            in_specs.append(pl.BlockSpec((h, w), lambda j, i, c0=c0, r=r: (jnp.maximum(i * r - 1, 0), c0 + j)))
        elif kind == "n":
            h = spec[4]
            r = tm // h
            last = n_rows // h - 1
            in_specs.append(
                pl.BlockSpec((h, w), lambda j, i, c0=c0, r=r, last=last: (jnp.minimum((i + 1) * r, last), c0 + j)))
        else:
            cs = spec[4]
            in_specs.append(pl.BlockSpec((arr.shape[0], w), lambda j, i, c0=c0, cs=cs: (0, c0 + cs * j)))
        arrays.append(arr)
    out_shape, out_specs = [], []
    for dtype, total, w in outs:
        out_shape.append(jax.ShapeDtypeStruct((n_rows, total), dtype))
        out_specs.append(pl.BlockSpec((tm, w), lambda j, i: (i, j)))
    for rows, total, w, follow in accs:
        out_shape.append(jax.ShapeDtypeStruct((rows, total), F32))
        out_specs.append(pl.BlockSpec((rows, w), (lambda j, i: (0, j)) if follow else (lambda j, i: (0, 0))))
    n_in, n_out, n_acc = len(ins), len(outs), len(accs)
    for token in after:
        in_specs.append(pl.BlockSpec(memory_space=pl.ANY))
        arrays.append(token)
    out0 = n_in + len(after)

    def body(*refs):
        j, i = pl.program_id(0), pl.program_id(1)
        res = fn(i, nrow, *(refs[:n_in] if refs_to_fn else [r[...] for r in refs[:n_in]]))
        if not isinstance(res, (tuple, list)):
            res = (res,)
        for k in range(n_out):
            refs[out0 + k][...] = res[k].astype(outs[k][0])
        for k in range(n_acc):
            ref, term = refs[out0 + n_out + k], res[n_out + k]
            first = (i == 0) if accs[k][3] else jnp.logical_and(i == 0, j == 0)

            @pl.when(first)
            def _():
                ref[...] = term

            @pl.when(jnp.logical_not(first))
            def _():
                ref[...] += term

    res = pl.pallas_call(
        body, name=name, grid=(ncol, nrow), in_specs=in_specs, out_specs=out_specs, out_shape=out_shape,
        compiler_params=_params(("arbitrary", "arbitrary")))(*arrays)
    return res


def _row_tile(n_rows, width, mult=SUBLANES):
    return _div(n_rows, max(mult, ROW_TILE_ELEMS // width), mult)


def _t(arr, w=None, col0=0, cstride=1, row0=0):
    return ("t", arr, arr.shape[1] if w is None else w, col0, cstride, row0)


def _b(arr, w=None, col0=0, cstride=0):
    return ("b", arr, arr.shape[1] if w is None else w, col0, cstride)


def _rms_fwd(name, x, g, after=()):
    n, d = x.shape

    def fn(i, nrow, x, g):
        r = lax.rsqrt(jnp.mean(x * x, axis=-1, keepdims=True) + EPS)
        return x * r * g

    return _rowwise(name, fn, n, _row_tile(n, d), [_t(x), _b(g)], [(BF16, d, d)], after=after)[0]


def _rms_bwd(name, x, g, dh, dres):
    n, d = x.shape

    def fn(i, nrow, x, g, dh, dres):
        r = lax.rsqrt(jnp.mean(x * x, axis=-1, keepdims=True) + EPS)
        w = dh * g
        dx = dres + r * w - x * (r * r * r) * jnp.mean(x * w, axis=-1, keepdims=True)
        return dx, _sum8(dh * x * r)

    return _rowwise(name, fn, n, _row_tile(n, 2 * d), [_t(x), _b(g), _t(dh), _t(dres)], [(F32, d, d)],
                    [(SUBLANES, d, d, True)])


def _swap_halves(x):
    lane = lax.broadcasted_iota(jnp.int32, x.shape, 1)
    return jnp.where(lane % 64 < 32, pltpu.roll(x, 96, axis=1), pltpu.roll(x, 32, axis=1))


def _head_norm_rope(name, z, col0, n_heads, g, cos, sin, out_scale=None, heads_per_step=1):
    n, hp = z.shape[0], heads_per_step
    assert col0 % hp == 0 and n_heads % hp == 0

    def fn(i, nrow, x, g, cos, sin):
        ys = []
        for h in range(hp):
            xh = x[:, h * HEAD_DIM:(h + 1) * HEAD_DIM]
            r = lax.rsqrt(jnp.mean(xh * xh, axis=-1, keepdims=True) + EPS)
            y = xh * r * g
            y = y * cos + _swap_halves(y) * sin
            ys.append(y if out_scale is None else y * out_scale)
        return ys[0] if hp == 1 else jnp.concatenate(ys, axis=1)

    tm = _div(n, 1024, SUBLANES)
    w = hp * HEAD_DIM
    return _rowwise(name, fn, n, tm, [_t(z, w, col0 // hp), _b(g), _t(cos, cstride=0), _t(sin, cstride=0)],
                    [(BF16, n_heads * HEAD_DIM, w)], ncol=n_heads // hp)[0]


def _head_norm_rope_bwd(name, douts, z, col0, n_heads, g, cos, sin, heads_per_step=1):
    n, hp = z.shape[0], heads_per_step
    n_d = len(douts)
    assert col0 % hp == 0 and n_heads % hp == 0 and (hp == 1 or all(c0 % hp == 0 and cs == 1 for _, c0, cs in douts))

    def fn(i, nrow, *a):
        dy = a[0]
        for k in range(1, n_d):
            dy = dy + a[k]
        x, g, cos, sin = a[n_d:]
        dxs, dg = [], None
        for h in range(hp):
            xh, dyh = x[:, h * HEAD_DIM:(h + 1) * HEAD_DIM], dy[:, h * HEAD_DIM:(h + 1) * HEAD_DIM]
            dn = dyh * cos + _swap_halves(dyh * sin)
            r = lax.rsqrt(jnp.mean(xh * xh, axis=-1, keepdims=True) + EPS)
            w = dn * g
            dxs.append(r * w - xh * (r * r * r) * jnp.mean(xh * w, axis=-1, keepdims=True))
            term = _sum8(dn * xh * r)
            dg = term if dg is None else dg + term
        return (dxs[0] if hp == 1 else jnp.concatenate(dxs, axis=1)), dg

    tm = _div(n, 1024, SUBLANES)
    w = hp * HEAD_DIM
    ins = [_t(arr, w, c0 // hp, cs) for arr, c0, cs in douts]
    ins += [_t(z, w, col0 // hp), _b(g), _t(cos, cstride=0), _t(sin, cstride=0)]
    return _rowwise(name, fn, n, tm, ins, [(BF16, n_heads * HEAD_DIM, w)],
                    [(SUBLANES, HEAD_DIM, HEAD_DIM, False)], ncol=n_heads // hp)


def _halo(arr, w, col0):
    return [("p", arr, w, col0, CONV_HALO), _t(arr, w, col0), ("n", arr, w, col0, CONV_HALO)]


def _extend(i, nrow, prev, cur, nxt):
    prev = jnp.where(i > 0, prev, 0.0)
    nxt = jnp.where(i < nrow - 1, nxt, 0.0)
    return jnp.concatenate([prev, cur, nxt], axis=0)


def _shifted(ext, offset, tm):
    n = ext.shape[0]
    rolled = ext if offset == 0 else pltpu.roll(ext, (-offset) % n, axis=0)
    return rolled[CONV_HALO:CONV_HALO + tm]


def _conv_fwd(name, z, cw, w_dw, ln_g, ln_b):
    n = z.shape[0]
    tm = _row_tile(n, 4 * cw, CONV_HALO)

    def fn(i, nrow, ap, a, an, bp, b, bn, w, g, beta):
        a = a[...]
        ext = _extend(i, nrow, ap[...] * _sigmoid(bp[...]), a * _sigmoid(b[...]), an[...] * _sigmoid(bn[...]))
        conv = jnp.zeros_like(a)
        for k in range(CONV_KERNEL):
            conv = conv + _shifted(ext, k - CONV_KERNEL // 2, tm) * w[k:k + 1, :]
        xc = conv - jnp.mean(conv, axis=-1, keepdims=True)
        ln = xc * lax.rsqrt(jnp.mean(xc * xc, axis=-1, keepdims=True) + EPS) * g[...] + beta[...]
        return conv, ln * _sigmoid(ln)

    ins = _halo(z, cw, 0) + _halo(z, cw, 1) + [_b(w_dw), _b(ln_g), _b(ln_b)]
    return _rowwise(name, fn, n, tm, ins, [(F32, cw, cw), (BF16, cw, cw)], refs_to_fn=True)


def _conv_ln_bwd(name, ds, conv, ln_g, ln_b):
    n, cw = conv.shape

    def fn(i, nrow, ds, conv, g, beta):
        xc = conv - jnp.mean(conv, axis=-1, keepdims=True)
        rstd = lax.rsqrt(jnp.mean(xc * xc, axis=-1, keepdims=True) + EPS)
        xhat = xc * rstd
        ln = xhat * g + beta
        sg = _sigmoid(ln)
        dln = ds * (sg * (1.0 + ln * (1.0 - sg)))
        dxh = dln * g
        dconv = rstd * (dxh - jnp.mean(dxh, axis=-1, keepdims=True)
                        - xhat * jnp.mean(dxh * xhat, axis=-1, keepdims=True))
        return dconv, _sum8(dln * xhat), _sum8(dln)

    return _rowwise(name, fn, n, _row_tile(n, 2 * cw), [_t(ds), _t(conv), _b(ln_g), _b(ln_b)], [(F32, cw, cw)],
                    [(SUBLANES, cw, cw, True), (SUBLANES, cw, cw, True)])


def _conv_glu_bwd(name, dconv, z, cw, w_dw):
    n = z.shape[0]
    tm = _row_tile(n, 4 * cw, CONV_HALO)

    def fn(i, nrow, dp, d, dn, ap, a, an, bp, b, bn, w):
        d, a = d[...], a[...]
        dext = _extend(i, nrow, dp[...], d, dn[...])
        sg = _sigmoid(b[...])
        uext = _extend(i, nrow, ap[...] * _sigmoid(bp[...]), a * sg, an[...] * _sigmoid(bn[...]))
        tap = lax.broadcasted_iota(jnp.int32, (CONV_TAPS_PAD, cw), 0)
        du = jnp.zeros_like(d)
        dw = jnp.zeros((CONV_TAPS_PAD, cw), F32)
        for k in range(CONV_KERNEL):
            off = k - CONV_KERNEL // 2
            du = du + _shifted(dext, -off, tm) * w[k:k + 1, :]
            dw_k = jnp.sum(d * _shifted(uext, off, tm), axis=0, keepdims=True)
            dw = dw + jnp.where(tap == k, dw_k, 0.0)
        return du * sg, du * a * sg * (1.0 - sg), dw

    ins = _halo(dconv, cw, 0) + _halo(z, cw, 0) + _halo(z, cw, 1) + [_b(w_dw)]
    return _rowwise(name, fn, n, tm, ins, [(BF16, cw, cw), (BF16, cw, cw)], [(CONV_TAPS_PAD, cw, cw, True)],
                    refs_to_fn=True)


def _merge_fwd(name, z, d, y_c, y_a):
    n, hw = z.shape[0], d // 2

    def fn(i, nrow, gc, ga, yc, ya):
        return _sigmoid(gc) * yc + _sigmoid(ga) * ya

    ins = [_t(z, hw, 5), _t(z, hw, 7), _t(y_c, hw), _t(y_a, hw)]
    return _rowwise(name, fn, n, _row_tile(n, 2 * hw), ins, [(BF16, d, hw)], ncol=2)[0]


def _merge_bwd(name, dm, z, d, y_c, y_a):
    n, hw = z.shape[0], d // 2

    def fn(i, nrow, dm, gc, ga, yc, ya):
        sc, sa = _sigmoid(gc), _sigmoid(ga)
        return dm * yc * sc * (1.0 - sc), dm * ya * sa * (1.0 - sa), dm * sc, dm * sa

    ins = [_t(dm, hw), _t(z, hw, 5), _t(z, hw, 7), _t(y_c, hw), _t(y_a, hw)]
    return _rowwise(name, fn, n, _row_tile(n, 2 * hw), ins, [(BF16, d, hw)] * 4, ncol=2)


def _relu2(f):
    r = jnp.maximum(f, 0.0)
    return r * r


def _relu2_bwd(da, f):
    return da * (2.0 * jnp.maximum(f, 0.0))


def _head_loss(name, x2, pre, pp, target, g):
    n, d = x2.shape

    def fn(i, nrow, x2, pre, pp, t, g):
        gate = _sigmoid(pre)
        x3 = x2 + gate * pp
        r = lax.rsqrt(jnp.mean(x3 * x3, axis=-1, keepdims=True) + EPS)
        xn = x3 * r
        e = xn * g - t
        part = 0.5 * jnp.sum(jnp.mean(e * e, axis=-1, keepdims=True), axis=0, keepdims=True)
        dy = e * (1.0 / d)
        w = dy * g
        dx3 = r * w - x3 * (r * r * r) * jnp.mean(x3 * w, axis=-1, keepdims=True)
        return (dx3, dx3 * gate, dx3 * pp * gate * (1.0 - gate), _sum8(dy * xn),
                jnp.broadcast_to(part, (SUBLANES, LANES)))

    ins = [_t(x2), _t(pre), _t(pp), _t(target), _b(g)]
    return _rowwise(name, fn, n, _row_tile(n, 2 * d), ins, [(F32, d, d), (BF16, d, d), (BF16, d, d)],
                    [(SUBLANES, d, d, True), (SUBLANES, LANES, LANES, True)])


def _group_sum(name, dvh, n_kv):
    n = dvh.shape[0]

    def fn(i, nrow, *a):
        s = a[0]
        for k in range(1, GROUP):
            s = s + a[k]
        return s

    ins = [_t(dvh, HEAD_DIM, k, GROUP) for k in range(GROUP)]
    return _rowwise(name, fn, n, _div(n, 1024, SUBLANES), ins, [(BF16, n_kv * HEAD_DIM, HEAD_DIM)], ncol=n_kv)[0]


MM_ROWS = 1024
MM_COLS = 1152
MM_REDUCE = 2048
MM_REDUCE_BLOCKS = 2304


def _accumulate(step, n_steps, part, acc, finish):
    if n_steps == 1:
        finish(part)
        return

    @pl.when(step == 0)
    def _():
        acc[...] = part

    @pl.when(jnp.logical_and(step > 0, step < n_steps - 1))
    def _():
        acc[...] += part

    @pl.when(step == n_steps - 1)
    def _():
        finish(acc[...] + part)


MXU_WIDTH = 256


def _paired(nb, ns):
    return ns % MXU_WIDTH != 0 and nb % 2 == 0 and ns <= MM_COLS


def _mm_nn(name, a, w3, out_dtype, residual=None, activation=None):
    m, k = a.shape
    nb, _, ns = w3.shape
    jb = 2 if _paired(nb, ns) else 1
    tm = _div(m, MM_ROWS // jb, SUBLANES)
    tn_w, tk = (ns if jb > 1 else _div(ns, MM_COLS, LANES)), _div(k, MM_REDUCE, LANES)
    tn = jb * tn_w
    per, nk = ns // tn_w, k // tk
    has_res = residual is not None
    n_out = 1 if activation is None else 2

    def body(*refs):
        a_ref, w_ref = refs[:2]
        acc = refs[-1]
        outs = refs[-1 - n_out:-1]
        w = w_ref[0] if jb == 1 else jnp.concatenate([w_ref[j] for j in range(jb)], axis=1)
        part = jnp.dot(a_ref[...].astype(BF16), w.astype(BF16), preferred_element_type=F32)

        def finish(total):
            if has_res:
                total = total + refs[2][...]
            outs[0][...] = total.astype(out_dtype)
            if activation is not None:
                outs[1][...] = activation(total).astype(BF16)

        _accumulate(pl.program_id(2), nk, part, acc, finish)

    in_specs = [pl.BlockSpec((tm, tk), lambda i, n, kk: (i, kk)),
                pl.BlockSpec((jb, tk, tn_w), lambda i, n, kk: (n // per, kk, n % per))]
    args = [a, w3]
    if has_res:
        in_specs.append(pl.BlockSpec((tm, tn), lambda i, n, kk: (i, n)))
        args.append(residual)
    out_dtypes = [out_dtype] + ([BF16] if activation is not None else [])
    res = pl.pallas_call(
        body, name=name, grid=(m // tm, (nb // jb) * per, nk), in_specs=in_specs,
        out_specs=[pl.BlockSpec((tm, tn), lambda i, n, kk: (i, n)) for _ in out_dtypes],
        out_shape=[jax.ShapeDtypeStruct((m, nb * ns), dt) for dt in out_dtypes],
        scratch_shapes=[pltpu.VMEM((tm, tn) if nk > 1 else (SUBLANES, LANES), F32)],
        compiler_params=_params(("parallel", "parallel", "arbitrary")))(*args)
    return res[0] if activation is None else res


def _mm_nt(name, dc, w3, out_dtype, after=(), through=None):
    m = dc.shape[0]
    nb, k, ns = w3.shape
    tm, to = _div(m, MM_ROWS, SUBLANES), _div(k, MM_ROWS, LANES)
    if ns > MM_REDUCE_BLOCKS // 2:
        jb, tr = 1, _div(ns, MM_REDUCE, LANES)
    else:
        jb, tr = max(j for j in range(1, nb + 1) if nb % j == 0 and j * ns <= MM_REDUCE_BLOCKS), ns
    per = ns // tr
    nr = (nb // jb) * per

    def body(dc_ref, w_ref, *rest):
        o_ref, acc = rest[-2:]
        w = w_ref[0] if jb == 1 else jnp.concatenate([w_ref[j] for j in range(jb)], axis=1)
        part = lax.dot_general(dc_ref[...].astype(BF16), w.astype(BF16), (((1,), (1,)), ((), ())),
                               preferred_element_type=F32)

        def finish(total):
            if through is not None:
                total = through[0](total, rest[0][...])
            o_ref[...] = total.astype(out_dtype)

        _accumulate(pl.program_id(2), nr, part, acc, finish)

    in_specs = [pl.BlockSpec((tm, jb * tr), lambda i, o, r: (i, r)),
                pl.BlockSpec((jb, to, tr), lambda i, o, r: (r // per, o, r % per))]
    args = [dc, w3]
    if through is not None:
        in_specs.append(pl.BlockSpec((tm, to), lambda i, o, r: (i, o)))
        args.append(through[1])
    for token in after:
        in_specs.append(pl.BlockSpec(memory_space=pl.ANY))
        args.append(token)
    return pl.pallas_call(
        body, name=name, grid=(m // tm, k // to, nr), in_specs=in_specs,
        out_specs=pl.BlockSpec((tm, to), lambda i, o, r: (i, o)),
        out_shape=jax.ShapeDtypeStruct((m, k), out_dtype),
        scratch_shapes=[pltpu.VMEM((tm, to) if nr > 1 else (SUBLANES, LANES), F32)],
        compiler_params=_params(("parallel", "parallel", "arbitrary")))(*args)


def _mm_tn(name, a, dc, nb):
    m, k = a.shape
    ns = dc.shape[1] // nb
    jb = 2 if _paired(nb, ns) else 1
    tr, to = _div(m, MM_REDUCE, LANES), _div(k, MM_ROWS // jb, LANES)
    tn_w = ns if jb > 1 else _div(ns, MM_COLS, LANES)
    tn = jb * tn_w
    per, nr = ns // tn_w, m // tr

    def body(a_ref, dc_ref, o_ref, acc):
        part = lax.dot_general(a_ref[...].astype(BF16), dc_ref[...].astype(BF16), (((0,), (0,)), ((), ())),
                               preferred_element_type=F32)

        def finish(total):
            for j in range(jb):
                o_ref[j] = total[:, j * tn_w:(j + 1) * tn_w].astype(BF16)

        _accumulate(pl.program_id(2), nr, part, acc, finish)

    return pl.pallas_call(
        body, name=name, grid=(k // to, (nb // jb) * per, nr),
        in_specs=[pl.BlockSpec((tr, to), lambda o, n, r: (r, o)),
                  pl.BlockSpec((tr, tn), lambda o, n, r: (r, n))],
        out_specs=pl.BlockSpec((jb, to, tn_w), lambda o, n, r: (n // per, o, n % per)),
        out_shape=jax.ShapeDtypeStruct((nb, k, ns), BF16),
        scratch_shapes=[pltpu.VMEM((to, tn) if nr > 1 else (SUBLANES, LANES), F32)],
        compiler_params=_params(("parallel", "parallel", "arbitrary")))(a, dc)


QK_SCALE = HEAD_DIM ** -0.5
LOG2_E = 1.4426950408889634
LN_2 = 0.6931471805599453


def _columns(row, n):
    return jnp.transpose(jnp.broadcast_to(row, (LANES, n)))


FLASH_TILE = 1024


def _flash_fwd(qr, kr, vt, n_heads):
    t = qr.shape[0]
    tq, tk = _div(t, FLASH_TILE, LANES), _div(t, 2 * FLASH_TILE, LANES)
    nk = t // tk

    def body(q_ref, k_ref, vt_ref, o_ref, lse_ref, m_scr, l_scr, acc_scr):
        kb = pl.program_id(2)

        @pl.when(kb == 0)
        def _():
            m_scr[...] = jnp.full_like(m_scr, -1e30)
            l_scr[...] = jnp.zeros_like(l_scr)
            acc_scr[...] = jnp.zeros_like(acc_scr)

        s_t = lax.dot_general(k_ref[...], q_ref[...], (((1,), (1,)), ((), ())), preferred_element_type=F32)
        m_prev = m_scr[...]
        m_new = jnp.maximum(m_prev, jnp.max(s_t, axis=0, keepdims=True))
        alpha = jnp.exp2(m_prev - m_new)
        p_t = jnp.exp2(s_t - m_new)
        l_scr[...] = alpha * l_scr[...] + jnp.sum(p_t, axis=0, keepdims=True)
        acc_scr[...] = alpha * acc_scr[...] + jnp.dot(vt_ref[...], p_t.astype(BF16), preferred_element_type=F32)
        m_scr[...] = m_new

        @pl.when(kb == nk - 1)
        def _():
            l = l_scr[...]
            o_ref[...] = jnp.transpose(acc_scr[...] / l)
            lse_ref[...] = jnp.broadcast_to(m_scr[...] + jnp.log(l) * LOG2_E, (SUBLANES, tq))

    return pl.pallas_call(
        body, name="flash_fwd", grid=(n_heads, t // tq, nk),
        in_specs=[pl.BlockSpec((tq, HEAD_DIM), lambda h, qi, ki: (qi, h)),
                  pl.BlockSpec((tk, HEAD_DIM), lambda h, qi, ki: (ki, h // GROUP)),
                  pl.BlockSpec((HEAD_DIM, tk), lambda h, qi, ki: (h // GROUP, ki))],
        out_specs=[pl.BlockSpec((tq, HEAD_DIM), lambda h, qi, ki: (qi, h)),
                   pl.BlockSpec((None, SUBLANES, tq), lambda h, qi, ki: (h, 0, qi))],
        out_shape=[jax.ShapeDtypeStruct((t, n_heads * HEAD_DIM), F32),
                   jax.ShapeDtypeStruct((n_heads, SUBLANES, t), F32)],
        scratch_shapes=[pltpu.VMEM((1, tq), F32), pltpu.VMEM((1, tq), F32), pltpu.VMEM((HEAD_DIM, tq), F32)],
        compiler_params=_params(("parallel", "parallel", "arbitrary")))(qr, kr, vt)


def _attn_delta(do, o, n_heads):
    t = o.shape[0]
    tm = _div(t, 1024, LANES)

    def body(do_ref, o_ref, d_ref):
        prod = do_ref[...].astype(F32) * o_ref[...]
        row = jnp.sum(jnp.transpose(prod), axis=0, keepdims=True)
        d_ref[...] = jnp.broadcast_to(row, (SUBLANES, tm))

    return pl.pallas_call(
        body, name="attn_delta", grid=(n_heads, t // tm),
        in_specs=[pl.BlockSpec((tm, HEAD_DIM), lambda h, i: (i, h)),
                  pl.BlockSpec((tm, HEAD_DIM), lambda h, i: (i, h))],
        out_specs=pl.BlockSpec((None, SUBLANES, tm), lambda h, i: (h, 0, i)),
        out_shape=jax.ShapeDtypeStruct((n_heads, SUBLANES, t), F32),
        compiler_params=_params(("parallel", "parallel")))(do, o)


def _flash_bwd(qr, kr, z, v_col0, do, lse, delta, n_heads):
    t = qr.shape[0]
    tq, tk = _div(t, 2 * FLASH_TILE, LANES), _div(t, FLASH_TILE, LANES)
    nq = t // tq

    def body(q_ref, k_ref, v_ref, do_ref, lse_ref, dl_ref, dq_ref, dk_ref, dv_ref, dk_acc, dv_acc):
        kb, qb = pl.program_id(1), pl.program_id(2)
        q, k, do_ = q_ref[...], k_ref[...], do_ref[...]
        v = v_ref[...].astype(BF16)
        s_t = lax.dot_general(k, q, (((1,), (1,)), ((), ())), preferred_element_type=F32)
        p_t = jnp.exp2(s_t - lse_ref[0:1, :])
        dp_t = lax.dot_general(v, do_, (((1,), (1,)), ((), ())), preferred_element_type=F32)
        ds_t = (p_t * (dp_t - dl_ref[0:1, :])).astype(BF16)
        dv_c = jnp.dot(p_t.astype(BF16), do_, preferred_element_type=F32)
        dk_c = jnp.dot(ds_t, q, preferred_element_type=F32) * LN_2
        dq_c = lax.dot_general(ds_t, k, (((0,), (0,)), ((), ())), preferred_element_type=F32) * QK_SCALE

        @pl.when(qb == 0)
        def _():
            dk_acc[...] = dk_c
            dv_acc[...] = dv_c

        @pl.when(qb > 0)
        def _():
            dk_acc[...] += dk_c
            dv_acc[...] += dv_c

        @pl.when(qb == nq - 1)
        def _():
            dk_ref[...] = dk_acc[...]
            dv_ref[...] = dv_acc[...]

        rows = pl.ds(pl.multiple_of(qb * tq, tq), tq)

        @pl.when(kb == 0)
        def _():
            dq_ref[rows, :] = dq_c

        @pl.when(kb > 0)
        def _():
            dq_ref[rows, :] += dq_c

    wide = jax.ShapeDtypeStruct((t, n_heads * HEAD_DIM), F32)
    return pl.pallas_call(
        body, name="flash_bwd", grid=(n_heads, t // tk, nq),
        in_specs=[pl.BlockSpec((tq, HEAD_DIM), lambda h, kb, qb: (qb, h)),
                  pl.BlockSpec((tk, HEAD_DIM), lambda h, kb, qb: (kb, h // GROUP)),
                  pl.BlockSpec((tk, HEAD_DIM), lambda h, kb, qb: (kb, v_col0 + h // GROUP)),
                  pl.BlockSpec((tq, HEAD_DIM), lambda h, kb, qb: (qb, h)),
                  pl.BlockSpec((None, SUBLANES, tq), lambda h, kb, qb: (h, 0, qb)),
                  pl.BlockSpec((None, SUBLANES, tq), lambda h, kb, qb: (h, 0, qb))],
        out_specs=[pl.BlockSpec((t, HEAD_DIM), lambda h, kb, qb: (0, h)),
                   pl.BlockSpec((tk, HEAD_DIM), lambda h, kb, qb: (kb, h)),
                   pl.BlockSpec((tk, HEAD_DIM), lambda h, kb, qb: (kb, h))],
        out_shape=[wide, wide, wide],
        scratch_shapes=[pltpu.VMEM((tk, HEAD_DIM), F32), pltpu.VMEM((tk, HEAD_DIM), F32)],
        compiler_params=_params(("parallel", "arbitrary", "arbitrary")))(qr, kr, z, do, lse, delta)


def _position():
    x, y, c = lax.axis_index("x"), lax.axis_index("y"), lax.axis_index("c")
    return x, y, c


def _index(x, y, c):
    return 4 * x + 2 * y + c


def _all_gather(name, shard):
    a, b = shard.shape

    def body(x_ref, out_ref, send_sems, recv_sems, local_sem):
        x, y, c = _position()
        me, sibling = (x, y, c), (x, y, 1 - c)
        chips = [(1 - x, y), (x, 1 - y), (1 - x, 1 - y)]

        def block(px, py, pc):
            return out_ref.at[_index(px, py, pc)]

        def copy(k, blk, to, src=None):
            return pltpu.make_async_remote_copy(
                src_ref=block(*blk) if src is None else src, dst_ref=block(*blk),
                send_sem=send_sems.at[k], recv_sem=recv_sems.at[k], device_id=to, device_id_type=MESH)

        mine = pltpu.make_async_copy(x_ref, block(*me), local_sem)
        mine.start()
        first = [copy(0, me, sibling, src=x_ref)]
        first += [copy(1 + j, me, (*chip, c), src=x_ref) for j, chip in enumerate(chips)]
        for cp in first:
            cp.start()
        passed = [copy(4 + j, (*chip, c), sibling) for j, chip in enumerate(chips)]
        for j, chip in enumerate(chips):
            copy(1 + j, (*chip, c), me).wait_recv()
            passed[j].start()
        copy(0, sibling, me).wait_recv()
        for j, chip in enumerate(chips):
            copy(4 + j, (*chip, 1 - c), me).wait_recv()
        for cp in first + passed:
            cp.wait_send()
        mine.wait()

    return pl.pallas_call(
        body, name=name, out_shape=jax.ShapeDtypeStruct((N_DEV, a, b), shard.dtype),
        in_specs=[pl.BlockSpec(memory_space=pltpu.HBM)], out_specs=pl.BlockSpec(memory_space=pltpu.HBM),
        scratch_shapes=[pltpu.SemaphoreType.DMA((7,)), pltpu.SemaphoreType.DMA((7,)), pltpu.SemaphoreType.DMA],
    )(shard)


HBM_SPEC = pl.BlockSpec(memory_space=pltpu.HBM)
SEM_SPEC = pl.BlockSpec(memory_space=pltpu.SEMAPHORE)
N_PEERS = N_DEV - 1
PEER_FLIPS = [(fx, fy, fc) for fx in (0, 1) for fy in (0, 1) for fc in (0, 1)][1:]


def _peers(x, y, c):
    return [((1 - x) if fx else x, (1 - y) if fy else y, (1 - c) if fc else c) for fx, fy, fc in PEER_FLIPS]


def _exchange_copies(g_ref, land_ref, send_sems, recv_sems):
    x, y, c = _position()
    me = _index(x, y, c)
    peers = _peers(x, y, c)
    blocked = len(g_ref.shape) == 3

    def copy(k, src_block, dst_block):
        return pltpu.make_async_remote_copy(
            src_ref=g_ref.at[src_block] if blocked else g_ref, dst_ref=land_ref.at[dst_block],
            send_sem=send_sems.at[k], recv_sem=recv_sems.at[k], device_id=peers[k], device_id_type=MESH)

    sent = [copy(k, _index(*peer), me) for k, peer in enumerate(peers)]
    arriving = [copy(k, me, _index(*peer)) for k, peer in enumerate(peers)]
    own = pltpu.make_async_copy(g_ref.at[me] if blocked else g_ref, land_ref.at[me], send_sems.at[N_PEERS])
    return sent, arriving, own


def _exchange_start(name, g3, after=()):
    def body(g_ref, land_ref, *rest):
        send_sems, recv_sems, _, _, token = rest[len(after):]
        sent, _, own = _exchange_copies(g_ref, land_ref, send_sems, recv_sems)
        for cp in sent:
            cp.start()
        own.start()
        token[...] = jnp.zeros_like(token)

    land_shape = g3.shape if g3.ndim == 3 else (N_DEV,) + g3.shape
    land = lax.empty(land_shape, g3.dtype)
    send_sems, recv_sems, g_thru, land_thru, token = pl.pallas_call(
        body, name=name,
        out_shape=(pltpu.SemaphoreType.DMA((N_PEERS + 1,)), pltpu.SemaphoreType.DMA((N_PEERS,)),
                   pltpu.HBM(g3.shape, g3.dtype), pltpu.HBM(land_shape, g3.dtype),
                   jax.ShapeDtypeStruct((SUBLANES, LANES), F32)),
        in_specs=(HBM_SPEC, HBM_SPEC) + (pl.BlockSpec(memory_space=pl.ANY),) * len(after),
        out_specs=(SEM_SPEC, SEM_SPEC, HBM_SPEC, HBM_SPEC, pl.BlockSpec(memory_space=pltpu.VMEM)),
        input_output_aliases={0: 2, 1: 3},
        compiler_params=pltpu.CompilerParams(has_side_effects=pltpu.SideEffectType.DATAFLOW_SIDE_EFFECTING),
    )(pltpu.with_memory_space_constraint(g3, pltpu.HBM), pltpu.with_memory_space_constraint(land, pltpu.HBM), *after)
    return (send_sems, recv_sems, g_thru, land_thru), token


def _exchange_wait(name, handle, after):
    send_sems, recv_sems, g_thru, land_thru = handle

    def body(g_ref, land_ref, send_sems, recv_sems, after_ref, g_dead, land_out):
        sent, arriving, own = _exchange_copies(g_ref, land_ref, send_sems, recv_sems)
        for cp in sent:
            cp.wait_send()
        for cp in arriving:
            cp.wait_recv()
        own.wait()

    return pl.pallas_call(
        body, name=name,
        out_shape=(pltpu.HBM(g_thru.shape, g_thru.dtype), pltpu.HBM(land_thru.shape, land_thru.dtype)),
        in_specs=(HBM_SPEC, HBM_SPEC, SEM_SPEC, SEM_SPEC, pl.BlockSpec(memory_space=pl.ANY)),
        out_specs=(HBM_SPEC, HBM_SPEC), input_output_aliases={0: 0, 1: 1},
        compiler_params=pltpu.CompilerParams(has_side_effects=pltpu.SideEffectType.DATAFLOW_SIDE_EFFECTING),
    )(g_thru, land_thru, send_sems, recv_sems, after)[1]


def _adamw(w, g, m, v):
    m = ADAM_B1 * m + (1.0 - ADAM_B1) * g
    v = ADAM_B2 * v + (1.0 - ADAM_B2) * (g * g)
    m_hat = m / (1.0 - ADAM_B1 ** ADAM_STEP)
    v_hat = v / (1.0 - ADAM_B2 ** ADAM_STEP)
    delta = -ADAM_LR * (m_hat / (jnp.sqrt(v_hat) + ADAM_EPS) + ADAM_WD * w)
    return delta, m, v


def _adamw_shard(name, land, w, m, v):
    _, a, b = land.shape
    tm = _div(a, max(16, ROW_TILE_ELEMS // (2 * b)), 16)
    per = a // tm

    def fn(i, nrow, *t):
        g = t[0].astype(F32)
        for s in range(1, N_DEV):
            g = g + t[s].astype(F32)
        w, m, v = t[N_DEV:]
        return (g,) + _adamw(w, g, m, v)

    land2 = land.reshape(N_DEV * a, b)
    ins = [_t(land2, row0=s * per) for s in range(N_DEV)] + [_t(w), _t(m), _t(v)]
    return _rowwise(name, fn, a, tm, ins, [(F32, b, b)] * 4)


def _adamw_small(name, parts, w, m, v):
    rows, n = w.shape

    def body(p_ref, w_ref, m_ref, v_ref, g_out, d_out, m_out, v_out):
        g = p_ref[0]
        for s in range(1, parts.shape[0]):
            g = g + p_ref[s]
        g = jnp.sum(g, axis=0, keepdims=True) if rows == 1 else g[0:rows, :]
        delta, m_new, v_new = _adamw(w_ref[...], g, m_ref[...], v_ref[...])
        g_out[...] = g
        d_out[...] = delta
        m_out[...] = m_new
        v_out[...] = v_new

    return pl.pallas_call(body, name=name, out_shape=[jax.ShapeDtypeStruct((rows, n), F32)] * 4)(parts, w, m, v)


def _rope_tables(t):
    pos = jnp.arange(t, dtype=jnp.int32)
    half = HEAD_DIM // 4
    inv_freq = ROPE_THETA ** (-jnp.arange(0, 2 * half, 2, dtype=F32) / (2 * half))
    ang_r = (pos // GRID_W).astype(F32)[:, None] * inv_freq[None, :]
    ang_c = (pos % GRID_W).astype(F32)[:, None] * inv_freq[None, :]
    cos = jnp.concatenate([jnp.cos(ang_r)] * 2 + [jnp.cos(ang_c)] * 2, axis=-1)
    sin = jnp.concatenate([-jnp.sin(ang_r), jnp.sin(ang_r), -jnp.sin(ang_c), jnp.sin(ang_c)], axis=-1)
    return cos, sin


def _gather_weight(name, w, cols):
    g = _all_gather(name, w[0].astype(BF16))
    return g if cols else g.reshape(1, N_DEV * g.shape[1], g.shape[2])


def kernel(x, p, norm_mix, w_in, w_dw, conv_ln_g, conv_ln_b, w_conv_proj, q_norm, k_norm, w_attn_proj, w_out, norm_ffn, w_ff1, w_ff2, norm_ple, w_ple_gate, w_ple_proj, norm_final, loss_target, m_norm_mix, m_w_in, m_w_dw, m_conv_ln_g, m_conv_ln_b, m_w_conv_proj, m_q_norm, m_k_norm, m_w_attn_proj, m_w_out, m_norm_ffn, m_w_ff1, m_w_ff2, m_norm_ple, m_w_ple_gate, m_w_ple_proj, m_norm_final, v_norm_mix, v_w_in, v_w_dw, v_conv_ln_g, v_conv_ln_b, v_w_conv_proj, v_q_norm, v_k_norm, v_w_attn_proj, v_w_out, v_norm_ffn, v_w_ff1, v_w_ff2, v_norm_ple, v_w_ple_gate, v_w_ple_proj, v_norm_final):
    t, d = x.shape[1], x.shape[2]
    cw = d // 2
    n_heads = d // HEAD_DIM
    n_kv = n_heads // GROUP
    col_q, col_k, col_v = d // HEAD_DIM, 2 * d // HEAD_DIM, (2 * d + d // 4) // HEAD_DIM
    x0, pe, tgt = x[0], p[0, 0], loss_target[0]
    g_final = norm_final.reshape(1, d)
    me = _index(*_position())

    wg_in = _gather_weight("ag_w_in", w_in, True)
    dw_pad = jnp.pad(w_dw[0], ((0, CONV_TAPS_PAD - CONV_KERNEL), (0, 0)))
    dw_all = _all_gather("ag_w_dw", dw_pad)
    dw_full = dw_all.transpose(1, 0, 2).reshape(CONV_TAPS_PAD, cw)
    cos, sin = _rope_tables(t)

    def fetch(name, w, prev):
        return _exchange_start("ag_start_" + name, w[0].astype(BF16), after=(wg_in, dw_all) if prev is None else (prev,))

    def arrived(name, handle, cols, after):
        g = _exchange_wait("ag_wait_" + name, handle, after)
        return g if cols else g.reshape(1, N_DEV * g.shape[1], g.shape[2])

    on_cp, tok = fetch("w_conv_proj", w_conv_proj, None)
    on_ap, tok = fetch("w_attn_proj", w_attn_proj, tok)
    on_out, tok = fetch("w_out", w_out, tok)
    on_ff1, tok = fetch("w_ff1", w_ff1, tok)
    on_ff2, tok = fetch("w_ff2", w_ff2, tok)
    on_pg, tok = fetch("w_ple_gate", w_ple_gate, tok)
    on_pp, tok = fetch("w_ple_proj", w_ple_proj, tok)

    h1 = _rms_fwd("rms_mix", x0, norm_mix, after=(tok,))
    z = _mm_nn("mm_in", h1, wg_in, F32)
    conv, s_c = _conv_fwd("conv_fwd", z, cw, dw_full, conv_ln_g, conv_ln_b)
    wg_cp = arrived("w_conv_proj", on_cp, True, s_c)
    y_c = _mm_nn("mm_conv_proj", s_c, wg_cp, F32)
    qr = _head_norm_rope("q_prep", z, col_q, n_heads, q_norm, cos, sin, out_scale=QK_SCALE * LOG2_E,
                         heads_per_step=GROUP)
    kr = _head_norm_rope("k_prep", z, col_k, n_kv, k_norm, cos, sin)
    vt = z[:, col_v * HEAD_DIM:(col_v + n_kv) * HEAD_DIM].T.astype(BF16)
    o, lse = _flash_fwd(qr, kr, vt, n_heads)
    wg_ap = arrived("w_attn_proj", on_ap, False, o)
    y_a = _mm_nn("mm_attn_proj", o, wg_ap, F32)
    merged = _merge_fwd("merge_fwd", z, d, y_c, y_a)
    wg_out = arrived("w_out", on_out, False, merged)
    x1 = _mm_nn("mm_out", merged, wg_out, F32, residual=x0)
    h2 = _rms_fwd("rms_ffn", x1, norm_ffn)
    wg_ff1 = arrived("w_ff1", on_ff1, True, h2)
    f, act = _mm_nn("mm_ff1", h2, wg_ff1, F32, activation=_relu2)
    wg_ff2 = arrived("w_ff2", on_ff2, False, act)
    x2 = _mm_nn("mm_ff2", act, wg_ff2, F32, residual=x1)
    hp = _rms_fwd("rms_ple", x2, norm_ple)
    wg_pg = arrived("w_ple_gate", on_pg, False, hp)
    pre = _mm_nn("mm_ple_gate", hp, wg_pg, F32)
    wg_pp = arrived("w_ple_proj", on_pp, True, pre)
    pp = _mm_nn("mm_ple_proj", pe, wg_pp, F32)

    dx3, dpp, dpre, dg_final, loss_part = _head_loss("head_loss", x2, pre, pp, tgt, g_final)
    loss = lax.psum(loss_part[0, 0], ("x", "y", "c"))
    def send(name, gw):
        blocks = gw if gw.shape[0] == N_DEV else gw.reshape(N_DEV, gw.shape[1] // N_DEV, gw.shape[2])
        return _exchange_start("rs_start_" + name, blocks)

    sent_pp, tok_pp = send("w_ple_proj", _mm_tn("mm_d_ple_proj", pe, dpp, N_DEV))
    sent_pg, tok = send("w_ple_gate", _mm_tn("mm_d_ple_gate", hp, dpre, 1))
    dhp = _mm_nt("mm_dhp", dpre, wg_pg, F32, after=(tok_pp, tok))
    dx2, dg_ple = _rms_bwd("rms_ple_bwd", x2, norm_ple, dhp, dx3)
    sent_ff2, tok = send("w_ff2", _mm_tn("mm_d_ff2", act, dx2, 1))
    df = _mm_nt("mm_dact", dx2, wg_ff2, BF16, after=(tok,), through=(_relu2_bwd, f))
    sent_ff1, tok = send("w_ff1", _mm_tn("mm_d_ff1", h2, df, N_DEV))
    dh2 = _mm_nt("mm_dh2", df, wg_ff1, F32, after=(tok,))
    dx1, dg_ffn = _rms_bwd("rms_ffn_bwd", x1, norm_ffn, dh2, dx2)
    sent_out, tok = send("w_out", _mm_tn("mm_d_out", merged, dx1, 1))
    dmerged = _mm_nt("mm_dmerged", dx1, wg_out, F32, after=(tok,))
    dgc, dga, dyc, dya = _merge_bwd("merge_bwd", dmerged, z, d, y_c, y_a)
    sent_cp, tok = send("w_conv_proj", _mm_tn("mm_d_conv_proj", s_c, dyc, N_DEV))
    ds_c = _mm_nt("mm_ds_c", dyc, wg_cp, F32, after=(tok,))
    dconv, dg_ln, db_ln = _conv_ln_bwd("conv_ln_bwd", ds_c, conv, conv_ln_g, conv_ln_b)
    dca, dcb, dg_dw = _conv_glu_bwd("conv_glu_bwd", dconv, z, cw, dw_full)
    sent_ap, tok = send("w_attn_proj", _mm_tn("mm_d_attn_proj", o, dya, 1))
    do = _mm_nt("mm_do", dya, wg_ap, BF16, after=(tok,))
    delta = _attn_delta(do, o, n_heads)
    dqr, dkh, dvh = _flash_bwd(qr, kr, z, col_v, do, lse, delta, n_heads)
    dq, dg_q = _head_norm_rope_bwd("q_prep_bwd", [(dqr, 0, 1)], z, col_q, n_heads, q_norm, cos, sin,
                                   heads_per_step=GROUP)
    dk, dg_k = _head_norm_rope_bwd("k_prep_bwd", [(dkh, g, GROUP) for g in range(GROUP)], z, col_k, n_kv, k_norm,
                                   cos, sin)
    dv = _group_sum("dv_sum", dvh, n_kv)
    dz = jnp.concatenate([dca, dcb, dq, dk, dv, dgc, dga], axis=1)
    sent_in, tok = send("w_in", _mm_tn("mm_d_in", h1, dz, N_DEV))
    dh1 = _mm_nt("mm_dh1", dz, wg_in, F32, after=(tok,))
    grad_x, dg_mix = _rms_bwd("rms_mix_bwd", x0, norm_mix, dh1, dx1)

    def pad(a):
        return jnp.pad(a, ((0, 0), (0, d - a.shape[1])))

    small = [dg_mix, dg_ffn, dg_ple, dg_final, dg_ln, db_ln, dg_q, dg_k, dg_dw]
    packed = _all_gather("ag_small", jnp.concatenate([pad(a) for a in small], axis=0))
    parts, row = [], 0
    for a in small:
        parts.append(packed[:, row:row + a.shape[0], :a.shape[1]])
        row += a.shape[0]
    p_mix, p_ffn, p_ple, p_final, p_lng, p_lnb, p_q, p_k, p_dw = parts
    p_dw = lax.dynamic_slice_in_dim(p_dw, me * (cw // N_DEV), cw // N_DEV, axis=2)

    def little(name, part, w, m, v):
        rows, n = (1 if w.ndim < 3 else w.shape[1]), w.shape[-1]
        padded = rows if rows == 1 else part.shape[1]

        def two_d(a):
            return jnp.pad(a.reshape(rows, n), ((0, padded - rows), (0, 0)))

        res = _adamw_small("adamw_" + name, part, two_d(w), two_d(m), two_d(v))
        return [r[:rows].reshape(w.shape) for r in res]

    done = {}
    after = packed
    for name, sent, w, m, v in [
            ("w_ple_proj", sent_pp, w_ple_proj, m_w_ple_proj, v_w_ple_proj),
            ("w_ple_gate", sent_pg, w_ple_gate, m_w_ple_gate, v_w_ple_gate),
            ("w_ff2", sent_ff2, w_ff2, m_w_ff2, v_w_ff2),
            ("w_ff1", sent_ff1, w_ff1, m_w_ff1, v_w_ff1),
            ("w_out", sent_out, w_out, m_w_out, v_w_out),
            ("w_conv_proj", sent_cp, w_conv_proj, m_w_conv_proj, v_w_conv_proj),
            ("w_attn_proj", sent_ap, w_attn_proj, m_w_attn_proj, v_w_attn_proj),
            ("w_in", sent_in, w_in, m_w_in, v_w_in)]:
        land = _exchange_wait("rs_wait_" + name, sent, after)
        res = _adamw_shard("adamw_" + name, land, w[0], m[0], v[0])
        after = res[0]
        done[name] = [r.reshape(w.shape) for r in res]

    results = [
        little("norm_mix", p_mix, norm_mix, m_norm_mix, v_norm_mix),
        done["w_in"],
        little("w_dw", p_dw, w_dw, m_w_dw, v_w_dw),
        little("conv_ln_g", p_lng, conv_ln_g, m_conv_ln_g, v_conv_ln_g),
        little("conv_ln_b", p_lnb, conv_ln_b, m_conv_ln_b, v_conv_ln_b),
        done["w_conv_proj"],
        little("q_norm", p_q, q_norm, m_q_norm, v_q_norm),
        little("k_norm", p_k, k_norm, m_k_norm, v_k_norm),
        done["w_attn_proj"],
        done["w_out"],
        little("norm_ffn", p_ffn, norm_ffn, m_norm_ffn, v_norm_ffn),
        done["w_ff1"],
        done["w_ff2"],
        little("norm_ple", p_ple, norm_ple, m_norm_ple, v_norm_ple),
        done["w_ple_gate"],
        done["w_ple_proj"],
        little("norm_final", p_final, norm_final, m_norm_final, v_norm_final),
    ]
    grads, deltas, new_m, new_v = zip(*results)
    return (loss, grad_x[None], *grads, *deltas, *new_m, *new_v)
```

```python
import functools

import jax
import jax.numpy as jnp
from jax import lax
from jax.experimental import pallas as pl
from jax.experimental.pallas import tpu as pltpu

F32 = jnp.float32
BF16 = jnp.bfloat16

EPS = 1e-6
HEAD_DIM = 128
GROUP = 4
CONV_KERNEL = 31
CONV_HALO = 16
CONV_TAPS_PAD = 32
GRID_W = 64
ROPE_THETA = 10000.0
ADAM_LR, ADAM_B1, ADAM_B2, ADAM_EPS, ADAM_WD, ADAM_STEP = 0.001, 0.9, 0.999, 1e-08, 0.01, 10

N_DEV = 8
MESH = pl.DeviceIdType.MESH
SUBLANES = 8
LANES = 128
VMEM_LIMIT = 48 * 1024 * 1024
ROW_TILE_ELEMS = 512 * 1024


def _div(n, pref, mult=1):
    if n <= pref:
        return n
    for d in range(pref, 0, -1):
        if n % d == 0 and d % mult == 0:
            return d
    return n


def _params(sem):
    return pltpu.CompilerParams(dimension_semantics=sem, vmem_limit_bytes=VMEM_LIMIT)


def _sum8(a):
    tm, w = a.shape
    return a.reshape(tm // SUBLANES, SUBLANES, w).sum(axis=0)


def _sigmoid(x):
    return jax.nn.sigmoid(x)


def _rowwise(name, fn, n_rows, tm, ins, outs, accs=(), ncol=1, refs_to_fn=False, after=()):
    nrow = n_rows // tm
    in_specs, arrays = [], []
    for spec in ins:
        kind, arr, w, c0 = spec[:4]
        if kind == "t":
            cs, r0 = spec[4], spec[5]
            in_specs.append(pl.BlockSpec((tm, w), lambda j, i, c0=c0, cs=cs, r0=r0: (r0 + i, c0 + cs * j)))
        elif kind == "p":
            h = spec[4]
            r = tm // h
            in_specs.append(pl.BlockSpec((h, w), lambda j, i, c0=c0, r=r: (jnp.maximum(i * r - 1, 0), c0 + j)))
        elif kind == "n":
            h = spec[4]
            r = tm // h
            last = n_rows // h - 1
            in_specs.append(
                pl.BlockSpec((h, w), lambda j, i, c0=c0, r=r, last=last: (jnp.minimum((i + 1) * r, last), c0 + j)))
        elif kind == "x":
            in_specs.append(pl.BlockSpec((w, tm), lambda j, i, c0=c0: (c0 + j, i)))
        else:
            cs = spec[4]
            in_specs.append(pl.BlockSpec((arr.shape[0], w), lambda j, i, c0=c0, cs=cs: (0, c0 + cs * j)))
        arrays.append(arr)
    out_shape, out_specs = [], []
    for dtype, total, w in outs:
        out_shape.append(jax.ShapeDtypeStruct((n_rows, total), dtype))
        out_specs.append(pl.BlockSpec((tm, w), lambda j, i: (i, j)))
    for rows, total, w, follow in accs:
        out_shape.append(jax.ShapeDtypeStruct((rows, total), F32))
        out_specs.append(pl.BlockSpec((rows, w), (lambda j, i: (0, j)) if follow else (lambda j, i: (0, 0))))
    n_in, n_out, n_acc = len(ins), len(outs), len(accs)
    for token in after:
        in_specs.append(pl.BlockSpec(memory_space=pl.ANY))
        arrays.append(token)
    out0 = n_in + len(after)

    def body(*refs):
        j, i = pl.program_id(0), pl.program_id(1)
        res = fn(i, nrow, *(refs[:n_in] if refs_to_fn else [r[...] for r in refs[:n_in]]))
        if not isinstance(res, (tuple, list)):
            res = (res,)
        for k in range(n_out):
            refs[out0 + k][...] = res[k].astype(outs[k][0])
        for k in range(n_acc):
            ref, term = refs[out0 + n_out + k], res[n_out + k]
            first = (i == 0) if accs[k][3] else jnp.logical_and(i == 0, j == 0)

            @pl.when(first)
            def _():
                ref[...] = term

            @pl.when(jnp.logical_not(first))
            def _():
                ref[...] += term

    res = pl.pallas_call(
        body, name=name, grid=(ncol, nrow), in_specs=in_specs, out_specs=out_specs, out_shape=out_shape,
        compiler_params=_params(("arbitrary", "arbitrary")))(*arrays)
    return res


def _row_tile(n_rows, width, mult=SUBLANES):
    return _div(n_rows, max(mult, ROW_TILE_ELEMS // width), mult)


def _t(arr, w=None, col0=0, cstride=1, row0=0):
    return ("t", arr, arr.shape[1] if w is None else w, col0, cstride, row0)


def _b(arr, w=None, col0=0, cstride=0):
    return ("b", arr, arr.shape[1] if w is None else w, col0, cstride)


def _rms_fwd(name, x, g, after=()):
    n, d = x.shape

    def fn(i, nrow, x, g):
        r = lax.rsqrt(jnp.mean(x * x, axis=-1, keepdims=True) + EPS)
        return x * r * g

    return _rowwise(name, fn, n, _row_tile(n, d), [_t(x), _b(g)], [(BF16, d, d)], after=after)[0]


def _rms_bwd(name, x, g, dh, dres):
    n, d = x.shape

    def fn(i, nrow, x, g, dh, dres):
        r = lax.rsqrt(jnp.mean(x * x, axis=-1, keepdims=True) + EPS)
        w = dh * g
        dx = dres + r * w - x * (r * r * r) * jnp.mean(x * w, axis=-1, keepdims=True)
        return dx, _sum8(dh * x * r)

    return _rowwise(name, fn, n, _row_tile(n, 2 * d), [_t(x), _b(g), _t(dh), _t(dres)], [(F32, d, d)],
                    [(SUBLANES, d, d, True)])


def _swap_halves(x):
    lane = lax.broadcasted_iota(jnp.int32, x.shape, 1)
    return jnp.where(lane % 64 < 32, pltpu.roll(x, 96, axis=1), pltpu.roll(x, 32, axis=1))


def _head_norm_rope(name, z, col0, n_heads, g, cos, sin, out_scale=None, heads_per_step=1):
    n, hp = z.shape[0], heads_per_step
    assert col0 % hp == 0 and n_heads % hp == 0

    def fn(i, nrow, x, g, cos, sin):
        ys = []
        for h in range(hp):
            xh = x[:, h * HEAD_DIM:(h + 1) * HEAD_DIM]
            r = lax.rsqrt(jnp.mean(xh * xh, axis=-1, keepdims=True) + EPS)
            y = xh * r * g
            y = y * cos + _swap_halves(y) * sin
            ys.append(y if out_scale is None else y * out_scale)
        return ys[0] if hp == 1 else jnp.concatenate(ys, axis=1)

    tm = _div(n, 1024, SUBLANES)
    w = hp * HEAD_DIM
    return _rowwise(name, fn, n, tm, [_t(z, w, col0 // hp), _b(g), _t(cos, cstride=0), _t(sin, cstride=0)],
                    [(BF16, n_heads * HEAD_DIM, w)], ncol=n_heads // hp)[0]


def _head_norm_rope_bwd(name, douts, z, col0, n_heads, g, cos, sin, heads_per_step=1, transposed=False):
    n, hp = z.shape[0], heads_per_step
    n_d = len(douts)
    assert col0 % hp == 0 and n_heads % hp == 0 and (hp == 1 or all(c0 % hp == 0 and cs == 1 for _, c0, cs in douts))
    assert not transposed or n_d == 1

    def fn(i, nrow, *a):
        dy = a[0]
        for k in range(1, n_d):
            dy = dy + a[k]
        x, g, cos, sin = a[n_d:]
        dxs, dg = [], None
        for h in range(hp):
            xh = x[:, h * HEAD_DIM:(h + 1) * HEAD_DIM]
            if transposed:
                dyh = jnp.transpose(dy[h * HEAD_DIM:(h + 1) * HEAD_DIM, :])
            else:
                dyh = dy[:, h * HEAD_DIM:(h + 1) * HEAD_DIM]
            dn = dyh * cos + _swap_halves(dyh * sin)
            r = lax.rsqrt(jnp.mean(xh * xh, axis=-1, keepdims=True) + EPS)
            w = dn * g
            dxs.append(r * w - xh * (r * r * r) * jnp.mean(xh * w, axis=-1, keepdims=True))
            term = _sum8(dn * xh * r)
            dg = term if dg is None else dg + term
        return (dxs[0] if hp == 1 else jnp.concatenate(dxs, axis=1)), dg

    tm = _div(n, 1024, SUBLANES)
    w = hp * HEAD_DIM
    if transposed:
        ins = [("x", douts[0][0], w, douts[0][1] // hp)]
    else:
        ins = [_t(arr, w, c0 // hp, cs) for arr, c0, cs in douts]
    ins += [_t(z, w, col0 // hp), _b(g), _t(cos, cstride=0), _t(sin, cstride=0)]
    return _rowwise(name, fn, n, tm, ins, [(BF16, n_heads * HEAD_DIM, w)],
                    [(SUBLANES, HEAD_DIM, HEAD_DIM, False)], ncol=n_heads // hp)


def _halo(arr, w, col0):
    return [("p", arr, w, col0, CONV_HALO), _t(arr, w, col0), ("n", arr, w, col0, CONV_HALO)]


def _extend(i, nrow, prev, cur, nxt):
    prev = jnp.where(i > 0, prev, 0.0)
    nxt = jnp.where(i < nrow - 1, nxt, 0.0)
    return jnp.concatenate([prev, cur, nxt], axis=0)


def _shifted(ext, offset, tm):
    n = ext.shape[0]
    rolled = ext if offset == 0 else pltpu.roll(ext, (-offset) % n, axis=0)
    return rolled[CONV_HALO:CONV_HALO + tm]


def _conv_fwd(name, z, cw, w_dw, ln_g, ln_b):
    n = z.shape[0]
    tm = _row_tile(n, 4 * cw, CONV_HALO)

    def fn(i, nrow, ap, a, an, bp, b, bn, w, g, beta):
        a = a[...]
        ext = _extend(i, nrow, ap[...] * _sigmoid(bp[...]), a * _sigmoid(b[...]), an[...] * _sigmoid(bn[...]))
        conv = jnp.zeros_like(a)
        for k in range(CONV_KERNEL):
            conv = conv + _shifted(ext, k - CONV_KERNEL // 2, tm) * w[k:k + 1, :]
        xc = conv - jnp.mean(conv, axis=-1, keepdims=True)
        ln = xc * lax.rsqrt(jnp.mean(xc * xc, axis=-1, keepdims=True) + EPS) * g[...] + beta[...]
        return conv, ln * _sigmoid(ln)

    ins = _halo(z, cw, 0) + _halo(z, cw, 1) + [_b(w_dw), _b(ln_g), _b(ln_b)]
    return _rowwise(name, fn, n, tm, ins, [(F32, cw, cw), (BF16, cw, cw)], refs_to_fn=True)


def _conv_ln_bwd(name, ds, conv, ln_g, ln_b):
    n, cw = conv.shape

    def fn(i, nrow, ds, conv, g, beta):
        xc = conv - jnp.mean(conv, axis=-1, keepdims=True)
        rstd = lax.rsqrt(jnp.mean(xc * xc, axis=-1, keepdims=True) + EPS)
        xhat = xc * rstd
        ln = xhat * g + beta
        sg = _sigmoid(ln)
        dln = ds * (sg * (1.0 + ln * (1.0 - sg)))
        dxh = dln * g
        dconv = rstd * (dxh - jnp.mean(dxh, axis=-1, keepdims=True)
                        - xhat * jnp.mean(dxh * xhat, axis=-1, keepdims=True))
        return dconv, _sum8(dln * xhat), _sum8(dln)

    return _rowwise(name, fn, n, _row_tile(n, 2 * cw), [_t(ds), _t(conv), _b(ln_g), _b(ln_b)], [(F32, cw, cw)],
                    [(SUBLANES, cw, cw, True), (SUBLANES, cw, cw, True)])


def _conv_glu_bwd(name, dconv, z, cw, w_dw):
    n = z.shape[0]
    tm = _row_tile(n, 4 * cw, CONV_HALO)

    def fn(i, nrow, dp, d, dn, ap, a, an, bp, b, bn, w):
        d, a = d[...], a[...]
        dext = _extend(i, nrow, dp[...], d, dn[...])
        sg = _sigmoid(b[...])
        uext = _extend(i, nrow, ap[...] * _sigmoid(bp[...]), a * sg, an[...] * _sigmoid(bn[...]))
        tap = lax.broadcasted_iota(jnp.int32, (CONV_TAPS_PAD, cw), 0)
        du = jnp.zeros_like(d)
        dw = jnp.zeros((CONV_TAPS_PAD, cw), F32)
        for k in range(CONV_KERNEL):
            off = k - CONV_KERNEL // 2
            du = du + _shifted(dext, -off, tm) * w[k:k + 1, :]
            dw_k = jnp.sum(d * _shifted(uext, off, tm), axis=0, keepdims=True)
            dw = dw + jnp.where(tap == k, dw_k, 0.0)
        return du * sg, du * a * sg * (1.0 - sg), dw

    ins = _halo(dconv, cw, 0) + _halo(z, cw, 0) + _halo(z, cw, 1) + [_b(w_dw)]
    return _rowwise(name, fn, n, tm, ins, [(BF16, cw, cw), (BF16, cw, cw)], [(CONV_TAPS_PAD, cw, cw, True)],
                    refs_to_fn=True)


def _merge_fwd(name, z, d, y_c, y_a):
    n, hw = z.shape[0], d // 2

    def fn(i, nrow, gc, ga, yc, ya):
        return _sigmoid(gc) * yc + _sigmoid(ga) * ya

    ins = [_t(z, hw, 5), _t(z, hw, 7), _t(y_c, hw), _t(y_a, hw)]
    return _rowwise(name, fn, n, _row_tile(n, 2 * hw), ins, [(BF16, d, hw)], ncol=2)[0]


def _merge_bwd(name, dm, z, d, y_c, y_a):
    n, hw = z.shape[0], d // 2

    def fn(i, nrow, dm, gc, ga, yc, ya):
        sc, sa = _sigmoid(gc), _sigmoid(ga)
        return dm * yc * sc * (1.0 - sc), dm * ya * sa * (1.0 - sa), dm * sc, dm * sa

    ins = [_t(dm, hw), _t(z, hw, 5), _t(z, hw, 7), _t(y_c, hw), _t(y_a, hw)]
    return _rowwise(name, fn, n, _row_tile(n, 2 * hw), ins, [(BF16, d, hw)] * 4, ncol=2)


def _relu2(f):
    r = jnp.maximum(f, 0.0)
    return r * r


def _relu2_bwd(da, f):
    return da * (2.0 * jnp.maximum(f, 0.0))


def _head_loss(name, x2, pre, pp, target, g):
    n, d = x2.shape

    def fn(i, nrow, x2, pre, pp, t, g):
        gate = _sigmoid(pre)
        x3 = x2 + gate * pp
        r = lax.rsqrt(jnp.mean(x3 * x3, axis=-1, keepdims=True) + EPS)
        xn = x3 * r
        e = xn * g - t
        part = 0.5 * jnp.sum(jnp.mean(e * e, axis=-1, keepdims=True), axis=0, keepdims=True)
        dy = e * (1.0 / d)
        w = dy * g
        dx3 = r * w - x3 * (r * r * r) * jnp.mean(x3 * w, axis=-1, keepdims=True)
        return (dx3, dx3 * gate, dx3 * pp * gate * (1.0 - gate), _sum8(dy * xn),
                jnp.broadcast_to(part, (SUBLANES, LANES)))

    ins = [_t(x2), _t(pre), _t(pp), _t(target), _b(g)]
    return _rowwise(name, fn, n, _row_tile(n, 2 * d), ins, [(F32, d, d), (BF16, d, d), (BF16, d, d)],
                    [(SUBLANES, d, d, True), (SUBLANES, LANES, LANES, True)])


def _group_sum(name, dvh, n_kv):
    n = dvh.shape[0]

    def fn(i, nrow, *a):
        s = a[0]
        for k in range(1, GROUP):
            s = s + a[k]
        return s

    ins = [_t(dvh, HEAD_DIM, k, GROUP) for k in range(GROUP)]
    return _rowwise(name, fn, n, _div(n, 1024, SUBLANES), ins, [(BF16, n_kv * HEAD_DIM, HEAD_DIM)], ncol=n_kv)[0]


MM_ROWS = 1024
MM_COLS = 1152
MM_REDUCE = 2048
MM_REDUCE_BLOCKS = 2304


def _accumulate(step, n_steps, part, acc, finish):
    if n_steps == 1:
        finish(part)
        return

    @pl.when(step == 0)
    def _():
        acc[...] = part

    @pl.when(jnp.logical_and(step > 0, step < n_steps - 1))
    def _():
        acc[...] += part

    @pl.when(step == n_steps - 1)
    def _():
        finish(acc[...] + part)


MXU_WIDTH = 256


def _paired(nb, ns):
    return ns % MXU_WIDTH != 0 and nb % 2 == 0 and ns <= MM_COLS


def _mm_nn(name, a, w3, out_dtype, residual=None, activation=None):
    m, k = a.shape
    nb, _, ns = w3.shape
    jb = 2 if _paired(nb, ns) else 1
    tm = _div(m, MM_ROWS // jb, SUBLANES)
    tn_w, tk = (ns if jb > 1 else _div(ns, MM_COLS, LANES)), _div(k, MM_REDUCE, LANES)
    tn = jb * tn_w
    per, nk = ns // tn_w, k // tk
    has_res = residual is not None
    n_out = 1 if activation is None else 2

    def body(*refs):
        a_ref, w_ref = refs[:2]
        acc = refs[-1]
        outs = refs[-1 - n_out:-1]
        w = w_ref[0] if jb == 1 else jnp.concatenate([w_ref[j] for j in range(jb)], axis=1)
        part = jnp.dot(a_ref[...].astype(BF16), w.astype(BF16), preferred_element_type=F32)

        def finish(total):
            if has_res:
                total = total + refs[2][...]
            outs[0][...] = total.astype(out_dtype)
            if activation is not None:
                outs[1][...] = activation(total).astype(BF16)

        _accumulate(pl.program_id(2), nk, part, acc, finish)

    in_specs = [pl.BlockSpec((tm, tk), lambda n, i, kk: (i, kk)),
                pl.BlockSpec((jb, tk, tn_w), lambda n, i, kk: (n // per, kk, n % per))]
    args = [a, w3]
    if has_res:
        in_specs.append(pl.BlockSpec((tm, tn), lambda n, i, kk: (i, n)))
        args.append(residual)
    out_dtypes = [out_dtype] + ([BF16] if activation is not None else [])
    res = pl.pallas_call(
        body, name=name, grid=((nb // jb) * per, m // tm, nk), in_specs=in_specs,
        out_specs=[pl.BlockSpec((tm, tn), lambda n, i, kk: (i, n)) for _ in out_dtypes],
        out_shape=[jax.ShapeDtypeStruct((m, nb * ns), dt) for dt in out_dtypes],
        scratch_shapes=[pltpu.VMEM((tm, tn) if nk > 1 else (SUBLANES, LANES), F32)],
        compiler_params=_params(("parallel", "parallel", "arbitrary")))(*args)
    return res[0] if activation is None else res


def _mm_nt(name, dc, w3, out_dtype, after=(), through=None):
    m = dc.shape[0]
    nb, k, ns = w3.shape
    tm, to = _div(m, MM_ROWS, SUBLANES), _div(k, MM_ROWS, LANES)
    if ns > MM_REDUCE_BLOCKS // 2:
        jb, tr = 1, _div(ns, MM_REDUCE, LANES)
    else:
        jb, tr = max(j for j in range(1, nb + 1) if nb % j == 0 and j * ns <= MM_REDUCE_BLOCKS), ns
    per = ns // tr
    nr = (nb // jb) * per

    def body(dc_ref, w_ref, *rest):
        o_ref, acc = rest[-2:]
        w = w_ref[0] if jb == 1 else jnp.concatenate([w_ref[j] for j in range(jb)], axis=1)
        part = lax.dot_general(dc_ref[...].astype(BF16), w.astype(BF16), (((1,), (1,)), ((), ())),
                               preferred_element_type=F32)

        def finish(total):
            if through is not None:
                total = through[0](total, rest[0][...])
            o_ref[...] = total.astype(out_dtype)

        _accumulate(pl.program_id(2), nr, part, acc, finish)

    in_specs = [pl.BlockSpec((tm, jb * tr), lambda o, i, r: (i, r)),
                pl.BlockSpec((jb, to, tr), lambda o, i, r: (r // per, o, r % per))]
    args = [dc, w3]
    if through is not None:
        in_specs.append(pl.BlockSpec((tm, to), lambda o, i, r: (i, o)))
        args.append(through[1])
    for token in after:
        in_specs.append(pl.BlockSpec(memory_space=pl.ANY))
        args.append(token)
    return pl.pallas_call(
        body, name=name, grid=(k // to, m // tm, nr), in_specs=in_specs,
        out_specs=pl.BlockSpec((tm, to), lambda o, i, r: (i, o)),
        out_shape=jax.ShapeDtypeStruct((m, k), out_dtype),
        scratch_shapes=[pltpu.VMEM((tm, to) if nr > 1 else (SUBLANES, LANES), F32)],
        compiler_params=_params(("parallel", "parallel", "arbitrary")))(*args)


def _mm_tn(name, a, dc, nb):
    m, k = a.shape
    ns = dc.shape[1] // nb
    jb = 2 if _paired(nb, ns) else 1
    tr, to = _div(m, MM_REDUCE, LANES), _div(k, MM_ROWS // jb, LANES)
    tn_w = ns if jb > 1 else _div(ns, MM_COLS, LANES)
    tn = jb * tn_w
    per, nr = ns // tn_w, m // tr

    def body(a_ref, dc_ref, o_ref, acc):
        part = lax.dot_general(a_ref[...].astype(BF16), dc_ref[...].astype(BF16), (((0,), (0,)), ((), ())),
                               preferred_element_type=F32)

        def finish(total):
            for j in range(jb):
                o_ref[j] = total[:, j * tn_w:(j + 1) * tn_w].astype(BF16)

        _accumulate(pl.program_id(2), nr, part, acc, finish)

    return pl.pallas_call(
        body, name=name, grid=(k // to, (nb // jb) * per, nr),
        in_specs=[pl.BlockSpec((tr, to), lambda o, n, r: (r, o)),
                  pl.BlockSpec((tr, tn), lambda o, n, r: (r, n))],
        out_specs=pl.BlockSpec((jb, to, tn_w), lambda o, n, r: (n // per, o, n % per)),
        out_shape=jax.ShapeDtypeStruct((nb, k, ns), BF16),
        scratch_shapes=[pltpu.VMEM((to, tn) if nr > 1 else (SUBLANES, LANES), F32)],
        compiler_params=_params(("parallel", "parallel", "arbitrary")))(a, dc)


QK_SCALE = HEAD_DIM ** -0.5
LOG2_E = 1.4426950408889634
LN_2 = 0.6931471805599453


def _columns(row, n):
    return jnp.transpose(jnp.broadcast_to(row, (LANES, n)))


FLASH_TILE = 1024


def _flash_fwd(qr, kr, vt, n_heads):
    t = qr.shape[0]
    tq, tk = _div(t, FLASH_TILE, LANES), _div(t, 2 * FLASH_TILE, LANES)
    nk = t // tk

    def body(q_ref, k_ref, vt_ref, o_ref, lse_ref, m_scr, l_scr, acc_scr):
        kb = pl.program_id(2)

        @pl.when(kb == 0)
        def _():
            m_scr[...] = jnp.full_like(m_scr, -1e30)
            l_scr[...] = jnp.zeros_like(l_scr)
            acc_scr[...] = jnp.zeros_like(acc_scr)

        s_t = lax.dot_general(k_ref[...], q_ref[...], (((1,), (1,)), ((), ())), preferred_element_type=F32)
        m_prev = m_scr[...]
        m_new = jnp.maximum(m_prev, jnp.max(s_t, axis=0, keepdims=True))
        alpha = jnp.exp2(m_prev - m_new)
        p_t = jnp.exp2(s_t - m_new)
        l_scr[...] = alpha * l_scr[...] + jnp.sum(p_t, axis=0, keepdims=True)
        acc_scr[...] = alpha * acc_scr[...] + jnp.dot(vt_ref[...], p_t.astype(BF16), preferred_element_type=F32)
        m_scr[...] = m_new

        @pl.when(kb == nk - 1)
        def _():
            l = l_scr[...]
            o_ref[...] = jnp.transpose(acc_scr[...] / l)
            lse_ref[...] = jnp.broadcast_to(m_scr[...] + jnp.log(l) * LOG2_E, (SUBLANES, tq))

    return pl.pallas_call(
        body, name="flash_fwd", grid=(n_heads, t // tq, nk),
        in_specs=[pl.BlockSpec((tq, HEAD_DIM), lambda h, qi, ki: (qi, h)),
                  pl.BlockSpec((tk, HEAD_DIM), lambda h, qi, ki: (ki, h // GROUP)),
                  pl.BlockSpec((HEAD_DIM, tk), lambda h, qi, ki: (h // GROUP, ki))],
        out_specs=[pl.BlockSpec((tq, HEAD_DIM), lambda h, qi, ki: (qi, h)),
                   pl.BlockSpec((None, SUBLANES, tq), lambda h, qi, ki: (h, 0, qi))],
        out_shape=[jax.ShapeDtypeStruct((t, n_heads * HEAD_DIM), F32),
                   jax.ShapeDtypeStruct((n_heads, SUBLANES, t), F32)],
        scratch_shapes=[pltpu.VMEM((1, tq), F32), pltpu.VMEM((1, tq), F32), pltpu.VMEM((HEAD_DIM, tq), F32)],
        compiler_params=_params(("parallel", "parallel", "arbitrary")))(qr, kr, vt)


def _attn_delta(do, o, n_heads):
    t = o.shape[0]
    tm = _div(t, 1024, LANES)

    def body(do_ref, o_ref, d_ref):
        prod = do_ref[...].astype(F32) * o_ref[...]
        row = jnp.sum(jnp.transpose(prod), axis=0, keepdims=True)
        d_ref[...] = jnp.broadcast_to(row, (SUBLANES, tm))

    return pl.pallas_call(
        body, name="attn_delta", grid=(n_heads, t // tm),
        in_specs=[pl.BlockSpec((tm, HEAD_DIM), lambda h, i: (i, h)),
                  pl.BlockSpec((tm, HEAD_DIM), lambda h, i: (i, h))],
        out_specs=pl.BlockSpec((None, SUBLANES, tm), lambda h, i: (h, 0, i)),
        out_shape=jax.ShapeDtypeStruct((n_heads, SUBLANES, t), F32),
        compiler_params=_params(("parallel", "parallel")))(do, o)


def _flash_bwd(qr, kr, kt, z, v_col0, do, lse, delta, n_heads):
    t = qr.shape[0]
    tq, tk = _div(t, 2 * FLASH_TILE, LANES), _div(t, FLASH_TILE, LANES)
    nq = t // tq

    def body(q_ref, k_ref, kt_ref, v_ref, do_ref, lse_ref, dl_ref, dq_ref, dk_ref, dv_ref, dk_acc, dv_acc):
        kb, qb = pl.program_id(1), pl.program_id(2)
        q, k, do_ = q_ref[...], k_ref[...], do_ref[...]
        v = v_ref[...].astype(BF16)
        s_t = lax.dot_general(k, q, (((1,), (1,)), ((), ())), preferred_element_type=F32)
        p_t = jnp.exp2(s_t - lse_ref[0:1, :])
        dp_t = lax.dot_general(v, do_, (((1,), (1,)), ((), ())), preferred_element_type=F32)
        ds_t = (p_t * (dp_t - dl_ref[0:1, :])).astype(BF16)
        dv_c = jnp.dot(p_t.astype(BF16), do_, preferred_element_type=F32)
        dk_c = jnp.dot(ds_t, q, preferred_element_type=F32) * LN_2
        dq_c = jnp.dot(kt_ref[...], ds_t, preferred_element_type=F32) * QK_SCALE

        @pl.when(qb == 0)
        def _():
            dk_acc[...] = dk_c
            dv_acc[...] = dv_c

        @pl.when(qb > 0)
        def _():
            dk_acc[...] += dk_c
            dv_acc[...] += dv_c

        @pl.when(qb == nq - 1)
        def _():
            dk_ref[...] = dk_acc[...]
            dv_ref[...] = dv_acc[...]

        cols = pl.ds(pl.multiple_of(qb * tq, tq), tq)

        @pl.when(kb == 0)
        def _():
            dq_ref[:, cols] = dq_c

        @pl.when(kb > 0)
        def _():
            dq_ref[:, cols] += dq_c

    wide = jax.ShapeDtypeStruct((t, n_heads * HEAD_DIM), F32)
    return pl.pallas_call(
        body, name="flash_bwd", grid=(n_heads, t // tk, nq),
        in_specs=[pl.BlockSpec((tq, HEAD_DIM), lambda h, kb, qb: (qb, h)),
                  pl.BlockSpec((tk, HEAD_DIM), lambda h, kb, qb: (kb, h // GROUP)),
                  pl.BlockSpec((HEAD_DIM, tk), lambda h, kb, qb: (h // GROUP, kb)),
                  pl.BlockSpec((tk, HEAD_DIM), lambda h, kb, qb: (kb, v_col0 + h // GROUP)),
                  pl.BlockSpec((tq, HEAD_DIM), lambda h, kb, qb: (qb, h)),
                  pl.BlockSpec((None, SUBLANES, tq), lambda h, kb, qb: (h, 0, qb)),
                  pl.BlockSpec((None, SUBLANES, tq), lambda h, kb, qb: (h, 0, qb))],
        out_specs=[pl.BlockSpec((HEAD_DIM, t), lambda h, kb, qb: (h, 0)),
                   pl.BlockSpec((tk, HEAD_DIM), lambda h, kb, qb: (kb, h)),
                   pl.BlockSpec((tk, HEAD_DIM), lambda h, kb, qb: (kb, h))],
        out_shape=[jax.ShapeDtypeStruct((n_heads * HEAD_DIM, t), F32), wide, wide],
        scratch_shapes=[pltpu.VMEM((tk, HEAD_DIM), F32), pltpu.VMEM((tk, HEAD_DIM), F32)],
        compiler_params=_params(("parallel", "arbitrary", "arbitrary")))(qr, kr, kt, z, do, lse, delta)


def _position():
    x, y, c = lax.axis_index("x"), lax.axis_index("y"), lax.axis_index("c")
    return x, y, c


def _index(x, y, c):
    return 4 * x + 2 * y + c


def _all_gather(name, shard):
    a, b = shard.shape

    def body(x_ref, out_ref, send_sems, recv_sems, local_sem):
        x, y, c = _position()
        me, sibling = (x, y, c), (x, y, 1 - c)
        chips = [(1 - x, y), (x, 1 - y), (1 - x, 1 - y)]

        def block(px, py, pc):
            return out_ref.at[_index(px, py, pc)]

        def copy(k, blk, to, src=None):
            return pltpu.make_async_remote_copy(
                src_ref=block(*blk) if src is None else src, dst_ref=block(*blk),
                send_sem=send_sems.at[k], recv_sem=recv_sems.at[k], device_id=to, device_id_type=MESH)

        mine = pltpu.make_async_copy(x_ref, block(*me), local_sem)
        mine.start()
        first = [copy(0, me, sibling, src=x_ref)]
        first += [copy(1 + j, me, (*chip, c), src=x_ref) for j, chip in enumerate(chips)]
        for cp in first:
            cp.start()
        passed = [copy(4 + j, (*chip, c), sibling) for j, chip in enumerate(chips)]
        for j, chip in enumerate(chips):
            copy(1 + j, (*chip, c), me).wait_recv()
            passed[j].start()
        copy(0, sibling, me).wait_recv()
        for j, chip in enumerate(chips):
            copy(4 + j, (*chip, 1 - c), me).wait_recv()
        for cp in first + passed:
            cp.wait_send()
        mine.wait()

    return pl.pallas_call(
        body, name=name, out_shape=jax.ShapeDtypeStruct((N_DEV, a, b), shard.dtype),
        in_specs=[pl.BlockSpec(memory_space=pltpu.HBM)], out_specs=pl.BlockSpec(memory_space=pltpu.HBM),
        scratch_shapes=[pltpu.SemaphoreType.DMA((7,)), pltpu.SemaphoreType.DMA((7,)), pltpu.SemaphoreType.DMA],
    )(shard)


HBM_SPEC = pl.BlockSpec(memory_space=pltpu.HBM)
SEM_SPEC = pl.BlockSpec(memory_space=pltpu.SEMAPHORE)
N_PEERS = N_DEV - 1
PEER_FLIPS = [(fx, fy, fc) for fx in (0, 1) for fy in (0, 1) for fc in (0, 1)][1:]


def _peers(x, y, c):
    return [((1 - x) if fx else x, (1 - y) if fy else y, (1 - c) if fc else c) for fx, fy, fc in PEER_FLIPS]


def _exchange_copies(g_ref, land_ref, send_sems, recv_sems):
    x, y, c = _position()
    me = _index(x, y, c)
    peers = _peers(x, y, c)
    blocked = len(g_ref.shape) == 3

    def copy(k, src_block, dst_block):
        return pltpu.make_async_remote_copy(
            src_ref=g_ref.at[src_block] if blocked else g_ref, dst_ref=land_ref.at[dst_block],
            send_sem=send_sems.at[k], recv_sem=recv_sems.at[k], device_id=peers[k], device_id_type=MESH)

    sent = [copy(k, _index(*peer), me) for k, peer in enumerate(peers)]
    arriving = [copy(k, me, _index(*peer)) for k, peer in enumerate(peers)]
    own = pltpu.make_async_copy(g_ref.at[me] if blocked else g_ref, land_ref.at[me], send_sems.at[N_PEERS])
    return sent, arriving, own


def _exchange_start(name, g3, after=()):
    def body(g_ref, land_ref, *rest):
        send_sems, recv_sems, _, _, token = rest[len(after):]
        sent, _, own = _exchange_copies(g_ref, land_ref, send_sems, recv_sems)
        for cp in sent:
            cp.start()
        own.start()
        token[...] = jnp.zeros_like(token)

    land_shape = g3.shape if g3.ndim == 3 else (N_DEV,) + g3.shape
    land = lax.empty(land_shape, g3.dtype)
    send_sems, recv_sems, g_thru, land_thru, token = pl.pallas_call(
        body, name=name,
        out_shape=(pltpu.SemaphoreType.DMA((N_PEERS + 1,)), pltpu.SemaphoreType.DMA((N_PEERS,)),
                   pltpu.HBM(g3.shape, g3.dtype), pltpu.HBM(land_shape, g3.dtype),
                   jax.ShapeDtypeStruct((SUBLANES, LANES), F32)),
        in_specs=(HBM_SPEC, HBM_SPEC) + (pl.BlockSpec(memory_space=pl.ANY),) * len(after),
        out_specs=(SEM_SPEC, SEM_SPEC, HBM_SPEC, HBM_SPEC, pl.BlockSpec(memory_space=pltpu.VMEM)),
        input_output_aliases={0: 2, 1: 3},
        compiler_params=pltpu.CompilerParams(has_side_effects=pltpu.SideEffectType.DATAFLOW_SIDE_EFFECTING),
    )(pltpu.with_memory_space_constraint(g3, pltpu.HBM), pltpu.with_memory_space_constraint(land, pltpu.HBM), *after)
    return (send_sems, recv_sems, g_thru, land_thru), token


def _exchange_wait(name, handle, after):
    send_sems, recv_sems, g_thru, land_thru = handle

    def body(g_ref, land_ref, send_sems, recv_sems, after_ref, g_dead, land_out):
        sent, arriving, own = _exchange_copies(g_ref, land_ref, send_sems, recv_sems)
        for cp in sent:
            cp.wait_send()
        for cp in arriving:
            cp.wait_recv()
        own.wait()

    return pl.pallas_call(
        body, name=name,
        out_shape=(pltpu.HBM(g_thru.shape, g_thru.dtype), pltpu.HBM(land_thru.shape, land_thru.dtype)),
        in_specs=(HBM_SPEC, HBM_SPEC, SEM_SPEC, SEM_SPEC, pl.BlockSpec(memory_space=pl.ANY)),
        out_specs=(HBM_SPEC, HBM_SPEC), input_output_aliases={0: 0, 1: 1},
        compiler_params=pltpu.CompilerParams(has_side_effects=pltpu.SideEffectType.DATAFLOW_SIDE_EFFECTING),
    )(g_thru, land_thru, send_sems, recv_sems, after)[1]


def _adamw(w, g, m, v):
    m = ADAM_B1 * m + (1.0 - ADAM_B1) * g
    v = ADAM_B2 * v + (1.0 - ADAM_B2) * (g * g)
    m_hat = m / (1.0 - ADAM_B1 ** ADAM_STEP)
    v_hat = v / (1.0 - ADAM_B2 ** ADAM_STEP)
    delta = -ADAM_LR * (m_hat / (jnp.sqrt(v_hat) + ADAM_EPS) + ADAM_WD * w)
    return delta, m, v


def _adamw_shard(name, land, w, m, v):
    _, a, b = land.shape
    tm = _div(a, max(16, ROW_TILE_ELEMS // (2 * b)), 16)
    per = a // tm

    def fn(i, nrow, *t):
        g = t[0].astype(F32)
        for s in range(1, N_DEV):
            g = g + t[s].astype(F32)
        w, m, v = t[N_DEV:]
        return (g,) + _adamw(w, g, m, v)

    land2 = land.reshape(N_DEV * a, b)
    ins = [_t(land2, row0=s * per) for s in range(N_DEV)] + [_t(w), _t(m), _t(v)]
    return _rowwise(name, fn, a, tm, ins, [(F32, b, b)] * 4)


def _adamw_small(name, parts, w, m, v):
    rows, n = w.shape

    def body(p_ref, w_ref, m_ref, v_ref, g_out, d_out, m_out, v_out):
        g = p_ref[0]
        for s in range(1, parts.shape[0]):
            g = g + p_ref[s]
        g = jnp.sum(g, axis=0, keepdims=True) if rows == 1 else g[0:rows, :]
        delta, m_new, v_new = _adamw(w_ref[...], g, m_ref[...], v_ref[...])
        g_out[...] = g
        d_out[...] = delta
        m_out[...] = m_new
        v_out[...] = v_new

    return pl.pallas_call(body, name=name, out_shape=[jax.ShapeDtypeStruct((rows, n), F32)] * 4)(parts, w, m, v)


def _rope_tables(t):
    pos = jnp.arange(t, dtype=jnp.int32)
    half = HEAD_DIM // 4
    inv_freq = ROPE_THETA ** (-jnp.arange(0, 2 * half, 2, dtype=F32) / (2 * half))
    ang_r = (pos // GRID_W).astype(F32)[:, None] * inv_freq[None, :]
    ang_c = (pos % GRID_W).astype(F32)[:, None] * inv_freq[None, :]
    cos = jnp.concatenate([jnp.cos(ang_r)] * 2 + [jnp.cos(ang_c)] * 2, axis=-1)
    sin = jnp.concatenate([-jnp.sin(ang_r), jnp.sin(ang_r), -jnp.sin(ang_c), jnp.sin(ang_c)], axis=-1)
    return cos, sin


def _gather_weight(name, w, cols):
    g = _all_gather(name, w[0].astype(BF16))
    return g if cols else g.reshape(1, N_DEV * g.shape[1], g.shape[2])


def kernel(x, p, norm_mix, w_in, w_dw, conv_ln_g, conv_ln_b, w_conv_proj, q_norm, k_norm, w_attn_proj, w_out, norm_ffn, w_ff1, w_ff2, norm_ple, w_ple_gate, w_ple_proj, norm_final, loss_target, m_norm_mix, m_w_in, m_w_dw, m_conv_ln_g, m_conv_ln_b, m_w_conv_proj, m_q_norm, m_k_norm, m_w_attn_proj, m_w_out, m_norm_ffn, m_w_ff1, m_w_ff2, m_norm_ple, m_w_ple_gate, m_w_ple_proj, m_norm_final, v_norm_mix, v_w_in, v_w_dw, v_conv_ln_g, v_conv_ln_b, v_w_conv_proj, v_q_norm, v_k_norm, v_w_attn_proj, v_w_out, v_norm_ffn, v_w_ff1, v_w_ff2, v_norm_ple, v_w_ple_gate, v_w_ple_proj, v_norm_final):
    t, d = x.shape[1], x.shape[2]
    cw = d // 2
    n_heads = d // HEAD_DIM
    n_kv = n_heads // GROUP
    col_q, col_k, col_v = d // HEAD_DIM, 2 * d // HEAD_DIM, (2 * d + d // 4) // HEAD_DIM
    x0, pe, tgt = x[0], p[0, 0], loss_target[0]
    g_final = norm_final.reshape(1, d)
    me = _index(*_position())

    wg_in = _gather_weight("ag_w_in", w_in, True)
    dw_pad = jnp.pad(w_dw[0], ((0, CONV_TAPS_PAD - CONV_KERNEL), (0, 0)))
    dw_all = _all_gather("ag_w_dw", dw_pad)
    dw_full = dw_all.transpose(1, 0, 2).reshape(CONV_TAPS_PAD, cw)
    cos, sin = _rope_tables(t)

    def fetch(name, w, prev):
        return _exchange_start("ag_start_" + name, w[0].astype(BF16), after=(wg_in, dw_all) if prev is None else (prev,))

    def arrived(name, handle, cols, after):
        g = _exchange_wait("ag_wait_" + name, handle, after)
        return g if cols else g.reshape(1, N_DEV * g.shape[1], g.shape[2])

    on_cp, tok = fetch("w_conv_proj", w_conv_proj, None)
    on_ap, tok = fetch("w_attn_proj", w_attn_proj, tok)
    on_out, tok = fetch("w_out", w_out, tok)
    on_ff1, tok = fetch("w_ff1", w_ff1, tok)
    on_ff2, tok = fetch("w_ff2", w_ff2, tok)
    on_pg, tok = fetch("w_ple_gate", w_ple_gate, tok)
    on_pp, tok = fetch("w_ple_proj", w_ple_proj, tok)

    h1 = _rms_fwd("rms_mix", x0, norm_mix, after=(tok,))
    z = _mm_nn("mm_in", h1, wg_in, F32)
    conv, s_c = _conv_fwd("conv_fwd", z, cw, dw_full, conv_ln_g, conv_ln_b)
    wg_cp = arrived("w_conv_proj", on_cp, True, s_c)
    y_c = _mm_nn("mm_conv_proj", s_c, wg_cp, F32)
    qr = _head_norm_rope("q_prep", z, col_q, n_heads, q_norm, cos, sin, out_scale=QK_SCALE * LOG2_E,
                         heads_per_step=GROUP)
    kr = _head_norm_rope("k_prep", z, col_k, n_kv, k_norm, cos, sin)
    vt = z[:, col_v * HEAD_DIM:(col_v + n_kv) * HEAD_DIM].T.astype(BF16)
    o, lse = _flash_fwd(qr, kr, vt, n_heads)
    wg_ap = arrived("w_attn_proj", on_ap, False, o)
    y_a = _mm_nn("mm_attn_proj", o, wg_ap, F32)
    merged = _merge_fwd("merge_fwd", z, d, y_c, y_a)
    wg_out = arrived("w_out", on_out, False, merged)
    x1 = _mm_nn("mm_out", merged, wg_out, F32, residual=x0)
    h2 = _rms_fwd("rms_ffn", x1, norm_ffn)
    wg_ff1 = arrived("w_ff1", on_ff1, True, h2)
    f, act = _mm_nn("mm_ff1", h2, wg_ff1, F32, activation=_relu2)
    wg_ff2 = arrived("w_ff2", on_ff2, False, act)
    x2 = _mm_nn("mm_ff2", act, wg_ff2, F32, residual=x1)
    hp = _rms_fwd("rms_ple", x2, norm_ple)
    wg_pg = arrived("w_ple_gate", on_pg, False, hp)
    pre = _mm_nn("mm_ple_gate", hp, wg_pg, F32)
    wg_pp = arrived("w_ple_proj", on_pp, True, pre)
    pp = _mm_nn("mm_ple_proj", pe, wg_pp, F32)

    dx3, dpp, dpre, dg_final, loss_part = _head_loss("head_loss", x2, pre, pp, tgt, g_final)
    loss = lax.psum(loss_part[0, 0], ("x", "y", "c"))
    def send(name, gw):
        blocks = gw if gw.shape[0] == N_DEV else gw.reshape(N_DEV, gw.shape[1] // N_DEV, gw.shape[2])
        return _exchange_start("rs_start_" + name, blocks)

    sent_pp, tok_pp = send("w_ple_proj", _mm_tn("mm_d_ple_proj", pe, dpp, N_DEV))
    sent_pg, tok = send("w_ple_gate", _mm_tn("mm_d_ple_gate", hp, dpre, 1))
    dhp = _mm_nt("mm_dhp", dpre, wg_pg, F32, after=(tok_pp, tok))
    dx2, dg_ple = _rms_bwd("rms_ple_bwd", x2, norm_ple, dhp, dx3)
    sent_ff2, tok = send("w_ff2", _mm_tn("mm_d_ff2", act, dx2, 1))
    df = _mm_nt("mm_dact", dx2, wg_ff2, BF16, after=(tok,), through=(_relu2_bwd, f))
    sent_ff1, tok = send("w_ff1", _mm_tn("mm_d_ff1", h2, df, N_DEV))
    dh2 = _mm_nt("mm_dh2", df, wg_ff1, F32, after=(tok,))
    dx1, dg_ffn = _rms_bwd("rms_ffn_bwd", x1, norm_ffn, dh2, dx2)
    sent_out, tok = send("w_out", _mm_tn("mm_d_out", merged, dx1, 1))
    dmerged = _mm_nt("mm_dmerged", dx1, wg_out, F32, after=(tok,))
    dgc, dga, dyc, dya = _merge_bwd("merge_bwd", dmerged, z, d, y_c, y_a)
    sent_cp, tok = send("w_conv_proj", _mm_tn("mm_d_conv_proj", s_c, dyc, N_DEV))
    ds_c = _mm_nt("mm_ds_c", dyc, wg_cp, F32, after=(tok,))
    dconv, dg_ln, db_ln = _conv_ln_bwd("conv_ln_bwd", ds_c, conv, conv_ln_g, conv_ln_b)
    dca, dcb, dg_dw = _conv_glu_bwd("conv_glu_bwd", dconv, z, cw, dw_full)
    sent_ap, tok = send("w_attn_proj", _mm_tn("mm_d_attn_proj", o, dya, 1))
    do = _mm_nt("mm_do", dya, wg_ap, BF16, after=(tok,))
    delta = _attn_delta(do, o, n_heads)
    dqt, dkh, dvh = _flash_bwd(qr, kr, kr.T, z, col_v, do, lse, delta, n_heads)
    dq, dg_q = _head_norm_rope_bwd("q_prep_bwd", [(dqt, 0, 1)], z, col_q, n_heads, q_norm, cos, sin,
                                   heads_per_step=GROUP, transposed=True)
    dk, dg_k = _head_norm_rope_bwd("k_prep_bwd", [(dkh, g, GROUP) for g in range(GROUP)], z, col_k, n_kv, k_norm,
                                   cos, sin)
    dv = _group_sum("dv_sum", dvh, n_kv)
    dz = jnp.concatenate([dca, dcb, dq, dk, dv, dgc, dga], axis=1)
    sent_in, tok = send("w_in", _mm_tn("mm_d_in", h1, dz, N_DEV))
    dh1 = _mm_nt("mm_dh1", dz, wg_in, F32, after=(tok,))
    grad_x, dg_mix = _rms_bwd("rms_mix_bwd", x0, norm_mix, dh1, dx1)

    def pad(a):
        return jnp.pad(a, ((0, 0), (0, d - a.shape[1])))

    small = [dg_mix, dg_ffn, dg_ple, dg_final, dg_ln, db_ln, dg_q, dg_k, dg_dw]
    packed = _all_gather("ag_small", jnp.concatenate([pad(a) for a in small], axis=0))
    parts, row = [], 0
    for a in small:
        parts.append(packed[:, row:row + a.shape[0], :a.shape[1]])
        row += a.shape[0]
    p_mix, p_ffn, p_ple, p_final, p_lng, p_lnb, p_q, p_k, p_dw = parts
    p_dw = lax.dynamic_slice_in_dim(p_dw, me * (cw // N_DEV), cw // N_DEV, axis=2)

    def little(name, part, w, m, v):
        rows, n = (1 if w.ndim < 3 else w.shape[1]), w.shape[-1]
        padded = rows if rows == 1 else part.shape[1]

        def two_d(a):
            return jnp.pad(a.reshape(rows, n), ((0, padded - rows), (0, 0)))

        res = _adamw_small("adamw_" + name, part, two_d(w), two_d(m), two_d(v))
        return [r[:rows].reshape(w.shape) for r in res]

    done = {}
    after = packed
    for name, sent, w, m, v in [
            ("w_ple_proj", sent_pp, w_ple_proj, m_w_ple_proj, v_w_ple_proj),
            ("w_ple_gate", sent_pg, w_ple_gate, m_w_ple_gate, v_w_ple_gate),
            ("w_ff2", sent_ff2, w_ff2, m_w_ff2, v_w_ff2),
            ("w_ff1", sent_ff1, w_ff1, m_w_ff1, v_w_ff1),
            ("w_out", sent_out, w_out, m_w_out, v_w_out),
            ("w_conv_proj", sent_cp, w_conv_proj, m_w_conv_proj, v_w_conv_proj),
            ("w_attn_proj", sent_ap, w_attn_proj, m_w_attn_proj, v_w_attn_proj),
            ("w_in", sent_in, w_in, m_w_in, v_w_in)]:
        land = _exchange_wait("rs_wait_" + name, sent, after)
        res = _adamw_shard("adamw_" + name, land, w[0], m[0], v[0])
        after = res[0]
        done[name] = [r.reshape(w.shape) for r in res]

    results = [
        little("norm_mix", p_mix, norm_mix, m_norm_mix, v_norm_mix),
        done["w_in"],
        little("w_dw", p_dw, w_dw, m_w_dw, v_w_dw),
        little("conv_ln_g", p_lng, conv_ln_g, m_conv_ln_g, v_conv_ln_g),
        little("conv_ln_b", p_lnb, conv_ln_b, m_conv_ln_b, v_conv_ln_b),
        done["w_conv_proj"],
        little("q_norm", p_q, q_norm, m_q_norm, v_q_norm),
        little("k_norm", p_k, k_norm, m_k_norm, v_k_norm),
        done["w_attn_proj"],
        done["w_out"],
        little("norm_ffn", p_ffn, norm_ffn, m_norm_ffn, v_norm_ffn),
        done["w_ff1"],
        done["w_ff2"],
        little("norm_ple", p_ple, norm_ple, m_norm_ple, v_norm_ple),
        done["w_ple_gate"],
        done["w_ple_proj"],
        little("norm_final", p_final, norm_final, m_norm_final, v_norm_final),
    ]
    grads, deltas, new_m, new_v = zip(*results)
    return (loss, grad_x[None], *grads, *deltas, *new_m, *new_v)
```

```python
import functools

import jax
import jax.numpy as jnp
from jax import lax
from jax.experimental import pallas as pl
from jax.experimental.pallas import tpu as pltpu

F32 = jnp.float32
BF16 = jnp.bfloat16

EPS = 1e-6
HEAD_DIM = 128
GROUP = 4
CONV_KERNEL = 31
CONV_HALO = 16
CONV_TAPS_PAD = 32
GRID_W = 64
ROPE_THETA = 10000.0
ADAM_LR, ADAM_B1, ADAM_B2, ADAM_EPS, ADAM_WD, ADAM_STEP = 0.001, 0.9, 0.999, 1e-08, 0.01, 10

N_DEV = 8
MESH = pl.DeviceIdType.MESH
SUBLANES = 8
LANES = 128
VMEM_LIMIT = 48 * 1024 * 1024
ROW_TILE_ELEMS = 512 * 1024


def _div(n, pref, mult=1):
    if n <= pref:
        return n
    for d in range(pref, 0, -1):
        if n % d == 0 and d % mult == 0:
            return d
    return n


def _params(sem):
    return pltpu.CompilerParams(dimension_semantics=sem, vmem_limit_bytes=VMEM_LIMIT)


def _sum8(a):
    tm, w = a.shape
    return a.reshape(tm // SUBLANES, SUBLANES, w).sum(axis=0)


def _sigmoid(x):
    return jax.nn.sigmoid(x)


def _rowwise(name, fn, n_rows, tm, ins, outs, accs=(), ncol=1, refs_to_fn=False, after=()):
    nrow = n_rows // tm
    in_specs, arrays = [], []
    for spec in ins:
        kind, arr, w, c0 = spec[:4]
        if kind == "t":
            cs, r0 = spec[4], spec[5]
            in_specs.append(pl.BlockSpec((tm, w), lambda j, i, c0=c0, cs=cs, r0=r0: (r0 + i, c0 + cs * j)))
        elif kind == "p":
            h = spec[4]
            r = tm // h
            in_specs.append(pl.BlockSpec((h, w), lambda j, i, c0=c0, r=r: (jnp.maximum(i * r - 1, 0), c0 + j)))
        elif kind == "n":
            h = spec[4]
            r = tm // h
            last = n_rows // h - 1
            in_specs.append(
                pl.BlockSpec((h, w), lambda j, i, c0=c0, r=r, last=last: (jnp.minimum((i + 1) * r, last), c0 + j)))
        elif kind == "x":
            in_specs.append(pl.BlockSpec((w, tm), lambda j, i, c0=c0: (c0 + j, i)))
        else:
            cs = spec[4]
            in_specs.append(pl.BlockSpec((arr.shape[0], w), lambda j, i, c0=c0, cs=cs: (0, c0 + cs * j)))
        arrays.append(arr)
    out_shape, out_specs = [], []
    for dtype, total, w in outs:
        out_shape.append(jax.ShapeDtypeStruct((n_rows, total), dtype))
        out_specs.append(pl.BlockSpec((tm, w), lambda j, i: (i, j)))
    for rows, total, w, follow in accs:
        out_shape.append(jax.ShapeDtypeStruct((rows, total), F32))
        out_specs.append(pl.BlockSpec((rows, w), (lambda j, i: (0, j)) if follow else (lambda j, i: (0, 0))))
    n_in, n_out, n_acc = len(ins), len(outs), len(accs)
    for token in after:
        in_specs.append(pl.BlockSpec(memory_space=pl.ANY))
        arrays.append(token)
    out0 = n_in + len(after)

    def body(*refs):
        j, i = pl.program_id(0), pl.program_id(1)
        res = fn(i, nrow, *(refs[:n_in] if refs_to_fn else [r[...] for r in refs[:n_in]]))
        if not isinstance(res, (tuple, list)):
            res = (res,)
        for k in range(n_out):
            refs[out0 + k][...] = res[k].astype(outs[k][0])
        for k in range(n_acc):
            ref, term = refs[out0 + n_out + k], res[n_out + k]
            first = (i == 0) if accs[k][3] else jnp.logical_and(i == 0, j == 0)

            @pl.when(first)
            def _():
                ref[...] = term

            @pl.when(jnp.logical_not(first))
            def _():
                ref[...] += term

    res = pl.pallas_call(
        body, name=name, grid=(ncol, nrow), in_specs=in_specs, out_specs=out_specs, out_shape=out_shape,
        compiler_params=_params(("arbitrary", "arbitrary")))(*arrays)
    return res


def _row_tile(n_rows, width, mult=SUBLANES):
    return _div(n_rows, max(mult, ROW_TILE_ELEMS // width), mult)


def _t(arr, w=None, col0=0, cstride=1, row0=0):
    return ("t", arr, arr.shape[1] if w is None else w, col0, cstride, row0)


def _b(arr, w=None, col0=0, cstride=0):
    return ("b", arr, arr.shape[1] if w is None else w, col0, cstride)


def _rms_fwd(name, x, g, after=()):
    n, d = x.shape

    def fn(i, nrow, x, g):
        r = lax.rsqrt(jnp.mean(x * x, axis=-1, keepdims=True) + EPS)
        return x * r * g

    return _rowwise(name, fn, n, _row_tile(n, d), [_t(x), _b(g)], [(BF16, d, d)], after=after)[0]


def _rms_bwd(name, x, g, dh, dres):
    n, d = x.shape

    def fn(i, nrow, x, g, dh, dres):
        r = lax.rsqrt(jnp.mean(x * x, axis=-1, keepdims=True) + EPS)
        w = dh * g
        dx = dres + r * w - x * (r * r * r) * jnp.mean(x * w, axis=-1, keepdims=True)
        return dx, _sum8(dh * x * r)

    return _rowwise(name, fn, n, _row_tile(n, 2 * d), [_t(x), _b(g), _t(dh), _t(dres)], [(F32, d, d)],
                    [(SUBLANES, d, d, True)])


def _swap_halves(x):
    lane = lax.broadcasted_iota(jnp.int32, x.shape, 1)
    return jnp.where(lane % 64 < 32, pltpu.roll(x, 96, axis=1), pltpu.roll(x, 32, axis=1))


def _head_norm_rope(name, z, col0, n_heads, g, cos, sin, out_scale=None, heads_per_step=1):
    n, hp = z.shape[0], heads_per_step
    assert col0 % hp == 0 and n_heads % hp == 0

    def fn(i, nrow, x, g, cos, sin):
        ys = []
        for h in range(hp):
            xh = x[:, h * HEAD_DIM:(h + 1) * HEAD_DIM]
            r = lax.rsqrt(jnp.mean(xh * xh, axis=-1, keepdims=True) + EPS)
            y = xh * r * g
            y = y * cos + _swap_halves(y) * sin
            ys.append(y if out_scale is None else y * out_scale)
        return ys[0] if hp == 1 else jnp.concatenate(ys, axis=1)

    tm = _div(n, 1024, SUBLANES)
    w = hp * HEAD_DIM
    return _rowwise(name, fn, n, tm, [_t(z, w, col0 // hp), _b(g), _t(cos, cstride=0), _t(sin, cstride=0)],
                    [(BF16, n_heads * HEAD_DIM, w)], ncol=n_heads // hp)[0]


def _head_norm_rope_bwd(name, douts, z, col0, n_heads, g, cos, sin, heads_per_step=1, transposed=False):
    n, hp = z.shape[0], heads_per_step
    n_d = len(douts)
    assert col0 % hp == 0 and n_heads % hp == 0 and (hp == 1 or all(c0 % hp == 0 and cs == 1 for _, c0, cs in douts))
    assert not transposed or n_d == 1

    def fn(i, nrow, *a):
        dy = a[0]
        for k in range(1, n_d):
            dy = dy + a[k]
        x, g, cos, sin = a[n_d:]
        dxs, dg = [], None
        for h in range(hp):
            xh = x[:, h * HEAD_DIM:(h + 1) * HEAD_DIM]
            if transposed:
                dyh = jnp.transpose(dy[h * HEAD_DIM:(h + 1) * HEAD_DIM, :])
            else:
                dyh = dy[:, h * HEAD_DIM:(h + 1) * HEAD_DIM]
            dn = dyh * cos + _swap_halves(dyh * sin)
            r = lax.rsqrt(jnp.mean(xh * xh, axis=-1, keepdims=True) + EPS)
            w = dn * g
            dxs.append(r * w - xh * (r * r * r) * jnp.mean(xh * w, axis=-1, keepdims=True))
            term = _sum8(dn * xh * r)
            dg = term if dg is None else dg + term
        return (dxs[0] if hp == 1 else jnp.concatenate(dxs, axis=1)), dg

    tm = _div(n, 1024, SUBLANES)
    w = hp * HEAD_DIM
    if transposed:
        ins = [("x", douts[0][0], w, douts[0][1] // hp)]
    else:
        ins = [_t(arr, w, c0 // hp, cs) for arr, c0, cs in douts]
    ins += [_t(z, w, col0 // hp), _b(g), _t(cos, cstride=0), _t(sin, cstride=0)]
    return _rowwise(name, fn, n, tm, ins, [(BF16, n_heads * HEAD_DIM, w)],
                    [(SUBLANES, HEAD_DIM, HEAD_DIM, False)], ncol=n_heads // hp)


def _halo(arr, w, col0):
    return [("p", arr, w, col0, CONV_HALO), _t(arr, w, col0), ("n", arr, w, col0, CONV_HALO)]


def _extend(i, nrow, prev, cur, nxt):
    prev = jnp.where(i > 0, prev, 0.0)
    nxt = jnp.where(i < nrow - 1, nxt, 0.0)
    return jnp.concatenate([prev, cur, nxt], axis=0)


def _shifted(ext, offset, tm):
    n = ext.shape[0]
    rolled = ext if offset == 0 else pltpu.roll(ext, (-offset) % n, axis=0)
    return rolled[CONV_HALO:CONV_HALO + tm]


def _conv_fwd(name, z, cw, w_dw, ln_g, ln_b):
    n = z.shape[0]
    tm = _row_tile(n, 4 * cw, CONV_HALO)

    def fn(i, nrow, ap, a, an, bp, b, bn, w, g, beta):
        a = a[...]
        ext = _extend(i, nrow, ap[...] * _sigmoid(bp[...]), a * _sigmoid(b[...]), an[...] * _sigmoid(bn[...]))
        conv = jnp.zeros_like(a)
        for k in range(CONV_KERNEL):
            conv = conv + _shifted(ext, k - CONV_KERNEL // 2, tm) * w[k:k + 1, :]
        xc = conv - jnp.mean(conv, axis=-1, keepdims=True)
        ln = xc * lax.rsqrt(jnp.mean(xc * xc, axis=-1, keepdims=True) + EPS) * g[...] + beta[...]
        return conv, ln * _sigmoid(ln)

    ins = _halo(z, cw, 0) + _halo(z, cw, 1) + [_b(w_dw), _b(ln_g), _b(ln_b)]
    return _rowwise(name, fn, n, tm, ins, [(F32, cw, cw), (BF16, cw, cw)], refs_to_fn=True)


def _conv_ln_bwd(name, ds, conv, ln_g, ln_b):
    n, cw = conv.shape

    def fn(i, nrow, ds, conv, g, beta):
        xc = conv - jnp.mean(conv, axis=-1, keepdims=True)
        rstd = lax.rsqrt(jnp.mean(xc * xc, axis=-1, keepdims=True) + EPS)
        xhat = xc * rstd
        ln = xhat * g + beta
        sg = _sigmoid(ln)
        dln = ds * (sg * (1.0 + ln * (1.0 - sg)))
        dxh = dln * g
        dconv = rstd * (dxh - jnp.mean(dxh, axis=-1, keepdims=True)
                        - xhat * jnp.mean(dxh * xhat, axis=-1, keepdims=True))
        return dconv, _sum8(dln * xhat), _sum8(dln)

    return _rowwise(name, fn, n, _row_tile(n, 2 * cw), [_t(ds), _t(conv), _b(ln_g), _b(ln_b)], [(F32, cw, cw)],
                    [(SUBLANES, cw, cw, True), (SUBLANES, cw, cw, True)])


def _conv_glu_bwd(name, dconv, z, cw, w_dw):
    n = z.shape[0]
    tm = _row_tile(n, 4 * cw, CONV_HALO)

    def fn(i, nrow, dp, d, dn, ap, a, an, bp, b, bn, w):
        d, a = d[...], a[...]
        dext = _extend(i, nrow, dp[...], d, dn[...])
        sg = _sigmoid(b[...])
        uext = _extend(i, nrow, ap[...] * _sigmoid(bp[...]), a * sg, an[...] * _sigmoid(bn[...]))
        tap = lax.broadcasted_iota(jnp.int32, (CONV_TAPS_PAD, cw), 0)
        du = jnp.zeros_like(d)
        dw = jnp.zeros((CONV_TAPS_PAD, cw), F32)
        for k in range(CONV_KERNEL):
            off = k - CONV_KERNEL // 2
            du = du + _shifted(dext, -off, tm) * w[k:k + 1, :]
            dw_k = jnp.sum(d * _shifted(uext, off, tm), axis=0, keepdims=True)
            dw = dw + jnp.where(tap == k, dw_k, 0.0)
        return du * sg, du * a * sg * (1.0 - sg), dw

    ins = _halo(dconv, cw, 0) + _halo(z, cw, 0) + _halo(z, cw, 1) + [_b(w_dw)]
    return _rowwise(name, fn, n, tm, ins, [(BF16, cw, cw), (BF16, cw, cw)], [(CONV_TAPS_PAD, cw, cw, True)],
                    refs_to_fn=True)


def _merge_fwd(name, z, d, y_c, y_a):
    n, hw = z.shape[0], d // 2

    def fn(i, nrow, gc, ga, yc, ya):
        return _sigmoid(gc) * yc + _sigmoid(ga) * ya

    ins = [_t(z, hw, 5), _t(z, hw, 7), _t(y_c, hw), _t(y_a, hw)]
    return _rowwise(name, fn, n, _row_tile(n, 2 * hw), ins, [(BF16, d, hw)], ncol=2)[0]


def _merge_bwd(name, dm, z, d, y_c, y_a):
    n, hw = z.shape[0], d // 2

    def fn(i, nrow, dm, gc, ga, yc, ya):
        sc, sa = _sigmoid(gc), _sigmoid(ga)
        return dm * yc * sc * (1.0 - sc), dm * ya * sa * (1.0 - sa), dm * sc, dm * sa

    ins = [_t(dm, hw), _t(z, hw, 5), _t(z, hw, 7), _t(y_c, hw), _t(y_a, hw)]
    return _rowwise(name, fn, n, _row_tile(n, 2 * hw), ins, [(BF16, d, hw)] * 4, ncol=2)


def _relu2(f):
    r = jnp.maximum(f, 0.0)
    return r * r


def _relu2_bwd(da, f):
    return da * (2.0 * jnp.maximum(f, 0.0))


def _head_loss(name, x2, pre, pp, target, g):
    n, d = x2.shape

    def fn(i, nrow, x2, pre, pp, t, g):
        gate = _sigmoid(pre)
        x3 = x2 + gate * pp
        r = lax.rsqrt(jnp.mean(x3 * x3, axis=-1, keepdims=True) + EPS)
        xn = x3 * r
        e = xn * g - t
        part = 0.5 * jnp.sum(jnp.mean(e * e, axis=-1, keepdims=True), axis=0, keepdims=True)
        dy = e * (1.0 / d)
        w = dy * g
        dx3 = r * w - x3 * (r * r * r) * jnp.mean(x3 * w, axis=-1, keepdims=True)
        return (dx3, dx3 * gate, dx3 * pp * gate * (1.0 - gate), _sum8(dy * xn),
                jnp.broadcast_to(part, (SUBLANES, LANES)))

    ins = [_t(x2), _t(pre), _t(pp), _t(target), _b(g)]
    return _rowwise(name, fn, n, _row_tile(n, 2 * d), ins, [(F32, d, d), (BF16, d, d), (BF16, d, d)],
                    [(SUBLANES, d, d, True), (SUBLANES, LANES, LANES, True)])


def _group_sum(name, dvh, n_kv):
    n = dvh.shape[0]

    def fn(i, nrow, *a):
        s = a[0]
        for k in range(1, GROUP):
            s = s + a[k]
        return s

    ins = [_t(dvh, HEAD_DIM, k, GROUP) for k in range(GROUP)]
    return _rowwise(name, fn, n, _div(n, 1024, SUBLANES), ins, [(BF16, n_kv * HEAD_DIM, HEAD_DIM)], ncol=n_kv)[0]


MM_ROWS = 1024
MM_COLS = 1152
MM_REDUCE = 2048
MM_REDUCE_BLOCKS = 2304


def _accumulate(step, n_steps, part, acc, finish):
    if n_steps == 1:
        finish(part)
        return

    @pl.when(step == 0)
    def _():
        acc[...] = part

    @pl.when(jnp.logical_and(step > 0, step < n_steps - 1))
    def _():
        acc[...] += part

    @pl.when(step == n_steps - 1)
    def _():
        finish(acc[...] + part)


MXU_WIDTH = 256


def _paired(nb, ns):
    return ns % MXU_WIDTH != 0 and nb % 2 == 0 and ns <= MM_COLS


def _mm_nn(name, a, w3, out_dtype, residual=None, activation=None):
    m, k = a.shape
    nb, _, ns = w3.shape
    jb = 2 if _paired(nb, ns) else 1
    tm = _div(m, MM_ROWS // jb, SUBLANES)
    tn_w, tk = (ns if jb > 1 else _div(ns, MM_COLS, LANES)), _div(k, MM_REDUCE, LANES)
    tn = jb * tn_w
    per, nk = ns // tn_w, k // tk
    has_res = residual is not None
    n_out = 1 if activation is None else 2

    def body(*refs):
        a_ref, w_ref = refs[:2]
        acc = refs[-1]
        outs = refs[-1 - n_out:-1]
        w = w_ref[0] if jb == 1 else jnp.concatenate([w_ref[j] for j in range(jb)], axis=1)
        part = jnp.dot(a_ref[...].astype(BF16), w.astype(BF16), preferred_element_type=F32)

        def finish(total):
            if has_res:
                total = total + refs[2][...]
            outs[0][...] = total.astype(out_dtype)
            if activation is not None:
                outs[1][...] = activation(total).astype(BF16)

        _accumulate(pl.program_id(2), nk, part, acc, finish)

    rows_outer = nk > 1 or tm * tk * a.dtype.itemsize > jb * tk * tn_w * w3.dtype.itemsize

    def at(f):
        return f if rows_outer else (lambda n, i, kk: f(i, n, kk))

    in_specs = [pl.BlockSpec((tm, tk), at(lambda i, n, kk: (i, kk))),
                pl.BlockSpec((jb, tk, tn_w), at(lambda i, n, kk: (n // per, kk, n % per)))]
    args = [a, w3]
    if has_res:
        in_specs.append(pl.BlockSpec((tm, tn), at(lambda i, n, kk: (i, n))))
        args.append(residual)
    out_dtypes = [out_dtype] + ([BF16] if activation is not None else [])
    n_rows, n_cols = m // tm, (nb // jb) * per
    res = pl.pallas_call(
        body, name=name, grid=(n_rows, n_cols, nk) if rows_outer else (n_cols, n_rows, nk), in_specs=in_specs,
        out_specs=[pl.BlockSpec((tm, tn), at(lambda i, n, kk: (i, n))) for _ in out_dtypes],
        out_shape=[jax.ShapeDtypeStruct((m, nb * ns), dt) for dt in out_dtypes],
        scratch_shapes=[pltpu.VMEM((tm, tn) if nk > 1 else (SUBLANES, LANES), F32)],
        compiler_params=_params(("parallel", "parallel", "arbitrary")))(*args)
    return res[0] if activation is None else res


def _mm_nt(name, dc, w3, out_dtype, after=(), through=None):
    m = dc.shape[0]
    nb, k, ns = w3.shape
    tm, to = _div(m, MM_ROWS, SUBLANES), _div(k, MM_ROWS, LANES)
    if ns > MM_REDUCE_BLOCKS // 2:
        jb, tr = 1, _div(ns, MM_REDUCE, LANES)
    else:
        jb, tr = max(j for j in range(1, nb + 1) if nb % j == 0 and j * ns <= MM_REDUCE_BLOCKS), ns
    per = ns // tr
    nr = (nb // jb) * per

    def body(dc_ref, w_ref, *rest):
        o_ref, acc = rest[-2:]
        w = w_ref[0] if jb == 1 else jnp.concatenate([w_ref[j] for j in range(jb)], axis=1)
        part = lax.dot_general(dc_ref[...].astype(BF16), w.astype(BF16), (((1,), (1,)), ((), ())),
                               preferred_element_type=F32)

        def finish(total):
            if through is not None:
                total = through[0](total, rest[0][...])
            o_ref[...] = total.astype(out_dtype)

        _accumulate(pl.program_id(2), nr, part, acc, finish)

    rows_outer = nr > 1 or tm * jb * tr * dc.dtype.itemsize > jb * to * tr * w3.dtype.itemsize

    def at(f):
        return f if rows_outer else (lambda o, i, r: f(i, o, r))

    in_specs = [pl.BlockSpec((tm, jb * tr), at(lambda i, o, r: (i, r))),
                pl.BlockSpec((jb, to, tr), at(lambda i, o, r: (r // per, o, r % per)))]
    args = [dc, w3]
    if through is not None:
        in_specs.append(pl.BlockSpec((tm, to), at(lambda i, o, r: (i, o))))
        args.append(through[1])
    for token in after:
        in_specs.append(pl.BlockSpec(memory_space=pl.ANY))
        args.append(token)
    return pl.pallas_call(
        body, name=name, grid=(m // tm, k // to, nr) if rows_outer else (k // to, m // tm, nr), in_specs=in_specs,
        out_specs=pl.BlockSpec((tm, to), at(lambda i, o, r: (i, o))),
        out_shape=jax.ShapeDtypeStruct((m, k), out_dtype),
        scratch_shapes=[pltpu.VMEM((tm, to) if nr > 1 else (SUBLANES, LANES), F32)],
        compiler_params=_params(("parallel", "parallel", "arbitrary")))(*args)


def _mm_tn(name, a, dc, nb):
    m, k = a.shape
    ns = dc.shape[1] // nb
    jb = 2 if _paired(nb, ns) else 1
    tr, to = _div(m, MM_REDUCE, LANES), _div(k, MM_ROWS // jb, LANES)
    tn_w = ns if jb > 1 else _div(ns, MM_COLS, LANES)
    tn = jb * tn_w
    per, nr = ns // tn_w, m // tr

    def body(a_ref, dc_ref, o_ref, acc):
        part = lax.dot_general(a_ref[...].astype(BF16), dc_ref[...].astype(BF16), (((0,), (0,)), ((), ())),
                               preferred_element_type=F32)

        def finish(total):
            for j in range(jb):
                o_ref[j] = total[:, j * tn_w:(j + 1) * tn_w].astype(BF16)

        _accumulate(pl.program_id(2), nr, part, acc, finish)

    return pl.pallas_call(
        body, name=name, grid=(k // to, (nb // jb) * per, nr),
        in_specs=[pl.BlockSpec((tr, to), lambda o, n, r: (r, o)),
                  pl.BlockSpec((tr, tn), lambda o, n, r: (r, n))],
        out_specs=pl.BlockSpec((jb, to, tn_w), lambda o, n, r: (n // per, o, n % per)),
        out_shape=jax.ShapeDtypeStruct((nb, k, ns), BF16),
        scratch_shapes=[pltpu.VMEM((to, tn) if nr > 1 else (SUBLANES, LANES), F32)],
        compiler_params=_params(("parallel", "parallel", "arbitrary")))(a, dc)


QK_SCALE = HEAD_DIM ** -0.5
LOG2_E = 1.4426950408889634
LN_2 = 0.6931471805599453


def _columns(row, n):
    return jnp.transpose(jnp.broadcast_to(row, (LANES, n)))


FLASH_TILE = 1024
ONES_ROWS = 16


def _flash_fwd(qr, kr, vt, n_heads):
    t = qr.shape[0]
    tq, tk = _div(t, FLASH_TILE, LANES), _div(t, 2 * FLASH_TILE, LANES)
    nk = t // tk
    rows = vt.shape[0] // (n_heads // GROUP)

    def body(q_ref, k_ref, vt_ref, o_ref, lse_ref, m_scr, acc_scr):
        kb = pl.program_id(2)

        @pl.when(kb == 0)
        def _():
            m_scr[...] = jnp.full_like(m_scr, -1e30)
            acc_scr[...] = jnp.zeros_like(acc_scr)

        s_t = lax.dot_general(k_ref[...], q_ref[...], (((1,), (1,)), ((), ())), preferred_element_type=F32)
        m_prev = m_scr[...]
        m_new = jnp.maximum(m_prev, jnp.max(s_t, axis=0, keepdims=True))
        alpha = jnp.exp2(m_prev - m_new)
        p_t = jnp.exp2(s_t - m_new).astype(BF16)
        acc_scr[...] = alpha * acc_scr[...] + jnp.dot(vt_ref[...], p_t, preferred_element_type=F32)
        m_scr[...] = m_new

        @pl.when(kb == nk - 1)
        def _():
            l = acc_scr[HEAD_DIM:HEAD_DIM + 1, :]
            o_ref[...] = jnp.transpose(acc_scr[0:HEAD_DIM, :] / l)
            lse_ref[...] = jnp.broadcast_to(m_scr[...] + jnp.log(l) * LOG2_E, (SUBLANES, tq))

    return pl.pallas_call(
        body, name="flash_fwd", grid=(n_heads, t // tq, nk),
        in_specs=[pl.BlockSpec((tq, HEAD_DIM), lambda h, qi, ki: (qi, h)),
                  pl.BlockSpec((tk, HEAD_DIM), lambda h, qi, ki: (ki, h // GROUP)),
                  pl.BlockSpec((rows, tk), lambda h, qi, ki: (h // GROUP, ki))],
        out_specs=[pl.BlockSpec((tq, HEAD_DIM), lambda h, qi, ki: (qi, h)),
                   pl.BlockSpec((None, SUBLANES, tq), lambda h, qi, ki: (h, 0, qi))],
        out_shape=[jax.ShapeDtypeStruct((t, n_heads * HEAD_DIM), F32),
                   jax.ShapeDtypeStruct((n_heads, SUBLANES, t), F32)],
        scratch_shapes=[pltpu.VMEM((1, tq), F32), pltpu.VMEM((rows, tq), F32)],
        compiler_params=_params(("parallel", "parallel", "arbitrary")))(qr, kr, vt)


def _attn_delta(do, o, n_heads):
    t = o.shape[0]
    tm = _div(t, 1024, LANES)

    def body(do_ref, o_ref, d_ref):
        prod = do_ref[...].astype(F32) * o_ref[...]
        row = jnp.sum(jnp.transpose(prod), axis=0, keepdims=True)
        d_ref[...] = jnp.broadcast_to(row, (SUBLANES, tm))

    return pl.pallas_call(
        body, name="attn_delta", grid=(n_heads, t // tm),
        in_specs=[pl.BlockSpec((tm, HEAD_DIM), lambda h, i: (i, h)),
                  pl.BlockSpec((tm, HEAD_DIM), lambda h, i: (i, h))],
        out_specs=pl.BlockSpec((None, SUBLANES, tm), lambda h, i: (h, 0, i)),
        out_shape=jax.ShapeDtypeStruct((n_heads, SUBLANES, t), F32),
        compiler_params=_params(("parallel", "parallel")))(do, o)


def _flash_bwd(qr, kr, kt, z, v_col0, do, lse, delta, n_heads):
    t = qr.shape[0]
    tq, tk = _div(t, 2 * FLASH_TILE, LANES), _div(t, FLASH_TILE, LANES)
    nq = t // tq

    def body(q_ref, k_ref, kt_ref, v_ref, do_ref, lse_ref, dl_ref, dq_ref, dk_ref, dv_ref, dk_acc, dv_acc):
        kb, qb = pl.program_id(1), pl.program_id(2)
        q, k, do_ = q_ref[...], k_ref[...], do_ref[...]
        v = v_ref[...].astype(BF16)
        s_t = lax.dot_general(k, q, (((1,), (1,)), ((), ())), preferred_element_type=F32)
        p_t = jnp.exp2(s_t - lse_ref[0:1, :])
        dp_t = lax.dot_general(v, do_, (((1,), (1,)), ((), ())), preferred_element_type=F32)
        ds_t = (p_t * (dp_t - dl_ref[0:1, :])).astype(BF16)
        dv_c = jnp.dot(p_t.astype(BF16), do_, preferred_element_type=F32)
        dk_c = jnp.dot(ds_t, q, preferred_element_type=F32) * LN_2
        dq_c = jnp.dot(kt_ref[...], ds_t, preferred_element_type=F32) * QK_SCALE

        @pl.when(qb == 0)
        def _():
            dk_acc[...] = dk_c
            dv_acc[...] = dv_c

        @pl.when(qb > 0)
        def _():
            dk_acc[...] += dk_c
            dv_acc[...] += dv_c

        @pl.when(qb == nq - 1)
        def _():
            dk_ref[...] = dk_acc[...]
            dv_ref[...] = dv_acc[...]

        cols = pl.ds(pl.multiple_of(qb * tq, tq), tq)

        @pl.when(kb == 0)
        def _():
            dq_ref[:, cols] = dq_c

        @pl.when(kb > 0)
        def _():
            dq_ref[:, cols] += dq_c

    wide = jax.ShapeDtypeStruct((t, n_heads * HEAD_DIM), F32)
    return pl.pallas_call(
        body, name="flash_bwd", grid=(n_heads, t // tk, nq),
        in_specs=[pl.BlockSpec((tq, HEAD_DIM), lambda h, kb, qb: (qb, h)),
                  pl.BlockSpec((tk, HEAD_DIM), lambda h, kb, qb: (kb, h // GROUP)),
                  pl.BlockSpec((HEAD_DIM, tk), lambda h, kb, qb: (h // GROUP, kb)),
                  pl.BlockSpec((tk, HEAD_DIM), lambda h, kb, qb: (kb, v_col0 + h // GROUP)),
                  pl.BlockSpec((tq, HEAD_DIM), lambda h, kb, qb: (qb, h)),
                  pl.BlockSpec((None, SUBLANES, tq), lambda h, kb, qb: (h, 0, qb)),
                  pl.BlockSpec((None, SUBLANES, tq), lambda h, kb, qb: (h, 0, qb))],
        out_specs=[pl.BlockSpec((HEAD_DIM, t), lambda h, kb, qb: (h, 0)),
                   pl.BlockSpec((tk, HEAD_DIM), lambda h, kb, qb: (kb, h)),
                   pl.BlockSpec((tk, HEAD_DIM), lambda h, kb, qb: (kb, h))],
        out_shape=[jax.ShapeDtypeStruct((n_heads * HEAD_DIM, t), F32), wide, wide],
        scratch_shapes=[pltpu.VMEM((tk, HEAD_DIM), F32), pltpu.VMEM((tk, HEAD_DIM), F32)],
        compiler_params=_params(("parallel", "arbitrary", "arbitrary")))(qr, kr, kt, z, do, lse, delta)


def _position():
    x, y, c = lax.axis_index("x"), lax.axis_index("y"), lax.axis_index("c")
    return x, y, c


def _index(x, y, c):
    return 4 * x + 2 * y + c


def _all_gather(name, shard):
    a, b = shard.shape

    def body(x_ref, out_ref, send_sems, recv_sems, local_sem):
        x, y, c = _position()
        me, sibling = (x, y, c), (x, y, 1 - c)
        chips = [(1 - x, y), (x, 1 - y), (1 - x, 1 - y)]

        def block(px, py, pc):
            return out_ref.at[_index(px, py, pc)]

        def copy(k, blk, to, src=None):
            return pltpu.make_async_remote_copy(
                src_ref=block(*blk) if src is None else src, dst_ref=block(*blk),
                send_sem=send_sems.at[k], recv_sem=recv_sems.at[k], device_id=to, device_id_type=MESH)

        mine = pltpu.make_async_copy(x_ref, block(*me), local_sem)
        mine.start()
        first = [copy(0, me, sibling, src=x_ref)]
        first += [copy(1 + j, me, (*chip, c), src=x_ref) for j, chip in enumerate(chips)]
        for cp in first:
            cp.start()
        passed = [copy(4 + j, (*chip, c), sibling) for j, chip in enumerate(chips)]
        for j, chip in enumerate(chips):
            copy(1 + j, (*chip, c), me).wait_recv()
            passed[j].start()
        copy(0, sibling, me).wait_recv()
        for j, chip in enumerate(chips):
            copy(4 + j, (*chip, 1 - c), me).wait_recv()
        for cp in first + passed:
            cp.wait_send()
        mine.wait()

    return pl.pallas_call(
        body, name=name, out_shape=jax.ShapeDtypeStruct((N_DEV, a, b), shard.dtype),
        in_specs=[pl.BlockSpec(memory_space=pltpu.HBM)], out_specs=pl.BlockSpec(memory_space=pltpu.HBM),
        scratch_shapes=[pltpu.SemaphoreType.DMA((7,)), pltpu.SemaphoreType.DMA((7,)), pltpu.SemaphoreType.DMA],
    )(shard)


HBM_SPEC = pl.BlockSpec(memory_space=pltpu.HBM)
SEM_SPEC = pl.BlockSpec(memory_space=pltpu.SEMAPHORE)
N_PEERS = N_DEV - 1
PEER_FLIPS = [(fx, fy, fc) for fx in (0, 1) for fy in (0, 1) for fc in (0, 1)][1:]


def _peers(x, y, c):
    return [((1 - x) if fx else x, (1 - y) if fy else y, (1 - c) if fc else c) for fx, fy, fc in PEER_FLIPS]


def _exchange_copies(g_ref, land_ref, send_sems, recv_sems):
    x, y, c = _position()
    me = _index(x, y, c)
    peers = _peers(x, y, c)
    blocked = len(g_ref.shape) == 3

    def copy(k, src_block, dst_block):
        return pltpu.make_async_remote_copy(
            src_ref=g_ref.at[src_block] if blocked else g_ref, dst_ref=land_ref.at[dst_block],
            send_sem=send_sems.at[k], recv_sem=recv_sems.at[k], device_id=peers[k], device_id_type=MESH)

    sent = [copy(k, _index(*peer), me) for k, peer in enumerate(peers)]
    arriving = [copy(k, me, _index(*peer)) for k, peer in enumerate(peers)]
    own = pltpu.make_async_copy(g_ref.at[me] if blocked else g_ref, land_ref.at[me], send_sems.at[N_PEERS])
    return sent, arriving, own


def _exchange_start(name, g3, after=()):
    def body(g_ref, land_ref, *rest):
        send_sems, recv_sems, _, _, token = rest[len(after):]
        sent, _, own = _exchange_copies(g_ref, land_ref, send_sems, recv_sems)
        for cp in sent:
            cp.start()
        own.start()
        token[...] = jnp.zeros_like(token)

    land_shape = g3.shape if g3.ndim == 3 else (N_DEV,) + g3.shape
    land = lax.empty(land_shape, g3.dtype)
    send_sems, recv_sems, g_thru, land_thru, token = pl.pallas_call(
        body, name=name,
        out_shape=(pltpu.SemaphoreType.DMA((N_PEERS + 1,)), pltpu.SemaphoreType.DMA((N_PEERS,)),
                   pltpu.HBM(g3.shape, g3.dtype), pltpu.HBM(land_shape, g3.dtype),
                   jax.ShapeDtypeStruct((SUBLANES, LANES), F32)),
        in_specs=(HBM_SPEC, HBM_SPEC) + (pl.BlockSpec(memory_space=pl.ANY),) * len(after),
        out_specs=(SEM_SPEC, SEM_SPEC, HBM_SPEC, HBM_SPEC, pl.BlockSpec(memory_space=pltpu.VMEM)),
        input_output_aliases={0: 2, 1: 3},
        compiler_params=pltpu.CompilerParams(has_side_effects=pltpu.SideEffectType.DATAFLOW_SIDE_EFFECTING),
    )(pltpu.with_memory_space_constraint(g3, pltpu.HBM), pltpu.with_memory_space_constraint(land, pltpu.HBM), *after)
    return (send_sems, recv_sems, g_thru, land_thru), token


def _exchange_wait(name, handle, after):
    send_sems, recv_sems, g_thru, land_thru = handle

    def body(g_ref, land_ref, send_sems, recv_sems, after_ref, g_dead, land_out):
        sent, arriving, own = _exchange_copies(g_ref, land_ref, send_sems, recv_sems)
        for cp in sent:
            cp.wait_send()
        for cp in arriving:
            cp.wait_recv()
        own.wait()

    return pl.pallas_call(
        body, name=name,
        out_shape=(pltpu.HBM(g_thru.shape, g_thru.dtype), pltpu.HBM(land_thru.shape, land_thru.dtype)),
        in_specs=(HBM_SPEC, HBM_SPEC, SEM_SPEC, SEM_SPEC, pl.BlockSpec(memory_space=pl.ANY)),
        out_specs=(HBM_SPEC, HBM_SPEC), input_output_aliases={0: 0, 1: 1},
        compiler_params=pltpu.CompilerParams(has_side_effects=pltpu.SideEffectType.DATAFLOW_SIDE_EFFECTING),
    )(g_thru, land_thru, send_sems, recv_sems, after)[1]


def _adamw(w, g, m, v):
    m = ADAM_B1 * m + (1.0 - ADAM_B1) * g
    v = ADAM_B2 * v + (1.0 - ADAM_B2) * (g * g)
    m_hat = m / (1.0 - ADAM_B1 ** ADAM_STEP)
    v_hat = v / (1.0 - ADAM_B2 ** ADAM_STEP)
    delta = -ADAM_LR * (m_hat / (jnp.sqrt(v_hat) + ADAM_EPS) + ADAM_WD * w)
    return delta, m, v


def _adamw_shard(name, land, w, m, v):
    _, a, b = land.shape
    tm = _div(a, max(16, ROW_TILE_ELEMS // (2 * b)), 16)
    per = a // tm

    def fn(i, nrow, *t):
        g = t[0].astype(F32)
        for s in range(1, N_DEV):
            g = g + t[s].astype(F32)
        w, m, v = t[N_DEV:]
        return (g,) + _adamw(w, g, m, v)

    land2 = land.reshape(N_DEV * a, b)
    ins = [_t(land2, row0=s * per) for s in range(N_DEV)] + [_t(w), _t(m), _t(v)]
    return _rowwise(name, fn, a, tm, ins, [(F32, b, b)] * 4)


def _adamw_small(name, parts, w, m, v):
    rows, n = w.shape

    def body(p_ref, w_ref, m_ref, v_ref, g_out, d_out, m_out, v_out):
        g = p_ref[0]
        for s in range(1, parts.shape[0]):
            g = g + p_ref[s]
        g = jnp.sum(g, axis=0, keepdims=True) if rows == 1 else g[0:rows, :]
        delta, m_new, v_new = _adamw(w_ref[...], g, m_ref[...], v_ref[...])
        g_out[...] = g
        d_out[...] = delta
        m_out[...] = m_new
        v_out[...] = v_new

    return pl.pallas_call(body, name=name, out_shape=[jax.ShapeDtypeStruct((rows, n), F32)] * 4)(parts, w, m, v)


def _rope_tables(t):
    pos = jnp.arange(t, dtype=jnp.int32)
    half = HEAD_DIM // 4
    inv_freq = ROPE_THETA ** (-jnp.arange(0, 2 * half, 2, dtype=F32) / (2 * half))
    ang_r = (pos // GRID_W).astype(F32)[:, None] * inv_freq[None, :]
    ang_c = (pos % GRID_W).astype(F32)[:, None] * inv_freq[None, :]
    cos = jnp.concatenate([jnp.cos(ang_r)] * 2 + [jnp.cos(ang_c)] * 2, axis=-1)
    sin = jnp.concatenate([-jnp.sin(ang_r), jnp.sin(ang_r), -jnp.sin(ang_c), jnp.sin(ang_c)], axis=-1)
    return cos, sin


def _gather_weight(name, w, cols):
    g = _all_gather(name, w[0].astype(BF16))
    return g if cols else g.reshape(1, N_DEV * g.shape[1], g.shape[2])


def kernel(x, p, norm_mix, w_in, w_dw, conv_ln_g, conv_ln_b, w_conv_proj, q_norm, k_norm, w_attn_proj, w_out, norm_ffn, w_ff1, w_ff2, norm_ple, w_ple_gate, w_ple_proj, norm_final, loss_target, m_norm_mix, m_w_in, m_w_dw, m_conv_ln_g, m_conv_ln_b, m_w_conv_proj, m_q_norm, m_k_norm, m_w_attn_proj, m_w_out, m_norm_ffn, m_w_ff1, m_w_ff2, m_norm_ple, m_w_ple_gate, m_w_ple_proj, m_norm_final, v_norm_mix, v_w_in, v_w_dw, v_conv_ln_g, v_conv_ln_b, v_w_conv_proj, v_q_norm, v_k_norm, v_w_attn_proj, v_w_out, v_norm_ffn, v_w_ff1, v_w_ff2, v_norm_ple, v_w_ple_gate, v_w_ple_proj, v_norm_final):
    t, d = x.shape[1], x.shape[2]
    cw = d // 2
    n_heads = d // HEAD_DIM
    n_kv = n_heads // GROUP
    col_q, col_k, col_v = d // HEAD_DIM, 2 * d // HEAD_DIM, (2 * d + d // 4) // HEAD_DIM
    x0, pe, tgt = x[0], p[0, 0], loss_target[0]
    g_final = norm_final.reshape(1, d)
    me = _index(*_position())

    wg_in = _gather_weight("ag_w_in", w_in, True)
    dw_pad = jnp.pad(w_dw[0], ((0, CONV_TAPS_PAD - CONV_KERNEL), (0, 0)))
    dw_all = _all_gather("ag_w_dw", dw_pad)
    dw_full = dw_all.transpose(1, 0, 2).reshape(CONV_TAPS_PAD, cw)
    cos, sin = _rope_tables(t)

    def fetch(name, w, prev):
        return _exchange_start("ag_start_" + name, w[0].astype(BF16), after=(wg_in, dw_all) if prev is None else (prev,))

    def arrived(name, handle, cols, after):
        g = _exchange_wait("ag_wait_" + name, handle, after)
        return g if cols else g.reshape(1, N_DEV * g.shape[1], g.shape[2])

    on_cp, tok = fetch("w_conv_proj", w_conv_proj, None)
    on_ap, tok = fetch("w_attn_proj", w_attn_proj, tok)
    on_out, tok = fetch("w_out", w_out, tok)
    on_ff1, tok = fetch("w_ff1", w_ff1, tok)
    on_ff2, tok = fetch("w_ff2", w_ff2, tok)
    on_pg, tok = fetch("w_ple_gate", w_ple_gate, tok)
    on_pp, tok = fetch("w_ple_proj", w_ple_proj, tok)

    h1 = _rms_fwd("rms_mix", x0, norm_mix, after=(tok,))
    z = _mm_nn("mm_in", h1, wg_in, F32)
    conv, s_c = _conv_fwd("conv_fwd", z, cw, dw_full, conv_ln_g, conv_ln_b)
    wg_cp = arrived("w_conv_proj", on_cp, True, s_c)
    y_c = _mm_nn("mm_conv_proj", s_c, wg_cp, F32)
    qr = _head_norm_rope("q_prep", z, col_q, n_heads, q_norm, cos, sin, out_scale=QK_SCALE * LOG2_E,
                         heads_per_step=GROUP)
    kr = _head_norm_rope("k_prep", z, col_k, n_kv, k_norm, cos, sin)
    vt = z[:, col_v * HEAD_DIM:(col_v + n_kv) * HEAD_DIM].T.astype(BF16).reshape(n_kv, HEAD_DIM, t)
    vt = jnp.concatenate([vt, jnp.ones((n_kv, ONES_ROWS, t), BF16)], axis=1).reshape(n_kv * (HEAD_DIM + ONES_ROWS), t)
    o, lse = _flash_fwd(qr, kr, vt, n_heads)
    wg_ap = arrived("w_attn_proj", on_ap, False, o)
    y_a = _mm_nn("mm_attn_proj", o, wg_ap, F32)
    merged = _merge_fwd("merge_fwd", z, d, y_c, y_a)
    wg_out = arrived("w_out", on_out, False, merged)
    x1 = _mm_nn("mm_out", merged, wg_out, F32, residual=x0)
    h2 = _rms_fwd("rms_ffn", x1, norm_ffn)
    wg_ff1 = arrived("w_ff1", on_ff1, True, h2)
    f, act = _mm_nn("mm_ff1", h2, wg_ff1, F32, activation=_relu2)
    wg_ff2 = arrived("w_ff2", on_ff2, False, act)
    x2 = _mm_nn("mm_ff2", act, wg_ff2, F32, residual=x1)
    hp = _rms_fwd("rms_ple", x2, norm_ple)
    wg_pg = arrived("w_ple_gate", on_pg, False, hp)
    pre = _mm_nn("mm_ple_gate", hp, wg_pg, F32)
    wg_pp = arrived("w_ple_proj", on_pp, True, pre)
    pp = _mm_nn("mm_ple_proj", pe, wg_pp, F32)

    dx3, dpp, dpre, dg_final, loss_part = _head_loss("head_loss", x2, pre, pp, tgt, g_final)
    loss = lax.psum(loss_part[0, 0], ("x", "y", "c"))
    def send(name, gw):
        blocks = gw if gw.shape[0] == N_DEV else gw.reshape(N_DEV, gw.shape[1] // N_DEV, gw.shape[2])
        return _exchange_start("rs_start_" + name, blocks)

    sent_pp, tok_pp = send("w_ple_proj", _mm_tn("mm_d_ple_proj", pe, dpp, N_DEV))
    sent_pg, tok = send("w_ple_gate", _mm_tn("mm_d_ple_gate", hp, dpre, 1))
    dhp = _mm_nt("mm_dhp", dpre, wg_pg, F32, after=(tok_pp, tok))
    dx2, dg_ple = _rms_bwd("rms_ple_bwd", x2, norm_ple, dhp, dx3)
    sent_ff2, tok = send("w_ff2", _mm_tn("mm_d_ff2", act, dx2, 1))
    df = _mm_nt("mm_dact", dx2, wg_ff2, BF16, after=(tok,), through=(_relu2_bwd, f))
    sent_ff1, tok = send("w_ff1", _mm_tn("mm_d_ff1", h2, df, N_DEV))
    dh2 = _mm_nt("mm_dh2", df, wg_ff1, F32, after=(tok,))
    dx1, dg_ffn = _rms_bwd("rms_ffn_bwd", x1, norm_ffn, dh2, dx2)
    sent_out, tok = send("w_out", _mm_tn("mm_d_out", merged, dx1, 1))
    dmerged = _mm_nt("mm_dmerged", dx1, wg_out, F32, after=(tok,))
    dgc, dga, dyc, dya = _merge_bwd("merge_bwd", dmerged, z, d, y_c, y_a)
    sent_cp, tok = send("w_conv_proj", _mm_tn("mm_d_conv_proj", s_c, dyc, N_DEV))
    ds_c = _mm_nt("mm_ds_c", dyc, wg_cp, F32, after=(tok,))
    dconv, dg_ln, db_ln = _conv_ln_bwd("conv_ln_bwd", ds_c, conv, conv_ln_g, conv_ln_b)
    dca, dcb, dg_dw = _conv_glu_bwd("conv_glu_bwd", dconv, z, cw, dw_full)
    sent_ap, tok = send("w_attn_proj", _mm_tn("mm_d_attn_proj", o, dya, 1))
    do = _mm_nt("mm_do", dya, wg_ap, BF16, after=(tok,))
    delta = _attn_delta(do, o, n_heads)
    dqt, dkh, dvh = _flash_bwd(qr, kr, kr.T, z, col_v, do, lse, delta, n_heads)
    dq, dg_q = _head_norm_rope_bwd("q_prep_bwd", [(dqt, 0, 1)], z, col_q, n_heads, q_norm, cos, sin,
                                   heads_per_step=GROUP, transposed=True)
    dk, dg_k = _head_norm_rope_bwd("k_prep_bwd", [(dkh, g, GROUP) for g in range(GROUP)], z, col_k, n_kv, k_norm,
                                   cos, sin)
    dv = _group_sum("dv_sum", dvh, n_kv)
    dz = jnp.concatenate([dca, dcb, dq, dk, dv, dgc, dga], axis=1)
    sent_in, tok = send("w_in", _mm_tn("mm_d_in", h1, dz, N_DEV))
    dh1 = _mm_nt("mm_dh1", dz, wg_in, F32, after=(tok,))
    grad_x, dg_mix = _rms_bwd("rms_mix_bwd", x0, norm_mix, dh1, dx1)

    def pad(a):
        return jnp.pad(a, ((0, 0), (0, d - a.shape[1])))

    small = [dg_mix, dg_ffn, dg_ple, dg_final, dg_ln, db_ln, dg_q, dg_k, dg_dw]
    packed = _all_gather("ag_small", jnp.concatenate([pad(a) for a in small], axis=0))
    parts, row = [], 0
    for a in small:
        parts.append(packed[:, row:row + a.shape[0], :a.shape[1]])
        row += a.shape[0]
    p_mix, p_ffn, p_ple, p_final, p_lng, p_lnb, p_q, p_k, p_dw = parts
    p_dw = lax.dynamic_slice_in_dim(p_dw, me * (cw // N_DEV), cw // N_DEV, axis=2)

    def little(name, part, w, m, v):
        rows, n = (1 if w.ndim < 3 else w.shape[1]), w.shape[-1]
        padded = rows if rows == 1 else part.shape[1]

        def two_d(a):
            return jnp.pad(a.reshape(rows, n), ((0, padded - rows), (0, 0)))

        res = _adamw_small("adamw_" + name, part, two_d(w), two_d(m), two_d(v))
        return [r[:rows].reshape(w.shape) for r in res]

    done = {}
    after = packed
    for name, sent, w, m, v in [
            ("w_ple_proj", sent_pp, w_ple_proj, m_w_ple_proj, v_w_ple_proj),
            ("w_ple_gate", sent_pg, w_ple_gate, m_w_ple_gate, v_w_ple_gate),
            ("w_ff2", sent_ff2, w_ff2, m_w_ff2, v_w_ff2),
            ("w_ff1", sent_ff1, w_ff1, m_w_ff1, v_w_ff1),
            ("w_out", sent_out, w_out, m_w_out, v_w_out),
            ("w_conv_proj", sent_cp, w_conv_proj, m_w_conv_proj, v_w_conv_proj),
            ("w_attn_proj", sent_ap, w_attn_proj, m_w_attn_proj, v_w_attn_proj),
            ("w_in", sent_in, w_in, m_w_in, v_w_in)]:
        land = _exchange_wait("rs_wait_" + name, sent, after)
        res = _adamw_shard("adamw_" + name, land, w[0], m[0], v[0])
        after = res[0]
        done[name] = [r.reshape(w.shape) for r in res]

    results = [
        little("norm_mix", p_mix, norm_mix, m_norm_mix, v_norm_mix),
        done["w_in"],
        little("w_dw", p_dw, w_dw, m_w_dw, v_w_dw),
        little("conv_ln_g", p_lng, conv_ln_g, m_conv_ln_g, v_conv_ln_g),
        little("conv_ln_b", p_lnb, conv_ln_b, m_conv_ln_b, v_conv_ln_b),
        done["w_conv_proj"],
        little("q_norm", p_q, q_norm, m_q_norm, v_q_norm),
        little("k_norm", p_k, k_norm, m_k_norm, v_k_norm),
        done["w_attn_proj"],
        done["w_out"],
        little("norm_ffn", p_ffn, norm_ffn, m_norm_ffn, v_norm_ffn),
        done["w_ff1"],
        done["w_ff2"],
        little("norm_ple", p_ple, norm_ple, m_norm_ple, v_norm_ple),
        done["w_ple_gate"],
        done["w_ple_proj"],
        little("norm_final", p_final, norm_final, m_norm_final, v_norm_final),
    ]
    grads, deltas, new_m, new_v = zip(*results)
    return (loss, grad_x[None], *grads, *deltas, *new_m, *new_v)
```

```python
import functools

import jax
import jax.numpy as jnp
from jax import lax
from jax.experimental import pallas as pl
from jax.experimental.pallas import tpu as pltpu

F32 = jnp.float32
BF16 = jnp.bfloat16

EPS = 1e-6
HEAD_DIM = 128
GROUP = 4
CONV_KERNEL = 31
CONV_HALO = 16
CONV_TAPS_PAD = 32
GRID_W = 64
ROPE_THETA = 10000.0
ADAM_LR, ADAM_B1, ADAM_B2, ADAM_EPS, ADAM_WD, ADAM_STEP = 0.001, 0.9, 0.999, 1e-08, 0.01, 10

N_DEV = 8
MESH = pl.DeviceIdType.MESH
SUBLANES = 8
LANES = 128
VMEM_LIMIT = 48 * 1024 * 1024
ROW_TILE_ELEMS = 512 * 1024


def _div(n, pref, mult=1):
    if n <= pref:
        return n
    for d in range(pref, 0, -1):
        if n % d == 0 and d % mult == 0:
            return d
    return n


def _params(sem):
    return pltpu.CompilerParams(dimension_semantics=sem, vmem_limit_bytes=VMEM_LIMIT)


def _sum8(a):
    tm, w = a.shape
    return a.reshape(tm // SUBLANES, SUBLANES, w).sum(axis=0)


def _sigmoid(x):
    return jax.nn.sigmoid(x)


def _rowwise(name, fn, n_rows, tm, ins, outs, accs=(), ncol=1, refs_to_fn=False, after=()):
    nrow = n_rows // tm
    in_specs, arrays = [], []
    for spec in ins:
        kind, arr, w, c0 = spec[:4]
        if kind == "t":
            cs, r0 = spec[4], spec[5]
            in_specs.append(pl.BlockSpec((tm, w), lambda j, i, c0=c0, cs=cs, r0=r0: (r0 + i, c0 + cs * j)))
        elif kind == "p":
            h = spec[4]
            r = tm // h
            in_specs.append(pl.BlockSpec((h, w), lambda j, i, c0=c0, r=r: (jnp.maximum(i * r - 1, 0), c0 + j)))
        elif kind == "n":
            h = spec[4]
            r = tm // h
            last = n_rows // h - 1
            in_specs.append(
                pl.BlockSpec((h, w), lambda j, i, c0=c0, r=r, last=last: (jnp.minimum((i + 1) * r, last), c0 + j)))
        elif kind == "x":
            in_specs.append(pl.BlockSpec((w, tm), lambda j, i, c0=c0: (c0 + j, i)))
        else:
            cs = spec[4]
            in_specs.append(pl.BlockSpec((arr.shape[0], w), lambda j, i, c0=c0, cs=cs: (0, c0 + cs * j)))
        arrays.append(arr)
    out_shape, out_specs = [], []
    for dtype, total, w in outs:
        out_shape.append(jax.ShapeDtypeStruct((n_rows, total), dtype))
        out_specs.append(pl.BlockSpec((tm, w), lambda j, i: (i, j)))
    for rows, total, w, follow in accs:
        out_shape.append(jax.ShapeDtypeStruct((rows, total), F32))
        out_specs.append(pl.BlockSpec((rows, w), (lambda j, i: (0, j)) if follow else (lambda j, i: (0, 0))))
    n_in, n_out, n_acc = len(ins), len(outs), len(accs)
    for token in after:
        in_specs.append(pl.BlockSpec(memory_space=pl.ANY))
        arrays.append(token)
    out0 = n_in + len(after)

    def body(*refs):
        j, i = pl.program_id(0), pl.program_id(1)
        res = fn(i, nrow, *(refs[:n_in] if refs_to_fn else [r[...] for r in refs[:n_in]]))
        if not isinstance(res, (tuple, list)):
            res = (res,)
        for k in range(n_out):
            refs[out0 + k][...] = res[k].astype(outs[k][0])
        for k in range(n_acc):
            ref, term = refs[out0 + n_out + k], res[n_out + k]
            first = (i == 0) if accs[k][3] else jnp.logical_and(i == 0, j == 0)

            @pl.when(first)
            def _():
                ref[...] = term

            @pl.when(jnp.logical_not(first))
            def _():
                ref[...] += term

    res = pl.pallas_call(
        body, name=name, grid=(ncol, nrow), in_specs=in_specs, out_specs=out_specs, out_shape=out_shape,
        compiler_params=_params(("arbitrary", "arbitrary")))(*arrays)
    return res


def _row_tile(n_rows, width, mult=SUBLANES):
    return _div(n_rows, max(mult, ROW_TILE_ELEMS // width), mult)


def _t(arr, w=None, col0=0, cstride=1, row0=0):
    return ("t", arr, arr.shape[1] if w is None else w, col0, cstride, row0)


def _b(arr, w=None, col0=0, cstride=0):
    return ("b", arr, arr.shape[1] if w is None else w, col0, cstride)


def _rms_fwd(name, x, g, after=()):
    n, d = x.shape

    def fn(i, nrow, x, g):
        r = lax.rsqrt(jnp.mean(x * x, axis=-1, keepdims=True) + EPS)
        return x * r * g

    return _rowwise(name, fn, n, _row_tile(n, d), [_t(x), _b(g)], [(BF16, d, d)], after=after)[0]


def _rms_bwd(name, x, g, dh, dres):
    n, d = x.shape

    def fn(i, nrow, x, g, dh, dres):
        r = lax.rsqrt(jnp.mean(x * x, axis=-1, keepdims=True) + EPS)
        w = dh * g
        dx = dres + r * w - x * (r * r * r) * jnp.mean(x * w, axis=-1, keepdims=True)
        return dx, _sum8(dh * x * r)

    return _rowwise(name, fn, n, _row_tile(n, 2 * d), [_t(x), _b(g), _t(dh), _t(dres)], [(F32, d, d)],
                    [(SUBLANES, d, d, True)])


def _swap_halves(x):
    lane = lax.broadcasted_iota(jnp.int32, x.shape, 1)
    return jnp.where(lane % 64 < 32, pltpu.roll(x, 96, axis=1), pltpu.roll(x, 32, axis=1))


def _head_norm_rope(name, z, col0, n_heads, g, cos, sin, out_scale=None, heads_per_step=1):
    n, hp = z.shape[0], heads_per_step
    assert col0 % hp == 0 and n_heads % hp == 0

    def fn(i, nrow, x, g, cos, sin):
        ys = []
        for h in range(hp):
            xh = x[:, h * HEAD_DIM:(h + 1) * HEAD_DIM]
            r = lax.rsqrt(jnp.mean(xh * xh, axis=-1, keepdims=True) + EPS)
            y = xh * r * g
            y = y * cos + _swap_halves(y) * sin
            ys.append(y if out_scale is None else y * out_scale)
        return ys[0] if hp == 1 else jnp.concatenate(ys, axis=1)

    tm = _div(n, 1024, SUBLANES)
    w = hp * HEAD_DIM
    return _rowwise(name, fn, n, tm, [_t(z, w, col0 // hp), _b(g), _t(cos, cstride=0), _t(sin, cstride=0)],
                    [(BF16, n_heads * HEAD_DIM, w)], ncol=n_heads // hp)[0]


def _head_norm_rope_bwd(name, douts, z, col0, n_heads, g, cos, sin, heads_per_step=1, transposed=False):
    n, hp = z.shape[0], heads_per_step
    n_d = len(douts)
    assert col0 % hp == 0 and n_heads % hp == 0 and (hp == 1 or all(c0 % hp == 0 and cs == 1 for _, c0, cs in douts))
    assert not transposed or n_d == 1

    def fn(i, nrow, *a):
        dy = a[0]
        for k in range(1, n_d):
            dy = dy + a[k]
        x, g, cos, sin = a[n_d:]
        dxs, dg = [], None
        for h in range(hp):
            xh = x[:, h * HEAD_DIM:(h + 1) * HEAD_DIM]
            if transposed:
                dyh = jnp.transpose(dy[h * HEAD_DIM:(h + 1) * HEAD_DIM, :])
            else:
                dyh = dy[:, h * HEAD_DIM:(h + 1) * HEAD_DIM]
            dn = dyh * cos + _swap_halves(dyh * sin)
            r = lax.rsqrt(jnp.mean(xh * xh, axis=-1, keepdims=True) + EPS)
            w = dn * g
            dxs.append(r * w - xh * (r * r * r) * jnp.mean(xh * w, axis=-1, keepdims=True))
            term = _sum8(dn * xh * r)
            dg = term if dg is None else dg + term
        return (dxs[0] if hp == 1 else jnp.concatenate(dxs, axis=1)), dg

    tm = _div(n, 1024, SUBLANES)
    w = hp * HEAD_DIM
    if transposed:
        ins = [("x", douts[0][0], w, douts[0][1] // hp)]
    else:
        ins = [_t(arr, w, c0 // hp, cs) for arr, c0, cs in douts]
    ins += [_t(z, w, col0 // hp), _b(g), _t(cos, cstride=0), _t(sin, cstride=0)]
    return _rowwise(name, fn, n, tm, ins, [(BF16, n_heads * HEAD_DIM, w)],
                    [(SUBLANES, HEAD_DIM, HEAD_DIM, False)], ncol=n_heads // hp)


def _halo(arr, w, col0):
    return [("p", arr, w, col0, CONV_HALO), _t(arr, w, col0), ("n", arr, w, col0, CONV_HALO)]


def _extend(i, nrow, prev, cur, nxt):
    prev = jnp.where(i > 0, prev, 0.0)
    nxt = jnp.where(i < nrow - 1, nxt, 0.0)
    return jnp.concatenate([prev, cur, nxt], axis=0)


def _shifted(ext, offset, tm):
    n = ext.shape[0]
    rolled = ext if offset == 0 else pltpu.roll(ext, (-offset) % n, axis=0)
    return rolled[CONV_HALO:CONV_HALO + tm]


def _conv_fwd(name, z, cw, w_dw, ln_g, ln_b):
    n = z.shape[0]
    tm = _row_tile(n, 4 * cw, CONV_HALO)

    def fn(i, nrow, ap, a, an, bp, b, bn, w, g, beta):
        a = a[...]
        ext = _extend(i, nrow, ap[...] * _sigmoid(bp[...]), a * _sigmoid(b[...]), an[...] * _sigmoid(bn[...]))
        conv = jnp.zeros_like(a)
        for k in range(CONV_KERNEL):
            conv = conv + _shifted(ext, k - CONV_KERNEL // 2, tm) * w[k:k + 1, :]
        xc = conv - jnp.mean(conv, axis=-1, keepdims=True)
        ln = xc * lax.rsqrt(jnp.mean(xc * xc, axis=-1, keepdims=True) + EPS) * g[...] + beta[...]
        return conv, ln * _sigmoid(ln)

    ins = _halo(z, cw, 0) + _halo(z, cw, 1) + [_b(w_dw), _b(ln_g), _b(ln_b)]
    return _rowwise(name, fn, n, tm, ins, [(F32, cw, cw), (BF16, cw, cw)], refs_to_fn=True)


def _conv_ln_bwd(name, ds, conv, ln_g, ln_b):
    n, cw = conv.shape

    def fn(i, nrow, ds, conv, g, beta):
        xc = conv - jnp.mean(conv, axis=-1, keepdims=True)
        rstd = lax.rsqrt(jnp.mean(xc * xc, axis=-1, keepdims=True) + EPS)
        xhat = xc * rstd
        ln = xhat * g + beta
        sg = _sigmoid(ln)
        dln = ds * (sg * (1.0 + ln * (1.0 - sg)))
        dxh = dln * g
        dconv = rstd * (dxh - jnp.mean(dxh, axis=-1, keepdims=True)
                        - xhat * jnp.mean(dxh * xhat, axis=-1, keepdims=True))
        return dconv, _sum8(dln * xhat), _sum8(dln)

    return _rowwise(name, fn, n, _row_tile(n, 2 * cw), [_t(ds), _t(conv), _b(ln_g), _b(ln_b)], [(F32, cw, cw)],
                    [(SUBLANES, cw, cw, True), (SUBLANES, cw, cw, True)])


def _conv_glu_bwd(name, dconv, z, cw, w_dw):
    n = z.shape[0]
    tm = _row_tile(n, 4 * cw, CONV_HALO)

    def fn(i, nrow, dp, d, dn, a, b, w):
        d, a = d[...], a[...]
        dext = _extend(i, nrow, dp[...], d, dn[...])
        sg = _sigmoid(b[...])
        u = a * sg
        tap = lax.broadcasted_iota(jnp.int32, (CONV_TAPS_PAD, cw), 0)
        du = jnp.zeros_like(d)
        dw = jnp.zeros((CONV_TAPS_PAD, cw), F32)
        for k in range(CONV_KERNEL):
            d_k = _shifted(dext, -(k - CONV_KERNEL // 2), tm)
            du = du + d_k * w[k:k + 1, :]
            dw = dw + jnp.where(tap == k, jnp.sum(d_k * u, axis=0, keepdims=True), 0.0)
        return du * sg, du * a * sg * (1.0 - sg), dw

    ins = _halo(dconv, cw, 0) + [_t(z, cw, 0), _t(z, cw, 1), _b(w_dw)]
    return _rowwise(name, fn, n, tm, ins, [(BF16, cw, cw), (BF16, cw, cw)], [(CONV_TAPS_PAD, cw, cw, True)],
                    refs_to_fn=True)


def _merge_fwd(name, z, d, y_c, y_a):
    n, hw = z.shape[0], d // 2

    def fn(i, nrow, gc, ga, yc, ya):
        return _sigmoid(gc) * yc + _sigmoid(ga) * ya

    ins = [_t(z, hw, 5), _t(z, hw, 7), _t(y_c, hw), _t(y_a, hw)]
    return _rowwise(name, fn, n, _row_tile(n, 2 * hw), ins, [(BF16, d, hw)], ncol=2)[0]


def _merge_bwd(name, dm, z, d, y_c, y_a):
    n, hw = z.shape[0], d // 2

    def fn(i, nrow, dm, gc, ga, yc, ya):
        sc, sa = _sigmoid(gc), _sigmoid(ga)
        return dm * yc * sc * (1.0 - sc), dm * ya * sa * (1.0 - sa), dm * sc, dm * sa

    ins = [_t(dm, hw), _t(z, hw, 5), _t(z, hw, 7), _t(y_c, hw), _t(y_a, hw)]
    return _rowwise(name, fn, n, _row_tile(n, 2 * hw), ins, [(BF16, d, hw)] * 4, ncol=2)


def _relu2(f):
    r = jnp.maximum(f, 0.0)
    return r * r


def _relu2_bwd(da, f):
    return da * (2.0 * jnp.maximum(f, 0.0))


def _head_loss(name, x2, pre, pp, target, g):
    n, d = x2.shape

    def fn(i, nrow, x2, pre, pp, t, g):
        gate = _sigmoid(pre)
        x3 = x2 + gate * pp
        r = lax.rsqrt(jnp.mean(x3 * x3, axis=-1, keepdims=True) + EPS)
        xn = x3 * r
        e = xn * g - t
        part = 0.5 * jnp.sum(jnp.mean(e * e, axis=-1, keepdims=True), axis=0, keepdims=True)
        dy = e * (1.0 / d)
        w = dy * g
        dx3 = r * w - x3 * (r * r * r) * jnp.mean(x3 * w, axis=-1, keepdims=True)
        return (dx3, dx3 * gate, dx3 * pp * gate * (1.0 - gate), _sum8(dy * xn),
                jnp.broadcast_to(part, (SUBLANES, LANES)))

    ins = [_t(x2), _t(pre), _t(pp), _t(target), _b(g)]
    return _rowwise(name, fn, n, _row_tile(n, 2 * d), ins, [(F32, d, d), (BF16, d, d), (BF16, d, d)],
                    [(SUBLANES, d, d, True), (SUBLANES, LANES, LANES, True)])


def _group_sum(name, dvh, n_kv):
    n = dvh.shape[0]

    def fn(i, nrow, *a):
        s = a[0]
        for k in range(1, GROUP):
            s = s + a[k]
        return s

    ins = [_t(dvh, HEAD_DIM, k, GROUP) for k in range(GROUP)]
    return _rowwise(name, fn, n, _div(n, 1024, SUBLANES), ins, [(BF16, n_kv * HEAD_DIM, HEAD_DIM)], ncol=n_kv)[0]


MM_ROWS = 1024
MM_COLS = 1152
MM_REDUCE = 2048
MM_REDUCE_BLOCKS = 2304


def _rows_outermost(n_rows, n_cols, row_tile_bytes, weight_tile_bytes):
    rows_outer = n_rows * (row_tile_bytes + n_cols * weight_tile_bytes)
    weight_outer = n_cols * (weight_tile_bytes + n_rows * row_tile_bytes)
    return 5 * rows_outer < 4 * weight_outer


def _accumulate(step, n_steps, part, acc, finish):
    if n_steps == 1:
        finish(part)
        return

    @pl.when(step == 0)
    def _():
        acc[...] = part

    @pl.when(jnp.logical_and(step > 0, step < n_steps - 1))
    def _():
        acc[...] += part

    @pl.when(step == n_steps - 1)
    def _():
        finish(acc[...] + part)


MXU_WIDTH = 256


def _paired(nb, ns):
    return ns % MXU_WIDTH != 0 and nb % 2 == 0 and ns <= MM_COLS


def _mm_nn(name, a, w3, out_dtype, residual=None, activation=None):
    m, k = a.shape
    nb, _, ns = w3.shape
    jb = 2 if _paired(nb, ns) else 1
    tm = _div(m, MM_ROWS // jb, SUBLANES)
    tn_w, tk = (ns if jb > 1 else _div(ns, MM_COLS, LANES)), _div(k, MM_REDUCE, LANES)
    tn = jb * tn_w
    per, nk = ns // tn_w, k // tk
    has_res = residual is not None
    n_out = 1 if activation is None else 2

    def body(*refs):
        a_ref, w_ref = refs[:2]
        acc = refs[-1]
        outs = refs[-1 - n_out:-1]
        w = w_ref[0] if jb == 1 else jnp.concatenate([w_ref[j] for j in range(jb)], axis=1)
        part = jnp.dot(a_ref[...].astype(BF16), w.astype(BF16), preferred_element_type=F32)

        def finish(total):
            if has_res:
                total = total + refs[2][...]
            outs[0][...] = total.astype(out_dtype)
            if activation is not None:
                outs[1][...] = activation(total).astype(BF16)

        _accumulate(pl.program_id(2), nk, part, acc, finish)

    n_rows, n_cols = m // tm, (nb // jb) * per
    rows_outer = nk > 1 or _rows_outermost(n_rows, n_cols, tm * tk * a.dtype.itemsize,
                                           jb * tk * tn_w * w3.dtype.itemsize)

    def at(f):
        return f if rows_outer else (lambda n, i, kk: f(i, n, kk))

    in_specs = [pl.BlockSpec((tm, tk), at(lambda i, n, kk: (i, kk))),
                pl.BlockSpec((jb, tk, tn_w), at(lambda i, n, kk: (n // per, kk, n % per)))]
    args = [a, w3]
    if has_res:
        in_specs.append(pl.BlockSpec((tm, tn), at(lambda i, n, kk: (i, n))))
        args.append(residual)
    out_dtypes = [out_dtype] + ([BF16] if activation is not None else [])
    res = pl.pallas_call(
        body, name=name, grid=(n_rows, n_cols, nk) if rows_outer else (n_cols, n_rows, nk), in_specs=in_specs,
        out_specs=[pl.BlockSpec((tm, tn), at(lambda i, n, kk: (i, n))) for _ in out_dtypes],
        out_shape=[jax.ShapeDtypeStruct((m, nb * ns), dt) for dt in out_dtypes],
        scratch_shapes=[pltpu.VMEM((tm, tn) if nk > 1 else (SUBLANES, LANES), F32)],
        compiler_params=_params(("parallel", "parallel", "arbitrary")))(*args)
    return res[0] if activation is None else res


def _mm_nt(name, dc, w3, out_dtype, after=(), through=None, head_dots=None):
    m = dc.shape[0]
    nb, k, ns = w3.shape
    tm, to = _div(m, MM_ROWS, SUBLANES), _div(k, MM_ROWS, LANES)
    if ns > MM_REDUCE_BLOCKS // 2:
        jb, tr = 1, _div(ns, MM_REDUCE, LANES)
    else:
        jb, tr = max(j for j in range(1, nb + 1) if nb % j == 0 and j * ns <= MM_REDUCE_BLOCKS), ns
    per = ns // tr
    nr = (nb // jb) * per
    y = through[1] if through is not None else head_dots
    n_out = 1 if head_dots is None else 2
    heads = to // HEAD_DIM

    def body(dc_ref, w_ref, *rest):
        acc = rest[-1]
        outs = rest[-1 - n_out:-1]
        w = w_ref[0] if jb == 1 else jnp.concatenate([w_ref[j] for j in range(jb)], axis=1)
        part = lax.dot_general(dc_ref[...].astype(BF16), w.astype(BF16), (((1,), (1,)), ((), ())),
                               preferred_element_type=F32)

        def finish(total):
            if through is not None:
                total = through[0](total, rest[0][...])
            result = total.astype(out_dtype)
            outs[0][...] = result
            if head_dots is not None:
                prod = result.astype(F32) * rest[0][...]
                for h in range(heads):
                    row = jnp.sum(jnp.transpose(prod[:, h * HEAD_DIM:(h + 1) * HEAD_DIM]), axis=0, keepdims=True)
                    outs[1][h] = jnp.broadcast_to(row, (SUBLANES, tm))

        _accumulate(pl.program_id(2), nr, part, acc, finish)

    rows_outer = nr > 1 or _rows_outermost(m // tm, k // to, tm * jb * tr * dc.dtype.itemsize,
                                           jb * to * tr * w3.dtype.itemsize)

    def at(f):
        return f if rows_outer else (lambda o, i, r: f(i, o, r))

    in_specs = [pl.BlockSpec((tm, jb * tr), at(lambda i, o, r: (i, r))),
                pl.BlockSpec((jb, to, tr), at(lambda i, o, r: (r // per, o, r % per)))]
    args = [dc, w3]
    if y is not None:
        in_specs.append(pl.BlockSpec((tm, to), at(lambda i, o, r: (i, o))))
        args.append(y)
    for token in after:
        in_specs.append(pl.BlockSpec(memory_space=pl.ANY))
        args.append(token)
    out_specs = [pl.BlockSpec((tm, to), at(lambda i, o, r: (i, o)))]
    out_shape = [jax.ShapeDtypeStruct((m, k), out_dtype)]
    if head_dots is not None:
        out_specs.append(pl.BlockSpec((heads, SUBLANES, tm), at(lambda i, o, r: (o, 0, i))))
        out_shape.append(jax.ShapeDtypeStruct((k // HEAD_DIM, SUBLANES, m), F32))
    res = pl.pallas_call(
        body, name=name, grid=(m // tm, k // to, nr) if rows_outer else (k // to, m // tm, nr), in_specs=in_specs,
        out_specs=out_specs, out_shape=out_shape,
        scratch_shapes=[pltpu.VMEM((tm, to) if nr > 1 else (SUBLANES, LANES), F32)],
        compiler_params=_params(("parallel", "parallel", "arbitrary")))(*args)
    return res[0] if head_dots is None else res


def _mm_tn(name, a, dc, nb):
    m, k = a.shape
    ns = dc.shape[1] // nb
    jb = 2 if _paired(nb, ns) else 1
    tr, to = _div(m, MM_REDUCE, LANES), _div(k, MM_ROWS // jb, LANES)
    tn_w = ns if jb > 1 else _div(ns, MM_COLS, LANES)
    tn = jb * tn_w
    per, nr = ns // tn_w, m // tr

    def body(a_ref, dc_ref, o_ref, acc):
        part = lax.dot_general(a_ref[...].astype(BF16), dc_ref[...].astype(BF16), (((0,), (0,)), ((), ())),
                               preferred_element_type=F32)

        def finish(total):
            for j in range(jb):
                o_ref[j] = total[:, j * tn_w:(j + 1) * tn_w].astype(BF16)

        _accumulate(pl.program_id(2), nr, part, acc, finish)

    return pl.pallas_call(
        body, name=name, grid=(k // to, (nb // jb) * per, nr),
        in_specs=[pl.BlockSpec((tr, to), lambda o, n, r: (r, o)),
                  pl.BlockSpec((tr, tn), lambda o, n, r: (r, n))],
        out_specs=pl.BlockSpec((jb, to, tn_w), lambda o, n, r: (n // per, o, n % per)),
        out_shape=jax.ShapeDtypeStruct((nb, k, ns), BF16),
        scratch_shapes=[pltpu.VMEM((to, tn) if nr > 1 else (SUBLANES, LANES), F32)],
        compiler_params=_params(("parallel", "parallel", "arbitrary")))(a, dc)


QK_SCALE = HEAD_DIM ** -0.5
LOG2_E = 1.4426950408889634
LN_2 = 0.6931471805599453


def _columns(row, n):
    return jnp.transpose(jnp.broadcast_to(row, (LANES, n)))


FLASH_TILE = 1024
ONES_ROWS = 16


def _flash_fwd(qr, kr, vt, n_heads):
    t = qr.shape[0]
    tq, tk = _div(t, FLASH_TILE, LANES), _div(t, 2 * FLASH_TILE, LANES)
    nk = t // tk
    rows = vt.shape[0] // (n_heads // GROUP)

    def body(q_ref, k_ref, vt_ref, o_ref, lse_ref, m_scr, acc_scr):
        kb = pl.program_id(2)

        @pl.when(kb == 0)
        def _():
            m_scr[...] = jnp.full_like(m_scr, -1e30)
            acc_scr[...] = jnp.zeros_like(acc_scr)

        s_t = lax.dot_general(k_ref[...], q_ref[...], (((1,), (1,)), ((), ())), preferred_element_type=F32)
        m_prev = m_scr[...]
        m_new = jnp.maximum(m_prev, jnp.max(s_t, axis=0, keepdims=True))
        alpha = jnp.exp2(m_prev - m_new)
        p_t = jnp.exp2(s_t - m_new).astype(BF16)
        acc_scr[...] = alpha * acc_scr[...] + jnp.dot(vt_ref[...], p_t, preferred_element_type=F32)
        m_scr[...] = m_new

        @pl.when(kb == nk - 1)
        def _():
            l = acc_scr[HEAD_DIM:HEAD_DIM + 1, :]
            o_ref[...] = jnp.transpose(acc_scr[0:HEAD_DIM, :] / l)
            lse_ref[...] = jnp.broadcast_to(m_scr[...] + jnp.log(l) * LOG2_E, (SUBLANES, tq))

    return pl.pallas_call(
        body, name="flash_fwd", grid=(n_heads, t // tq, nk),
        in_specs=[pl.BlockSpec((tq, HEAD_DIM), lambda h, qi, ki: (qi, h)),
                  pl.BlockSpec((tk, HEAD_DIM), lambda h, qi, ki: (ki, h // GROUP)),
                  pl.BlockSpec((rows, tk), lambda h, qi, ki: (h // GROUP, ki))],
        out_specs=[pl.BlockSpec((tq, HEAD_DIM), lambda h, qi, ki: (qi, h)),
                   pl.BlockSpec((None, SUBLANES, tq), lambda h, qi, ki: (h, 0, qi))],
        out_shape=[jax.ShapeDtypeStruct((t, n_heads * HEAD_DIM), F32),
                   jax.ShapeDtypeStruct((n_heads, SUBLANES, t), F32)],
        scratch_shapes=[pltpu.VMEM((1, tq), F32), pltpu.VMEM((rows, tq), F32)],
        compiler_params=_params(("parallel", "parallel", "arbitrary")))(qr, kr, vt)


def _flash_bwd(qr, kr, kt, z, v_col0, do, lse, delta, n_heads):
    t = qr.shape[0]
    tq, tk = _div(t, 2 * FLASH_TILE, LANES), _div(t, FLASH_TILE, LANES)
    nq = t // tq

    def body(q_ref, k_ref, kt_ref, v_ref, do_ref, lse_ref, dl_ref, dq_ref, dk_ref, dv_ref, dk_acc, dv_acc):
        kb, qb = pl.program_id(1), pl.program_id(2)
        q, k, do_ = q_ref[...], k_ref[...], do_ref[...]
        v = v_ref[...].astype(BF16)
        s_t = lax.dot_general(k, q, (((1,), (1,)), ((), ())), preferred_element_type=F32)
        p_t = jnp.exp2(s_t - lse_ref[0:1, :])
        dp_t = lax.dot_general(v, do_, (((1,), (1,)), ((), ())), preferred_element_type=F32)
        ds_t = (p_t * (dp_t - dl_ref[0:1, :])).astype(BF16)
        dv_c = jnp.dot(p_t.astype(BF16), do_, preferred_element_type=F32)
        dk_c = jnp.dot(ds_t, q, preferred_element_type=F32) * LN_2
        dq_c = jnp.dot(kt_ref[...], ds_t, preferred_element_type=F32) * QK_SCALE

        @pl.when(qb == 0)
        def _():
            dk_acc[...] = dk_c
            dv_acc[...] = dv_c

        @pl.when(qb > 0)
        def _():
            dk_acc[...] += dk_c
            dv_acc[...] += dv_c

        @pl.when(qb == nq - 1)
        def _():
            dk_ref[...] = dk_acc[...]
            dv_ref[...] = dv_acc[...]

        cols = pl.ds(pl.multiple_of(qb * tq, tq), tq)

        @pl.when(kb == 0)
        def _():
            dq_ref[:, cols] = dq_c

        @pl.when(kb > 0)
        def _():
            dq_ref[:, cols] += dq_c

    wide = jax.ShapeDtypeStruct((t, n_heads * HEAD_DIM), F32)
    return pl.pallas_call(
        body, name="flash_bwd", grid=(n_heads, t // tk, nq),
        in_specs=[pl.BlockSpec((tq, HEAD_DIM), lambda h, kb, qb: (qb, h)),
                  pl.BlockSpec((tk, HEAD_DIM), lambda h, kb, qb: (kb, h // GROUP)),
                  pl.BlockSpec((HEAD_DIM, tk), lambda h, kb, qb: (h // GROUP, kb)),
                  pl.BlockSpec((tk, HEAD_DIM), lambda h, kb, qb: (kb, v_col0 + h // GROUP)),
                  pl.BlockSpec((tq, HEAD_DIM), lambda h, kb, qb: (qb, h)),
                  pl.BlockSpec((None, SUBLANES, tq), lambda h, kb, qb: (h, 0, qb)),
                  pl.BlockSpec((None, SUBLANES, tq), lambda h, kb, qb: (h, 0, qb))],
        out_specs=[pl.BlockSpec((HEAD_DIM, t), lambda h, kb, qb: (h, 0)),
                   pl.BlockSpec((tk, HEAD_DIM), lambda h, kb, qb: (kb, h)),
                   pl.BlockSpec((tk, HEAD_DIM), lambda h, kb, qb: (kb, h))],
        out_shape=[jax.ShapeDtypeStruct((n_heads * HEAD_DIM, t), F32), wide, wide],
        scratch_shapes=[pltpu.VMEM((tk, HEAD_DIM), F32), pltpu.VMEM((tk, HEAD_DIM), F32)],
        compiler_params=_params(("parallel", "arbitrary", "arbitrary")))(qr, kr, kt, z, do, lse, delta)


def _position():
    x, y, c = lax.axis_index("x"), lax.axis_index("y"), lax.axis_index("c")
    return x, y, c


def _index(x, y, c):
    return 4 * x + 2 * y + c


def _all_gather(name, shard):
    a, b = shard.shape

    def body(x_ref, out_ref, send_sems, recv_sems, local_sem):
        x, y, c = _position()
        me, sibling = (x, y, c), (x, y, 1 - c)
        chips = [(1 - x, y), (x, 1 - y), (1 - x, 1 - y)]

        def block(px, py, pc):
            return out_ref.at[_index(px, py, pc)]

        def copy(k, blk, to, src=None):
            return pltpu.make_async_remote_copy(
                src_ref=block(*blk) if src is None else src, dst_ref=block(*blk),
                send_sem=send_sems.at[k], recv_sem=recv_sems.at[k], device_id=to, device_id_type=MESH)

        mine = pltpu.make_async_copy(x_ref, block(*me), local_sem)
        mine.start()
        first = [copy(0, me, sibling, src=x_ref)]
        first += [copy(1 + j, me, (*chip, c), src=x_ref) for j, chip in enumerate(chips)]
        for cp in first:
            cp.start()
        passed = [copy(4 + j, (*chip, c), sibling) for j, chip in enumerate(chips)]
        for j, chip in enumerate(chips):
            copy(1 + j, (*chip, c), me).wait_recv()
            passed[j].start()
        copy(0, sibling, me).wait_recv()
        for j, chip in enumerate(chips):
            copy(4 + j, (*chip, 1 - c), me).wait_recv()
        for cp in first + passed:
            cp.wait_send()
        mine.wait()

    return pl.pallas_call(
        body, name=name, out_shape=jax.ShapeDtypeStruct((N_DEV, a, b), shard.dtype),
        in_specs=[pl.BlockSpec(memory_space=pltpu.HBM)], out_specs=pl.BlockSpec(memory_space=pltpu.HBM),
        scratch_shapes=[pltpu.SemaphoreType.DMA((7,)), pltpu.SemaphoreType.DMA((7,)), pltpu.SemaphoreType.DMA],
    )(shard)


HBM_SPEC = pl.BlockSpec(memory_space=pltpu.HBM)
SEM_SPEC = pl.BlockSpec(memory_space=pltpu.SEMAPHORE)
N_PEERS = N_DEV - 1
PEER_FLIPS = [(fx, fy, fc) for fx in (0, 1) for fy in (0, 1) for fc in (0, 1)][1:]


def _peers(x, y, c):
    return [((1 - x) if fx else x, (1 - y) if fy else y, (1 - c) if fc else c) for fx, fy, fc in PEER_FLIPS]


def _exchange_copies(g_ref, land_ref, send_sems, recv_sems):
    x, y, c = _position()
    me = _index(x, y, c)
    peers = _peers(x, y, c)
    blocked = len(g_ref.shape) == 3

    def copy(k, src_block, dst_block):
        return pltpu.make_async_remote_copy(
            src_ref=g_ref.at[src_block] if blocked else g_ref, dst_ref=land_ref.at[dst_block],
            send_sem=send_sems.at[k], recv_sem=recv_sems.at[k], device_id=peers[k], device_id_type=MESH)

    sent = [copy(k, _index(*peer), me) for k, peer in enumerate(peers)]
    arriving = [copy(k, me, _index(*peer)) for k, peer in enumerate(peers)]
    own = pltpu.make_async_copy(g_ref.at[me] if blocked else g_ref, land_ref.at[me], send_sems.at[N_PEERS])
    return sent, arriving, own


def _exchange_start(name, g3, after=()):
    def body(g_ref, land_ref, *rest):
        send_sems, recv_sems, _, _, token = rest[len(after):]
        sent, _, own = _exchange_copies(g_ref, land_ref, send_sems, recv_sems)
        for cp in sent:
            cp.start()
        own.start()
        token[...] = jnp.zeros_like(token)

    land_shape = g3.shape if g3.ndim == 3 else (N_DEV,) + g3.shape
    land = lax.empty(land_shape, g3.dtype)
    send_sems, recv_sems, g_thru, land_thru, token = pl.pallas_call(
        body, name=name,
        out_shape=(pltpu.SemaphoreType.DMA((N_PEERS + 1,)), pltpu.SemaphoreType.DMA((N_PEERS,)),
                   pltpu.HBM(g3.shape, g3.dtype), pltpu.HBM(land_shape, g3.dtype),
                   jax.ShapeDtypeStruct((SUBLANES, LANES), F32)),
        in_specs=(HBM_SPEC, HBM_SPEC) + (pl.BlockSpec(memory_space=pl.ANY),) * len(after),
        out_specs=(SEM_SPEC, SEM_SPEC, HBM_SPEC, HBM_SPEC, pl.BlockSpec(memory_space=pltpu.VMEM)),
        input_output_aliases={0: 2, 1: 3},
        compiler_params=pltpu.CompilerParams(has_side_effects=pltpu.SideEffectType.DATAFLOW_SIDE_EFFECTING),
    )(pltpu.with_memory_space_constraint(g3, pltpu.HBM), pltpu.with_memory_space_constraint(land, pltpu.HBM), *after)
    return (send_sems, recv_sems, g_thru, land_thru), token


def _exchange_wait(name, handle, after):
    send_sems, recv_sems, g_thru, land_thru = handle

    def body(g_ref, land_ref, send_sems, recv_sems, after_ref, g_dead, land_out):
        sent, arriving, own = _exchange_copies(g_ref, land_ref, send_sems, recv_sems)
        for cp in sent:
            cp.wait_send()
        for cp in arriving:
            cp.wait_recv()
        own.wait()

    return pl.pallas_call(
        body, name=name,
        out_shape=(pltpu.HBM(g_thru.shape, g_thru.dtype), pltpu.HBM(land_thru.shape, land_thru.dtype)),
        in_specs=(HBM_SPEC, HBM_SPEC, SEM_SPEC, SEM_SPEC, pl.BlockSpec(memory_space=pl.ANY)),
        out_specs=(HBM_SPEC, HBM_SPEC), input_output_aliases={0: 0, 1: 1},
        compiler_params=pltpu.CompilerParams(has_side_effects=pltpu.SideEffectType.DATAFLOW_SIDE_EFFECTING),
    )(g_thru, land_thru, send_sems, recv_sems, after)[1]


def _adamw(w, g, m, v):
    m = ADAM_B1 * m + (1.0 - ADAM_B1) * g
    v = ADAM_B2 * v + (1.0 - ADAM_B2) * (g * g)
    m_hat = m / (1.0 - ADAM_B1 ** ADAM_STEP)
    v_hat = v / (1.0 - ADAM_B2 ** ADAM_STEP)
    delta = -ADAM_LR * (m_hat / (jnp.sqrt(v_hat) + ADAM_EPS) + ADAM_WD * w)
    return delta, m, v


def _adamw_shard(name, land, w, m, v):
    _, a, b = land.shape
    tm = _div(a, max(16, ROW_TILE_ELEMS // (2 * b)), 16)
    per = a // tm

    def fn(i, nrow, *t):
        g = t[0].astype(F32)
        for s in range(1, N_DEV):
            g = g + t[s].astype(F32)
        w, m, v = t[N_DEV:]
        return (g,) + _adamw(w, g, m, v)

    land2 = land.reshape(N_DEV * a, b)
    ins = [_t(land2, row0=s * per) for s in range(N_DEV)] + [_t(w), _t(m), _t(v)]
    return _rowwise(name, fn, a, tm, ins, [(F32, b, b)] * 4)


def _adamw_small(name, parts, w, m, v):
    rows, n = w.shape

    def body(p_ref, w_ref, m_ref, v_ref, g_out, d_out, m_out, v_out):
        g = p_ref[0]
        for s in range(1, parts.shape[0]):
            g = g + p_ref[s]
        g = jnp.sum(g, axis=0, keepdims=True) if rows == 1 else g[0:rows, :]
        delta, m_new, v_new = _adamw(w_ref[...], g, m_ref[...], v_ref[...])
        g_out[...] = g
        d_out[...] = delta
        m_out[...] = m_new
        v_out[...] = v_new

    return pl.pallas_call(body, name=name, out_shape=[jax.ShapeDtypeStruct((rows, n), F32)] * 4)(parts, w, m, v)


def _rope_tables(t):
    pos = jnp.arange(t, dtype=jnp.int32)
    half = HEAD_DIM // 4
    inv_freq = ROPE_THETA ** (-jnp.arange(0, 2 * half, 2, dtype=F32) / (2 * half))
    ang_r = (pos // GRID_W).astype(F32)[:, None] * inv_freq[None, :]
    ang_c = (pos % GRID_W).astype(F32)[:, None] * inv_freq[None, :]
    cos = jnp.concatenate([jnp.cos(ang_r)] * 2 + [jnp.cos(ang_c)] * 2, axis=-1)
    sin = jnp.concatenate([-jnp.sin(ang_r), jnp.sin(ang_r), -jnp.sin(ang_c), jnp.sin(ang_c)], axis=-1)
    return cos, sin


def _gather_weight(name, w, cols):
    g = _all_gather(name, w[0].astype(BF16))
    return g if cols else g.reshape(1, N_DEV * g.shape[1], g.shape[2])


def kernel(x, p, norm_mix, w_in, w_dw, conv_ln_g, conv_ln_b, w_conv_proj, q_norm, k_norm, w_attn_proj, w_out, norm_ffn, w_ff1, w_ff2, norm_ple, w_ple_gate, w_ple_proj, norm_final, loss_target, m_norm_mix, m_w_in, m_w_dw, m_conv_ln_g, m_conv_ln_b, m_w_conv_proj, m_q_norm, m_k_norm, m_w_attn_proj, m_w_out, m_norm_ffn, m_w_ff1, m_w_ff2, m_norm_ple, m_w_ple_gate, m_w_ple_proj, m_norm_final, v_norm_mix, v_w_in, v_w_dw, v_conv_ln_g, v_conv_ln_b, v_w_conv_proj, v_q_norm, v_k_norm, v_w_attn_proj, v_w_out, v_norm_ffn, v_w_ff1, v_w_ff2, v_norm_ple, v_w_ple_gate, v_w_ple_proj, v_norm_final):
    t, d = x.shape[1], x.shape[2]
    cw = d // 2
    n_heads = d // HEAD_DIM
    n_kv = n_heads // GROUP
    col_q, col_k, col_v = d // HEAD_DIM, 2 * d // HEAD_DIM, (2 * d + d // 4) // HEAD_DIM
    x0, pe, tgt = x[0], p[0, 0], loss_target[0]
    g_final = norm_final.reshape(1, d)
    me = _index(*_position())

    wg_in = _gather_weight("ag_w_in", w_in, True)
    dw_pad = jnp.pad(w_dw[0], ((0, CONV_TAPS_PAD - CONV_KERNEL), (0, 0)))
    dw_all = _all_gather("ag_w_dw", dw_pad)
    dw_full = dw_all.transpose(1, 0, 2).reshape(CONV_TAPS_PAD, cw)
    cos, sin = _rope_tables(t)

    def fetch(name, w, prev):
        return _exchange_start("ag_start_" + name, w[0].astype(BF16), after=(wg_in, dw_all) if prev is None else (prev,))

    def arrived(name, handle, cols, after):
        g = _exchange_wait("ag_wait_" + name, handle, after)
        return g if cols else g.reshape(1, N_DEV * g.shape[1], g.shape[2])

    on_cp, tok = fetch("w_conv_proj", w_conv_proj, None)
    on_ap, tok = fetch("w_attn_proj", w_attn_proj, tok)
    on_out, tok = fetch("w_out", w_out, tok)
    on_ff1, tok = fetch("w_ff1", w_ff1, tok)
    on_ff2, tok = fetch("w_ff2", w_ff2, tok)
    on_pg, tok = fetch("w_ple_gate", w_ple_gate, tok)
    on_pp, tok = fetch("w_ple_proj", w_ple_proj, tok)

    h1 = _rms_fwd("rms_mix", x0, norm_mix, after=(tok,))
    z = _mm_nn("mm_in", h1, wg_in, F32)
    conv, s_c = _conv_fwd("conv_fwd", z, cw, dw_full, conv_ln_g, conv_ln_b)
    wg_cp = arrived("w_conv_proj", on_cp, True, s_c)
    y_c = _mm_nn("mm_conv_proj", s_c, wg_cp, F32)
    qr = _head_norm_rope("q_prep", z, col_q, n_heads, q_norm, cos, sin, out_scale=QK_SCALE * LOG2_E,
                         heads_per_step=GROUP)
    kr = _head_norm_rope("k_prep", z, col_k, n_kv, k_norm, cos, sin)
    vt = z[:, col_v * HEAD_DIM:(col_v + n_kv) * HEAD_DIM].T.astype(BF16).reshape(n_kv, HEAD_DIM, t)
    vt = jnp.concatenate([vt, jnp.ones((n_kv, ONES_ROWS, t), BF16)], axis=1).reshape(n_kv * (HEAD_DIM + ONES_ROWS), t)
    o, lse = _flash_fwd(qr, kr, vt, n_heads)
    wg_ap = arrived("w_attn_proj", on_ap, False, o)
    y_a = _mm_nn("mm_attn_proj", o, wg_ap, F32)
    merged = _merge_fwd("merge_fwd", z, d, y_c, y_a)
    wg_out = arrived("w_out", on_out, False, merged)
    x1 = _mm_nn("mm_out", merged, wg_out, F32, residual=x0)
    h2 = _rms_fwd("rms_ffn", x1, norm_ffn)
    wg_ff1 = arrived("w_ff1", on_ff1, True, h2)
    f, act = _mm_nn("mm_ff1", h2, wg_ff1, F32, activation=_relu2)
    wg_ff2 = arrived("w_ff2", on_ff2, False, act)
    x2 = _mm_nn("mm_ff2", act, wg_ff2, F32, residual=x1)
    hp = _rms_fwd("rms_ple", x2, norm_ple)
    wg_pg = arrived("w_ple_gate", on_pg, False, hp)
    pre = _mm_nn("mm_ple_gate", hp, wg_pg, F32)
    wg_pp = arrived("w_ple_proj", on_pp, True, pre)
    pp = _mm_nn("mm_ple_proj", pe, wg_pp, F32)

    dx3, dpp, dpre, dg_final, loss_part = _head_loss("head_loss", x2, pre, pp, tgt, g_final)
    loss = lax.psum(loss_part[0, 0], ("x", "y", "c"))
    def send(name, gw):
        blocks = gw if gw.shape[0] == N_DEV else gw.reshape(N_DEV, gw.shape[1] // N_DEV, gw.shape[2])
        return _exchange_start("rs_start_" + name, blocks)

    sent_pp, tok_pp = send("w_ple_proj", _mm_tn("mm_d_ple_proj", pe, dpp, N_DEV))
    sent_pg, tok = send("w_ple_gate", _mm_tn("mm_d_ple_gate", hp, dpre, 1))
    dhp = _mm_nt("mm_dhp", dpre, wg_pg, F32, after=(tok_pp, tok))
    dx2, dg_ple = _rms_bwd("rms_ple_bwd", x2, norm_ple, dhp, dx3)
    sent_ff2, tok = send("w_ff2", _mm_tn("mm_d_ff2", act, dx2, 1))
    df = _mm_nt("mm_dact", dx2, wg_ff2, BF16, after=(tok,), through=(_relu2_bwd, f))
    sent_ff1, tok = send("w_ff1", _mm_tn("mm_d_ff1", h2, df, N_DEV))
    dh2 = _mm_nt("mm_dh2", df, wg_ff1, F32, after=(tok,))
    dx1, dg_ffn = _rms_bwd("rms_ffn_bwd", x1, norm_ffn, dh2, dx2)
    sent_out, tok = send("w_out", _mm_tn("mm_d_out", merged, dx1, 1))
    dmerged = _mm_nt("mm_dmerged", dx1, wg_out, F32, after=(tok,))
    dgc, dga, dyc, dya = _merge_bwd("merge_bwd", dmerged, z, d, y_c, y_a)
    sent_cp, tok = send("w_conv_proj", _mm_tn("mm_d_conv_proj", s_c, dyc, N_DEV))
    ds_c = _mm_nt("mm_ds_c", dyc, wg_cp, F32, after=(tok,))
    dconv, dg_ln, db_ln = _conv_ln_bwd("conv_ln_bwd", ds_c, conv, conv_ln_g, conv_ln_b)
    dca, dcb, dg_dw = _conv_glu_bwd("conv_glu_bwd", dconv, z, cw, dw_full)
    sent_ap, tok = send("w_attn_proj", _mm_tn("mm_d_attn_proj", o, dya, 1))
    do, delta = _mm_nt("mm_do", dya, wg_ap, BF16, after=(tok,), head_dots=o)
    dqt, dkh, dvh = _flash_bwd(qr, kr, kr.T, z, col_v, do, lse, delta, n_heads)
    dq, dg_q = _head_norm_rope_bwd("q_prep_bwd", [(dqt, 0, 1)], z, col_q, n_heads, q_norm, cos, sin,
                                   heads_per_step=GROUP, transposed=True)
    dk, dg_k = _head_norm_rope_bwd("k_prep_bwd", [(dkh, g, GROUP) for g in range(GROUP)], z, col_k, n_kv, k_norm,
                                   cos, sin)
    dv = _group_sum("dv_sum", dvh, n_kv)
    dz = jnp.concatenate([dca, dcb, dq, dk, dv, dgc, dga], axis=1)
    sent_in, tok = send("w_in", _mm_tn("mm_d_in", h1, dz, N_DEV))
    dh1 = _mm_nt("mm_dh1", dz, wg_in, F32, after=(tok,))
    grad_x, dg_mix = _rms_bwd("rms_mix_bwd", x0, norm_mix, dh1, dx1)

    def pad(a):
        return jnp.pad(a, ((0, 0), (0, d - a.shape[1])))

    small = [dg_mix, dg_ffn, dg_ple, dg_final, dg_ln, db_ln, dg_q, dg_k, dg_dw]
    packed = _all_gather("ag_small", jnp.concatenate([pad(a) for a in small], axis=0))
    parts, row = [], 0
    for a in small:
        parts.append(packed[:, row:row + a.shape[0], :a.shape[1]])
        row += a.shape[0]
    p_mix, p_ffn, p_ple, p_final, p_lng, p_lnb, p_q, p_k, p_dw = parts
    p_dw = lax.dynamic_slice_in_dim(p_dw, me * (cw // N_DEV), cw // N_DEV, axis=2)

    def little(name, part, w, m, v):
        rows, n = (1 if w.ndim < 3 else w.shape[1]), w.shape[-1]
        padded = rows if rows == 1 else part.shape[1]

        def two_d(a):
            return jnp.pad(a.reshape(rows, n), ((0, padded - rows), (0, 0)))

        res = _adamw_small("adamw_" + name, part, two_d(w), two_d(m), two_d(v))
        return [r[:rows].reshape(w.shape) for r in res]

    done = {}
    after = packed
    for name, sent, w, m, v in [
            ("w_ple_proj", sent_pp, w_ple_proj, m_w_ple_proj, v_w_ple_proj),
            ("w_ple_gate", sent_pg, w_ple_gate, m_w_ple_gate, v_w_ple_gate),
            ("w_ff2", sent_ff2, w_ff2, m_w_ff2, v_w_ff2),
            ("w_ff1", sent_ff1, w_ff1, m_w_ff1, v_w_ff1),
            ("w_out", sent_out, w_out, m_w_out, v_w_out),
            ("w_conv_proj", sent_cp, w_conv_proj, m_w_conv_proj, v_w_conv_proj),
            ("w_attn_proj", sent_ap, w_attn_proj, m_w_attn_proj, v_w_attn_proj),
            ("w_in", sent_in, w_in, m_w_in, v_w_in)]:
        land = _exchange_wait("rs_wait_" + name, sent, after)
        res = _adamw_shard("adamw_" + name, land, w[0], m[0], v[0])
        after = res[0]
        done[name] = [r.reshape(w.shape) for r in res]

    results = [
        little("norm_mix", p_mix, norm_mix, m_norm_mix, v_norm_mix),
        done["w_in"],
        little("w_dw", p_dw, w_dw, m_w_dw, v_w_dw),
        little("conv_ln_g", p_lng, conv_ln_g, m_conv_ln_g, v_conv_ln_g),
        little("conv_ln_b", p_lnb, conv_ln_b, m_conv_ln_b, v_conv_ln_b),
        done["w_conv_proj"],
        little("q_norm", p_q, q_norm, m_q_norm, v_q_norm),
        little("k_norm", p_k, k_norm, m_k_norm, v_k_norm),
        done["w_attn_proj"],
        done["w_out"],
        little("norm_ffn", p_ffn, norm_ffn, m_norm_ffn, v_norm_ffn),
        done["w_ff1"],
        done["w_ff2"],
        little("norm_ple", p_ple, norm_ple, m_norm_ple, v_norm_ple),
        done["w_ple_gate"],
        done["w_ple_proj"],
        little("norm_final", p_final, norm_final, m_norm_final, v_norm_final),
    ]
    grads, deltas, new_m, new_v = zip(*results)
    return (loss, grad_x[None], *grads, *deltas, *new_m, *new_v)
```

```python
import functools

import jax
import jax.numpy as jnp
from jax import lax
from jax.experimental import pallas as pl
from jax.experimental.pallas import tpu as pltpu

F32 = jnp.float32
BF16 = jnp.bfloat16

EPS = 1e-6
HEAD_DIM = 128
GROUP = 4
CONV_KERNEL = 31
CONV_HALO = 16
CONV_TAPS_PAD = 32
GRID_W = 64
ROPE_THETA = 10000.0
ADAM_LR, ADAM_B1, ADAM_B2, ADAM_EPS, ADAM_WD, ADAM_STEP = 0.001, 0.9, 0.999, 1e-08, 0.01, 10

N_DEV = 8
MESH = pl.DeviceIdType.MESH
SUBLANES = 8
LANES = 128
VMEM_LIMIT = 48 * 1024 * 1024
ROW_TILE_ELEMS = 512 * 1024


def _div(n, pref, mult=1):
    if n <= pref:
        return n
    for d in range(pref, 0, -1):
        if n % d == 0 and d % mult == 0:
            return d
    return n


def _params(sem):
    return pltpu.CompilerParams(dimension_semantics=sem, vmem_limit_bytes=VMEM_LIMIT)


def _sum8(a):
    tm, w = a.shape
    return a.reshape(tm // SUBLANES, SUBLANES, w).sum(axis=0)


def _sigmoid(x):
    return jax.nn.sigmoid(x)


def _rowwise(name, fn, n_rows, tm, ins, outs, accs=(), ncol=1, refs_to_fn=False, after=()):
    nrow = n_rows // tm
    in_specs, arrays = [], []
    for spec in ins:
        kind, arr, w, c0 = spec[:4]
        if kind == "t":
            cs, r0 = spec[4], spec[5]
            in_specs.append(pl.BlockSpec((tm, w), lambda j, i, c0=c0, cs=cs, r0=r0: (r0 + i, c0 + cs * j)))
        elif kind == "p":
            h = spec[4]
            r = tm // h
            in_specs.append(pl.BlockSpec((h, w), lambda j, i, c0=c0, r=r: (jnp.maximum(i * r - 1, 0), c0 + j)))
        elif kind == "n":
            h = spec[4]
            r = tm // h
            last = n_rows // h - 1
            in_specs.append(
                pl.BlockSpec((h, w), lambda j, i, c0=c0, r=r, last=last: (jnp.minimum((i + 1) * r, last), c0 + j)))
        elif kind == "x":
            in_specs.append(pl.BlockSpec((w, tm), lambda j, i, c0=c0: (c0 + j, i)))
        else:
            cs = spec[4]
            in_specs.append(pl.BlockSpec((arr.shape[0], w), lambda j, i, c0=c0, cs=cs: (0, c0 + cs * j)))
        arrays.append(arr)
    out_shape, out_specs = [], []
    for dtype, total, w in outs:
        out_shape.append(jax.ShapeDtypeStruct((n_rows, total), dtype))
        out_specs.append(pl.BlockSpec((tm, w), lambda j, i: (i, j)))
    for rows, total, w, follow in accs:
        out_shape.append(jax.ShapeDtypeStruct((rows, total), F32))
        out_specs.append(pl.BlockSpec((rows, w), (lambda j, i: (0, j)) if follow else (lambda j, i: (0, 0))))
    n_in, n_out, n_acc = len(ins), len(outs), len(accs)
    for token in after:
        in_specs.append(pl.BlockSpec(memory_space=pl.ANY))
        arrays.append(token)
    out0 = n_in + len(after)

    def body(*refs):
        j, i = pl.program_id(0), pl.program_id(1)
        res = fn(i, nrow, *(refs[:n_in] if refs_to_fn else [r[...] for r in refs[:n_in]]))
        if not isinstance(res, (tuple, list)):
            res = (res,)
        for k in range(n_out):
            refs[out0 + k][...] = res[k].astype(outs[k][0])
        for k in range(n_acc):
            ref, term = refs[out0 + n_out + k], res[n_out + k]
            first = (i == 0) if accs[k][3] else jnp.logical_and(i == 0, j == 0)

            @pl.when(first)
            def _():
                ref[...] = term

            @pl.when(jnp.logical_not(first))
            def _():
                ref[...] += term

    res = pl.pallas_call(
        body, name=name, grid=(ncol, nrow), in_specs=in_specs, out_specs=out_specs, out_shape=out_shape,
        compiler_params=_params(("arbitrary", "arbitrary")))(*arrays)
    return res


def _row_tile(n_rows, width, mult=SUBLANES):
    return _div(n_rows, max(mult, ROW_TILE_ELEMS // width), mult)


def _t(arr, w=None, col0=0, cstride=1, row0=0):
    return ("t", arr, arr.shape[1] if w is None else w, col0, cstride, row0)


def _b(arr, w=None, col0=0, cstride=0):
    return ("b", arr, arr.shape[1] if w is None else w, col0, cstride)


def _rms_fwd(name, x, g, after=()):
    n, d = x.shape

    def fn(i, nrow, x, g):
        r = lax.rsqrt(jnp.mean(x * x, axis=-1, keepdims=True) + EPS)
        return x * r * g

    return _rowwise(name, fn, n, _row_tile(n, d), [_t(x), _b(g)], [(BF16, d, d)], after=after)[0]


def _rms_bwd(name, x, g, dh, dres):
    n, d = x.shape

    def fn(i, nrow, x, g, dh, dres):
        r = lax.rsqrt(jnp.mean(x * x, axis=-1, keepdims=True) + EPS)
        w = dh * g
        dx = dres + r * w - x * (r * r * r) * jnp.mean(x * w, axis=-1, keepdims=True)
        return dx, _sum8(dh * x * r)

    return _rowwise(name, fn, n, _row_tile(n, 2 * d), [_t(x), _b(g), _t(dh), _t(dres)], [(F32, d, d)],
                    [(SUBLANES, d, d, True)])


def _swap_halves(x):
    lane = lax.broadcasted_iota(jnp.int32, x.shape, 1)
    return jnp.where(lane % 64 < 32, pltpu.roll(x, 96, axis=1), pltpu.roll(x, 32, axis=1))


def _head_norm_rope(name, z, col0, n_heads, g, cos, sin, out_scale=None, heads_per_step=1):
    n, hp = z.shape[0], heads_per_step
    assert col0 % hp == 0 and n_heads % hp == 0

    def fn(i, nrow, x, g, cos, sin):
        ys = []
        for h in range(hp):
            xh = x[:, h * HEAD_DIM:(h + 1) * HEAD_DIM]
            r = lax.rsqrt(jnp.mean(xh * xh, axis=-1, keepdims=True) + EPS)
            y = xh * r * g
            y = y * cos + _swap_halves(y) * sin
            ys.append(y if out_scale is None else y * out_scale)
        return ys[0] if hp == 1 else jnp.concatenate(ys, axis=1)

    tm = _div(n, 1024, SUBLANES)
    w = hp * HEAD_DIM
    return _rowwise(name, fn, n, tm, [_t(z, w, col0 // hp), _b(g), _t(cos, cstride=0), _t(sin, cstride=0)],
                    [(BF16, n_heads * HEAD_DIM, w)], ncol=n_heads // hp)[0]


def _head_norm_rope_bwd(name, douts, z, col0, n_heads, g, cos, sin, heads_per_step=1, transposed=False):
    n, hp = z.shape[0], heads_per_step
    n_d = len(douts)
    assert col0 % hp == 0 and n_heads % hp == 0 and (hp == 1 or all(c0 % hp == 0 and cs == 1 for _, c0, cs in douts))
    assert not transposed or n_d == 1

    def fn(i, nrow, *a):
        dy = a[0]
        for k in range(1, n_d):
            dy = dy + a[k]
        x, g, cos, sin = a[n_d:]
        dxs, dg = [], None
        for h in range(hp):
            xh = x[:, h * HEAD_DIM:(h + 1) * HEAD_DIM]
            if transposed:
                dyh = jnp.transpose(dy[h * HEAD_DIM:(h + 1) * HEAD_DIM, :])
            else:
                dyh = dy[:, h * HEAD_DIM:(h + 1) * HEAD_DIM]
            dn = dyh * cos + _swap_halves(dyh * sin)
            r = lax.rsqrt(jnp.mean(xh * xh, axis=-1, keepdims=True) + EPS)
            w = dn * g
            dxs.append(r * w - xh * (r * r * r) * jnp.mean(xh * w, axis=-1, keepdims=True))
            term = _sum8(dn * xh * r)
            dg = term if dg is None else dg + term
        return (dxs[0] if hp == 1 else jnp.concatenate(dxs, axis=1)), dg

    tm = _div(n, 1024, SUBLANES)
    w = hp * HEAD_DIM
    if transposed:
        ins = [("x", douts[0][0], w, douts[0][1] // hp)]
    else:
        ins = [_t(arr, w, c0 // hp, cs) for arr, c0, cs in douts]
    ins += [_t(z, w, col0 // hp), _b(g), _t(cos, cstride=0), _t(sin, cstride=0)]
    return _rowwise(name, fn, n, tm, ins, [(BF16, n_heads * HEAD_DIM, w)],
                    [(SUBLANES, HEAD_DIM, HEAD_DIM, False)], ncol=n_heads // hp)


def _halo(arr, w, col0):
    return [("p", arr, w, col0, CONV_HALO), _t(arr, w, col0), ("n", arr, w, col0, CONV_HALO)]


def _extend(i, nrow, prev, cur, nxt):
    prev = jnp.where(i > 0, prev, 0.0)
    nxt = jnp.where(i < nrow - 1, nxt, 0.0)
    return jnp.concatenate([prev, cur, nxt], axis=0)


def _shifted(ext, offset, tm):
    n = ext.shape[0]
    rolled = ext if offset == 0 else pltpu.roll(ext, (-offset) % n, axis=0)
    return rolled[CONV_HALO:CONV_HALO + tm]


def _conv_fwd(name, z, cw, w_dw, ln_g, ln_b):
    n = z.shape[0]
    tm = _row_tile(n, 4 * cw, CONV_HALO)

    def fn(i, nrow, ap, a, an, bp, b, bn, w, g, beta):
        a = a[...]
        ext = _extend(i, nrow, ap[...] * _sigmoid(bp[...]), a * _sigmoid(b[...]), an[...] * _sigmoid(bn[...]))
        conv = jnp.zeros_like(a)
        for k in range(CONV_KERNEL):
            conv = conv + _shifted(ext, k - CONV_KERNEL // 2, tm) * w[k:k + 1, :]
        xc = conv - jnp.mean(conv, axis=-1, keepdims=True)
        ln = xc * lax.rsqrt(jnp.mean(xc * xc, axis=-1, keepdims=True) + EPS) * g[...] + beta[...]
        return conv, ln * _sigmoid(ln)

    ins = _halo(z, cw, 0) + _halo(z, cw, 1) + [_b(w_dw), _b(ln_g), _b(ln_b)]
    return _rowwise(name, fn, n, tm, ins, [(F32, cw, cw), (BF16, cw, cw)], refs_to_fn=True)


def _conv_ln_bwd(name, ds, conv, ln_g, ln_b):
    n, cw = conv.shape

    def fn(i, nrow, ds, conv, g, beta):
        xc = conv - jnp.mean(conv, axis=-1, keepdims=True)
        rstd = lax.rsqrt(jnp.mean(xc * xc, axis=-1, keepdims=True) + EPS)
        xhat = xc * rstd
        ln = xhat * g + beta
        sg = _sigmoid(ln)
        dln = ds * (sg * (1.0 + ln * (1.0 - sg)))
        dxh = dln * g
        dconv = rstd * (dxh - jnp.mean(dxh, axis=-1, keepdims=True)
                        - xhat * jnp.mean(dxh * xhat, axis=-1, keepdims=True))
        return dconv, _sum8(dln * xhat), _sum8(dln)

    return _rowwise(name, fn, n, _row_tile(n, 2 * cw), [_t(ds), _t(conv), _b(ln_g), _b(ln_b)], [(F32, cw, cw)],
                    [(SUBLANES, cw, cw, True), (SUBLANES, cw, cw, True)])


def _conv_glu_bwd(name, dconv, z, cw, w_dw):
    n = z.shape[0]
    tm = _row_tile(n, 4 * cw, CONV_HALO)

    def fn(i, nrow, dp, d, dn, a, b, w):
        d, a = d[...], a[...]
        dext = _extend(i, nrow, dp[...], d, dn[...])
        sg = _sigmoid(b[...])
        u = a * sg
        tap = lax.broadcasted_iota(jnp.int32, (CONV_TAPS_PAD, cw), 0)
        du = jnp.zeros_like(d)
        dw = jnp.zeros((CONV_TAPS_PAD, cw), F32)
        for k in range(CONV_KERNEL):
            d_k = _shifted(dext, -(k - CONV_KERNEL // 2), tm)
            du = du + d_k * w[k:k + 1, :]
            dw = dw + jnp.where(tap == k, jnp.sum(d_k * u, axis=0, keepdims=True), 0.0)
        return du * sg, du * a * sg * (1.0 - sg), dw

    ins = _halo(dconv, cw, 0) + [_t(z, cw, 0), _t(z, cw, 1), _b(w_dw)]
    return _rowwise(name, fn, n, tm, ins, [(BF16, cw, cw), (BF16, cw, cw)], [(CONV_TAPS_PAD, cw, cw, True)],
                    refs_to_fn=True)


def _gated_merge(y_a, gc, ga, y_c):
    return _sigmoid(gc) * y_c + _sigmoid(ga) * y_a


def _merge_bwd(name, dm, z, d, y_c, y_a):
    n, hw = z.shape[0], d // 2

    def fn(i, nrow, dm, gc, ga, yc, ya):
        sc, sa = _sigmoid(gc), _sigmoid(ga)
        return dm * yc * sc * (1.0 - sc), dm * ya * sa * (1.0 - sa), dm * sc, dm * sa

    ins = [_t(dm, hw), _t(z, hw, 5), _t(z, hw, 7), _t(y_c, hw), _t(y_a, hw)]
    return _rowwise(name, fn, n, _row_tile(n, 2 * hw), ins, [(BF16, d, hw)] * 4, ncol=2)


def _relu2(f):
    r = jnp.maximum(f, 0.0)
    return r * r


def _relu2_bwd(da, f):
    return da * (2.0 * jnp.maximum(f, 0.0))


def _head_loss(name, x2, pre, pp, target, g):
    n, d = x2.shape

    def fn(i, nrow, x2, pre, pp, t, g):
        gate = _sigmoid(pre)
        x3 = x2 + gate * pp
        r = lax.rsqrt(jnp.mean(x3 * x3, axis=-1, keepdims=True) + EPS)
        xn = x3 * r
        e = xn * g - t
        part = 0.5 * jnp.sum(jnp.mean(e * e, axis=-1, keepdims=True), axis=0, keepdims=True)
        dy = e * (1.0 / d)
        w = dy * g
        dx3 = r * w - x3 * (r * r * r) * jnp.mean(x3 * w, axis=-1, keepdims=True)
        return (dx3, dx3 * gate, dx3 * pp * gate * (1.0 - gate), _sum8(dy * xn),
                jnp.broadcast_to(part, (SUBLANES, LANES)))

    ins = [_t(x2), _t(pre), _t(pp), _t(target), _b(g)]
    return _rowwise(name, fn, n, _row_tile(n, 2 * d), ins, [(F32, d, d), (BF16, d, d), (BF16, d, d)],
                    [(SUBLANES, d, d, True), (SUBLANES, LANES, LANES, True)])


def _group_sum(name, dvh, n_kv):
    n = dvh.shape[0]

    def fn(i, nrow, *a):
        s = a[0]
        for k in range(1, GROUP):
            s = s + a[k]
        return s

    ins = [_t(dvh, HEAD_DIM, k, GROUP) for k in range(GROUP)]
    return _rowwise(name, fn, n, _div(n, 1024, SUBLANES), ins, [(BF16, n_kv * HEAD_DIM, HEAD_DIM)], ncol=n_kv)[0]


MM_ROWS = 1024
MM_COLS = 1152
MM_REDUCE = 2048
MM_REDUCE_BLOCKS = 2304


def _rows_outermost(n_rows, n_cols, row_tile_bytes, weight_tile_bytes):
    rows_outer = n_rows * (row_tile_bytes + n_cols * weight_tile_bytes)
    weight_outer = n_cols * (weight_tile_bytes + n_rows * row_tile_bytes)
    return 5 * rows_outer < 4 * weight_outer


def _accumulate(step, n_steps, part, acc, finish):
    if n_steps == 1:
        finish(part)
        return

    @pl.when(step == 0)
    def _():
        acc[...] = part

    @pl.when(jnp.logical_and(step > 0, step < n_steps - 1))
    def _():
        acc[...] += part

    @pl.when(step == n_steps - 1)
    def _():
        finish(acc[...] + part)


MXU_WIDTH = 256


def _paired(nb, ns):
    return ns % MXU_WIDTH != 0 and nb % 2 == 0 and ns <= MM_COLS


def _mm_nn(name, a, w3, out_dtype, residual=None, activation=None, beside=(), col_tile=None):
    m, k = a.shape
    nb, _, ns = w3.shape
    jb = 2 if _paired(nb, ns) else 1
    tm = _div(m, MM_ROWS // max(jb, 1 + len(beside) // 2), SUBLANES)
    tn_w, tk = (ns if jb > 1 else _div(ns, col_tile or MM_COLS, LANES)), _div(k, MM_REDUCE, LANES)
    tn = jb * tn_w
    per, nk = ns // tn_w, k // tk
    has_res = residual is not None
    n_out = 1 if activation is None else 2
    first_beside = 3 if has_res else 2

    def body(*refs):
        a_ref, w_ref = refs[:2]
        acc = refs[-1]
        outs = refs[-1 - n_out:-1]
        w = w_ref[0] if jb == 1 else jnp.concatenate([w_ref[j] for j in range(jb)], axis=1)
        part = jnp.dot(a_ref[...].astype(BF16), w.astype(BF16), preferred_element_type=F32)

        def finish(total):
            if has_res:
                total = total + refs[2][...]
            outs[0][...] = total.astype(out_dtype)
            if activation is not None:
                tiles = [r[...] for r in refs[first_beside:first_beside + len(beside)]]
                outs[1][...] = activation(total, *tiles).astype(BF16)

        _accumulate(pl.program_id(2), nk, part, acc, finish)

    n_rows, n_cols = m // tm, (nb // jb) * per
    rows_outer = nk > 1 or _rows_outermost(n_rows, n_cols, tm * tk * a.dtype.itemsize,
                                           jb * tk * tn_w * w3.dtype.itemsize)

    def at(f):
        return f if rows_outer else (lambda n, i, kk: f(i, n, kk))

    in_specs = [pl.BlockSpec((tm, tk), at(lambda i, n, kk: (i, kk))),
                pl.BlockSpec((jb, tk, tn_w), at(lambda i, n, kk: (n // per, kk, n % per)))]
    args = [a, w3]
    if has_res:
        in_specs.append(pl.BlockSpec((tm, tn), at(lambda i, n, kk: (i, n))))
        args.append(residual)
    for arr, col0 in beside:
        in_specs.append(pl.BlockSpec((tm, tn), at(lambda i, n, kk, col0=col0: (i, col0 + n))))
        args.append(arr)
    out_dtypes = [out_dtype] + ([BF16] if activation is not None else [])
    res = pl.pallas_call(
        body, name=name, grid=(n_rows, n_cols, nk) if rows_outer else (n_cols, n_rows, nk), in_specs=in_specs,
        out_specs=[pl.BlockSpec((tm, tn), at(lambda i, n, kk: (i, n))) for _ in out_dtypes],
        out_shape=[jax.ShapeDtypeStruct((m, nb * ns), dt) for dt in out_dtypes],
        scratch_shapes=[pltpu.VMEM((tm, tn) if nk > 1 else (SUBLANES, LANES), F32)],
        compiler_params=_params(("parallel", "parallel", "arbitrary")))(*args)
    return res[0] if activation is None else res


def _mm_nt(name, dc, w3, out_dtype, after=(), through=None, head_dots=None):
    m = dc.shape[0]
    nb, k, ns = w3.shape
    tm, to = _div(m, MM_ROWS, SUBLANES), _div(k, MM_ROWS, LANES)
    if ns > MM_REDUCE_BLOCKS // 2:
        jb, tr = 1, _div(ns, MM_REDUCE, LANES)
    else:
        jb, tr = max(j for j in range(1, nb + 1) if nb % j == 0 and j * ns <= MM_REDUCE_BLOCKS), ns
    per = ns // tr
    nr = (nb // jb) * per
    y = through[1] if through is not None else head_dots
    n_out = 1 if head_dots is None else 2
    heads = to // HEAD_DIM

    def body(dc_ref, w_ref, *rest):
        acc = rest[-1]
        outs = rest[-1 - n_out:-1]
        w = w_ref[0] if jb == 1 else jnp.concatenate([w_ref[j] for j in range(jb)], axis=1)
        part = lax.dot_general(dc_ref[...].astype(BF16), w.astype(BF16), (((1,), (1,)), ((), ())),
                               preferred_element_type=F32)

        def finish(total):
            if through is not None:
                total = through[0](total, rest[0][...])
            result = total.astype(out_dtype)
            outs[0][...] = result
            if head_dots is not None:
                prod = result.astype(F32) * rest[0][...]
                for h in range(heads):
                    row = jnp.sum(jnp.transpose(prod[:, h * HEAD_DIM:(h + 1) * HEAD_DIM]), axis=0, keepdims=True)
                    outs[1][h] = jnp.broadcast_to(row, (SUBLANES, tm))

        _accumulate(pl.program_id(2), nr, part, acc, finish)

    rows_outer = nr > 1 or _rows_outermost(m // tm, k // to, tm * jb * tr * dc.dtype.itemsize,
                                           jb * to * tr * w3.dtype.itemsize)

    def at(f):
        return f if rows_outer else (lambda o, i, r: f(i, o, r))

    in_specs = [pl.BlockSpec((tm, jb * tr), at(lambda i, o, r: (i, r))),
                pl.BlockSpec((jb, to, tr), at(lambda i, o, r: (r // per, o, r % per)))]
    args = [dc, w3]
    if y is not None:
        in_specs.append(pl.BlockSpec((tm, to), at(lambda i, o, r: (i, o))))
        args.append(y)
    for token in after:
        in_specs.append(pl.BlockSpec(memory_space=pl.ANY))
        args.append(token)
    out_specs = [pl.BlockSpec((tm, to), at(lambda i, o, r: (i, o)))]
    out_shape = [jax.ShapeDtypeStruct((m, k), out_dtype)]
    if head_dots is not None:
        out_specs.append(pl.BlockSpec((heads, SUBLANES, tm), at(lambda i, o, r: (o, 0, i))))
        out_shape.append(jax.ShapeDtypeStruct((k // HEAD_DIM, SUBLANES, m), F32))
    res = pl.pallas_call(
        body, name=name, grid=(m // tm, k // to, nr) if rows_outer else (k // to, m // tm, nr), in_specs=in_specs,
        out_specs=out_specs, out_shape=out_shape,
        scratch_shapes=[pltpu.VMEM((tm, to) if nr > 1 else (SUBLANES, LANES), F32)],
        compiler_params=_params(("parallel", "parallel", "arbitrary")))(*args)
    return res[0] if head_dots is None else res


def _mm_tn(name, a, dc, nb):
    m, k = a.shape
    ns = dc.shape[1] // nb
    jb = 2 if _paired(nb, ns) else 1
    tr, to = _div(m, MM_REDUCE, LANES), _div(k, MM_ROWS // jb, LANES)
    tn_w = ns if jb > 1 else _div(ns, MM_COLS, LANES)
    tn = jb * tn_w
    per, nr = ns // tn_w, m // tr

    def body(a_ref, dc_ref, o_ref, acc):
        part = lax.dot_general(a_ref[...].astype(BF16), dc_ref[...].astype(BF16), (((0,), (0,)), ((), ())),
                               preferred_element_type=F32)

        def finish(total):
            for j in range(jb):
                o_ref[j] = total[:, j * tn_w:(j + 1) * tn_w].astype(BF16)

        _accumulate(pl.program_id(2), nr, part, acc, finish)

    return pl.pallas_call(
        body, name=name, grid=(k // to, (nb // jb) * per, nr),
        in_specs=[pl.BlockSpec((tr, to), lambda o, n, r: (r, o)),
                  pl.BlockSpec((tr, tn), lambda o, n, r: (r, n))],
        out_specs=pl.BlockSpec((jb, to, tn_w), lambda o, n, r: (n // per, o, n % per)),
        out_shape=jax.ShapeDtypeStruct((nb, k, ns), BF16),
        scratch_shapes=[pltpu.VMEM((to, tn) if nr > 1 else (SUBLANES, LANES), F32)],
        compiler_params=_params(("parallel", "parallel", "arbitrary")))(a, dc)


QK_SCALE = HEAD_DIM ** -0.5
LOG2_E = 1.4426950408889634
LN_2 = 0.6931471805599453


def _columns(row, n):
    return jnp.transpose(jnp.broadcast_to(row, (LANES, n)))


FLASH_TILE = 1024
ONES_ROWS = 16


def _flash_fwd(qr, kr, vt, n_heads):
    t = qr.shape[0]
    tq, tk = _div(t, FLASH_TILE, LANES), _div(t, 2 * FLASH_TILE, LANES)
    nk = t // tk
    rows = vt.shape[0] // (n_heads // GROUP)

    def body(q_ref, k_ref, vt_ref, o_ref, lse_ref, m_scr, acc_scr):
        kb = pl.program_id(2)

        @pl.when(kb == 0)
        def _():
            m_scr[...] = jnp.full_like(m_scr, -1e30)
            acc_scr[...] = jnp.zeros_like(acc_scr)

        s_t = lax.dot_general(k_ref[...], q_ref[...], (((1,), (1,)), ((), ())), preferred_element_type=F32)
        m_prev = m_scr[...]
        m_new = jnp.maximum(m_prev, jnp.max(s_t, axis=0, keepdims=True))
        alpha = jnp.exp2(m_prev - m_new)
        p_t = jnp.exp2(s_t - m_new).astype(BF16)
        acc_scr[...] = alpha * acc_scr[...] + jnp.dot(vt_ref[...], p_t, preferred_element_type=F32)
        m_scr[...] = m_new

        @pl.when(kb == nk - 1)
        def _():
            l = acc_scr[HEAD_DIM:HEAD_DIM + 1, :]
            o_ref[...] = jnp.transpose(acc_scr[0:HEAD_DIM, :] / l)
            lse_ref[...] = jnp.broadcast_to(m_scr[...] + jnp.log(l) * LOG2_E, (SUBLANES, tq))

    return pl.pallas_call(
        body, name="flash_fwd", grid=(n_heads, t // tq, nk),
        in_specs=[pl.BlockSpec((tq, HEAD_DIM), lambda h, qi, ki: (qi, h)),
                  pl.BlockSpec((tk, HEAD_DIM), lambda h, qi, ki: (ki, h // GROUP)),
                  pl.BlockSpec((rows, tk), lambda h, qi, ki: (h // GROUP, ki))],
        out_specs=[pl.BlockSpec((tq, HEAD_DIM), lambda h, qi, ki: (qi, h)),
                   pl.BlockSpec((None, SUBLANES, tq), lambda h, qi, ki: (h, 0, qi))],
        out_shape=[jax.ShapeDtypeStruct((t, n_heads * HEAD_DIM), F32),
                   jax.ShapeDtypeStruct((n_heads, SUBLANES, t), F32)],
        scratch_shapes=[pltpu.VMEM((1, tq), F32), pltpu.VMEM((rows, tq), F32)],
        compiler_params=_params(("parallel", "parallel", "arbitrary")))(qr, kr, vt)


def _flash_bwd(qr, kr, kt, z, v_col0, do, lse, delta, n_heads):
    t = qr.shape[0]
    tq, tk = _div(t, 2 * FLASH_TILE, LANES), _div(t, FLASH_TILE, LANES)
    nq = t // tq

    def body(q_ref, k_ref, kt_ref, v_ref, do_ref, lse_ref, dl_ref, dq_ref, dk_ref, dv_ref, dk_acc, dv_acc):
        kb, qb = pl.program_id(1), pl.program_id(2)
        q, k, do_ = q_ref[...], k_ref[...], do_ref[...]
        v = v_ref[...].astype(BF16)
        s_t = lax.dot_general(k, q, (((1,), (1,)), ((), ())), preferred_element_type=F32)
        p_t = jnp.exp2(s_t - lse_ref[0:1, :])
        dp_t = lax.dot_general(v, do_, (((1,), (1,)), ((), ())), preferred_element_type=F32)
        ds_t = (p_t * (dp_t - dl_ref[0:1, :])).astype(BF16)
        dv_c = jnp.dot(p_t.astype(BF16), do_, preferred_element_type=F32)
        dk_c = jnp.dot(ds_t, q, preferred_element_type=F32) * LN_2
        dq_c = jnp.dot(kt_ref[...], ds_t, preferred_element_type=F32) * QK_SCALE

        @pl.when(qb == 0)
        def _():
            dk_acc[...] = dk_c
            dv_acc[...] = dv_c

        @pl.when(qb > 0)
        def _():
            dk_acc[...] += dk_c
            dv_acc[...] += dv_c

        @pl.when(qb == nq - 1)
        def _():
            dk_ref[...] = dk_acc[...]
            dv_ref[...] = dv_acc[...]

        cols = pl.ds(pl.multiple_of(qb * tq, tq), tq)

        @pl.when(kb == 0)
        def _():
            dq_ref[:, cols] = dq_c

        @pl.when(kb > 0)
        def _():
            dq_ref[:, cols] += dq_c

    wide = jax.ShapeDtypeStruct((t, n_heads * HEAD_DIM), F32)
    return pl.pallas_call(
        body, name="flash_bwd", grid=(n_heads, t // tk, nq),
        in_specs=[pl.BlockSpec((tq, HEAD_DIM), lambda h, kb, qb: (qb, h)),
                  pl.BlockSpec((tk, HEAD_DIM), lambda h, kb, qb: (kb, h // GROUP)),
                  pl.BlockSpec((HEAD_DIM, tk), lambda h, kb, qb: (h // GROUP, kb)),
                  pl.BlockSpec((tk, HEAD_DIM), lambda h, kb, qb: (kb, v_col0 + h // GROUP)),
                  pl.BlockSpec((tq, HEAD_DIM), lambda h, kb, qb: (qb, h)),
                  pl.BlockSpec((None, SUBLANES, tq), lambda h, kb, qb: (h, 0, qb)),
                  pl.BlockSpec((None, SUBLANES, tq), lambda h, kb, qb: (h, 0, qb))],
        out_specs=[pl.BlockSpec((HEAD_DIM, t), lambda h, kb, qb: (h, 0)),
                   pl.BlockSpec((tk, HEAD_DIM), lambda h, kb, qb: (kb, h)),
                   pl.BlockSpec((tk, HEAD_DIM), lambda h, kb, qb: (kb, h))],
        out_shape=[jax.ShapeDtypeStruct((n_heads * HEAD_DIM, t), F32), wide, wide],
        scratch_shapes=[pltpu.VMEM((tk, HEAD_DIM), F32), pltpu.VMEM((tk, HEAD_DIM), F32)],
        compiler_params=_params(("parallel", "arbitrary", "arbitrary")))(qr, kr, kt, z, do, lse, delta)


def _position():
    x, y, c = lax.axis_index("x"), lax.axis_index("y"), lax.axis_index("c")
    return x, y, c


def _index(x, y, c):
    return 4 * x + 2 * y + c


def _all_gather(name, shard):
    a, b = shard.shape

    def body(x_ref, out_ref, send_sems, recv_sems, local_sem):
        x, y, c = _position()
        me, sibling = (x, y, c), (x, y, 1 - c)
        chips = [(1 - x, y), (x, 1 - y), (1 - x, 1 - y)]

        def block(px, py, pc):
            return out_ref.at[_index(px, py, pc)]

        def copy(k, blk, to, src=None):
            return pltpu.make_async_remote_copy(
                src_ref=block(*blk) if src is None else src, dst_ref=block(*blk),
                send_sem=send_sems.at[k], recv_sem=recv_sems.at[k], device_id=to, device_id_type=MESH)

        mine = pltpu.make_async_copy(x_ref, block(*me), local_sem)
        mine.start()
        first = [copy(0, me, sibling, src=x_ref)]
        first += [copy(1 + j, me, (*chip, c), src=x_ref) for j, chip in enumerate(chips)]
        for cp in first:
            cp.start()
        passed = [copy(4 + j, (*chip, c), sibling) for j, chip in enumerate(chips)]
        for j, chip in enumerate(chips):
            copy(1 + j, (*chip, c), me).wait_recv()
            passed[j].start()
        copy(0, sibling, me).wait_recv()
        for j, chip in enumerate(chips):
            copy(4 + j, (*chip, 1 - c), me).wait_recv()
        for cp in first + passed:
            cp.wait_send()
        mine.wait()

    return pl.pallas_call(
        body, name=name, out_shape=jax.ShapeDtypeStruct((N_DEV, a, b), shard.dtype),
        in_specs=[pl.BlockSpec(memory_space=pltpu.HBM)], out_specs=pl.BlockSpec(memory_space=pltpu.HBM),
        scratch_shapes=[pltpu.SemaphoreType.DMA((7,)), pltpu.SemaphoreType.DMA((7,)), pltpu.SemaphoreType.DMA],
    )(shard)


HBM_SPEC = pl.BlockSpec(memory_space=pltpu.HBM)
SEM_SPEC = pl.BlockSpec(memory_space=pltpu.SEMAPHORE)
N_PEERS = N_DEV - 1
PEER_FLIPS = [(fx, fy, fc) for fx in (0, 1) for fy in (0, 1) for fc in (0, 1)][1:]


def _peers(x, y, c):
    return [((1 - x) if fx else x, (1 - y) if fy else y, (1 - c) if fc else c) for fx, fy, fc in PEER_FLIPS]


def _exchange_copies(g_ref, land_ref, send_sems, recv_sems):
    x, y, c = _position()
    me = _index(x, y, c)
    peers = _peers(x, y, c)
    blocked = len(g_ref.shape) == 3

    def copy(k, src_block, dst_block):
        return pltpu.make_async_remote_copy(
            src_ref=g_ref.at[src_block] if blocked else g_ref, dst_ref=land_ref.at[dst_block],
            send_sem=send_sems.at[k], recv_sem=recv_sems.at[k], device_id=peers[k], device_id_type=MESH)

    sent = [copy(k, _index(*peer), me) for k, peer in enumerate(peers)]
    arriving = [copy(k, me, _index(*peer)) for k, peer in enumerate(peers)]
    own = pltpu.make_async_copy(g_ref.at[me] if blocked else g_ref, land_ref.at[me], send_sems.at[N_PEERS])
    return sent, arriving, own


def _exchange_start(name, g3, after=()):
    def body(g_ref, land_ref, *rest):
        send_sems, recv_sems, _, _, token = rest[len(after):]
        sent, _, own = _exchange_copies(g_ref, land_ref, send_sems, recv_sems)
        for cp in sent:
            cp.start()
        own.start()
        token[...] = jnp.zeros_like(token)

    land_shape = g3.shape if g3.ndim == 3 else (N_DEV,) + g3.shape
    land = lax.empty(land_shape, g3.dtype)
    send_sems, recv_sems, g_thru, land_thru, token = pl.pallas_call(
        body, name=name,
        out_shape=(pltpu.SemaphoreType.DMA((N_PEERS + 1,)), pltpu.SemaphoreType.DMA((N_PEERS,)),
                   pltpu.HBM(g3.shape, g3.dtype), pltpu.HBM(land_shape, g3.dtype),
                   jax.ShapeDtypeStruct((SUBLANES, LANES), F32)),
        in_specs=(HBM_SPEC, HBM_SPEC) + (pl.BlockSpec(memory_space=pl.ANY),) * len(after),
        out_specs=(SEM_SPEC, SEM_SPEC, HBM_SPEC, HBM_SPEC, pl.BlockSpec(memory_space=pltpu.VMEM)),
        input_output_aliases={0: 2, 1: 3},
        compiler_params=pltpu.CompilerParams(has_side_effects=pltpu.SideEffectType.DATAFLOW_SIDE_EFFECTING),
    )(pltpu.with_memory_space_constraint(g3, pltpu.HBM), pltpu.with_memory_space_constraint(land, pltpu.HBM), *after)
    return (send_sems, recv_sems, g_thru, land_thru), token


def _exchange_wait(name, handle, after):
    send_sems, recv_sems, g_thru, land_thru = handle

    def body(g_ref, land_ref, send_sems, recv_sems, after_ref, g_dead, land_out):
        sent, arriving, own = _exchange_copies(g_ref, land_ref, send_sems, recv_sems)
        for cp in sent:
            cp.wait_send()
        for cp in arriving:
            cp.wait_recv()
        own.wait()

    return pl.pallas_call(
        body, name=name,
        out_shape=(pltpu.HBM(g_thru.shape, g_thru.dtype), pltpu.HBM(land_thru.shape, land_thru.dtype)),
        in_specs=(HBM_SPEC, HBM_SPEC, SEM_SPEC, SEM_SPEC, pl.BlockSpec(memory_space=pl.ANY)),
        out_specs=(HBM_SPEC, HBM_SPEC), input_output_aliases={0: 0, 1: 1},
        compiler_params=pltpu.CompilerParams(has_side_effects=pltpu.SideEffectType.DATAFLOW_SIDE_EFFECTING),
    )(g_thru, land_thru, send_sems, recv_sems, after)[1]


def _adamw(w, g, m, v):
    m = ADAM_B1 * m + (1.0 - ADAM_B1) * g
    v = ADAM_B2 * v + (1.0 - ADAM_B2) * (g * g)
    m_hat = m / (1.0 - ADAM_B1 ** ADAM_STEP)
    v_hat = v / (1.0 - ADAM_B2 ** ADAM_STEP)
    delta = -ADAM_LR * (m_hat / (jnp.sqrt(v_hat) + ADAM_EPS) + ADAM_WD * w)
    return delta, m, v


def _adamw_shard(name, land, w, m, v):
    _, a, b = land.shape
    tm = _div(a, max(16, ROW_TILE_ELEMS // (2 * b)), 16)
    per = a // tm

    def fn(i, nrow, *t):
        g = t[0].astype(F32)
        for s in range(1, N_DEV):
            g = g + t[s].astype(F32)
        w, m, v = t[N_DEV:]
        return (g,) + _adamw(w, g, m, v)

    land2 = land.reshape(N_DEV * a, b)
    ins = [_t(land2, row0=s * per) for s in range(N_DEV)] + [_t(w), _t(m), _t(v)]
    return _rowwise(name, fn, a, tm, ins, [(F32, b, b)] * 4)


def _adamw_small(name, parts, w, m, v):
    rows, n = w.shape

    def body(p_ref, w_ref, m_ref, v_ref, g_out, d_out, m_out, v_out):
        g = p_ref[0]
        for s in range(1, parts.shape[0]):
            g = g + p_ref[s]
        g = jnp.sum(g, axis=0, keepdims=True) if rows == 1 else g[0:rows, :]
        delta, m_new, v_new = _adamw(w_ref[...], g, m_ref[...], v_ref[...])
        g_out[...] = g
        d_out[...] = delta
        m_out[...] = m_new
        v_out[...] = v_new

    return pl.pallas_call(body, name=name, out_shape=[jax.ShapeDtypeStruct((rows, n), F32)] * 4)(parts, w, m, v)


def _rope_tables(t):
    pos = jnp.arange(t, dtype=jnp.int32)
    half = HEAD_DIM // 4
    inv_freq = ROPE_THETA ** (-jnp.arange(0, 2 * half, 2, dtype=F32) / (2 * half))
    ang_r = (pos // GRID_W).astype(F32)[:, None] * inv_freq[None, :]
    ang_c = (pos % GRID_W).astype(F32)[:, None] * inv_freq[None, :]
    cos = jnp.concatenate([jnp.cos(ang_r)] * 2 + [jnp.cos(ang_c)] * 2, axis=-1)
    sin = jnp.concatenate([-jnp.sin(ang_r), jnp.sin(ang_r), -jnp.sin(ang_c), jnp.sin(ang_c)], axis=-1)
    return cos, sin


def _gather_weight(name, w, cols):
    g = _all_gather(name, w[0].astype(BF16))
    return g if cols else g.reshape(1, N_DEV * g.shape[1], g.shape[2])


def kernel(x, p, norm_mix, w_in, w_dw, conv_ln_g, conv_ln_b, w_conv_proj, q_norm, k_norm, w_attn_proj, w_out, norm_ffn, w_ff1, w_ff2, norm_ple, w_ple_gate, w_ple_proj, norm_final, loss_target, m_norm_mix, m_w_in, m_w_dw, m_conv_ln_g, m_conv_ln_b, m_w_conv_proj, m_q_norm, m_k_norm, m_w_attn_proj, m_w_out, m_norm_ffn, m_w_ff1, m_w_ff2, m_norm_ple, m_w_ple_gate, m_w_ple_proj, m_norm_final, v_norm_mix, v_w_in, v_w_dw, v_conv_ln_g, v_conv_ln_b, v_w_conv_proj, v_q_norm, v_k_norm, v_w_attn_proj, v_w_out, v_norm_ffn, v_w_ff1, v_w_ff2, v_norm_ple, v_w_ple_gate, v_w_ple_proj, v_norm_final):
    t, d = x.shape[1], x.shape[2]
    cw = d // 2
    n_heads = d // HEAD_DIM
    n_kv = n_heads // GROUP
    col_q, col_k, col_v = d // HEAD_DIM, 2 * d // HEAD_DIM, (2 * d + d // 4) // HEAD_DIM
    x0, pe, tgt = x[0], p[0, 0], loss_target[0]
    g_final = norm_final.reshape(1, d)
    me = _index(*_position())

    wg_in = _gather_weight("ag_w_in", w_in, True)
    dw_pad = jnp.pad(w_dw[0], ((0, CONV_TAPS_PAD - CONV_KERNEL), (0, 0)))
    dw_all = _all_gather("ag_w_dw", dw_pad)
    dw_full = dw_all.transpose(1, 0, 2).reshape(CONV_TAPS_PAD, cw)
    cos, sin = _rope_tables(t)

    def fetch(name, w, prev):
        return _exchange_start("ag_start_" + name, w[0].astype(BF16), after=(wg_in, dw_all) if prev is None else (prev,))

    def arrived(name, handle, cols, after):
        g = _exchange_wait("ag_wait_" + name, handle, after)
        return g if cols else g.reshape(1, N_DEV * g.shape[1], g.shape[2])

    on_cp, tok = fetch("w_conv_proj", w_conv_proj, None)
    on_ap, tok = fetch("w_attn_proj", w_attn_proj, tok)
    on_out, tok = fetch("w_out", w_out, tok)
    on_ff1, tok = fetch("w_ff1", w_ff1, tok)
    on_ff2, tok = fetch("w_ff2", w_ff2, tok)
    on_pg, tok = fetch("w_ple_gate", w_ple_gate, tok)
    on_pp, tok = fetch("w_ple_proj", w_ple_proj, tok)

    h1 = _rms_fwd("rms_mix", x0, norm_mix, after=(tok,))
    z = _mm_nn("mm_in", h1, wg_in, F32)
    conv, s_c = _conv_fwd("conv_fwd", z, cw, dw_full, conv_ln_g, conv_ln_b)
    wg_cp = arrived("w_conv_proj", on_cp, True, s_c)
    y_c = _mm_nn("mm_conv_proj", s_c, wg_cp, F32)
    qr = _head_norm_rope("q_prep", z, col_q, n_heads, q_norm, cos, sin, out_scale=QK_SCALE * LOG2_E,
                         heads_per_step=GROUP)
    kr = _head_norm_rope("k_prep", z, col_k, n_kv, k_norm, cos, sin)
    vt = z[:, col_v * HEAD_DIM:(col_v + n_kv) * HEAD_DIM].T.astype(BF16).reshape(n_kv, HEAD_DIM, t)
    vt = jnp.concatenate([vt, jnp.ones((n_kv, ONES_ROWS, t), BF16)], axis=1).reshape(n_kv * (HEAD_DIM + ONES_ROWS), t)
    o, lse = _flash_fwd(qr, kr, vt, n_heads)
    wg_ap = arrived("w_attn_proj", on_ap, False, o)
    y_a, merged = _mm_nn("mm_attn_proj", o, wg_ap, F32, activation=_gated_merge,
                         beside=[(z, 5), (z, 7), (y_c, 0)], col_tile=d // 2)
    wg_out = arrived("w_out", on_out, False, merged)
    x1 = _mm_nn("mm_out", merged, wg_out, F32, residual=x0)
    h2 = _rms_fwd("rms_ffn", x1, norm_ffn)
    wg_ff1 = arrived("w_ff1", on_ff1, True, h2)
    f, act = _mm_nn("mm_ff1", h2, wg_ff1, F32, activation=_relu2)
    wg_ff2 = arrived("w_ff2", on_ff2, False, act)
    x2 = _mm_nn("mm_ff2", act, wg_ff2, F32, residual=x1)
    hp = _rms_fwd("rms_ple", x2, norm_ple)
    wg_pg = arrived("w_ple_gate", on_pg, False, hp)
    pre = _mm_nn("mm_ple_gate", hp, wg_pg, F32)
    wg_pp = arrived("w_ple_proj", on_pp, True, pre)
    pp = _mm_nn("mm_ple_proj", pe, wg_pp, F32)

    dx3, dpp, dpre, dg_final, loss_part = _head_loss("head_loss", x2, pre, pp, tgt, g_final)
    loss = lax.psum(loss_part[0, 0], ("x", "y", "c"))
    def send(name, gw):
        blocks = gw if gw.shape[0] == N_DEV else gw.reshape(N_DEV, gw.shape[1] // N_DEV, gw.shape[2])
        return _exchange_start("rs_start_" + name, blocks)

    sent_pp, tok_pp = send("w_ple_proj", _mm_tn("mm_d_ple_proj", pe, dpp, N_DEV))
    sent_pg, tok = send("w_ple_gate", _mm_tn("mm_d_ple_gate", hp, dpre, 1))
    dhp = _mm_nt("mm_dhp", dpre, wg_pg, F32, after=(tok_pp, tok))
    dx2, dg_ple = _rms_bwd("rms_ple_bwd", x2, norm_ple, dhp, dx3)
    sent_ff2, tok = send("w_ff2", _mm_tn("mm_d_ff2", act, dx2, 1))
    df = _mm_nt("mm_dact", dx2, wg_ff2, BF16, after=(tok,), through=(_relu2_bwd, f))
    sent_ff1, tok = send("w_ff1", _mm_tn("mm_d_ff1", h2, df, N_DEV))
    dh2 = _mm_nt("mm_dh2", df, wg_ff1, F32, after=(tok,))
    dx1, dg_ffn = _rms_bwd("rms_ffn_bwd", x1, norm_ffn, dh2, dx2)
    sent_out, tok = send("w_out", _mm_tn("mm_d_out", merged, dx1, 1))
    dmerged = _mm_nt("mm_dmerged", dx1, wg_out, F32, after=(tok,))
    dgc, dga, dyc, dya = _merge_bwd("merge_bwd", dmerged, z, d, y_c, y_a)
    sent_cp, tok = send("w_conv_proj", _mm_tn("mm_d_conv_proj", s_c, dyc, N_DEV))
    ds_c = _mm_nt("mm_ds_c", dyc, wg_cp, F32, after=(tok,))
    dconv, dg_ln, db_ln = _conv_ln_bwd("conv_ln_bwd", ds_c, conv, conv_ln_g, conv_ln_b)
    dca, dcb, dg_dw = _conv_glu_bwd("conv_glu_bwd", dconv, z, cw, dw_full)
    sent_ap, tok = send("w_attn_proj", _mm_tn("mm_d_attn_proj", o, dya, 1))
    do, delta = _mm_nt("mm_do", dya, wg_ap, BF16, after=(tok,), head_dots=o)
    dqt, dkh, dvh = _flash_bwd(qr, kr, kr.T, z, col_v, do, lse, delta, n_heads)
    dq, dg_q = _head_norm_rope_bwd("q_prep_bwd", [(dqt, 0, 1)], z, col_q, n_heads, q_norm, cos, sin,
                                   heads_per_step=GROUP, transposed=True)
    dk, dg_k = _head_norm_rope_bwd("k_prep_bwd", [(dkh, g, GROUP) for g in range(GROUP)], z, col_k, n_kv, k_norm,
                                   cos, sin)
    dv = _group_sum("dv_sum", dvh, n_kv)
    dz = jnp.concatenate([dca, dcb, dq, dk, dv, dgc, dga], axis=1)
    sent_in, tok = send("w_in", _mm_tn("mm_d_in", h1, dz, N_DEV))
    dh1 = _mm_nt("mm_dh1", dz, wg_in, F32, after=(tok,))
    grad_x, dg_mix = _rms_bwd("rms_mix_bwd", x0, norm_mix, dh1, dx1)

    def pad(a):
        return jnp.pad(a, ((0, 0), (0, d - a.shape[1])))

    small = [dg_mix, dg_ffn, dg_ple, dg_final, dg_ln, db_ln, dg_q, dg_k, dg_dw]
    packed = _all_gather("ag_small", jnp.concatenate([pad(a) for a in small], axis=0))
    parts, row = [], 0
    for a in small:
        parts.append(packed[:, row:row + a.shape[0], :a.shape[1]])
        row += a.shape[0]
    p_mix, p_ffn, p_ple, p_final, p_lng, p_lnb, p_q, p_k, p_dw = parts
    p_dw = lax.dynamic_slice_in_dim(p_dw, me * (cw // N_DEV), cw // N_DEV, axis=2)

    def little(name, part, w, m, v):
        rows, n = (1 if w.ndim < 3 else w.shape[1]), w.shape[-1]
        padded = rows if rows == 1 else part.shape[1]

        def two_d(a):
            return jnp.pad(a.reshape(rows, n), ((0, padded - rows), (0, 0)))

        res = _adamw_small("adamw_" + name, part, two_d(w), two_d(m), two_d(v))
        return [r[:rows].reshape(w.shape) for r in res]

    done = {}
    after = packed
    for name, sent, w, m, v in [
            ("w_ple_proj", sent_pp, w_ple_proj, m_w_ple_proj, v_w_ple_proj),
            ("w_ple_gate", sent_pg, w_ple_gate, m_w_ple_gate, v_w_ple_gate),
            ("w_ff2", sent_ff2, w_ff2, m_w_ff2, v_w_ff2),
            ("w_ff1", sent_ff1, w_ff1, m_w_ff1, v_w_ff1),
            ("w_out", sent_out, w_out, m_w_out, v_w_out),
            ("w_conv_proj", sent_cp, w_conv_proj, m_w_conv_proj, v_w_conv_proj),
            ("w_attn_proj", sent_ap, w_attn_proj, m_w_attn_proj, v_w_attn_proj),
            ("w_in", sent_in, w_in, m_w_in, v_w_in)]:
        land = _exchange_wait("rs_wait_" + name, sent, after)
        res = _adamw_shard("adamw_" + name, land, w[0], m[0], v[0])
        after = res[0]
        done[name] = [r.reshape(w.shape) for r in res]

    results = [
        little("norm_mix", p_mix, norm_mix, m_norm_mix, v_norm_mix),
        done["w_in"],
        little("w_dw", p_dw, w_dw, m_w_dw, v_w_dw),
        little("conv_ln_g", p_lng, conv_ln_g, m_conv_ln_g, v_conv_ln_g),
        little("conv_ln_b", p_lnb, conv_ln_b, m_conv_ln_b, v_conv_ln_b),
        done["w_conv_proj"],
        little("q_norm", p_q, q_norm, m_q_norm, v_q_norm),
        little("k_norm", p_k, k_norm, m_k_norm, v_k_norm),
        done["w_attn_proj"],
        done["w_out"],
        little("norm_ffn", p_ffn, norm_ffn, m_norm_ffn, v_norm_ffn),
        done["w_ff1"],
        done["w_ff2"],
        little("norm_ple", p_ple, norm_ple, m_norm_ple, v_norm_ple),
        done["w_ple_gate"],
        done["w_ple_proj"],
        little("norm_final", p_final, norm_final, m_norm_final, v_norm_final),
    ]
    grads, deltas, new_m, new_v = zip(*results)
    return (loss, grad_x[None], *grads, *deltas, *new_m, *new_v)
```

```python
import functools

import jax
import jax.numpy as jnp
from jax import lax
from jax.experimental import pallas as pl
from jax.experimental.pallas import tpu as pltpu

F32 = jnp.float32
BF16 = jnp.bfloat16

EPS = 1e-6
HEAD_DIM = 128
GROUP = 4
CONV_KERNEL = 31
CONV_HALO = 16
CONV_TAPS_PAD = 32
GRID_W = 64
ROPE_THETA = 10000.0
ADAM_LR, ADAM_B1, ADAM_B2, ADAM_EPS, ADAM_WD, ADAM_STEP = 0.001, 0.9, 0.999, 1e-08, 0.01, 10

N_DEV = 8
MESH = pl.DeviceIdType.MESH
SUBLANES = 8
LANES = 128
VMEM_LIMIT = 48 * 1024 * 1024
ROW_TILE_ELEMS = 512 * 1024


def _div(n, pref, mult=1):
    if n <= pref:
        return n
    for d in range(pref, 0, -1):
        if n % d == 0 and d % mult == 0:
            return d
    return n


def _params(sem):
    return pltpu.CompilerParams(dimension_semantics=sem, vmem_limit_bytes=VMEM_LIMIT)


def _sum8(a):
    tm, w = a.shape
    return a.reshape(tm // SUBLANES, SUBLANES, w).sum(axis=0)


def _sigmoid(x):
    return jax.nn.sigmoid(x)


def _rowwise(name, fn, n_rows, tm, ins, outs, accs=(), ncol=1, refs_to_fn=False, after=()):
    nrow = n_rows // tm
    in_specs, arrays = [], []
    for spec in ins:
        kind, arr, w, c0 = spec[:4]
        if kind == "t":
            cs, r0 = spec[4], spec[5]
            in_specs.append(pl.BlockSpec((tm, w), lambda j, i, c0=c0, cs=cs, r0=r0: (r0 + i, c0 + cs * j)))
        elif kind == "p":
            h = spec[4]
            r = tm // h
            in_specs.append(pl.BlockSpec((h, w), lambda j, i, c0=c0, r=r: (jnp.maximum(i * r - 1, 0), c0 + j)))
        elif kind == "n":
            h = spec[4]
            r = tm // h
            last = n_rows // h - 1
            in_specs.append(
                pl.BlockSpec((h, w), lambda j, i, c0=c0, r=r, last=last: (jnp.minimum((i + 1) * r, last), c0 + j)))
        elif kind == "x":
            in_specs.append(pl.BlockSpec((w, tm), lambda j, i, c0=c0: (c0 + j, i)))
        else:
            cs = spec[4]
            in_specs.append(pl.BlockSpec((arr.shape[0], w), lambda j, i, c0=c0, cs=cs: (0, c0 + cs * j)))
        arrays.append(arr)
    out_shape, out_specs = [], []
    for dtype, total, w in outs:
        out_shape.append(jax.ShapeDtypeStruct((n_rows, total), dtype))
        out_specs.append(pl.BlockSpec((tm, w), lambda j, i: (i, j)))
    for rows, total, w, follow in accs:
        out_shape.append(jax.ShapeDtypeStruct((rows, total), F32))
        out_specs.append(pl.BlockSpec((rows, w), (lambda j, i: (0, j)) if follow else (lambda j, i: (0, 0))))
    n_in, n_out, n_acc = len(ins), len(outs), len(accs)
    for token in after:
        in_specs.append(pl.BlockSpec(memory_space=pl.ANY))
        arrays.append(token)
    out0 = n_in + len(after)

    def body(*refs):
        j, i = pl.program_id(0), pl.program_id(1)
        res = fn(i, nrow, *(refs[:n_in] if refs_to_fn else [r[...] for r in refs[:n_in]]))
        if not isinstance(res, (tuple, list)):
            res = (res,)
        for k in range(n_out):
            refs[out0 + k][...] = res[k].astype(outs[k][0])
        for k in range(n_acc):
            ref, term = refs[out0 + n_out + k], res[n_out + k]
            first = (i == 0) if accs[k][3] else jnp.logical_and(i == 0, j == 0)

            @pl.when(first)
            def _():
                ref[...] = term

            @pl.when(jnp.logical_not(first))
            def _():
                ref[...] += term

    res = pl.pallas_call(
        body, name=name, grid=(ncol, nrow), in_specs=in_specs, out_specs=out_specs, out_shape=out_shape,
        compiler_params=_params(("arbitrary", "arbitrary")))(*arrays)
    return res


def _row_tile(n_rows, width, mult=SUBLANES):
    return _div(n_rows, max(mult, ROW_TILE_ELEMS // width), mult)


def _t(arr, w=None, col0=0, cstride=1, row0=0):
    return ("t", arr, arr.shape[1] if w is None else w, col0, cstride, row0)


def _b(arr, w=None, col0=0, cstride=0):
    return ("b", arr, arr.shape[1] if w is None else w, col0, cstride)


def _rms_fwd(name, x, g, after=()):
    n, d = x.shape

    def fn(i, nrow, x, g):
        r = lax.rsqrt(jnp.mean(x * x, axis=-1, keepdims=True) + EPS)
        return x * r * g

    return _rowwise(name, fn, n, _row_tile(n, d), [_t(x), _b(g)], [(BF16, d, d)], after=after)[0]


def _rms_bwd(name, x, g, dh, dres):
    n, d = x.shape

    def fn(i, nrow, x, g, dh, dres):
        r = lax.rsqrt(jnp.mean(x * x, axis=-1, keepdims=True) + EPS)
        w = dh * g
        dx = dres + r * w - x * (r * r * r) * jnp.mean(x * w, axis=-1, keepdims=True)
        return dx, _sum8(dh * x * r)

    return _rowwise(name, fn, n, _row_tile(n, 2 * d), [_t(x), _b(g), _t(dh), _t(dres)], [(F32, d, d)],
                    [(SUBLANES, d, d, True)])


def _swap_halves(x):
    lane = lax.broadcasted_iota(jnp.int32, x.shape, 1)
    return jnp.where(lane % 64 < 32, pltpu.roll(x, 96, axis=1), pltpu.roll(x, 32, axis=1))


def _head_norm_rope(name, z, col0, n_heads, g, cos, sin, out_scale=None, heads_per_step=1):
    n, hp = z.shape[0], heads_per_step
    assert col0 % hp == 0 and n_heads % hp == 0

    def fn(i, nrow, x, g, cos, sin):
        ys = []
        for h in range(hp):
            xh = x[:, h * HEAD_DIM:(h + 1) * HEAD_DIM]
            r = lax.rsqrt(jnp.mean(xh * xh, axis=-1, keepdims=True) + EPS)
            y = xh * r * g
            y = y * cos + _swap_halves(y) * sin
            ys.append(y if out_scale is None else y * out_scale)
        return ys[0] if hp == 1 else jnp.concatenate(ys, axis=1)

    tm = _div(n, 1024, SUBLANES)
    w = hp * HEAD_DIM
    return _rowwise(name, fn, n, tm, [_t(z, w, col0 // hp), _b(g), _t(cos, cstride=0), _t(sin, cstride=0)],
                    [(BF16, n_heads * HEAD_DIM, w)], ncol=n_heads // hp)[0]


def _head_norm_rope_bwd(name, douts, z, col0, n_heads, g, cos, sin, heads_per_step=1, transposed=False):
    n, hp = z.shape[0], heads_per_step
    n_d = len(douts)
    assert col0 % hp == 0 and n_heads % hp == 0 and (hp == 1 or all(c0 % hp == 0 and cs == 1 for _, c0, cs in douts))
    assert not transposed or n_d == 1

    def fn(i, nrow, *a):
        dy = a[0]
        for k in range(1, n_d):
            dy = dy + a[k]
        x, g, cos, sin = a[n_d:]
        dxs, dg = [], None
        for h in range(hp):
            xh = x[:, h * HEAD_DIM:(h + 1) * HEAD_DIM]
            if transposed:
                dyh = jnp.transpose(dy[h * HEAD_DIM:(h + 1) * HEAD_DIM, :])
            else:
                dyh = dy[:, h * HEAD_DIM:(h + 1) * HEAD_DIM]
            dn = dyh * cos + _swap_halves(dyh * sin)
            r = lax.rsqrt(jnp.mean(xh * xh, axis=-1, keepdims=True) + EPS)
            w = dn * g
            dxs.append(r * w - xh * (r * r * r) * jnp.mean(xh * w, axis=-1, keepdims=True))
            term = _sum8(dn * xh * r)
            dg = term if dg is None else dg + term
        return (dxs[0] if hp == 1 else jnp.concatenate(dxs, axis=1)), dg

    tm = _div(n, 1024, SUBLANES)
    w = hp * HEAD_DIM
    if transposed:
        ins = [("x", douts[0][0], w, douts[0][1] // hp)]
    else:
        ins = [_t(arr, w, c0 // hp, cs) for arr, c0, cs in douts]
    ins += [_t(z, w, col0 // hp), _b(g), _t(cos, cstride=0), _t(sin, cstride=0)]
    return _rowwise(name, fn, n, tm, ins, [(BF16, n_heads * HEAD_DIM, w)],
                    [(SUBLANES, HEAD_DIM, HEAD_DIM, False)], ncol=n_heads // hp)


def _halo(arr, w, col0):
    return [("p", arr, w, col0, CONV_HALO), _t(arr, w, col0), ("n", arr, w, col0, CONV_HALO)]


def _extend(i, nrow, prev, cur, nxt):
    prev = jnp.where(i > 0, prev, 0.0)
    nxt = jnp.where(i < nrow - 1, nxt, 0.0)
    return jnp.concatenate([prev, cur, nxt], axis=0)


def _shifted(ext, offset, tm):
    n = ext.shape[0]
    rolled = ext if offset == 0 else pltpu.roll(ext, (-offset) % n, axis=0)
    return rolled[CONV_HALO:CONV_HALO + tm]


def _conv_fwd(name, z, cw, w_dw, ln_g, ln_b):
    n = z.shape[0]
    tm = _row_tile(n, 4 * cw, CONV_HALO)

    def fn(i, nrow, ap, a, an, bp, b, bn, w, g, beta):
        a = a[...]
        ext = _extend(i, nrow, ap[...] * _sigmoid(bp[...]), a * _sigmoid(b[...]), an[...] * _sigmoid(bn[...]))
        conv = jnp.zeros_like(a)
        for k in range(CONV_KERNEL):
            conv = conv + _shifted(ext, k - CONV_KERNEL // 2, tm) * w[k:k + 1, :]
        xc = conv - jnp.mean(conv, axis=-1, keepdims=True)
        ln = xc * lax.rsqrt(jnp.mean(xc * xc, axis=-1, keepdims=True) + EPS) * g[...] + beta[...]
        return conv, ln * _sigmoid(ln)

    ins = _halo(z, cw, 0) + _halo(z, cw, 1) + [_b(w_dw), _b(ln_g), _b(ln_b)]
    return _rowwise(name, fn, n, tm, ins, [(F32, cw, cw), (BF16, cw, cw)], refs_to_fn=True)


def _conv_ln_bwd(name, ds, conv, ln_g, ln_b):
    n, cw = conv.shape

    def fn(i, nrow, ds, conv, g, beta):
        xc = conv - jnp.mean(conv, axis=-1, keepdims=True)
        rstd = lax.rsqrt(jnp.mean(xc * xc, axis=-1, keepdims=True) + EPS)
        xhat = xc * rstd
        ln = xhat * g + beta
        sg = _sigmoid(ln)
        dln = ds * (sg * (1.0 + ln * (1.0 - sg)))
        dxh = dln * g
        dconv = rstd * (dxh - jnp.mean(dxh, axis=-1, keepdims=True)
                        - xhat * jnp.mean(dxh * xhat, axis=-1, keepdims=True))
        return dconv, _sum8(dln * xhat), _sum8(dln)

    return _rowwise(name, fn, n, _row_tile(n, 2 * cw), [_t(ds), _t(conv), _b(ln_g), _b(ln_b)], [(F32, cw, cw)],
                    [(SUBLANES, cw, cw, True), (SUBLANES, cw, cw, True)])


def _conv_glu_bwd(name, dconv, z, cw, w_dw):
    n = z.shape[0]
    tm = _row_tile(n, 4 * cw, CONV_HALO)

    def fn(i, nrow, dp, d, dn, a, b, w):
        d, a = d[...], a[...]
        dext = _extend(i, nrow, dp[...], d, dn[...])
        sg = _sigmoid(b[...])
        u = a * sg
        tap = lax.broadcasted_iota(jnp.int32, (CONV_TAPS_PAD, cw), 0)
        du = jnp.zeros_like(d)
        dw = jnp.zeros((CONV_TAPS_PAD, cw), F32)
        for k in range(CONV_KERNEL):
            d_k = _shifted(dext, -(k - CONV_KERNEL // 2), tm)
            du = du + d_k * w[k:k + 1, :]
            dw = dw + jnp.where(tap == k, jnp.sum(d_k * u, axis=0, keepdims=True), 0.0)
        return du * sg, du * a * sg * (1.0 - sg), dw

    ins = _halo(dconv, cw, 0) + [_t(z, cw, 0), _t(z, cw, 1), _b(w_dw)]
    return _rowwise(name, fn, n, tm, ins, [(BF16, cw, cw), (BF16, cw, cw)], [(CONV_TAPS_PAD, cw, cw, True)],
                    refs_to_fn=True)


def _gated_merge(y_a, gc, ga, y_c):
    return _sigmoid(gc) * y_c + _sigmoid(ga) * y_a


def _gated_merge_bwd(dm, gc, ga, yc, ya):
    sc, sa = _sigmoid(gc), _sigmoid(ga)
    return dm * yc * sc * (1.0 - sc), dm * ya * sa * (1.0 - sa), dm * sc, dm * sa


def _relu2(f):
    r = jnp.maximum(f, 0.0)
    return r * r


def _relu2_bwd(da, f):
    return da * (2.0 * jnp.maximum(f, 0.0))


def _head_loss(name, x2, pre, pp, target, g):
    n, d = x2.shape

    def fn(i, nrow, x2, pre, pp, t, g):
        gate = _sigmoid(pre)
        x3 = x2 + gate * pp
        r = lax.rsqrt(jnp.mean(x3 * x3, axis=-1, keepdims=True) + EPS)
        xn = x3 * r
        e = xn * g - t
        part = 0.5 * jnp.sum(jnp.mean(e * e, axis=-1, keepdims=True), axis=0, keepdims=True)
        dy = e * (1.0 / d)
        w = dy * g
        dx3 = r * w - x3 * (r * r * r) * jnp.mean(x3 * w, axis=-1, keepdims=True)
        return (dx3, dx3 * gate, dx3 * pp * gate * (1.0 - gate), _sum8(dy * xn),
                jnp.broadcast_to(part, (SUBLANES, LANES)))

    ins = [_t(x2), _t(pre), _t(pp), _t(target), _b(g)]
    return _rowwise(name, fn, n, _row_tile(n, 2 * d), ins, [(F32, d, d), (BF16, d, d), (BF16, d, d)],
                    [(SUBLANES, d, d, True), (SUBLANES, LANES, LANES, True)])


def _group_sum(name, dvh, n_kv):
    n = dvh.shape[0]

    def fn(i, nrow, *a):
        s = a[0]
        for k in range(1, GROUP):
            s = s + a[k]
        return s

    ins = [_t(dvh, HEAD_DIM, k, GROUP) for k in range(GROUP)]
    return _rowwise(name, fn, n, _div(n, 1024, SUBLANES), ins, [(BF16, n_kv * HEAD_DIM, HEAD_DIM)], ncol=n_kv)[0]


MM_ROWS = 1024
MM_COLS = 1152
MM_REDUCE = 2048
MM_REDUCE_BLOCKS = 2304


def _rows_outermost(n_rows, n_cols, row_tile_bytes, weight_tile_bytes):
    rows_outer = n_rows * (row_tile_bytes + n_cols * weight_tile_bytes)
    weight_outer = n_cols * (weight_tile_bytes + n_rows * row_tile_bytes)
    return 5 * rows_outer < 4 * weight_outer


def _accumulate(step, n_steps, part, acc, finish):
    if n_steps == 1:
        finish(part)
        return

    @pl.when(step == 0)
    def _():
        acc[...] = part

    @pl.when(jnp.logical_and(step > 0, step < n_steps - 1))
    def _():
        acc[...] += part

    @pl.when(step == n_steps - 1)
    def _():
        finish(acc[...] + part)


MXU_WIDTH = 256


def _paired(nb, ns):
    return ns % MXU_WIDTH != 0 and nb % 2 == 0 and ns <= MM_COLS


def _mm_nn(name, a, w3, out_dtype, residual=None, activation=None, beside=(), col_tile=None):
    m, k = a.shape
    nb, _, ns = w3.shape
    jb = 2 if _paired(nb, ns) else 1
    tm = _div(m, MM_ROWS // max(jb, 1 + len(beside) // 2), SUBLANES)
    tn_w, tk = (ns if jb > 1 else _div(ns, col_tile or MM_COLS, LANES)), _div(k, MM_REDUCE, LANES)
    tn = jb * tn_w
    per, nk = ns // tn_w, k // tk
    has_res = residual is not None
    n_out = 1 if activation is None else 2
    first_beside = 3 if has_res else 2

    def body(*refs):
        a_ref, w_ref = refs[:2]
        acc = refs[-1]
        outs = refs[-1 - n_out:-1]
        w = w_ref[0] if jb == 1 else jnp.concatenate([w_ref[j] for j in range(jb)], axis=1)
        part = jnp.dot(a_ref[...].astype(BF16), w.astype(BF16), preferred_element_type=F32)

        def finish(total):
            if has_res:
                total = total + refs[2][...]
            outs[0][...] = total.astype(out_dtype)
            if activation is not None:
                tiles = [r[...] for r in refs[first_beside:first_beside + len(beside)]]
                outs[1][...] = activation(total, *tiles).astype(BF16)

        _accumulate(pl.program_id(2), nk, part, acc, finish)

    n_rows, n_cols = m // tm, (nb // jb) * per
    rows_outer = nk > 1 or _rows_outermost(n_rows, n_cols, tm * tk * a.dtype.itemsize,
                                           jb * tk * tn_w * w3.dtype.itemsize)

    def at(f):
        return f if rows_outer else (lambda n, i, kk: f(i, n, kk))

    in_specs = [pl.BlockSpec((tm, tk), at(lambda i, n, kk: (i, kk))),
                pl.BlockSpec((jb, tk, tn_w), at(lambda i, n, kk: (n // per, kk, n % per)))]
    args = [a, w3]
    if has_res:
        in_specs.append(pl.BlockSpec((tm, tn), at(lambda i, n, kk: (i, n))))
        args.append(residual)
    for arr, col0 in beside:
        in_specs.append(pl.BlockSpec((tm, tn), at(lambda i, n, kk, col0=col0: (i, col0 + n))))
        args.append(arr)
    out_dtypes = [out_dtype] + ([BF16] if activation is not None else [])
    res = pl.pallas_call(
        body, name=name, grid=(n_rows, n_cols, nk) if rows_outer else (n_cols, n_rows, nk), in_specs=in_specs,
        out_specs=[pl.BlockSpec((tm, tn), at(lambda i, n, kk: (i, n))) for _ in out_dtypes],
        out_shape=[jax.ShapeDtypeStruct((m, nb * ns), dt) for dt in out_dtypes],
        scratch_shapes=[pltpu.VMEM((tm, tn) if nk > 1 else (SUBLANES, LANES), F32)],
        compiler_params=_params(("parallel", "parallel", "arbitrary")))(*args)
    return res[0] if activation is None else res


def _mm_nt(name, dc, w3, out_dtype, after=(), through=None, beside=(), n_results=1, head_dots=None, col_tile=None):
    m = dc.shape[0]
    nb, k, ns = w3.shape
    tm = _div(m, MM_ROWS // (1 + len(beside) // 3), SUBLANES)
    to = _div(k, col_tile or MM_ROWS, LANES)
    if ns > MM_REDUCE_BLOCKS // 2:
        jb, tr = 1, _div(ns, MM_REDUCE, LANES)
    else:
        jb, tr = max(j for j in range(1, nb + 1) if nb % j == 0 and j * ns <= MM_REDUCE_BLOCKS), ns
    per = ns // tr
    nr = (nb // jb) * per
    tiles_in = list(beside) + ([(head_dots, 0)] if head_dots is not None else [])
    n_out = n_results + (1 if head_dots is not None else 0)
    heads = to // HEAD_DIM

    def body(dc_ref, w_ref, *rest):
        acc = rest[-1]
        outs = rest[-1 - n_out:-1]
        w = w_ref[0] if jb == 1 else jnp.concatenate([w_ref[j] for j in range(jb)], axis=1)
        part = lax.dot_general(dc_ref[...].astype(BF16), w.astype(BF16), (((1,), (1,)), ((), ())),
                               preferred_element_type=F32)

        def finish(total):
            results = (total,) if through is None else through(total, *[r[...] for r in rest[:len(beside)]])
            if not isinstance(results, (tuple, list)):
                results = (results,)
            for out, res in zip(outs, results):
                out[...] = res.astype(out_dtype)
            if head_dots is not None:
                prod = results[0].astype(out_dtype).astype(F32) * rest[len(beside)][...]
                for h in range(heads):
                    row = jnp.sum(jnp.transpose(prod[:, h * HEAD_DIM:(h + 1) * HEAD_DIM]), axis=0, keepdims=True)
                    outs[n_results][h] = jnp.broadcast_to(row, (SUBLANES, tm))

        _accumulate(pl.program_id(2), nr, part, acc, finish)

    rows_outer = nr > 1 or _rows_outermost(m // tm, k // to, tm * jb * tr * dc.dtype.itemsize,
                                           jb * to * tr * w3.dtype.itemsize)

    def at(f):
        return f if rows_outer else (lambda o, i, r: f(i, o, r))

    in_specs = [pl.BlockSpec((tm, jb * tr), at(lambda i, o, r: (i, r))),
                pl.BlockSpec((jb, to, tr), at(lambda i, o, r: (r // per, o, r % per)))]
    args = [dc, w3]
    for arr, col0 in tiles_in:
        in_specs.append(pl.BlockSpec((tm, to), at(lambda i, o, r, col0=col0: (i, col0 + o))))
        args.append(arr)
    for token in after:
        in_specs.append(pl.BlockSpec(memory_space=pl.ANY))
        args.append(token)
    out_specs = [pl.BlockSpec((tm, to), at(lambda i, o, r: (i, o))) for _ in range(n_results)]
    out_shape = [jax.ShapeDtypeStruct((m, k), out_dtype) for _ in range(n_results)]
    if head_dots is not None:
        out_specs.append(pl.BlockSpec((heads, SUBLANES, tm), at(lambda i, o, r: (o, 0, i))))
        out_shape.append(jax.ShapeDtypeStruct((k // HEAD_DIM, SUBLANES, m), F32))
    res = pl.pallas_call(
        body, name=name, grid=(m // tm, k // to, nr) if rows_outer else (k // to, m // tm, nr), in_specs=in_specs,
        out_specs=out_specs, out_shape=out_shape,
        scratch_shapes=[pltpu.VMEM((tm, to) if nr > 1 else (SUBLANES, LANES), F32)],
        compiler_params=_params(("parallel", "parallel", "arbitrary")))(*args)
    return res[0] if n_out == 1 else res


def _mm_tn(name, a, dc, nb):
    m, k = a.shape
    ns = dc.shape[1] // nb
    jb = 2 if _paired(nb, ns) else 1
    tr, to = _div(m, MM_REDUCE, LANES), _div(k, MM_ROWS // jb, LANES)
    tn_w = ns if jb > 1 else _div(ns, MM_COLS, LANES)
    tn = jb * tn_w
    per, nr = ns // tn_w, m // tr

    def body(a_ref, dc_ref, o_ref, acc):
        part = lax.dot_general(a_ref[...].astype(BF16), dc_ref[...].astype(BF16), (((0,), (0,)), ((), ())),
                               preferred_element_type=F32)

        def finish(total):
            for j in range(jb):
                o_ref[j] = total[:, j * tn_w:(j + 1) * tn_w].astype(BF16)

        _accumulate(pl.program_id(2), nr, part, acc, finish)

    return pl.pallas_call(
        body, name=name, grid=(k // to, (nb // jb) * per, nr),
        in_specs=[pl.BlockSpec((tr, to), lambda o, n, r: (r, o)),
                  pl.BlockSpec((tr, tn), lambda o, n, r: (r, n))],
        out_specs=pl.BlockSpec((jb, to, tn_w), lambda o, n, r: (n // per, o, n % per)),
        out_shape=jax.ShapeDtypeStruct((nb, k, ns), BF16),
        scratch_shapes=[pltpu.VMEM((to, tn) if nr > 1 else (SUBLANES, LANES), F32)],
        compiler_params=_params(("parallel", "parallel", "arbitrary")))(a, dc)


QK_SCALE = HEAD_DIM ** -0.5
LOG2_E = 1.4426950408889634
LN_2 = 0.6931471805599453


def _columns(row, n):
    return jnp.transpose(jnp.broadcast_to(row, (LANES, n)))


FLASH_TILE = 1024
ONES_ROWS = 16


def _flash_fwd(qr, kr, vt, n_heads):
    t = qr.shape[0]
    tq, tk = _div(t, FLASH_TILE, LANES), _div(t, 2 * FLASH_TILE, LANES)
    nk = t // tk
    rows = vt.shape[0] // (n_heads // GROUP)

    def body(q_ref, k_ref, vt_ref, o_ref, lse_ref, m_scr, acc_scr):
        kb = pl.program_id(2)

        @pl.when(kb == 0)
        def _():
            m_scr[...] = jnp.full_like(m_scr, -1e30)
            acc_scr[...] = jnp.zeros_like(acc_scr)

        s_t = lax.dot_general(k_ref[...], q_ref[...], (((1,), (1,)), ((), ())), preferred_element_type=F32)
        m_prev = m_scr[...]
        m_new = jnp.maximum(m_prev, jnp.max(s_t, axis=0, keepdims=True))
        alpha = jnp.exp2(m_prev - m_new)
        p_t = jnp.exp2(s_t - m_new).astype(BF16)
        acc_scr[...] = alpha * acc_scr[...] + jnp.dot(vt_ref[...], p_t, preferred_element_type=F32)
        m_scr[...] = m_new

        @pl.when(kb == nk - 1)
        def _():
            l = acc_scr[HEAD_DIM:HEAD_DIM + 1, :]
            o_ref[...] = jnp.transpose(acc_scr[0:HEAD_DIM, :] / l)
            lse_ref[...] = jnp.broadcast_to(m_scr[...] + jnp.log(l) * LOG2_E, (SUBLANES, tq))

    return pl.pallas_call(
        body, name="flash_fwd", grid=(n_heads, t // tq, nk),
        in_specs=[pl.BlockSpec((tq, HEAD_DIM), lambda h, qi, ki: (qi, h)),
                  pl.BlockSpec((tk, HEAD_DIM), lambda h, qi, ki: (ki, h // GROUP)),
                  pl.BlockSpec((rows, tk), lambda h, qi, ki: (h // GROUP, ki))],
        out_specs=[pl.BlockSpec((tq, HEAD_DIM), lambda h, qi, ki: (qi, h)),
                   pl.BlockSpec((None, SUBLANES, tq), lambda h, qi, ki: (h, 0, qi))],
        out_shape=[jax.ShapeDtypeStruct((t, n_heads * HEAD_DIM), F32),
                   jax.ShapeDtypeStruct((n_heads, SUBLANES, t), F32)],
        scratch_shapes=[pltpu.VMEM((1, tq), F32), pltpu.VMEM((rows, tq), F32)],
        compiler_params=_params(("parallel", "parallel", "arbitrary")))(qr, kr, vt)


def _flash_bwd(qr, kr, kt, z, v_col0, do, lse, delta, n_heads):
    t = qr.shape[0]
    tq, tk = _div(t, 2 * FLASH_TILE, LANES), _div(t, FLASH_TILE, LANES)
    nq = t // tq

    def body(q_ref, k_ref, kt_ref, v_ref, do_ref, lse_ref, dl_ref, dq_ref, dk_ref, dv_ref, dk_acc, dv_acc):
        kb, qb = pl.program_id(1), pl.program_id(2)
        q, k, do_ = q_ref[...], k_ref[...], do_ref[...]
        v = v_ref[...].astype(BF16)
        s_t = lax.dot_general(k, q, (((1,), (1,)), ((), ())), preferred_element_type=F32)
        p_t = jnp.exp2(s_t - lse_ref[0:1, :])
        dp_t = lax.dot_general(v, do_, (((1,), (1,)), ((), ())), preferred_element_type=F32)
        ds_t = (p_t * (dp_t - dl_ref[0:1, :])).astype(BF16)
        dv_c = jnp.dot(p_t.astype(BF16), do_, preferred_element_type=F32)
        dk_c = jnp.dot(ds_t, q, preferred_element_type=F32) * LN_2
        dq_c = jnp.dot(kt_ref[...], ds_t, preferred_element_type=F32) * QK_SCALE

        @pl.when(qb == 0)
        def _():
            dk_acc[...] = dk_c
            dv_acc[...] = dv_c

        @pl.when(qb > 0)
        def _():
            dk_acc[...] += dk_c
            dv_acc[...] += dv_c

        @pl.when(qb == nq - 1)
        def _():
            dk_ref[...] = dk_acc[...]
            dv_ref[...] = dv_acc[...]

        cols = pl.ds(pl.multiple_of(qb * tq, tq), tq)

        @pl.when(kb == 0)
        def _():
            dq_ref[:, cols] = dq_c

        @pl.when(kb > 0)
        def _():
            dq_ref[:, cols] += dq_c

    wide = jax.ShapeDtypeStruct((t, n_heads * HEAD_DIM), F32)
    return pl.pallas_call(
        body, name="flash_bwd", grid=(n_heads, t // tk, nq),
        in_specs=[pl.BlockSpec((tq, HEAD_DIM), lambda h, kb, qb: (qb, h)),
                  pl.BlockSpec((tk, HEAD_DIM), lambda h, kb, qb: (kb, h // GROUP)),
                  pl.BlockSpec((HEAD_DIM, tk), lambda h, kb, qb: (h // GROUP, kb)),
                  pl.BlockSpec((tk, HEAD_DIM), lambda h, kb, qb: (kb, v_col0 + h // GROUP)),
                  pl.BlockSpec((tq, HEAD_DIM), lambda h, kb, qb: (qb, h)),
                  pl.BlockSpec((None, SUBLANES, tq), lambda h, kb, qb: (h, 0, qb)),
                  pl.BlockSpec((None, SUBLANES, tq), lambda h, kb, qb: (h, 0, qb))],
        out_specs=[pl.BlockSpec((HEAD_DIM, t), lambda h, kb, qb: (h, 0)),
                   pl.BlockSpec((tk, HEAD_DIM), lambda h, kb, qb: (kb, h)),
                   pl.BlockSpec((tk, HEAD_DIM), lambda h, kb, qb: (kb, h))],
        out_shape=[jax.ShapeDtypeStruct((n_heads * HEAD_DIM, t), F32), wide, wide],
        scratch_shapes=[pltpu.VMEM((tk, HEAD_DIM), F32), pltpu.VMEM((tk, HEAD_DIM), F32)],
        compiler_params=_params(("parallel", "arbitrary", "arbitrary")))(qr, kr, kt, z, do, lse, delta)


def _position():
    x, y, c = lax.axis_index("x"), lax.axis_index("y"), lax.axis_index("c")
    return x, y, c


def _index(x, y, c):
    return 4 * x + 2 * y + c


def _all_gather(name, shard):
    a, b = shard.shape

    def body(x_ref, out_ref, send_sems, recv_sems, local_sem):
        x, y, c = _position()
        me, sibling = (x, y, c), (x, y, 1 - c)
        chips = [(1 - x, y), (x, 1 - y), (1 - x, 1 - y)]

        def block(px, py, pc):
            return out_ref.at[_index(px, py, pc)]

        def copy(k, blk, to, src=None):
            return pltpu.make_async_remote_copy(
                src_ref=block(*blk) if src is None else src, dst_ref=block(*blk),
                send_sem=send_sems.at[k], recv_sem=recv_sems.at[k], device_id=to, device_id_type=MESH)

        mine = pltpu.make_async_copy(x_ref, block(*me), local_sem)
        mine.start()
        first = [copy(0, me, sibling, src=x_ref)]
        first += [copy(1 + j, me, (*chip, c), src=x_ref) for j, chip in enumerate(chips)]
        for cp in first:
            cp.start()
        passed = [copy(4 + j, (*chip, c), sibling) for j, chip in enumerate(chips)]
        for j, chip in enumerate(chips):
            copy(1 + j, (*chip, c), me).wait_recv()
            passed[j].start()
        copy(0, sibling, me).wait_recv()
        for j, chip in enumerate(chips):
            copy(4 + j, (*chip, 1 - c), me).wait_recv()
        for cp in first + passed:
            cp.wait_send()
        mine.wait()

    return pl.pallas_call(
        body, name=name, out_shape=jax.ShapeDtypeStruct((N_DEV, a, b), shard.dtype),
        in_specs=[pl.BlockSpec(memory_space=pltpu.HBM)], out_specs=pl.BlockSpec(memory_space=pltpu.HBM),
        scratch_shapes=[pltpu.SemaphoreType.DMA((7,)), pltpu.SemaphoreType.DMA((7,)), pltpu.SemaphoreType.DMA],
    )(shard)


HBM_SPEC = pl.BlockSpec(memory_space=pltpu.HBM)
SEM_SPEC = pl.BlockSpec(memory_space=pltpu.SEMAPHORE)
N_PEERS = N_DEV - 1
PEER_FLIPS = [(fx, fy, fc) for fx in (0, 1) for fy in (0, 1) for fc in (0, 1)][1:]


def _peers(x, y, c):
    return [((1 - x) if fx else x, (1 - y) if fy else y, (1 - c) if fc else c) for fx, fy, fc in PEER_FLIPS]


def _exchange_copies(g_ref, land_ref, send_sems, recv_sems):
    x, y, c = _position()
    me = _index(x, y, c)
    peers = _peers(x, y, c)
    blocked = len(g_ref.shape) == 3

    def copy(k, src_block, dst_block):
        return pltpu.make_async_remote_copy(
            src_ref=g_ref.at[src_block] if blocked else g_ref, dst_ref=land_ref.at[dst_block],
            send_sem=send_sems.at[k], recv_sem=recv_sems.at[k], device_id=peers[k], device_id_type=MESH)

    sent = [copy(k, _index(*peer), me) for k, peer in enumerate(peers)]
    arriving = [copy(k, me, _index(*peer)) for k, peer in enumerate(peers)]
    own = pltpu.make_async_copy(g_ref.at[me] if blocked else g_ref, land_ref.at[me], send_sems.at[N_PEERS])
    return sent, arriving, own


def _exchange_start(name, g3, after=()):
    def body(g_ref, land_ref, *rest):
        send_sems, recv_sems, _, _, token = rest[len(after):]
        sent, _, own = _exchange_copies(g_ref, land_ref, send_sems, recv_sems)
        for cp in sent:
            cp.start()
        own.start()
        token[...] = jnp.zeros_like(token)

    land_shape = g3.shape if g3.ndim == 3 else (N_DEV,) + g3.shape
    land = lax.empty(land_shape, g3.dtype)
    send_sems, recv_sems, g_thru, land_thru, token = pl.pallas_call(
        body, name=name,
        out_shape=(pltpu.SemaphoreType.DMA((N_PEERS + 1,)), pltpu.SemaphoreType.DMA((N_PEERS,)),
                   pltpu.HBM(g3.shape, g3.dtype), pltpu.HBM(land_shape, g3.dtype),
                   jax.ShapeDtypeStruct((SUBLANES, LANES), F32)),
        in_specs=(HBM_SPEC, HBM_SPEC) + (pl.BlockSpec(memory_space=pl.ANY),) * len(after),
        out_specs=(SEM_SPEC, SEM_SPEC, HBM_SPEC, HBM_SPEC, pl.BlockSpec(memory_space=pltpu.VMEM)),
        input_output_aliases={0: 2, 1: 3},
        compiler_params=pltpu.CompilerParams(has_side_effects=pltpu.SideEffectType.DATAFLOW_SIDE_EFFECTING),
    )(pltpu.with_memory_space_constraint(g3, pltpu.HBM), pltpu.with_memory_space_constraint(land, pltpu.HBM), *after)
    return (send_sems, recv_sems, g_thru, land_thru), token


def _exchange_wait(name, handle, after):
    send_sems, recv_sems, g_thru, land_thru = handle

    def body(g_ref, land_ref, send_sems, recv_sems, after_ref, g_dead, land_out):
        sent, arriving, own = _exchange_copies(g_ref, land_ref, send_sems, recv_sems)
        for cp in sent:
            cp.wait_send()
        for cp in arriving:
            cp.wait_recv()
        own.wait()

    return pl.pallas_call(
        body, name=name,
        out_shape=(pltpu.HBM(g_thru.shape, g_thru.dtype), pltpu.HBM(land_thru.shape, land_thru.dtype)),
        in_specs=(HBM_SPEC, HBM_SPEC, SEM_SPEC, SEM_SPEC, pl.BlockSpec(memory_space=pl.ANY)),
        out_specs=(HBM_SPEC, HBM_SPEC), input_output_aliases={0: 0, 1: 1},
        compiler_params=pltpu.CompilerParams(has_side_effects=pltpu.SideEffectType.DATAFLOW_SIDE_EFFECTING),
    )(g_thru, land_thru, send_sems, recv_sems, after)[1]


def _adamw(w, g, m, v):
    m = ADAM_B1 * m + (1.0 - ADAM_B1) * g
    v = ADAM_B2 * v + (1.0 - ADAM_B2) * (g * g)
    m_hat = m / (1.0 - ADAM_B1 ** ADAM_STEP)
    v_hat = v / (1.0 - ADAM_B2 ** ADAM_STEP)
    delta = -ADAM_LR * (m_hat / (jnp.sqrt(v_hat) + ADAM_EPS) + ADAM_WD * w)
    return delta, m, v


def _adamw_shard(name, land, w, m, v):
    _, a, b = land.shape
    tm = _div(a, max(16, ROW_TILE_ELEMS // (2 * b)), 16)
    per = a // tm

    def fn(i, nrow, *t):
        g = t[0].astype(F32)
        for s in range(1, N_DEV):
            g = g + t[s].astype(F32)
        w, m, v = t[N_DEV:]
        return (g,) + _adamw(w, g, m, v)

    land2 = land.reshape(N_DEV * a, b)
    ins = [_t(land2, row0=s * per) for s in range(N_DEV)] + [_t(w), _t(m), _t(v)]
    return _rowwise(name, fn, a, tm, ins, [(F32, b, b)] * 4)


def _adamw_small(name, parts, w, m, v):
    rows, n = w.shape

    def body(p_ref, w_ref, m_ref, v_ref, g_out, d_out, m_out, v_out):
        g = p_ref[0]
        for s in range(1, parts.shape[0]):
            g = g + p_ref[s]
        g = jnp.sum(g, axis=0, keepdims=True) if rows == 1 else g[0:rows, :]
        delta, m_new, v_new = _adamw(w_ref[...], g, m_ref[...], v_ref[...])
        g_out[...] = g
        d_out[...] = delta
        m_out[...] = m_new
        v_out[...] = v_new

    return pl.pallas_call(body, name=name, out_shape=[jax.ShapeDtypeStruct((rows, n), F32)] * 4)(parts, w, m, v)


def _rope_tables(t):
    pos = jnp.arange(t, dtype=jnp.int32)
    half = HEAD_DIM // 4
    inv_freq = ROPE_THETA ** (-jnp.arange(0, 2 * half, 2, dtype=F32) / (2 * half))
    ang_r = (pos // GRID_W).astype(F32)[:, None] * inv_freq[None, :]
    ang_c = (pos % GRID_W).astype(F32)[:, None] * inv_freq[None, :]
    cos = jnp.concatenate([jnp.cos(ang_r)] * 2 + [jnp.cos(ang_c)] * 2, axis=-1)
    sin = jnp.concatenate([-jnp.sin(ang_r), jnp.sin(ang_r), -jnp.sin(ang_c), jnp.sin(ang_c)], axis=-1)
    return cos, sin


def _gather_weight(name, w, cols):
    g = _all_gather(name, w[0].astype(BF16))
    return g if cols else g.reshape(1, N_DEV * g.shape[1], g.shape[2])


def kernel(x, p, norm_mix, w_in, w_dw, conv_ln_g, conv_ln_b, w_conv_proj, q_norm, k_norm, w_attn_proj, w_out, norm_ffn, w_ff1, w_ff2, norm_ple, w_ple_gate, w_ple_proj, norm_final, loss_target, m_norm_mix, m_w_in, m_w_dw, m_conv_ln_g, m_conv_ln_b, m_w_conv_proj, m_q_norm, m_k_norm, m_w_attn_proj, m_w_out, m_norm_ffn, m_w_ff1, m_w_ff2, m_norm_ple, m_w_ple_gate, m_w_ple_proj, m_norm_final, v_norm_mix, v_w_in, v_w_dw, v_conv_ln_g, v_conv_ln_b, v_w_conv_proj, v_q_norm, v_k_norm, v_w_attn_proj, v_w_out, v_norm_ffn, v_w_ff1, v_w_ff2, v_norm_ple, v_w_ple_gate, v_w_ple_proj, v_norm_final):
    t, d = x.shape[1], x.shape[2]
    cw = d // 2
    n_heads = d // HEAD_DIM
    n_kv = n_heads // GROUP
    col_q, col_k, col_v = d // HEAD_DIM, 2 * d // HEAD_DIM, (2 * d + d // 4) // HEAD_DIM
    x0, pe, tgt = x[0], p[0, 0], loss_target[0]
    g_final = norm_final.reshape(1, d)
    me = _index(*_position())

    wg_in = _gather_weight("ag_w_in", w_in, True)
    dw_pad = jnp.pad(w_dw[0], ((0, CONV_TAPS_PAD - CONV_KERNEL), (0, 0)))
    dw_all = _all_gather("ag_w_dw", dw_pad)
    dw_full = dw_all.transpose(1, 0, 2).reshape(CONV_TAPS_PAD, cw)
    cos, sin = _rope_tables(t)

    def fetch(name, w, prev):
        return _exchange_start("ag_start_" + name, w[0].astype(BF16), after=(wg_in, dw_all) if prev is None else (prev,))

    def arrived(name, handle, cols, after):
        g = _exchange_wait("ag_wait_" + name, handle, after)
        return g if cols else g.reshape(1, N_DEV * g.shape[1], g.shape[2])

    on_cp, tok = fetch("w_conv_proj", w_conv_proj, None)
    on_ap, tok = fetch("w_attn_proj", w_attn_proj, tok)
    on_out, tok = fetch("w_out", w_out, tok)
    on_ff1, tok = fetch("w_ff1", w_ff1, tok)
    on_ff2, tok = fetch("w_ff2", w_ff2, tok)
    on_pg, tok = fetch("w_ple_gate", w_ple_gate, tok)
    on_pp, tok = fetch("w_ple_proj", w_ple_proj, tok)

    h1 = _rms_fwd("rms_mix", x0, norm_mix, after=(tok,))
    z = _mm_nn("mm_in", h1, wg_in, F32)
    conv, s_c = _conv_fwd("conv_fwd", z, cw, dw_full, conv_ln_g, conv_ln_b)
    wg_cp = arrived("w_conv_proj", on_cp, True, s_c)
    y_c = _mm_nn("mm_conv_proj", s_c, wg_cp, F32)
    qr = _head_norm_rope("q_prep", z, col_q, n_heads, q_norm, cos, sin, out_scale=QK_SCALE * LOG2_E,
                         heads_per_step=GROUP)
    kr = _head_norm_rope("k_prep", z, col_k, n_kv, k_norm, cos, sin)
    vt = z[:, col_v * HEAD_DIM:(col_v + n_kv) * HEAD_DIM].T.astype(BF16).reshape(n_kv, HEAD_DIM, t)
    vt = jnp.concatenate([vt, jnp.ones((n_kv, ONES_ROWS, t), BF16)], axis=1).reshape(n_kv * (HEAD_DIM + ONES_ROWS), t)
    o, lse = _flash_fwd(qr, kr, vt, n_heads)
    wg_ap = arrived("w_attn_proj", on_ap, False, o)
    y_a, merged = _mm_nn("mm_attn_proj", o, wg_ap, F32, activation=_gated_merge,
                         beside=[(z, 5), (z, 7), (y_c, 0)], col_tile=d // 2)
    wg_out = arrived("w_out", on_out, False, merged)
    x1 = _mm_nn("mm_out", merged, wg_out, F32, residual=x0)
    h2 = _rms_fwd("rms_ffn", x1, norm_ffn)
    wg_ff1 = arrived("w_ff1", on_ff1, True, h2)
    f, act = _mm_nn("mm_ff1", h2, wg_ff1, F32, activation=_relu2)
    wg_ff2 = arrived("w_ff2", on_ff2, False, act)
    x2 = _mm_nn("mm_ff2", act, wg_ff2, F32, residual=x1)
    hp = _rms_fwd("rms_ple", x2, norm_ple)
    wg_pg = arrived("w_ple_gate", on_pg, False, hp)
    pre = _mm_nn("mm_ple_gate", hp, wg_pg, F32)
    wg_pp = arrived("w_ple_proj", on_pp, True, pre)
    pp = _mm_nn("mm_ple_proj", pe, wg_pp, F32)

    dx3, dpp, dpre, dg_final, loss_part = _head_loss("head_loss", x2, pre, pp, tgt, g_final)
    loss = lax.psum(loss_part[0, 0], ("x", "y", "c"))
    def send(name, gw):
        blocks = gw if gw.shape[0] == N_DEV else gw.reshape(N_DEV, gw.shape[1] // N_DEV, gw.shape[2])
        return _exchange_start("rs_start_" + name, blocks)

    sent_pp, tok_pp = send("w_ple_proj", _mm_tn("mm_d_ple_proj", pe, dpp, N_DEV))
    sent_pg, tok = send("w_ple_gate", _mm_tn("mm_d_ple_gate", hp, dpre, 1))
    dhp = _mm_nt("mm_dhp", dpre, wg_pg, F32, after=(tok_pp, tok))
    dx2, dg_ple = _rms_bwd("rms_ple_bwd", x2, norm_ple, dhp, dx3)
    sent_ff2, tok = send("w_ff2", _mm_tn("mm_d_ff2", act, dx2, 1))
    df = _mm_nt("mm_dact", dx2, wg_ff2, BF16, after=(tok,), through=_relu2_bwd, beside=[(f, 0)])
    sent_ff1, tok = send("w_ff1", _mm_tn("mm_d_ff1", h2, df, N_DEV))
    dh2 = _mm_nt("mm_dh2", df, wg_ff1, F32, after=(tok,))
    dx1, dg_ffn = _rms_bwd("rms_ffn_bwd", x1, norm_ffn, dh2, dx2)
    sent_out, tok = send("w_out", _mm_tn("mm_d_out", merged, dx1, 1))
    dgc, dga, dyc, dya = _mm_nt("mm_dmerged", dx1, wg_out, BF16, after=(tok,), through=_gated_merge_bwd,
                                beside=[(z, 5), (z, 7), (y_c, 0), (y_a, 0)], n_results=4, col_tile=d // 2)
    sent_cp, tok = send("w_conv_proj", _mm_tn("mm_d_conv_proj", s_c, dyc, N_DEV))
    ds_c = _mm_nt("mm_ds_c", dyc, wg_cp, F32, after=(tok,))
    dconv, dg_ln, db_ln = _conv_ln_bwd("conv_ln_bwd", ds_c, conv, conv_ln_g, conv_ln_b)
    dca, dcb, dg_dw = _conv_glu_bwd("conv_glu_bwd", dconv, z, cw, dw_full)
    sent_ap, tok = send("w_attn_proj", _mm_tn("mm_d_attn_proj", o, dya, 1))
    do, delta = _mm_nt("mm_do", dya, wg_ap, BF16, after=(tok,), head_dots=o)
    dqt, dkh, dvh = _flash_bwd(qr, kr, kr.T, z, col_v, do, lse, delta, n_heads)
    dq, dg_q = _head_norm_rope_bwd("q_prep_bwd", [(dqt, 0, 1)], z, col_q, n_heads, q_norm, cos, sin,
                                   heads_per_step=GROUP, transposed=True)
    dk, dg_k = _head_norm_rope_bwd("k_prep_bwd", [(dkh, g, GROUP) for g in range(GROUP)], z, col_k, n_kv, k_norm,
                                   cos, sin)
    dv = _group_sum("dv_sum", dvh, n_kv)
    dz = jnp.concatenate([dca, dcb, dq, dk, dv, dgc, dga], axis=1)
    sent_in, tok = send("w_in", _mm_tn("mm_d_in", h1, dz, N_DEV))
    dh1 = _mm_nt("mm_dh1", dz, wg_in, F32, after=(tok,))
    grad_x, dg_mix = _rms_bwd("rms_mix_bwd", x0, norm_mix, dh1, dx1)

    def pad(a):
        return jnp.pad(a, ((0, 0), (0, d - a.shape[1])))

    small = [dg_mix, dg_ffn, dg_ple, dg_final, dg_ln, db_ln, dg_q, dg_k, dg_dw]
    packed = _all_gather("ag_small", jnp.concatenate([pad(a) for a in small], axis=0))
    parts, row = [], 0
    for a in small:
        parts.append(packed[:, row:row + a.shape[0], :a.shape[1]])
        row += a.shape[0]
    p_mix, p_ffn, p_ple, p_final, p_lng, p_lnb, p_q, p_k, p_dw = parts
    p_dw = lax.dynamic_slice_in_dim(p_dw, me * (cw // N_DEV), cw // N_DEV, axis=2)

    def little(name, part, w, m, v):
        rows, n = (1 if w.ndim < 3 else w.shape[1]), w.shape[-1]
        padded = rows if rows == 1 else part.shape[1]

        def two_d(a):
            return jnp.pad(a.reshape(rows, n), ((0, padded - rows), (0, 0)))

        res = _adamw_small("adamw_" + name, part, two_d(w), two_d(m), two_d(v))
        return [r[:rows].reshape(w.shape) for r in res]

    done = {}
    after = packed
    for name, sent, w, m, v in [
            ("w_ple_proj", sent_pp, w_ple_proj, m_w_ple_proj, v_w_ple_proj),
            ("w_ple_gate", sent_pg, w_ple_gate, m_w_ple_gate, v_w_ple_gate),
            ("w_ff2", sent_ff2, w_ff2, m_w_ff2, v_w_ff2),
            ("w_ff1", sent_ff1, w_ff1, m_w_ff1, v_w_ff1),
            ("w_out", sent_out, w_out, m_w_out, v_w_out),
            ("w_conv_proj", sent_cp, w_conv_proj, m_w_conv_proj, v_w_conv_proj),
            ("w_attn_proj", sent_ap, w_attn_proj, m_w_attn_proj, v_w_attn_proj),
            ("w_in", sent_in, w_in, m_w_in, v_w_in)]:
        land = _exchange_wait("rs_wait_" + name, sent, after)
        res = _adamw_shard("adamw_" + name, land, w[0], m[0], v[0])
        after = res[0]
        done[name] = [r.reshape(w.shape) for r in res]

    results = [
        little("norm_mix", p_mix, norm_mix, m_norm_mix, v_norm_mix),
        done["w_in"],
        little("w_dw", p_dw, w_dw, m_w_dw, v_w_dw),
        little("conv_ln_g", p_lng, conv_ln_g, m_conv_ln_g, v_conv_ln_g),
        little("conv_ln_b", p_lnb, conv_ln_b, m_conv_ln_b, v_conv_ln_b),
        done["w_conv_proj"],
        little("q_norm", p_q, q_norm, m_q_norm, v_q_norm),
        little("k_norm", p_k, k_norm, m_k_norm, v_k_norm),
        done["w_attn_proj"],
        done["w_out"],
        little("norm_ffn", p_ffn, norm_ffn, m_norm_ffn, v_norm_ffn),
        done["w_ff1"],
        done["w_ff2"],
        little("norm_ple", p_ple, norm_ple, m_norm_ple, v_norm_ple),
        done["w_ple_gate"],
        done["w_ple_proj"],
        little("norm_final", p_final, norm_final, m_norm_final, v_norm_final),
    ]
    grads, deltas, new_m, new_v = zip(*results)
    return (loss, grad_x[None], *grads, *deltas, *new_m, *new_v)
```

```python
import functools

import jax
import jax.numpy as jnp
from jax import lax
from jax.experimental import pallas as pl
from jax.experimental.pallas import tpu as pltpu

F32 = jnp.float32
BF16 = jnp.bfloat16

EPS = 1e-6
HEAD_DIM = 128
GROUP = 4
CONV_KERNEL = 31
CONV_HALO = 16
CONV_TAPS_PAD = 32
GRID_W = 64
ROPE_THETA = 10000.0
ADAM_LR, ADAM_B1, ADAM_B2, ADAM_EPS, ADAM_WD, ADAM_STEP = 0.001, 0.9, 0.999, 1e-08, 0.01, 10

N_DEV = 8
MESH = pl.DeviceIdType.MESH
SUBLANES = 8
LANES = 128
VMEM_LIMIT = 48 * 1024 * 1024
ROW_TILE_ELEMS = 512 * 1024


def _div(n, pref, mult=1):
    if n <= pref:
        return n
    for d in range(pref, 0, -1):
        if n % d == 0 and d % mult == 0:
            return d
    return n


def _params(sem):
    return pltpu.CompilerParams(dimension_semantics=sem, vmem_limit_bytes=VMEM_LIMIT)


def _sum8(a):
    tm, w = a.shape
    return a.reshape(tm // SUBLANES, SUBLANES, w).sum(axis=0)


def _sigmoid(x):
    return jax.nn.sigmoid(x)


def _rowwise(name, fn, n_rows, tm, ins, outs, accs=(), ncol=1, refs_to_fn=False, after=()):
    nrow = n_rows // tm
    in_specs, arrays = [], []
    for spec in ins:
        kind, arr, w, c0 = spec[:4]
        if kind == "t":
            cs, r0 = spec[4], spec[5]
            in_specs.append(pl.BlockSpec((tm, w), lambda j, i, c0=c0, cs=cs, r0=r0: (r0 + i, c0 + cs * j)))
        elif kind == "p":
            h = spec[4]
            r = tm // h
            in_specs.append(pl.BlockSpec((h, w), lambda j, i, c0=c0, r=r: (jnp.maximum(i * r - 1, 0), c0 + j)))
        elif kind == "n":
            h = spec[4]
            r = tm // h
            last = n_rows // h - 1
            in_specs.append(
                pl.BlockSpec((h, w), lambda j, i, c0=c0, r=r, last=last: (jnp.minimum((i + 1) * r, last), c0 + j)))
        elif kind == "x":
            in_specs.append(pl.BlockSpec((w, tm), lambda j, i, c0=c0: (c0 + j, i)))
        else:
            cs = spec[4]
            in_specs.append(pl.BlockSpec((arr.shape[0], w), lambda j, i, c0=c0, cs=cs: (0, c0 + cs * j)))
        arrays.append(arr)
    out_shape, out_specs = [], []
    for dtype, total, w in outs:
        out_shape.append(jax.ShapeDtypeStruct((n_rows, total), dtype))
        out_specs.append(pl.BlockSpec((tm, w), lambda j, i: (i, j)))
    for rows, total, w, follow in accs:
        out_shape.append(jax.ShapeDtypeStruct((rows, total), F32))
        out_specs.append(pl.BlockSpec((rows, w), (lambda j, i: (0, j)) if follow else (lambda j, i: (0, 0))))
    n_in, n_out, n_acc = len(ins), len(outs), len(accs)
    for token in after:
        in_specs.append(pl.BlockSpec(memory_space=pl.ANY))
        arrays.append(token)
    out0 = n_in + len(after)

    def body(*refs):
        j, i = pl.program_id(0), pl.program_id(1)
        res = fn(i, nrow, *(refs[:n_in] if refs_to_fn else [r[...] for r in refs[:n_in]]))
        if not isinstance(res, (tuple, list)):
            res = (res,)
        for k in range(n_out):
            refs[out0 + k][...] = res[k].astype(outs[k][0])
        for k in range(n_acc):
            ref, term = refs[out0 + n_out + k], res[n_out + k]
            first = (i == 0) if accs[k][3] else jnp.logical_and(i == 0, j == 0)

            @pl.when(first)
            def _():
                ref[...] = term

            @pl.when(jnp.logical_not(first))
            def _():
                ref[...] += term

    res = pl.pallas_call(
        body, name=name, grid=(ncol, nrow), in_specs=in_specs, out_specs=out_specs, out_shape=out_shape,
        compiler_params=_params(("arbitrary", "arbitrary")))(*arrays)
    return res


def _row_tile(n_rows, width, mult=SUBLANES):
    return _div(n_rows, max(mult, ROW_TILE_ELEMS // width), mult)


def _t(arr, w=None, col0=0, cstride=1, row0=0):
    return ("t", arr, arr.shape[1] if w is None else w, col0, cstride, row0)


def _b(arr, w=None, col0=0, cstride=0):
    return ("b", arr, arr.shape[1] if w is None else w, col0, cstride)


def _rms_fwd(name, x, g, after=()):
    n, d = x.shape

    def fn(i, nrow, x, g):
        r = lax.rsqrt(jnp.mean(x * x, axis=-1, keepdims=True) + EPS)
        return x * r * g

    return _rowwise(name, fn, n, _row_tile(n, d), [_t(x), _b(g)], [(BF16, d, d)], after=after)[0]


def _rms_bwd(name, x, g, dh, dres):
    n, d = x.shape

    def fn(i, nrow, x, g, dh, dres):
        r = lax.rsqrt(jnp.mean(x * x, axis=-1, keepdims=True) + EPS)
        w = dh * g
        dx = dres + r * w - x * (r * r * r) * jnp.mean(x * w, axis=-1, keepdims=True)
        return dx, _sum8(dh * x * r)

    return _rowwise(name, fn, n, _row_tile(n, 2 * d), [_t(x), _b(g), _t(dh), _t(dres)], [(F32, d, d)],
                    [(SUBLANES, d, d, True)])


def _swap_halves(x):
    lane = lax.broadcasted_iota(jnp.int32, x.shape, 1)
    return jnp.where(lane % 64 < 32, pltpu.roll(x, 96, axis=1), pltpu.roll(x, 32, axis=1))


def _head_norm_rope(name, z, col0, n_heads, g, cos, sin, out_scale=None, heads_per_step=1):
    n, hp = z.shape[0], heads_per_step
    assert col0 % hp == 0 and n_heads % hp == 0

    def fn(i, nrow, x, g, cos, sin):
        ys = []
        for h in range(hp):
            xh = x[:, h * HEAD_DIM:(h + 1) * HEAD_DIM]
            r = lax.rsqrt(jnp.mean(xh * xh, axis=-1, keepdims=True) + EPS)
            y = xh * r * g
            y = y * cos + _swap_halves(y) * sin
            ys.append(y if out_scale is None else y * out_scale)
        return ys[0] if hp == 1 else jnp.concatenate(ys, axis=1)

    tm = _div(n, 1024, SUBLANES)
    w = hp * HEAD_DIM
    return _rowwise(name, fn, n, tm, [_t(z, w, col0 // hp), _b(g), _t(cos, cstride=0), _t(sin, cstride=0)],
                    [(BF16, n_heads * HEAD_DIM, w)], ncol=n_heads // hp)[0]


def _head_norm_rope_bwd(name, douts, z, col0, n_heads, g, cos, sin, heads_per_step=1, transposed=False):
    n, hp = z.shape[0], heads_per_step
    n_d = len(douts)
    assert col0 % hp == 0 and n_heads % hp == 0 and (hp == 1 or all(c0 % hp == 0 and cs == 1 for _, c0, cs in douts))
    assert not transposed or n_d == 1

    def fn(i, nrow, *a):
        dy = a[0]
        for k in range(1, n_d):
            dy = dy + a[k]
        x, g, cos, sin = a[n_d:]
        dxs, dg = [], None
        for h in range(hp):
            xh = x[:, h * HEAD_DIM:(h + 1) * HEAD_DIM]
            if transposed:
                dyh = jnp.transpose(dy[h * HEAD_DIM:(h + 1) * HEAD_DIM, :])
            else:
                dyh = dy[:, h * HEAD_DIM:(h + 1) * HEAD_DIM]
            dn = dyh * cos + _swap_halves(dyh * sin)
            r = lax.rsqrt(jnp.mean(xh * xh, axis=-1, keepdims=True) + EPS)
            w = dn * g
            dxs.append(r * w - xh * (r * r * r) * jnp.mean(xh * w, axis=-1, keepdims=True))
            term = _sum8(dn * xh * r)
            dg = term if dg is None else dg + term
        return (dxs[0] if hp == 1 else jnp.concatenate(dxs, axis=1)), dg

    tm = _div(n, 1024, SUBLANES)
    w = hp * HEAD_DIM
    if transposed:
        ins = [("x", douts[0][0], w, douts[0][1] // hp)]
    else:
        ins = [_t(arr, w, c0 // hp, cs) for arr, c0, cs in douts]
    ins += [_t(z, w, col0 // hp), _b(g), _t(cos, cstride=0), _t(sin, cstride=0)]
    return _rowwise(name, fn, n, tm, ins, [(BF16, n_heads * HEAD_DIM, w)],
                    [(SUBLANES, HEAD_DIM, HEAD_DIM, False)], ncol=n_heads // hp)


def _halo(arr, w, col0):
    return [("p", arr, w, col0, CONV_HALO), _t(arr, w, col0), ("n", arr, w, col0, CONV_HALO)]


def _extend(i, nrow, prev, cur, nxt):
    prev = jnp.where(i > 0, prev, 0.0)
    nxt = jnp.where(i < nrow - 1, nxt, 0.0)
    return jnp.concatenate([prev, cur, nxt], axis=0)


def _shifted(ext, offset, tm):
    n = ext.shape[0]
    rolled = ext if offset == 0 else pltpu.roll(ext, (-offset) % n, axis=0)
    return rolled[CONV_HALO:CONV_HALO + tm]


def _conv_fwd(name, z, cw, w_dw, ln_g, ln_b):
    n = z.shape[0]
    tm = _row_tile(n, 4 * cw, CONV_HALO)

    def fn(i, nrow, ap, a, an, bp, b, bn, w, g, beta):
        a = a[...]
        ext = _extend(i, nrow, ap[...] * _sigmoid(bp[...]), a * _sigmoid(b[...]), an[...] * _sigmoid(bn[...]))
        conv = jnp.zeros_like(a)
        for k in range(CONV_KERNEL):
            conv = conv + _shifted(ext, k - CONV_KERNEL // 2, tm) * w[k:k + 1, :]
        xc = conv - jnp.mean(conv, axis=-1, keepdims=True)
        ln = xc * lax.rsqrt(jnp.mean(xc * xc, axis=-1, keepdims=True) + EPS) * g[...] + beta[...]
        return conv, ln * _sigmoid(ln)

    ins = _halo(z, cw, 0) + _halo(z, cw, 1) + [_b(w_dw), _b(ln_g), _b(ln_b)]
    return _rowwise(name, fn, n, tm, ins, [(F32, cw, cw), (BF16, cw, cw)], refs_to_fn=True)


def _conv_ln_bwd(name, ds, conv, ln_g, ln_b):
    n, cw = conv.shape

    def fn(i, nrow, ds, conv, g, beta):
        xc = conv - jnp.mean(conv, axis=-1, keepdims=True)
        rstd = lax.rsqrt(jnp.mean(xc * xc, axis=-1, keepdims=True) + EPS)
        xhat = xc * rstd
        ln = xhat * g + beta
        sg = _sigmoid(ln)
        dln = ds * (sg * (1.0 + ln * (1.0 - sg)))
        dxh = dln * g
        dconv = rstd * (dxh - jnp.mean(dxh, axis=-1, keepdims=True)
                        - xhat * jnp.mean(dxh * xhat, axis=-1, keepdims=True))
        return dconv, _sum8(dln * xhat), _sum8(dln)

    return _rowwise(name, fn, n, _row_tile(n, 2 * cw), [_t(ds), _t(conv), _b(ln_g), _b(ln_b)], [(F32, cw, cw)],
                    [(SUBLANES, cw, cw, True), (SUBLANES, cw, cw, True)])


def _conv_glu_bwd(name, dconv, z, cw, w_dw):
    n = z.shape[0]
    tm = _row_tile(n, 4 * cw, CONV_HALO)

    def fn(i, nrow, dp, d, dn, a, b, w):
        d, a = d[...], a[...]
        dext = _extend(i, nrow, dp[...], d, dn[...])
        sg = _sigmoid(b[...])
        u = a * sg
        tap = lax.broadcasted_iota(jnp.int32, (CONV_TAPS_PAD, cw), 0)
        du = jnp.zeros_like(d)
        dw = jnp.zeros((CONV_TAPS_PAD, cw), F32)
        for k in range(CONV_KERNEL):
            d_k = _shifted(dext, -(k - CONV_KERNEL // 2), tm)
            du = du + d_k * w[k:k + 1, :]
            dw = dw + jnp.where(tap == k, jnp.sum(d_k * u, axis=0, keepdims=True), 0.0)
        return du * sg, du * a * sg * (1.0 - sg), dw

    ins = _halo(dconv, cw, 0) + [_t(z, cw, 0), _t(z, cw, 1), _b(w_dw)]
    return _rowwise(name, fn, n, tm, ins, [(BF16, cw, cw), (BF16, cw, cw)], [(CONV_TAPS_PAD, cw, cw, True)],
                    refs_to_fn=True)


def _gated_merge(y_a, gc, ga, y_c):
    return _sigmoid(gc) * y_c + _sigmoid(ga) * y_a


def _gated_merge_bwd(dm, gc, ga, yc, ya):
    sc, sa = _sigmoid(gc), _sigmoid(ga)
    return dm * yc * sc * (1.0 - sc), dm * ya * sa * (1.0 - sa), dm * sc, dm * sa


def _relu2(f):
    r = jnp.maximum(f, 0.0)
    return r * r


def _relu2_bwd(da, f):
    return da * (2.0 * jnp.maximum(f, 0.0))


def _head_loss(name, x2, pre, pp, target, g):
    n, d = x2.shape

    def fn(i, nrow, x2, pre, pp, t, g):
        gate = _sigmoid(pre)
        x3 = x2 + gate * pp
        r = lax.rsqrt(jnp.mean(x3 * x3, axis=-1, keepdims=True) + EPS)
        xn = x3 * r
        e = xn * g - t
        part = 0.5 * jnp.sum(jnp.mean(e * e, axis=-1, keepdims=True), axis=0, keepdims=True)
        dy = e * (1.0 / d)
        w = dy * g
        dx3 = r * w - x3 * (r * r * r) * jnp.mean(x3 * w, axis=-1, keepdims=True)
        return (dx3, dx3 * gate, dx3 * pp * gate * (1.0 - gate), _sum8(dy * xn),
                jnp.broadcast_to(part, (SUBLANES, LANES)))

    ins = [_t(x2), _t(pre), _t(pp), _t(target), _b(g)]
    return _rowwise(name, fn, n, _row_tile(n, 2 * d), ins, [(F32, d, d), (BF16, d, d), (BF16, d, d)],
                    [(SUBLANES, d, d, True), (SUBLANES, LANES, LANES, True)])


def _group_sum(name, dvh, n_kv):
    n = dvh.shape[0]

    def fn(i, nrow, *a):
        s = a[0]
        for k in range(1, GROUP):
            s = s + a[k]
        return s

    ins = [_t(dvh, HEAD_DIM, k, GROUP) for k in range(GROUP)]
    return _rowwise(name, fn, n, _div(n, 1024, SUBLANES), ins, [(BF16, n_kv * HEAD_DIM, HEAD_DIM)], ncol=n_kv)[0]


MM_ROWS = 1024
MM_COLS = 1152
MM_REDUCE = 2048
MM_REDUCE_BLOCKS = 2304


def _rows_outermost(n_rows, n_cols, row_tile_bytes, weight_tile_bytes):
    rows_outer = n_rows * (row_tile_bytes + n_cols * weight_tile_bytes)
    weight_outer = n_cols * (weight_tile_bytes + n_rows * row_tile_bytes)
    return 5 * rows_outer < 4 * weight_outer


def _accumulate(step, n_steps, part, acc, finish):
    if n_steps == 1:
        finish(part)
        return

    @pl.when(step == 0)
    def _():
        acc[...] = part

    @pl.when(jnp.logical_and(step > 0, step < n_steps - 1))
    def _():
        acc[...] += part

    @pl.when(step == n_steps - 1)
    def _():
        finish(acc[...] + part)


MXU_WIDTH = 256


def _paired(nb, ns):
    return ns % MXU_WIDTH != 0 and nb % 2 == 0 and ns <= MM_COLS


def _mm_nn(name, a, w3, out_dtype, residual=None, activation=None, beside=(), col_tile=None):
    m, k = a.shape
    nb, _, ns = w3.shape
    jb = 2 if _paired(nb, ns) else 1
    tm = _div(m, MM_ROWS // max(jb, 1 + len(beside) // 2), SUBLANES)
    tn_w, tk = (ns if jb > 1 else _div(ns, col_tile or MM_COLS, LANES)), _div(k, MM_REDUCE, LANES)
    tn = jb * tn_w
    per, nk = ns // tn_w, k // tk
    has_res = residual is not None
    n_out = 1 if activation is None else 2
    first_beside = 3 if has_res else 2

    def body(*refs):
        a_ref, w_ref = refs[:2]
        acc = refs[-1]
        outs = refs[-1 - n_out:-1]
        w = w_ref[0] if jb == 1 else jnp.concatenate([w_ref[j] for j in range(jb)], axis=1)
        part = jnp.dot(a_ref[...].astype(BF16), w.astype(BF16), preferred_element_type=F32)

        def finish(total):
            if has_res:
                total = total + refs[2][...]
            outs[0][...] = total.astype(out_dtype)
            if activation is not None:
                tiles = [r[...] for r in refs[first_beside:first_beside + len(beside)]]
                outs[1][...] = activation(total, *tiles).astype(BF16)

        _accumulate(pl.program_id(2), nk, part, acc, finish)

    n_rows, n_cols = m // tm, (nb // jb) * per
    rows_outer = nk > 1 or _rows_outermost(n_rows, n_cols, tm * tk * a.dtype.itemsize,
                                           jb * tk * tn_w * w3.dtype.itemsize)

    def at(f):
        return f if rows_outer else (lambda n, i, kk: f(i, n, kk))

    in_specs = [pl.BlockSpec((tm, tk), at(lambda i, n, kk: (i, kk))),
                pl.BlockSpec((jb, tk, tn_w), at(lambda i, n, kk: (n // per, kk, n % per)))]
    args = [a, w3]
    if has_res:
        in_specs.append(pl.BlockSpec((tm, tn), at(lambda i, n, kk: (i, n))))
        args.append(residual)
    for arr, col0 in beside:
        in_specs.append(pl.BlockSpec((tm, tn), at(lambda i, n, kk, col0=col0: (i, col0 + n))))
        args.append(arr)
    out_dtypes = [out_dtype] + ([BF16] if activation is not None else [])
    res = pl.pallas_call(
        body, name=name, grid=(n_rows, n_cols, nk) if rows_outer else (n_cols, n_rows, nk), in_specs=in_specs,
        out_specs=[pl.BlockSpec((tm, tn), at(lambda i, n, kk: (i, n))) for _ in out_dtypes],
        out_shape=[jax.ShapeDtypeStruct((m, nb * ns), dt) for dt in out_dtypes],
        scratch_shapes=[pltpu.VMEM((tm, tn) if nk > 1 else (SUBLANES, LANES), F32)],
        compiler_params=_params(("parallel", "parallel", "arbitrary")))(*args)
    return res[0] if activation is None else res


def _mm_nt(name, dc, w3, out_dtype, after=(), through=None, beside=(), n_results=1, head_dots=None, col_tile=None):
    m = dc.shape[0]
    nb, k, ns = w3.shape
    tm = _div(m, MM_ROWS // (1 + len(beside) // 3), SUBLANES)
    to = _div(k, col_tile or MM_ROWS, LANES)
    if ns > MM_REDUCE_BLOCKS // 2:
        jb, tr = 1, _div(ns, MM_REDUCE, LANES)
    else:
        jb, tr = max(j for j in range(1, nb + 1) if nb % j == 0 and j * ns <= MM_REDUCE_BLOCKS), ns
    per = ns // tr
    nr = (nb // jb) * per
    tiles_in = list(beside) + ([(head_dots, 0)] if head_dots is not None else [])
    n_out = n_results + (1 if head_dots is not None else 0)
    heads = to // HEAD_DIM

    def body(dc_ref, w_ref, *rest):
        acc = rest[-1]
        outs = rest[-1 - n_out:-1]
        w = w_ref[0] if jb == 1 else jnp.concatenate([w_ref[j] for j in range(jb)], axis=1)
        part = lax.dot_general(dc_ref[...].astype(BF16), w.astype(BF16), (((1,), (1,)), ((), ())),
                               preferred_element_type=F32)

        def finish(total):
            results = (total,) if through is None else through(total, *[r[...] for r in rest[:len(beside)]])
            if not isinstance(results, (tuple, list)):
                results = (results,)
            for out, res in zip(outs, results):
                out[...] = res.astype(out_dtype)
            if head_dots is not None:
                prod = results[0].astype(out_dtype).astype(F32) * rest[len(beside)][...]
                for h in range(heads):
                    row = jnp.sum(jnp.transpose(prod[:, h * HEAD_DIM:(h + 1) * HEAD_DIM]), axis=0, keepdims=True)
                    outs[n_results][h] = jnp.broadcast_to(row, (SUBLANES, tm))

        _accumulate(pl.program_id(2), nr, part, acc, finish)

    rows_outer = nr > 1 or _rows_outermost(m // tm, k // to, tm * jb * tr * dc.dtype.itemsize,
                                           jb * to * tr * w3.dtype.itemsize)

    def at(f):
        return f if rows_outer else (lambda o, i, r: f(i, o, r))

    in_specs = [pl.BlockSpec((tm, jb * tr), at(lambda i, o, r: (i, r))),
                pl.BlockSpec((jb, to, tr), at(lambda i, o, r: (r // per, o, r % per)))]
    args = [dc, w3]
    for arr, col0 in tiles_in:
        in_specs.append(pl.BlockSpec((tm, to), at(lambda i, o, r, col0=col0: (i, col0 + o))))
        args.append(arr)
    for token in after:
        in_specs.append(pl.BlockSpec(memory_space=pl.ANY))
        args.append(token)
    out_specs = [pl.BlockSpec((tm, to), at(lambda i, o, r: (i, o))) for _ in range(n_results)]
    out_shape = [jax.ShapeDtypeStruct((m, k), out_dtype) for _ in range(n_results)]
    if head_dots is not None:
        out_specs.append(pl.BlockSpec((heads, SUBLANES, tm), at(lambda i, o, r: (o, 0, i))))
        out_shape.append(jax.ShapeDtypeStruct((k // HEAD_DIM, SUBLANES, m), F32))
    res = pl.pallas_call(
        body, name=name, grid=(m // tm, k // to, nr) if rows_outer else (k // to, m // tm, nr), in_specs=in_specs,
        out_specs=out_specs, out_shape=out_shape,
        scratch_shapes=[pltpu.VMEM((tm, to) if nr > 1 else (SUBLANES, LANES), F32)],
        compiler_params=_params(("parallel", "parallel", "arbitrary")))(*args)
    return res[0] if n_out == 1 else res


def _mm_tn(name, a, dc, nb):
    m, k = a.shape
    ns = dc.shape[1] // nb
    jb = 2 if _paired(nb, ns) else 1
    tr, to = _div(m, MM_REDUCE, LANES), _div(k, MM_ROWS // jb, LANES)
    tn_w = ns if jb > 1 else _div(ns, MM_COLS, LANES)
    tn = jb * tn_w
    per, nr = ns // tn_w, m // tr

    def body(a_ref, dc_ref, o_ref, acc):
        part = lax.dot_general(a_ref[...].astype(BF16), dc_ref[...].astype(BF16), (((0,), (0,)), ((), ())),
                               preferred_element_type=F32)

        def finish(total):
            for j in range(jb):
                o_ref[j] = total[:, j * tn_w:(j + 1) * tn_w].astype(BF16)

        _accumulate(pl.program_id(2), nr, part, acc, finish)

    return pl.pallas_call(
        body, name=name, grid=(k // to, (nb // jb) * per, nr),
        in_specs=[pl.BlockSpec((tr, to), lambda o, n, r: (r, o)),
                  pl.BlockSpec((tr, tn), lambda o, n, r: (r, n))],
        out_specs=pl.BlockSpec((jb, to, tn_w), lambda o, n, r: (n // per, o, n % per)),
        out_shape=jax.ShapeDtypeStruct((nb, k, ns), BF16),
        scratch_shapes=[pltpu.VMEM((to, tn) if nr > 1 else (SUBLANES, LANES), F32)],
        compiler_params=_params(("parallel", "parallel", "arbitrary")))(a, dc)


QK_SCALE = HEAD_DIM ** -0.5
LOG2_E = 1.4426950408889634
LN_2 = 0.6931471805599453


def _columns(row, n):
    return jnp.transpose(jnp.broadcast_to(row, (LANES, n)))


FLASH_TILE = 1024
ONES_ROWS = 16


def _flash_fwd(qr, kr, vt, n_heads):
    t = qr.shape[0]
    tq, tk = _div(t, 2 * FLASH_TILE, LANES), _div(t, FLASH_TILE, LANES)
    nk = t // tk
    rows = vt.shape[0] // (n_heads // GROUP)

    def body(q_ref, k_ref, vt_ref, o_ref, lse_ref, m_scr, acc_scr):
        kb = pl.program_id(2)

        @pl.when(kb == 0)
        def _():
            m_scr[...] = jnp.full_like(m_scr, -1e30)
            acc_scr[...] = jnp.zeros_like(acc_scr)

        s_t = lax.dot_general(k_ref[...], q_ref[...], (((1,), (1,)), ((), ())), preferred_element_type=F32)
        m_prev = m_scr[...]
        m_new = jnp.maximum(m_prev, jnp.max(s_t, axis=0, keepdims=True))
        alpha = jnp.exp2(m_prev - m_new)
        p_t = jnp.exp2(s_t - m_new).astype(BF16)
        acc_scr[...] = alpha * acc_scr[...] + jnp.dot(vt_ref[...], p_t, preferred_element_type=F32)
        m_scr[...] = m_new

        @pl.when(kb == nk - 1)
        def _():
            l = acc_scr[HEAD_DIM:HEAD_DIM + 1, :]
            o_ref[...] = jnp.transpose(acc_scr[0:HEAD_DIM, :] / l)
            lse_ref[...] = jnp.broadcast_to(m_scr[...] + jnp.log(l) * LOG2_E, (SUBLANES, tq))

    return pl.pallas_call(
        body, name="flash_fwd", grid=(n_heads, t // tq, nk),
        in_specs=[pl.BlockSpec((tq, HEAD_DIM), lambda h, qi, ki: (qi, h)),
                  pl.BlockSpec((tk, HEAD_DIM), lambda h, qi, ki: (ki, h // GROUP)),
                  pl.BlockSpec((rows, tk), lambda h, qi, ki: (h // GROUP, ki))],
        out_specs=[pl.BlockSpec((tq, HEAD_DIM), lambda h, qi, ki: (qi, h)),
                   pl.BlockSpec((None, SUBLANES, tq), lambda h, qi, ki: (h, 0, qi))],
        out_shape=[jax.ShapeDtypeStruct((t, n_heads * HEAD_DIM), F32),
                   jax.ShapeDtypeStruct((n_heads, SUBLANES, t), F32)],
        scratch_shapes=[pltpu.VMEM((1, tq), F32), pltpu.VMEM((rows, tq), F32)],
        compiler_params=_params(("parallel", "parallel", "arbitrary")))(qr, kr, vt)


def _flash_bwd(qr, kr, kt, z, v_col0, do, lse, delta, n_heads):
    t = qr.shape[0]
    tq, tk = _div(t, 2 * FLASH_TILE, LANES), _div(t, FLASH_TILE, LANES)
    nq = t // tq

    def body(q_ref, k_ref, kt_ref, v_ref, do_ref, lse_ref, dl_ref, dq_ref, dk_ref, dv_ref, dk_acc, dv_acc):
        kb, qb = pl.program_id(1), pl.program_id(2)
        q, k, do_ = q_ref[...], k_ref[...], do_ref[...]
        v = v_ref[...].astype(BF16)
        s_t = lax.dot_general(k, q, (((1,), (1,)), ((), ())), preferred_element_type=F32)
        p_t = jnp.exp2(s_t - lse_ref[0:1, :])
        dp_t = lax.dot_general(v, do_, (((1,), (1,)), ((), ())), preferred_element_type=F32)
        ds_t = (p_t * (dp_t - dl_ref[0:1, :])).astype(BF16)
        dv_c = jnp.dot(p_t.astype(BF16), do_, preferred_element_type=F32)
        dk_c = jnp.dot(ds_t, q, preferred_element_type=F32) * LN_2
        dq_c = jnp.dot(kt_ref[...], ds_t, preferred_element_type=F32) * QK_SCALE

        @pl.when(qb == 0)
        def _():
            dk_acc[...] = dk_c
            dv_acc[...] = dv_c

        @pl.when(qb > 0)
        def _():
            dk_acc[...] += dk_c
            dv_acc[...] += dv_c

        @pl.when(qb == nq - 1)
        def _():
            dk_ref[...] = dk_acc[...]
            dv_ref[...] = dv_acc[...]

        cols = pl.ds(pl.multiple_of(qb * tq, tq), tq)

        @pl.when(kb == 0)
        def _():
            dq_ref[:, cols] = dq_c

        @pl.when(kb > 0)
        def _():
            dq_ref[:, cols] += dq_c

    wide = jax.ShapeDtypeStruct((t, n_heads * HEAD_DIM), F32)
    return pl.pallas_call(
        body, name="flash_bwd", grid=(n_heads, t // tk, nq),
        in_specs=[pl.BlockSpec((tq, HEAD_DIM), lambda h, kb, qb: (qb, h)),
                  pl.BlockSpec((tk, HEAD_DIM), lambda h, kb, qb: (kb, h // GROUP)),
                  pl.BlockSpec((HEAD_DIM, tk), lambda h, kb, qb: (h // GROUP, kb)),
                  pl.BlockSpec((tk, HEAD_DIM), lambda h, kb, qb: (kb, v_col0 + h // GROUP)),
                  pl.BlockSpec((tq, HEAD_DIM), lambda h, kb, qb: (qb, h)),
                  pl.BlockSpec((None, SUBLANES, tq), lambda h, kb, qb: (h, 0, qb)),
                  pl.BlockSpec((None, SUBLANES, tq), lambda h, kb, qb: (h, 0, qb))],
        out_specs=[pl.BlockSpec((HEAD_DIM, t), lambda h, kb, qb: (h, 0)),
                   pl.BlockSpec((tk, HEAD_DIM), lambda h, kb, qb: (kb, h)),
                   pl.BlockSpec((tk, HEAD_DIM), lambda h, kb, qb: (kb, h))],
        out_shape=[jax.ShapeDtypeStruct((n_heads * HEAD_DIM, t), F32), wide, wide],
        scratch_shapes=[pltpu.VMEM((tk, HEAD_DIM), F32), pltpu.VMEM((tk, HEAD_DIM), F32)],
        compiler_params=_params(("parallel", "arbitrary", "arbitrary")))(qr, kr, kt, z, do, lse, delta)


def _position():
    x, y, c = lax.axis_index("x"), lax.axis_index("y"), lax.axis_index("c")
    return x, y, c


def _index(x, y, c):
    return 4 * x + 2 * y + c


def _all_gather(name, shard):
    a, b = shard.shape

    def body(x_ref, out_ref, send_sems, recv_sems, local_sem):
        x, y, c = _position()
        me, sibling = (x, y, c), (x, y, 1 - c)
        chips = [(1 - x, y), (x, 1 - y), (1 - x, 1 - y)]

        def block(px, py, pc):
            return out_ref.at[_index(px, py, pc)]

        def copy(k, blk, to, src=None):
            return pltpu.make_async_remote_copy(
                src_ref=block(*blk) if src is None else src, dst_ref=block(*blk),
                send_sem=send_sems.at[k], recv_sem=recv_sems.at[k], device_id=to, device_id_type=MESH)

        mine = pltpu.make_async_copy(x_ref, block(*me), local_sem)
        mine.start()
        first = [copy(0, me, sibling, src=x_ref)]
        first += [copy(1 + j, me, (*chip, c), src=x_ref) for j, chip in enumerate(chips)]
        for cp in first:
            cp.start()
        passed = [copy(4 + j, (*chip, c), sibling) for j, chip in enumerate(chips)]
        for j, chip in enumerate(chips):
            copy(1 + j, (*chip, c), me).wait_recv()
            passed[j].start()
        copy(0, sibling, me).wait_recv()
        for j, chip in enumerate(chips):
            copy(4 + j, (*chip, 1 - c), me).wait_recv()
        for cp in first + passed:
            cp.wait_send()
        mine.wait()

    return pl.pallas_call(
        body, name=name, out_shape=jax.ShapeDtypeStruct((N_DEV, a, b), shard.dtype),
        in_specs=[pl.BlockSpec(memory_space=pltpu.HBM)], out_specs=pl.BlockSpec(memory_space=pltpu.HBM),
        scratch_shapes=[pltpu.SemaphoreType.DMA((7,)), pltpu.SemaphoreType.DMA((7,)), pltpu.SemaphoreType.DMA],
    )(shard)


HBM_SPEC = pl.BlockSpec(memory_space=pltpu.HBM)
SEM_SPEC = pl.BlockSpec(memory_space=pltpu.SEMAPHORE)
N_PEERS = N_DEV - 1
PEER_FLIPS = [(fx, fy, fc) for fx in (0, 1) for fy in (0, 1) for fc in (0, 1)][1:]


def _peers(x, y, c):
    return [((1 - x) if fx else x, (1 - y) if fy else y, (1 - c) if fc else c) for fx, fy, fc in PEER_FLIPS]


def _exchange_copies(g_ref, land_ref, send_sems, recv_sems):
    x, y, c = _position()
    me = _index(x, y, c)
    peers = _peers(x, y, c)
    blocked = len(g_ref.shape) == 3

    def copy(k, src_block, dst_block):
        return pltpu.make_async_remote_copy(
            src_ref=g_ref.at[src_block] if blocked else g_ref, dst_ref=land_ref.at[dst_block],
            send_sem=send_sems.at[k], recv_sem=recv_sems.at[k], device_id=peers[k], device_id_type=MESH)

    sent = [copy(k, _index(*peer), me) for k, peer in enumerate(peers)]
    arriving = [copy(k, me, _index(*peer)) for k, peer in enumerate(peers)]
    own = pltpu.make_async_copy(g_ref.at[me] if blocked else g_ref, land_ref.at[me], send_sems.at[N_PEERS])
    return sent, arriving, own


def _exchange_start(name, g3, after=()):
    def body(g_ref, land_ref, *rest):
        send_sems, recv_sems, _, _, token = rest[len(after):]
        sent, _, own = _exchange_copies(g_ref, land_ref, send_sems, recv_sems)
        for cp in sent:
            cp.start()
        own.start()
        token[...] = jnp.zeros_like(token)

    land_shape = g3.shape if g3.ndim == 3 else (N_DEV,) + g3.shape
    land = lax.empty(land_shape, g3.dtype)
    send_sems, recv_sems, g_thru, land_thru, token = pl.pallas_call(
        body, name=name,
        out_shape=(pltpu.SemaphoreType.DMA((N_PEERS + 1,)), pltpu.SemaphoreType.DMA((N_PEERS,)),
                   pltpu.HBM(g3.shape, g3.dtype), pltpu.HBM(land_shape, g3.dtype),
                   jax.ShapeDtypeStruct((SUBLANES, LANES), F32)),
        in_specs=(HBM_SPEC, HBM_SPEC) + (pl.BlockSpec(memory_space=pl.ANY),) * len(after),
        out_specs=(SEM_SPEC, SEM_SPEC, HBM_SPEC, HBM_SPEC, pl.BlockSpec(memory_space=pltpu.VMEM)),
        input_output_aliases={0: 2, 1: 3},
        compiler_params=pltpu.CompilerParams(has_side_effects=pltpu.SideEffectType.DATAFLOW_SIDE_EFFECTING),
    )(pltpu.with_memory_space_constraint(g3, pltpu.HBM), pltpu.with_memory_space_constraint(land, pltpu.HBM), *after)
    return (send_sems, recv_sems, g_thru, land_thru), token


def _exchange_wait(name, handle, after):
    send_sems, recv_sems, g_thru, land_thru = handle

    def body(g_ref, land_ref, send_sems, recv_sems, after_ref, g_dead, land_out):
        sent, arriving, own = _exchange_copies(g_ref, land_ref, send_sems, recv_sems)
        for cp in sent:
            cp.wait_send()
        for cp in arriving:
            cp.wait_recv()
        own.wait()

    return pl.pallas_call(
        body, name=name,
        out_shape=(pltpu.HBM(g_thru.shape, g_thru.dtype), pltpu.HBM(land_thru.shape, land_thru.dtype)),
        in_specs=(HBM_SPEC, HBM_SPEC, SEM_SPEC, SEM_SPEC, pl.BlockSpec(memory_space=pl.ANY)),
        out_specs=(HBM_SPEC, HBM_SPEC), input_output_aliases={0: 0, 1: 1},
        compiler_params=pltpu.CompilerParams(has_side_effects=pltpu.SideEffectType.DATAFLOW_SIDE_EFFECTING),
    )(g_thru, land_thru, send_sems, recv_sems, after)[1]


def _adamw(w, g, m, v):
    m = ADAM_B1 * m + (1.0 - ADAM_B1) * g
    v = ADAM_B2 * v + (1.0 - ADAM_B2) * (g * g)
    m_hat = m / (1.0 - ADAM_B1 ** ADAM_STEP)
    v_hat = v / (1.0 - ADAM_B2 ** ADAM_STEP)
    delta = -ADAM_LR * (m_hat / (jnp.sqrt(v_hat) + ADAM_EPS) + ADAM_WD * w)
    return delta, m, v


def _adamw_shard(name, land, w, m, v):
    _, a, b = land.shape
    tm = _div(a, max(16, ROW_TILE_ELEMS // (2 * b)), 16)
    per = a // tm

    def fn(i, nrow, *t):
        g = t[0].astype(F32)
        for s in range(1, N_DEV):
            g = g + t[s].astype(F32)
        w, m, v = t[N_DEV:]
        return (g,) + _adamw(w, g, m, v)

    land2 = land.reshape(N_DEV * a, b)
    ins = [_t(land2, row0=s * per) for s in range(N_DEV)] + [_t(w), _t(m), _t(v)]
    return _rowwise(name, fn, a, tm, ins, [(F32, b, b)] * 4)


def _adamw_small(name, parts, w, m, v):
    rows, n = w.shape

    def body(p_ref, w_ref, m_ref, v_ref, g_out, d_out, m_out, v_out):
        g = p_ref[0]
        for s in range(1, parts.shape[0]):
            g = g + p_ref[s]
        g = jnp.sum(g, axis=0, keepdims=True) if rows == 1 else g[0:rows, :]
        delta, m_new, v_new = _adamw(w_ref[...], g, m_ref[...], v_ref[...])
        g_out[...] = g
        d_out[...] = delta
        m_out[...] = m_new
        v_out[...] = v_new

    return pl.pallas_call(body, name=name, out_shape=[jax.ShapeDtypeStruct((rows, n), F32)] * 4)(parts, w, m, v)


def _rope_tables(t):
    pos = jnp.arange(t, dtype=jnp.int32)
    half = HEAD_DIM // 4
    inv_freq = ROPE_THETA ** (-jnp.arange(0, 2 * half, 2, dtype=F32) / (2 * half))
    ang_r = (pos // GRID_W).astype(F32)[:, None] * inv_freq[None, :]
    ang_c = (pos % GRID_W).astype(F32)[:, None] * inv_freq[None, :]
    cos = jnp.concatenate([jnp.cos(ang_r)] * 2 + [jnp.cos(ang_c)] * 2, axis=-1)
    sin = jnp.concatenate([-jnp.sin(ang_r), jnp.sin(ang_r), -jnp.sin(ang_c), jnp.sin(ang_c)], axis=-1)
    return cos, sin


def _gather_weight(name, w, cols):
    g = _all_gather(name, w[0].astype(BF16))
    return g if cols else g.reshape(1, N_DEV * g.shape[1], g.shape[2])


def kernel(x, p, norm_mix, w_in, w_dw, conv_ln_g, conv_ln_b, w_conv_proj, q_norm, k_norm, w_attn_proj, w_out, norm_ffn, w_ff1, w_ff2, norm_ple, w_ple_gate, w_ple_proj, norm_final, loss_target, m_norm_mix, m_w_in, m_w_dw, m_conv_ln_g, m_conv_ln_b, m_w_conv_proj, m_q_norm, m_k_norm, m_w_attn_proj, m_w_out, m_norm_ffn, m_w_ff1, m_w_ff2, m_norm_ple, m_w_ple_gate, m_w_ple_proj, m_norm_final, v_norm_mix, v_w_in, v_w_dw, v_conv_ln_g, v_conv_ln_b, v_w_conv_proj, v_q_norm, v_k_norm, v_w_attn_proj, v_w_out, v_norm_ffn, v_w_ff1, v_w_ff2, v_norm_ple, v_w_ple_gate, v_w_ple_proj, v_norm_final):
    t, d = x.shape[1], x.shape[2]
    cw = d // 2
    n_heads = d // HEAD_DIM
    n_kv = n_heads // GROUP
    col_q, col_k, col_v = d // HEAD_DIM, 2 * d // HEAD_DIM, (2 * d + d // 4) // HEAD_DIM
    x0, pe, tgt = x[0], p[0, 0], loss_target[0]
    g_final = norm_final.reshape(1, d)
    me = _index(*_position())

    wg_in = _gather_weight("ag_w_in", w_in, True)
    dw_pad = jnp.pad(w_dw[0], ((0, CONV_TAPS_PAD - CONV_KERNEL), (0, 0)))
    dw_all = _all_gather("ag_w_dw", dw_pad)
    dw_full = dw_all.transpose(1, 0, 2).reshape(CONV_TAPS_PAD, cw)
    cos, sin = _rope_tables(t)

    def fetch(name, w, prev):
        return _exchange_start("ag_start_" + name, w[0].astype(BF16), after=(wg_in, dw_all) if prev is None else (prev,))

    def arrived(name, handle, cols, after):
        g = _exchange_wait("ag_wait_" + name, handle, after)
        return g if cols else g.reshape(1, N_DEV * g.shape[1], g.shape[2])

    on_cp, tok = fetch("w_conv_proj", w_conv_proj, None)
    on_ap, tok = fetch("w_attn_proj", w_attn_proj, tok)
    on_out, tok = fetch("w_out", w_out, tok)
    on_ff1, tok = fetch("w_ff1", w_ff1, tok)
    on_ff2, tok = fetch("w_ff2", w_ff2, tok)
    on_pg, tok = fetch("w_ple_gate", w_ple_gate, tok)
    on_pp, tok = fetch("w_ple_proj", w_ple_proj, tok)

    h1 = _rms_fwd("rms_mix", x0, norm_mix, after=(tok,))
    z = _mm_nn("mm_in", h1, wg_in, F32)
    conv, s_c = _conv_fwd("conv_fwd", z, cw, dw_full, conv_ln_g, conv_ln_b)
    wg_cp = arrived("w_conv_proj", on_cp, True, s_c)
    y_c = _mm_nn("mm_conv_proj", s_c, wg_cp, F32)
    qr = _head_norm_rope("q_prep", z, col_q, n_heads, q_norm, cos, sin, out_scale=QK_SCALE * LOG2_E,
                         heads_per_step=GROUP)
    kr = _head_norm_rope("k_prep", z, col_k, n_kv, k_norm, cos, sin)
    vt = z[:, col_v * HEAD_DIM:(col_v + n_kv) * HEAD_DIM].T.astype(BF16).reshape(n_kv, HEAD_DIM, t)
    vt = jnp.concatenate([vt, jnp.ones((n_kv, ONES_ROWS, t), BF16)], axis=1).reshape(n_kv * (HEAD_DIM + ONES_ROWS), t)
    o, lse = _flash_fwd(qr, kr, vt, n_heads)
    wg_ap = arrived("w_attn_proj", on_ap, False, o)
    y_a, merged = _mm_nn("mm_attn_proj", o, wg_ap, F32, activation=_gated_merge,
                         beside=[(z, 5), (z, 7), (y_c, 0)], col_tile=d // 2)
    wg_out = arrived("w_out", on_out, False, merged)
    x1 = _mm_nn("mm_out", merged, wg_out, F32, residual=x0)
    h2 = _rms_fwd("rms_ffn", x1, norm_ffn)
    wg_ff1 = arrived("w_ff1", on_ff1, True, h2)
    f, act = _mm_nn("mm_ff1", h2, wg_ff1, F32, activation=_relu2)
    wg_ff2 = arrived("w_ff2", on_ff2, False, act)
    x2 = _mm_nn("mm_ff2", act, wg_ff2, F32, residual=x1)
    hp = _rms_fwd("rms_ple", x2, norm_ple)
    wg_pg = arrived("w_ple_gate", on_pg, False, hp)
    pre = _mm_nn("mm_ple_gate", hp, wg_pg, F32)
    wg_pp = arrived("w_ple_proj", on_pp, True, pre)
    pp = _mm_nn("mm_ple_proj", pe, wg_pp, F32)

    dx3, dpp, dpre, dg_final, loss_part = _head_loss("head_loss", x2, pre, pp, tgt, g_final)
    loss = lax.psum(loss_part[0, 0], ("x", "y", "c"))
    def send(name, gw):
        blocks = gw if gw.shape[0] == N_DEV else gw.reshape(N_DEV, gw.shape[1] // N_DEV, gw.shape[2])
        return _exchange_start("rs_start_" + name, blocks)

    sent_pp, tok_pp = send("w_ple_proj", _mm_tn("mm_d_ple_proj", pe, dpp, N_DEV))
    sent_pg, tok = send("w_ple_gate", _mm_tn("mm_d_ple_gate", hp, dpre, 1))
    dhp = _mm_nt("mm_dhp", dpre, wg_pg, F32, after=(tok_pp, tok))
    dx2, dg_ple = _rms_bwd("rms_ple_bwd", x2, norm_ple, dhp, dx3)
    sent_ff2, tok = send("w_ff2", _mm_tn("mm_d_ff2", act, dx2, 1))
    df = _mm_nt("mm_dact", dx2, wg_ff2, BF16, after=(tok,), through=_relu2_bwd, beside=[(f, 0)])
    sent_ff1, tok = send("w_ff1", _mm_tn("mm_d_ff1", h2, df, N_DEV))
    dh2 = _mm_nt("mm_dh2", df, wg_ff1, F32, after=(tok,))
    dx1, dg_ffn = _rms_bwd("rms_ffn_bwd", x1, norm_ffn, dh2, dx2)
    sent_out, tok = send("w_out", _mm_tn("mm_d_out", merged, dx1, 1))
    dgc, dga, dyc, dya = _mm_nt("mm_dmerged", dx1, wg_out, BF16, after=(tok,), through=_gated_merge_bwd,
                                beside=[(z, 5), (z, 7), (y_c, 0), (y_a, 0)], n_results=4, col_tile=d // 2)
    sent_cp, tok = send("w_conv_proj", _mm_tn("mm_d_conv_proj", s_c, dyc, N_DEV))
    ds_c = _mm_nt("mm_ds_c", dyc, wg_cp, F32, after=(tok,))
    dconv, dg_ln, db_ln = _conv_ln_bwd("conv_ln_bwd", ds_c, conv, conv_ln_g, conv_ln_b)
    dca, dcb, dg_dw = _conv_glu_bwd("conv_glu_bwd", dconv, z, cw, dw_full)
    sent_ap, tok = send("w_attn_proj", _mm_tn("mm_d_attn_proj", o, dya, 1))
    do, delta = _mm_nt("mm_do", dya, wg_ap, BF16, after=(tok,), head_dots=o)
    dqt, dkh, dvh = _flash_bwd(qr, kr, kr.T, z, col_v, do, lse, delta, n_heads)
    dq, dg_q = _head_norm_rope_bwd("q_prep_bwd", [(dqt, 0, 1)], z, col_q, n_heads, q_norm, cos, sin,
                                   heads_per_step=GROUP, transposed=True)
    dk, dg_k = _head_norm_rope_bwd("k_prep_bwd", [(dkh, g, GROUP) for g in range(GROUP)], z, col_k, n_kv, k_norm,
                                   cos, sin)
    dv = _group_sum("dv_sum", dvh, n_kv)
    dz = jnp.concatenate([dca, dcb, dq, dk, dv, dgc, dga], axis=1)
    sent_in, tok = send("w_in", _mm_tn("mm_d_in", h1, dz, N_DEV))
    dh1 = _mm_nt("mm_dh1", dz, wg_in, F32, after=(tok,))
    grad_x, dg_mix = _rms_bwd("rms_mix_bwd", x0, norm_mix, dh1, dx1)

    def pad(a):
        return jnp.pad(a, ((0, 0), (0, d - a.shape[1])))

    small = [dg_mix, dg_ffn, dg_ple, dg_final, dg_ln, db_ln, dg_q, dg_k, dg_dw]
    packed = _all_gather("ag_small", jnp.concatenate([pad(a) for a in small], axis=0))
    parts, row = [], 0
    for a in small:
        parts.append(packed[:, row:row + a.shape[0], :a.shape[1]])
        row += a.shape[0]
    p_mix, p_ffn, p_ple, p_final, p_lng, p_lnb, p_q, p_k, p_dw = parts
    p_dw = lax.dynamic_slice_in_dim(p_dw, me * (cw // N_DEV), cw // N_DEV, axis=2)

    def little(name, part, w, m, v):
        rows, n = (1 if w.ndim < 3 else w.shape[1]), w.shape[-1]
        padded = rows if rows == 1 else part.shape[1]

        def two_d(a):
            return jnp.pad(a.reshape(rows, n), ((0, padded - rows), (0, 0)))

        res = _adamw_small("adamw_" + name, part, two_d(w), two_d(m), two_d(v))
        return [r[:rows].reshape(w.shape) for r in res]

    done = {}
    after = packed
    for name, sent, w, m, v in [
            ("w_ple_proj", sent_pp, w_ple_proj, m_w_ple_proj, v_w_ple_proj),
            ("w_ple_gate", sent_pg, w_ple_gate, m_w_ple_gate, v_w_ple_gate),
            ("w_ff2", sent_ff2, w_ff2, m_w_ff2, v_w_ff2),
            ("w_ff1", sent_ff1, w_ff1, m_w_ff1, v_w_ff1),
            ("w_out", sent_out, w_out, m_w_out, v_w_out),
            ("w_conv_proj", sent_cp, w_conv_proj, m_w_conv_proj, v_w_conv_proj),
            ("w_attn_proj", sent_ap, w_attn_proj, m_w_attn_proj, v_w_attn_proj),
            ("w_in", sent_in, w_in, m_w_in, v_w_in)]:
        land = _exchange_wait("rs_wait_" + name, sent, after)
        res = _adamw_shard("adamw_" + name, land, w[0], m[0], v[0])
        after = res[0]
        done[name] = [r.reshape(w.shape) for r in res]

    results = [
        little("norm_mix", p_mix, norm_mix, m_norm_mix, v_norm_mix),
        done["w_in"],
        little("w_dw", p_dw, w_dw, m_w_dw, v_w_dw),
        little("conv_ln_g", p_lng, conv_ln_g, m_conv_ln_g, v_conv_ln_g),
        little("conv_ln_b", p_lnb, conv_ln_b, m_conv_ln_b, v_conv_ln_b),
        done["w_conv_proj"],
        little("q_norm", p_q, q_norm, m_q_norm, v_q_norm),
        little("k_norm", p_k, k_norm, m_k_norm, v_k_norm),
        done["w_attn_proj"],
        done["w_out"],
        little("norm_ffn", p_ffn, norm_ffn, m_norm_ffn, v_norm_ffn),
        done["w_ff1"],
        done["w_ff2"],
        little("norm_ple", p_ple, norm_ple, m_norm_ple, v_norm_ple),
        done["w_ple_gate"],
        done["w_ple_proj"],
        little("norm_final", p_final, norm_final, m_norm_final, v_norm_final),
    ]
    grads, deltas, new_m, new_v = zip(*results)
    return (loss, grad_x[None], *grads, *deltas, *new_m, *new_v)
```

```python
import functools

import jax
import jax.numpy as jnp
from jax import lax
from jax.experimental import pallas as pl
from jax.experimental.pallas import tpu as pltpu

F32 = jnp.float32
BF16 = jnp.bfloat16

EPS = 1e-6
HEAD_DIM = 128
GROUP = 4
CONV_KERNEL = 31
CONV_HALO = 16
CONV_TAPS_PAD = 32
GRID_W = 64
ROPE_THETA = 10000.0
ADAM_LR, ADAM_B1, ADAM_B2, ADAM_EPS, ADAM_WD, ADAM_STEP = 0.001, 0.9, 0.999, 1e-08, 0.01, 10

N_DEV = 8
MESH = pl.DeviceIdType.MESH
SUBLANES = 8
LANES = 128
VMEM_LIMIT = 48 * 1024 * 1024
ROW_TILE_ELEMS = 512 * 1024


def _div(n, pref, mult=1):
    if n <= pref:
        return n
    for d in range(pref, 0, -1):
        if n % d == 0 and d % mult == 0:
            return d
    return n


def _params(sem):
    return pltpu.CompilerParams(dimension_semantics=sem, vmem_limit_bytes=VMEM_LIMIT)


def _sum8(a):
    tm, w = a.shape
    return a.reshape(tm // SUBLANES, SUBLANES, w).sum(axis=0)


def _sigmoid(x):
    return jax.nn.sigmoid(x)


def _rowwise(name, fn, n_rows, tm, ins, outs, accs=(), ncol=1, refs_to_fn=False, after=()):
    nrow = n_rows // tm
    in_specs, arrays = [], []
    for spec in ins:
        kind, arr, w, c0 = spec[:4]
        if kind == "t":
            cs, r0 = spec[4], spec[5]
            in_specs.append(pl.BlockSpec((tm, w), lambda j, i, c0=c0, cs=cs, r0=r0: (r0 + i, c0 + cs * j)))
        elif kind == "p":
            h = spec[4]
            r = tm // h
            in_specs.append(pl.BlockSpec((h, w), lambda j, i, c0=c0, r=r: (jnp.maximum(i * r - 1, 0), c0 + j)))
        elif kind == "n":
            h = spec[4]
            r = tm // h
            last = n_rows // h - 1
            in_specs.append(
                pl.BlockSpec((h, w), lambda j, i, c0=c0, r=r, last=last: (jnp.minimum((i + 1) * r, last), c0 + j)))
        elif kind == "x":
            in_specs.append(pl.BlockSpec((w, tm), lambda j, i, c0=c0: (c0 + j, i)))
        else:
            cs = spec[4]
            in_specs.append(pl.BlockSpec((arr.shape[0], w), lambda j, i, c0=c0, cs=cs: (0, c0 + cs * j)))
        arrays.append(arr)
    out_shape, out_specs = [], []
    for dtype, total, w in outs:
        out_shape.append(jax.ShapeDtypeStruct((n_rows, total), dtype))
        out_specs.append(pl.BlockSpec((tm, w), lambda j, i: (i, j)))
    for rows, total, w, follow in accs:
        out_shape.append(jax.ShapeDtypeStruct((rows, total), F32))
        out_specs.append(pl.BlockSpec((rows, w), (lambda j, i: (0, j)) if follow else (lambda j, i: (0, 0))))
    n_in, n_out, n_acc = len(ins), len(outs), len(accs)
    for token in after:
        in_specs.append(pl.BlockSpec(memory_space=pl.ANY))
        arrays.append(token)
    out0 = n_in + len(after)

    def body(*refs):
        j, i = pl.program_id(0), pl.program_id(1)
        res = fn(i, nrow, *(refs[:n_in] if refs_to_fn else [r[...] for r in refs[:n_in]]))
        if not isinstance(res, (tuple, list)):
            res = (res,)
        for k in range(n_out):
            refs[out0 + k][...] = res[k].astype(outs[k][0])
        for k in range(n_acc):
            ref, term = refs[out0 + n_out + k], res[n_out + k]
            first = (i == 0) if accs[k][3] else jnp.logical_and(i == 0, j == 0)

            @pl.when(first)
            def _():
                ref[...] = term

            @pl.when(jnp.logical_not(first))
            def _():
                ref[...] += term

    res = pl.pallas_call(
        body, name=name, grid=(ncol, nrow), in_specs=in_specs, out_specs=out_specs, out_shape=out_shape,
        compiler_params=_params(("arbitrary", "arbitrary")))(*arrays)
    return res


def _row_tile(n_rows, width, mult=SUBLANES):
    return _div(n_rows, max(mult, ROW_TILE_ELEMS // width), mult)


def _t(arr, w=None, col0=0, cstride=1, row0=0):
    return ("t", arr, arr.shape[1] if w is None else w, col0, cstride, row0)


def _b(arr, w=None, col0=0, cstride=0):
    return ("b", arr, arr.shape[1] if w is None else w, col0, cstride)


def _rms_fwd(name, x, g, after=()):
    n, d = x.shape

    def fn(i, nrow, x, g):
        r = lax.rsqrt(jnp.mean(x * x, axis=-1, keepdims=True) + EPS)
        return x * r * g

    return _rowwise(name, fn, n, _row_tile(n, d), [_t(x), _b(g)], [(BF16, d, d)], after=after)[0]


def _rms_bwd(name, x, g, dh, dres):
    n, d = x.shape

    def fn(i, nrow, x, g, dh, dres):
        r = lax.rsqrt(jnp.mean(x * x, axis=-1, keepdims=True) + EPS)
        w = dh * g
        dx = dres + r * w - x * (r * r * r) * jnp.mean(x * w, axis=-1, keepdims=True)
        return dx, _sum8(dh * x * r)

    return _rowwise(name, fn, n, _row_tile(n, d), [_t(x), _b(g), _t(dh), _t(dres)], [(F32, d, d)],
                    [(SUBLANES, d, d, True)])


def _swap_halves(x):
    lane = lax.broadcasted_iota(jnp.int32, x.shape, 1)
    return jnp.where(lane % 64 < 32, pltpu.roll(x, 96, axis=1), pltpu.roll(x, 32, axis=1))


def _head_norm_rope(name, z, col0, n_heads, g, cos, sin, out_scale=None, heads_per_step=1):
    n, hp = z.shape[0], heads_per_step
    assert col0 % hp == 0 and n_heads % hp == 0

    def fn(i, nrow, x, g, cos, sin):
        ys = []
        for h in range(hp):
            xh = x[:, h * HEAD_DIM:(h + 1) * HEAD_DIM]
            r = lax.rsqrt(jnp.mean(xh * xh, axis=-1, keepdims=True) + EPS)
            y = xh * r * g
            y = y * cos + _swap_halves(y) * sin
            ys.append(y if out_scale is None else y * out_scale)
        return ys[0] if hp == 1 else jnp.concatenate(ys, axis=1)

    tm = _div(n, 1024, SUBLANES)
    w = hp * HEAD_DIM
    return _rowwise(name, fn, n, tm, [_t(z, w, col0 // hp), _b(g), _t(cos, cstride=0), _t(sin, cstride=0)],
                    [(BF16, n_heads * HEAD_DIM, w)], ncol=n_heads // hp)[0]


def _head_norm_rope_bwd(name, douts, z, col0, n_heads, g, cos, sin, heads_per_step=1, transposed=False):
    n, hp = z.shape[0], heads_per_step
    n_d = len(douts)
    assert col0 % hp == 0 and n_heads % hp == 0 and (hp == 1 or all(c0 % hp == 0 and cs == 1 for _, c0, cs in douts))
    assert not transposed or n_d == 1

    def fn(i, nrow, *a):
        dy = a[0]
        for k in range(1, n_d):
            dy = dy + a[k]
        x, g, cos, sin = a[n_d:]
        dxs, dg = [], None
        for h in range(hp):
            xh = x[:, h * HEAD_DIM:(h + 1) * HEAD_DIM]
            if transposed:
                dyh = jnp.transpose(dy[h * HEAD_DIM:(h + 1) * HEAD_DIM, :])
            else:
                dyh = dy[:, h * HEAD_DIM:(h + 1) * HEAD_DIM]
            dn = dyh * cos + _swap_halves(dyh * sin)
            r = lax.rsqrt(jnp.mean(xh * xh, axis=-1, keepdims=True) + EPS)
            w = dn * g
            dxs.append(r * w - xh * (r * r * r) * jnp.mean(xh * w, axis=-1, keepdims=True))
            term = _sum8(dn * xh * r)
            dg = term if dg is None else dg + term
        return (dxs[0] if hp == 1 else jnp.concatenate(dxs, axis=1)), dg

    tm = _div(n, 1024, SUBLANES)
    w = hp * HEAD_DIM
    if transposed:
        ins = [("x", douts[0][0], w, douts[0][1] // hp)]
    else:
        ins = [_t(arr, w, c0 // hp, cs) for arr, c0, cs in douts]
    ins += [_t(z, w, col0 // hp), _b(g), _t(cos, cstride=0), _t(sin, cstride=0)]
    return _rowwise(name, fn, n, tm, ins, [(BF16, n_heads * HEAD_DIM, w)],
                    [(SUBLANES, HEAD_DIM, HEAD_DIM, False)], ncol=n_heads // hp)


def _halo(arr, w, col0):
    return [("p", arr, w, col0, CONV_HALO), _t(arr, w, col0), ("n", arr, w, col0, CONV_HALO)]


def _extend(i, nrow, prev, cur, nxt):
    prev = jnp.where(i > 0, prev, 0.0)
    nxt = jnp.where(i < nrow - 1, nxt, 0.0)
    return jnp.concatenate([prev, cur, nxt], axis=0)


def _shifted(ext, offset, tm):
    n = ext.shape[0]
    rolled = ext if offset == 0 else pltpu.roll(ext, (-offset) % n, axis=0)
    return rolled[CONV_HALO:CONV_HALO + tm]


def _conv_fwd(name, z, cw, w_dw, ln_g, ln_b):
    n = z.shape[0]
    tm = _row_tile(n, 4 * cw, CONV_HALO)

    def fn(i, nrow, ap, a, an, bp, b, bn, w, g, beta):
        a = a[...]
        ext = _extend(i, nrow, ap[...] * _sigmoid(bp[...]), a * _sigmoid(b[...]), an[...] * _sigmoid(bn[...]))
        conv = jnp.zeros_like(a)
        for k in range(CONV_KERNEL):
            conv = conv + _shifted(ext, k - CONV_KERNEL // 2, tm) * w[k:k + 1, :]
        xc = conv - jnp.mean(conv, axis=-1, keepdims=True)
        ln = xc * lax.rsqrt(jnp.mean(xc * xc, axis=-1, keepdims=True) + EPS) * g[...] + beta[...]
        return conv, ln * _sigmoid(ln)

    ins = _halo(z, cw, 0) + _halo(z, cw, 1) + [_b(w_dw), _b(ln_g), _b(ln_b)]
    return _rowwise(name, fn, n, tm, ins, [(F32, cw, cw), (BF16, cw, cw)], refs_to_fn=True)


def _conv_ln_bwd(name, ds, conv, ln_g, ln_b):
    n, cw = conv.shape

    def fn(i, nrow, ds, conv, g, beta):
        xc = conv - jnp.mean(conv, axis=-1, keepdims=True)
        rstd = lax.rsqrt(jnp.mean(xc * xc, axis=-1, keepdims=True) + EPS)
        xhat = xc * rstd
        ln = xhat * g + beta
        sg = _sigmoid(ln)
        dln = ds * (sg * (1.0 + ln * (1.0 - sg)))
        dxh = dln * g
        dconv = rstd * (dxh - jnp.mean(dxh, axis=-1, keepdims=True)
                        - xhat * jnp.mean(dxh * xhat, axis=-1, keepdims=True))
        return dconv, _sum8(dln * xhat), _sum8(dln)

    return _rowwise(name, fn, n, _row_tile(n, 2 * cw), [_t(ds), _t(conv), _b(ln_g), _b(ln_b)], [(F32, cw, cw)],
                    [(SUBLANES, cw, cw, True), (SUBLANES, cw, cw, True)])


def _conv_glu_bwd(name, dconv, z, cw, w_dw):
    n = z.shape[0]
    tm = _row_tile(n, 4 * cw, CONV_HALO)

    def fn(i, nrow, dp, d, dn, a, b, w):
        d, a = d[...], a[...]
        dext = _extend(i, nrow, dp[...], d, dn[...])
        sg = _sigmoid(b[...])
        u = a * sg
        tap = lax.broadcasted_iota(jnp.int32, (CONV_TAPS_PAD, cw), 0)
        du = jnp.zeros_like(d)
        dw = jnp.zeros((CONV_TAPS_PAD, cw), F32)
        for k in range(CONV_KERNEL):
            d_k = _shifted(dext, -(k - CONV_KERNEL // 2), tm)
            du = du + d_k * w[k:k + 1, :]
            dw = dw + jnp.where(tap == k, jnp.sum(d_k * u, axis=0, keepdims=True), 0.0)
        return du * sg, du * a * sg * (1.0 - sg), dw

    ins = _halo(dconv, cw, 0) + [_t(z, cw, 0), _t(z, cw, 1), _b(w_dw)]
    return _rowwise(name, fn, n, tm, ins, [(BF16, cw, cw), (BF16, cw, cw)], [(CONV_TAPS_PAD, cw, cw, True)],
                    refs_to_fn=True)


def _gated_merge(y_a, gc, ga, y_c):
    return _sigmoid(gc) * y_c + _sigmoid(ga) * y_a


def _gated_merge_bwd(dm, gc, ga, yc, ya):
    sc, sa = _sigmoid(gc), _sigmoid(ga)
    return dm * yc * sc * (1.0 - sc), dm * ya * sa * (1.0 - sa), dm * sc, dm * sa


def _relu2(f):
    r = jnp.maximum(f, 0.0)
    return r * r


def _relu2_bwd(da, f):
    return da * (2.0 * jnp.maximum(f, 0.0))


def _head_loss(name, x2, pre, pp, target, g):
    n, d = x2.shape

    def fn(i, nrow, x2, pre, pp, t, g):
        gate = _sigmoid(pre)
        x3 = x2 + gate * pp
        r = lax.rsqrt(jnp.mean(x3 * x3, axis=-1, keepdims=True) + EPS)
        xn = x3 * r
        e = xn * g - t
        part = 0.5 * jnp.sum(jnp.mean(e * e, axis=-1, keepdims=True), axis=0, keepdims=True)
        dy = e * (1.0 / d)
        w = dy * g
        dx3 = r * w - x3 * (r * r * r) * jnp.mean(x3 * w, axis=-1, keepdims=True)
        return (dx3, dx3 * gate, dx3 * pp * gate * (1.0 - gate), _sum8(dy * xn),
                jnp.broadcast_to(part, (SUBLANES, LANES)))

    ins = [_t(x2), _t(pre), _t(pp), _t(target), _b(g)]
    return _rowwise(name, fn, n, _row_tile(n, 2 * d), ins, [(F32, d, d), (BF16, d, d), (BF16, d, d)],
                    [(SUBLANES, d, d, True), (SUBLANES, LANES, LANES, True)])


def _group_sum(name, dvh, n_kv):
    n = dvh.shape[0]

    def fn(i, nrow, *a):
        s = a[0]
        for k in range(1, GROUP):
            s = s + a[k]
        return s

    ins = [_t(dvh, HEAD_DIM, k, GROUP) for k in range(GROUP)]
    return _rowwise(name, fn, n, _div(n, 1024, SUBLANES), ins, [(BF16, n_kv * HEAD_DIM, HEAD_DIM)], ncol=n_kv)[0]


MM_ROWS = 1024
MM_COLS = 1152
MM_REDUCE = 2048
MM_REDUCE_BLOCKS = 2304


def _rows_outermost(n_rows, n_cols, row_tile_bytes, weight_tile_bytes):
    rows_outer = n_rows * (row_tile_bytes + n_cols * weight_tile_bytes)
    weight_outer = n_cols * (weight_tile_bytes + n_rows * row_tile_bytes)
    return 5 * rows_outer < 4 * weight_outer


def _accumulate(step, n_steps, part, acc, finish):
    if n_steps == 1:
        finish(part)
        return

    @pl.when(step == 0)
    def _():
        acc[...] = part

    @pl.when(jnp.logical_and(step > 0, step < n_steps - 1))
    def _():
        acc[...] += part

    @pl.when(step == n_steps - 1)
    def _():
        finish(acc[...] + part)


MXU_WIDTH = 256


def _paired(nb, ns):
    return ns % MXU_WIDTH != 0 and nb % 2 == 0 and ns <= MM_COLS


def _mm_nn(name, a, w3, out_dtype, residual=None, activation=None, beside=(), col_tile=None):
    m, k = a.shape
    nb, _, ns = w3.shape
    jb = 2 if _paired(nb, ns) else 1
    tm = _div(m, MM_ROWS // max(jb, 1 + len(beside) // 2), SUBLANES)
    tn_w, tk = (ns if jb > 1 else _div(ns, col_tile or MM_COLS, LANES)), _div(k, MM_REDUCE, LANES)
    tn = jb * tn_w
    per, nk = ns // tn_w, k // tk
    has_res = residual is not None
    n_out = 1 if activation is None else 2
    first_beside = 3 if has_res else 2

    def body(*refs):
        a_ref, w_ref = refs[:2]
        acc = refs[-1]
        outs = refs[-1 - n_out:-1]
        w = w_ref[0] if jb == 1 else jnp.concatenate([w_ref[j] for j in range(jb)], axis=1)
        part = jnp.dot(a_ref[...].astype(BF16), w.astype(BF16), preferred_element_type=F32)

        def finish(total):
            if has_res:
                total = total + refs[2][...]
            outs[0][...] = total.astype(out_dtype)
            if activation is not None:
                tiles = [r[...] for r in refs[first_beside:first_beside + len(beside)]]
                outs[1][...] = activation(total, *tiles).astype(BF16)

        _accumulate(pl.program_id(2), nk, part, acc, finish)

    n_rows, n_cols = m // tm, (nb // jb) * per
    rows_outer = nk > 1 or _rows_outermost(n_rows, n_cols, tm * tk * a.dtype.itemsize,
                                           jb * tk * tn_w * w3.dtype.itemsize)

    def at(f):
        return f if rows_outer else (lambda n, i, kk: f(i, n, kk))

    in_specs = [pl.BlockSpec((tm, tk), at(lambda i, n, kk: (i, kk))),
                pl.BlockSpec((jb, tk, tn_w), at(lambda i, n, kk: (n // per, kk, n % per)))]
    args = [a, w3]
    if has_res:
        in_specs.append(pl.BlockSpec((tm, tn), at(lambda i, n, kk: (i, n))))
        args.append(residual)
    for arr, col0 in beside:
        in_specs.append(pl.BlockSpec((tm, tn), at(lambda i, n, kk, col0=col0: (i, col0 + n))))
        args.append(arr)
    out_dtypes = [out_dtype] + ([BF16] if activation is not None else [])
    res = pl.pallas_call(
        body, name=name, grid=(n_rows, n_cols, nk) if rows_outer else (n_cols, n_rows, nk), in_specs=in_specs,
        out_specs=[pl.BlockSpec((tm, tn), at(lambda i, n, kk: (i, n))) for _ in out_dtypes],
        out_shape=[jax.ShapeDtypeStruct((m, nb * ns), dt) for dt in out_dtypes],
        scratch_shapes=[pltpu.VMEM((tm, tn) if nk > 1 else (SUBLANES, LANES), F32)],
        compiler_params=_params(("parallel", "parallel", "arbitrary")))(*args)
    return res[0] if activation is None else res


def _mm_nt(name, dc, w3, out_dtype, after=(), through=None, beside=(), n_results=1, head_dots=None, col_tile=None):
    m = dc.shape[0]
    nb, k, ns = w3.shape
    tm = _div(m, MM_ROWS // (1 + len(beside) // 3), SUBLANES)
    to = _div(k, col_tile or MM_ROWS, LANES)
    if ns > MM_REDUCE_BLOCKS // 2:
        jb, tr = 1, _div(ns, MM_REDUCE, LANES)
    else:
        jb, tr = max(j for j in range(1, nb + 1) if nb % j == 0 and j * ns <= MM_REDUCE_BLOCKS), ns
    per = ns // tr
    nr = (nb // jb) * per
    tiles_in = list(beside) + ([(head_dots, 0)] if head_dots is not None else [])
    n_out = n_results + (1 if head_dots is not None else 0)
    heads = to // HEAD_DIM

    def body(dc_ref, w_ref, *rest):
        acc = rest[-1]
        outs = rest[-1 - n_out:-1]
        w = w_ref[0] if jb == 1 else jnp.concatenate([w_ref[j] for j in range(jb)], axis=1)
        part = lax.dot_general(dc_ref[...].astype(BF16), w.astype(BF16), (((1,), (1,)), ((), ())),
                               preferred_element_type=F32)

        def finish(total):
            results = (total,) if through is None else through(total, *[r[...] for r in rest[:len(beside)]])
            if not isinstance(results, (tuple, list)):
                results = (results,)
            for out, res in zip(outs, results):
                out[...] = res.astype(out_dtype)
            if head_dots is not None:
                prod = results[0].astype(out_dtype).astype(F32) * rest[len(beside)][...]
                for h in range(heads):
                    row = jnp.sum(jnp.transpose(prod[:, h * HEAD_DIM:(h + 1) * HEAD_DIM]), axis=0, keepdims=True)
                    outs[n_results][h] = jnp.broadcast_to(row, (SUBLANES, tm))

        _accumulate(pl.program_id(2), nr, part, acc, finish)

    rows_outer = nr > 1 or _rows_outermost(m // tm, k // to, tm * jb * tr * dc.dtype.itemsize,
                                           jb * to * tr * w3.dtype.itemsize)

    def at(f):
        return f if rows_outer else (lambda o, i, r: f(i, o, r))

    in_specs = [pl.BlockSpec((tm, jb * tr), at(lambda i, o, r: (i, r))),
                pl.BlockSpec((jb, to, tr), at(lambda i, o, r: (r // per, o, r % per)))]
    args = [dc, w3]
    for arr, col0 in tiles_in:
        in_specs.append(pl.BlockSpec((tm, to), at(lambda i, o, r, col0=col0: (i, col0 + o))))
        args.append(arr)
    for token in after:
        in_specs.append(pl.BlockSpec(memory_space=pl.ANY))
        args.append(token)
    out_specs = [pl.BlockSpec((tm, to), at(lambda i, o, r: (i, o))) for _ in range(n_results)]
    out_shape = [jax.ShapeDtypeStruct((m, k), out_dtype) for _ in range(n_results)]
    if head_dots is not None:
        out_specs.append(pl.BlockSpec((heads, SUBLANES, tm), at(lambda i, o, r: (o, 0, i))))
        out_shape.append(jax.ShapeDtypeStruct((k // HEAD_DIM, SUBLANES, m), F32))
    res = pl.pallas_call(
        body, name=name, grid=(m // tm, k // to, nr) if rows_outer else (k // to, m // tm, nr), in_specs=in_specs,
        out_specs=out_specs, out_shape=out_shape,
        scratch_shapes=[pltpu.VMEM((tm, to) if nr > 1 else (SUBLANES, LANES), F32)],
        compiler_params=_params(("parallel", "parallel", "arbitrary")))(*args)
    return res[0] if n_out == 1 else res


def _mm_tn(name, a, dc, nb):
    m, k = a.shape
    ns = dc.shape[1] // nb
    jb = 2 if _paired(nb, ns) else 1
    tr, to = _div(m, MM_REDUCE, LANES), _div(k, MM_ROWS // jb, LANES)
    tn_w = ns if jb > 1 else _div(ns, MM_COLS, LANES)
    tn = jb * tn_w
    per, nr = ns // tn_w, m // tr

    def body(a_ref, dc_ref, o_ref, acc):
        part = lax.dot_general(a_ref[...].astype(BF16), dc_ref[...].astype(BF16), (((0,), (0,)), ((), ())),
                               preferred_element_type=F32)

        def finish(total):
            for j in range(jb):
                o_ref[j] = total[:, j * tn_w:(j + 1) * tn_w].astype(BF16)

        _accumulate(pl.program_id(2), nr, part, acc, finish)

    return pl.pallas_call(
        body, name=name, grid=(k // to, (nb // jb) * per, nr),
        in_specs=[pl.BlockSpec((tr, to), lambda o, n, r: (r, o)),
                  pl.BlockSpec((tr, tn), lambda o, n, r: (r, n))],
        out_specs=pl.BlockSpec((jb, to, tn_w), lambda o, n, r: (n // per, o, n % per)),
        out_shape=jax.ShapeDtypeStruct((nb, k, ns), BF16),
        scratch_shapes=[pltpu.VMEM((to, tn) if nr > 1 else (SUBLANES, LANES), F32)],
        compiler_params=_params(("parallel", "parallel", "arbitrary")))(a, dc)


QK_SCALE = HEAD_DIM ** -0.5
LOG2_E = 1.4426950408889634
LN_2 = 0.6931471805599453


def _columns(row, n):
    return jnp.transpose(jnp.broadcast_to(row, (LANES, n)))


FLASH_TILE = 1024
ONES_ROWS = 16


def _flash_fwd(qr, kr, vt, n_heads):
    t = qr.shape[0]
    tq, tk = _div(t, 2 * FLASH_TILE, LANES), _div(t, FLASH_TILE, LANES)
    nk = t // tk
    rows = vt.shape[0] // (n_heads // GROUP)

    def body(q_ref, k_ref, vt_ref, o_ref, lse_ref, m_scr, acc_scr):
        kb = pl.program_id(2)

        @pl.when(kb == 0)
        def _():
            m_scr[...] = jnp.full_like(m_scr, -1e30)
            acc_scr[...] = jnp.zeros_like(acc_scr)

        s_t = lax.dot_general(k_ref[...], q_ref[...], (((1,), (1,)), ((), ())), preferred_element_type=F32)
        m_prev = m_scr[...]
        m_new = jnp.maximum(m_prev, jnp.max(s_t, axis=0, keepdims=True))
        alpha = jnp.exp2(m_prev - m_new)
        p_t = jnp.exp2(s_t - m_new).astype(BF16)
        acc_scr[...] = alpha * acc_scr[...] + jnp.dot(vt_ref[...], p_t, preferred_element_type=F32)
        m_scr[...] = m_new

        @pl.when(kb == nk - 1)
        def _():
            l = acc_scr[HEAD_DIM:HEAD_DIM + 1, :]
            o_ref[...] = jnp.transpose(acc_scr[0:HEAD_DIM, :] / l)
            lse_ref[...] = jnp.broadcast_to(m_scr[...] + jnp.log(l) * LOG2_E, (SUBLANES, tq))

    return pl.pallas_call(
        body, name="flash_fwd", grid=(n_heads, t // tq, nk),
        in_specs=[pl.BlockSpec((tq, HEAD_DIM), lambda h, qi, ki: (qi, h)),
                  pl.BlockSpec((tk, HEAD_DIM), lambda h, qi, ki: (ki, h // GROUP)),
                  pl.BlockSpec((rows, tk), lambda h, qi, ki: (h // GROUP, ki))],
        out_specs=[pl.BlockSpec((tq, HEAD_DIM), lambda h, qi, ki: (qi, h)),
                   pl.BlockSpec((None, SUBLANES, tq), lambda h, qi, ki: (h, 0, qi))],
        out_shape=[jax.ShapeDtypeStruct((t, n_heads * HEAD_DIM), F32),
                   jax.ShapeDtypeStruct((n_heads, SUBLANES, t), F32)],
        scratch_shapes=[pltpu.VMEM((1, tq), F32), pltpu.VMEM((rows, tq), F32)],
        compiler_params=_params(("parallel", "parallel", "arbitrary")))(qr, kr, vt)


def _flash_bwd(qr, kr, kt, z, v_col0, do, lse, delta, n_heads):
    t = qr.shape[0]
    tq, tk = _div(t, 2 * FLASH_TILE, LANES), _div(t, FLASH_TILE, LANES)
    nq = t // tq

    def body(q_ref, k_ref, kt_ref, v_ref, do_ref, lse_ref, dl_ref, dq_ref, dk_ref, dv_ref, dk_acc, dv_acc):
        kb, qb = pl.program_id(1), pl.program_id(2)
        q, k, do_ = q_ref[...], k_ref[...], do_ref[...]
        v = v_ref[...].astype(BF16)
        s_t = lax.dot_general(k, q, (((1,), (1,)), ((), ())), preferred_element_type=F32)
        p_t = jnp.exp2(s_t - lse_ref[0:1, :])
        dp_t = lax.dot_general(v, do_, (((1,), (1,)), ((), ())), preferred_element_type=F32)
        ds_t = (p_t * (dp_t - dl_ref[0:1, :])).astype(BF16)
        dv_c = jnp.dot(p_t.astype(BF16), do_, preferred_element_type=F32)
        dk_c = jnp.dot(ds_t, q, preferred_element_type=F32) * LN_2
        dq_c = jnp.dot(kt_ref[...], ds_t, preferred_element_type=F32) * QK_SCALE

        @pl.when(qb == 0)
        def _():
            dk_acc[...] = dk_c
            dv_acc[...] = dv_c

        @pl.when(qb > 0)
        def _():
            dk_acc[...] += dk_c
            dv_acc[...] += dv_c

        @pl.when(qb == nq - 1)
        def _():
            dk_ref[...] = dk_acc[...]
            dv_ref[...] = dv_acc[...]

        cols = pl.ds(pl.multiple_of(qb * tq, tq), tq)

        @pl.when(kb == 0)
        def _():
            dq_ref[:, cols] = dq_c

        @pl.when(kb > 0)
        def _():
            dq_ref[:, cols] += dq_c

    wide = jax.ShapeDtypeStruct((t, n_heads * HEAD_DIM), F32)
    return pl.pallas_call(
        body, name="flash_bwd", grid=(n_heads, t // tk, nq),
        in_specs=[pl.BlockSpec((tq, HEAD_DIM), lambda h, kb, qb: (qb, h)),
                  pl.BlockSpec((tk, HEAD_DIM), lambda h, kb, qb: (kb, h // GROUP)),
                  pl.BlockSpec((HEAD_DIM, tk), lambda h, kb, qb: (h // GROUP, kb)),
                  pl.BlockSpec((tk, HEAD_DIM), lambda h, kb, qb: (kb, v_col0 + h // GROUP)),
                  pl.BlockSpec((tq, HEAD_DIM), lambda h, kb, qb: (qb, h)),
                  pl.BlockSpec((None, SUBLANES, tq), lambda h, kb, qb: (h, 0, qb)),
                  pl.BlockSpec((None, SUBLANES, tq), lambda h, kb, qb: (h, 0, qb))],
        out_specs=[pl.BlockSpec((HEAD_DIM, t), lambda h, kb, qb: (h, 0)),
                   pl.BlockSpec((tk, HEAD_DIM), lambda h, kb, qb: (kb, h)),
                   pl.BlockSpec((tk, HEAD_DIM), lambda h, kb, qb: (kb, h))],
        out_shape=[jax.ShapeDtypeStruct((n_heads * HEAD_DIM, t), F32), wide, wide],
        scratch_shapes=[pltpu.VMEM((tk, HEAD_DIM), F32), pltpu.VMEM((tk, HEAD_DIM), F32)],
        compiler_params=_params(("parallel", "arbitrary", "arbitrary")))(qr, kr, kt, z, do, lse, delta)


def _position():
    x, y, c = lax.axis_index("x"), lax.axis_index("y"), lax.axis_index("c")
    return x, y, c


def _index(x, y, c):
    return 4 * x + 2 * y + c


def _all_gather(name, shard):
    a, b = shard.shape

    def body(x_ref, out_ref, send_sems, recv_sems, local_sem):
        x, y, c = _position()
        me, sibling = (x, y, c), (x, y, 1 - c)
        chips = [(1 - x, y), (x, 1 - y), (1 - x, 1 - y)]

        def block(px, py, pc):
            return out_ref.at[_index(px, py, pc)]

        def copy(k, blk, to, src=None):
            return pltpu.make_async_remote_copy(
                src_ref=block(*blk) if src is None else src, dst_ref=block(*blk),
                send_sem=send_sems.at[k], recv_sem=recv_sems.at[k], device_id=to, device_id_type=MESH)

        mine = pltpu.make_async_copy(x_ref, block(*me), local_sem)
        mine.start()
        first = [copy(0, me, sibling, src=x_ref)]
        first += [copy(1 + j, me, (*chip, c), src=x_ref) for j, chip in enumerate(chips)]
        for cp in first:
            cp.start()
        passed = [copy(4 + j, (*chip, c), sibling) for j, chip in enumerate(chips)]
        for j, chip in enumerate(chips):
            copy(1 + j, (*chip, c), me).wait_recv()
            passed[j].start()
        copy(0, sibling, me).wait_recv()
        for j, chip in enumerate(chips):
            copy(4 + j, (*chip, 1 - c), me).wait_recv()
        for cp in first + passed:
            cp.wait_send()
        mine.wait()

    return pl.pallas_call(
        body, name=name, out_shape=jax.ShapeDtypeStruct((N_DEV, a, b), shard.dtype),
        in_specs=[pl.BlockSpec(memory_space=pltpu.HBM)], out_specs=pl.BlockSpec(memory_space=pltpu.HBM),
        scratch_shapes=[pltpu.SemaphoreType.DMA((7,)), pltpu.SemaphoreType.DMA((7,)), pltpu.SemaphoreType.DMA],
    )(shard)


HBM_SPEC = pl.BlockSpec(memory_space=pltpu.HBM)
SEM_SPEC = pl.BlockSpec(memory_space=pltpu.SEMAPHORE)
N_PEERS = N_DEV - 1
PEER_FLIPS = [(fx, fy, fc) for fx in (0, 1) for fy in (0, 1) for fc in (0, 1)][1:]


def _peers(x, y, c):
    return [((1 - x) if fx else x, (1 - y) if fy else y, (1 - c) if fc else c) for fx, fy, fc in PEER_FLIPS]


def _exchange_copies(g_ref, land_ref, send_sems, recv_sems):
    x, y, c = _position()
    me = _index(x, y, c)
    peers = _peers(x, y, c)
    blocked = len(g_ref.shape) == 3

    def copy(k, src_block, dst_block):
        return pltpu.make_async_remote_copy(
            src_ref=g_ref.at[src_block] if blocked else g_ref, dst_ref=land_ref.at[dst_block],
            send_sem=send_sems.at[k], recv_sem=recv_sems.at[k], device_id=peers[k], device_id_type=MESH)

    sent = [copy(k, _index(*peer), me) for k, peer in enumerate(peers)]
    arriving = [copy(k, me, _index(*peer)) for k, peer in enumerate(peers)]
    own = pltpu.make_async_copy(g_ref.at[me] if blocked else g_ref, land_ref.at[me], send_sems.at[N_PEERS])
    return sent, arriving, own


def _exchange_start(name, g3, after=()):
    def body(g_ref, land_ref, *rest):
        send_sems, recv_sems, _, _, token = rest[len(after):]
        sent, _, own = _exchange_copies(g_ref, land_ref, send_sems, recv_sems)
        for cp in sent:
            cp.start()
        own.start()
        token[...] = jnp.zeros_like(token)

    land_shape = g3.shape if g3.ndim == 3 else (N_DEV,) + g3.shape
    land = lax.empty(land_shape, g3.dtype)
    send_sems, recv_sems, g_thru, land_thru, token = pl.pallas_call(
        body, name=name,
        out_shape=(pltpu.SemaphoreType.DMA((N_PEERS + 1,)), pltpu.SemaphoreType.DMA((N_PEERS,)),
                   pltpu.HBM(g3.shape, g3.dtype), pltpu.HBM(land_shape, g3.dtype),
                   jax.ShapeDtypeStruct((SUBLANES, LANES), F32)),
        in_specs=(HBM_SPEC, HBM_SPEC) + (pl.BlockSpec(memory_space=pl.ANY),) * len(after),
        out_specs=(SEM_SPEC, SEM_SPEC, HBM_SPEC, HBM_SPEC, pl.BlockSpec(memory_space=pltpu.VMEM)),
        input_output_aliases={0: 2, 1: 3},
        compiler_params=pltpu.CompilerParams(has_side_effects=pltpu.SideEffectType.DATAFLOW_SIDE_EFFECTING),
    )(pltpu.with_memory_space_constraint(g3, pltpu.HBM), pltpu.with_memory_space_constraint(land, pltpu.HBM), *after)
    return (send_sems, recv_sems, g_thru, land_thru), token


def _exchange_wait(name, handle, after):
    send_sems, recv_sems, g_thru, land_thru = handle

    def body(g_ref, land_ref, send_sems, recv_sems, after_ref, g_dead, land_out):
        sent, arriving, own = _exchange_copies(g_ref, land_ref, send_sems, recv_sems)
        for cp in sent:
            cp.wait_send()
        for cp in arriving:
            cp.wait_recv()
        own.wait()

    return pl.pallas_call(
        body, name=name,
        out_shape=(pltpu.HBM(g_thru.shape, g_thru.dtype), pltpu.HBM(land_thru.shape, land_thru.dtype)),
        in_specs=(HBM_SPEC, HBM_SPEC, SEM_SPEC, SEM_SPEC, pl.BlockSpec(memory_space=pl.ANY)),
        out_specs=(HBM_SPEC, HBM_SPEC), input_output_aliases={0: 0, 1: 1},
        compiler_params=pltpu.CompilerParams(has_side_effects=pltpu.SideEffectType.DATAFLOW_SIDE_EFFECTING),
    )(g_thru, land_thru, send_sems, recv_sems, after)[1]


def _adamw(w, g, m, v):
    m = ADAM_B1 * m + (1.0 - ADAM_B1) * g
    v = ADAM_B2 * v + (1.0 - ADAM_B2) * (g * g)
    m_hat = m / (1.0 - ADAM_B1 ** ADAM_STEP)
    v_hat = v / (1.0 - ADAM_B2 ** ADAM_STEP)
    delta = -ADAM_LR * (m_hat / (jnp.sqrt(v_hat) + ADAM_EPS) + ADAM_WD * w)
    return delta, m, v


def _adamw_shard(name, land, w, m, v):
    _, a, b = land.shape
    tm = _div(a, max(16, ROW_TILE_ELEMS // (2 * b)), 16)
    per = a // tm

    def fn(i, nrow, *t):
        g = t[0].astype(F32)
        for s in range(1, N_DEV):
            g = g + t[s].astype(F32)
        w, m, v = t[N_DEV:]
        return (g,) + _adamw(w, g, m, v)

    land2 = land.reshape(N_DEV * a, b)
    ins = [_t(land2, row0=s * per) for s in range(N_DEV)] + [_t(w), _t(m), _t(v)]
    return _rowwise(name, fn, a, tm, ins, [(F32, b, b)] * 4)


def _adamw_small(name, parts, w, m, v):
    rows, n = w.shape

    def body(p_ref, w_ref, m_ref, v_ref, g_out, d_out, m_out, v_out):
        g = p_ref[0]
        for s in range(1, parts.shape[0]):
            g = g + p_ref[s]
        g = jnp.sum(g, axis=0, keepdims=True) if rows == 1 else g[0:rows, :]
        delta, m_new, v_new = _adamw(w_ref[...], g, m_ref[...], v_ref[...])
        g_out[...] = g
        d_out[...] = delta
        m_out[...] = m_new
        v_out[...] = v_new

    return pl.pallas_call(body, name=name, out_shape=[jax.ShapeDtypeStruct((rows, n), F32)] * 4)(parts, w, m, v)


def _rope_tables(t):
    half = HEAD_DIM // 4
    n_rows = t // GRID_W
    inv_freq = ROPE_THETA ** (-jnp.arange(0, 2 * half, 2, dtype=F32) / (2 * half))
    ang_r = jnp.arange(n_rows, dtype=jnp.int32).astype(F32)[:, None] * inv_freq[None, :]
    ang_c = jnp.arange(GRID_W, dtype=jnp.int32).astype(F32)[:, None] * inv_freq[None, :]

    def by_row(a):
        return jnp.repeat(a, GRID_W, axis=0)

    def by_col(a):
        return jnp.tile(a, (n_rows, 1))

    cos_r, sin_r, cos_c, sin_c = by_row(jnp.cos(ang_r)), by_row(jnp.sin(ang_r)), by_col(jnp.cos(ang_c)), by_col(jnp.sin(ang_c))
    cos = jnp.concatenate([cos_r, cos_r, cos_c, cos_c], axis=-1)
    sin = jnp.concatenate([-sin_r, sin_r, -sin_c, sin_c], axis=-1)
    return cos, sin


def _gather_weight(name, w, cols):
    g = _all_gather(name, w[0].astype(BF16))
    return g if cols else g.reshape(1, N_DEV * g.shape[1], g.shape[2])


def kernel(x, p, norm_mix, w_in, w_dw, conv_ln_g, conv_ln_b, w_conv_proj, q_norm, k_norm, w_attn_proj, w_out, norm_ffn, w_ff1, w_ff2, norm_ple, w_ple_gate, w_ple_proj, norm_final, loss_target, m_norm_mix, m_w_in, m_w_dw, m_conv_ln_g, m_conv_ln_b, m_w_conv_proj, m_q_norm, m_k_norm, m_w_attn_proj, m_w_out, m_norm_ffn, m_w_ff1, m_w_ff2, m_norm_ple, m_w_ple_gate, m_w_ple_proj, m_norm_final, v_norm_mix, v_w_in, v_w_dw, v_conv_ln_g, v_conv_ln_b, v_w_conv_proj, v_q_norm, v_k_norm, v_w_attn_proj, v_w_out, v_norm_ffn, v_w_ff1, v_w_ff2, v_norm_ple, v_w_ple_gate, v_w_ple_proj, v_norm_final):
    t, d = x.shape[1], x.shape[2]
    cw = d // 2
    n_heads = d // HEAD_DIM
    n_kv = n_heads // GROUP
    col_q, col_k, col_v = d // HEAD_DIM, 2 * d // HEAD_DIM, (2 * d + d // 4) // HEAD_DIM
    x0, pe, tgt = x[0], p[0, 0], loss_target[0]
    g_final = norm_final.reshape(1, d)
    me = _index(*_position())

    wg_in = _gather_weight("ag_w_in", w_in, True)
    dw_pad = jnp.pad(w_dw[0], ((0, CONV_TAPS_PAD - CONV_KERNEL), (0, 0)))
    dw_all = _all_gather("ag_w_dw", dw_pad)
    dw_full = dw_all.transpose(1, 0, 2).reshape(CONV_TAPS_PAD, cw)
    cos, sin = _rope_tables(t)

    def fetch(name, w, prev):
        return _exchange_start("ag_start_" + name, w[0].astype(BF16), after=(wg_in, dw_all) if prev is None else (prev,))

    def arrived(name, handle, cols, after):
        g = _exchange_wait("ag_wait_" + name, handle, after)
        return g if cols else g.reshape(1, N_DEV * g.shape[1], g.shape[2])

    on_cp, tok = fetch("w_conv_proj", w_conv_proj, None)
    on_ap, tok = fetch("w_attn_proj", w_attn_proj, tok)
    on_out, tok = fetch("w_out", w_out, tok)
    on_ff1, tok = fetch("w_ff1", w_ff1, tok)
    on_ff2, tok = fetch("w_ff2", w_ff2, tok)
    on_pg, tok = fetch("w_ple_gate", w_ple_gate, tok)
    on_pp, tok = fetch("w_ple_proj", w_ple_proj, tok)

    h1 = _rms_fwd("rms_mix", x0, norm_mix, after=(tok,))
    z = _mm_nn("mm_in", h1, wg_in, F32)
    conv, s_c = _conv_fwd("conv_fwd", z, cw, dw_full, conv_ln_g, conv_ln_b)
    wg_cp = arrived("w_conv_proj", on_cp, True, s_c)
    y_c = _mm_nn("mm_conv_proj", s_c, wg_cp, F32)
    qr = _head_norm_rope("q_prep", z, col_q, n_heads, q_norm, cos, sin, out_scale=QK_SCALE * LOG2_E,
                         heads_per_step=GROUP)
    kr = _head_norm_rope("k_prep", z, col_k, n_kv, k_norm, cos, sin)
    vt = z[:, col_v * HEAD_DIM:(col_v + n_kv) * HEAD_DIM].T.astype(BF16).reshape(n_kv, HEAD_DIM, t)
    vt = jnp.concatenate([vt, jnp.ones((n_kv, ONES_ROWS, t), BF16)], axis=1).reshape(n_kv * (HEAD_DIM + ONES_ROWS), t)
    o, lse = _flash_fwd(qr, kr, vt, n_heads)
    wg_ap = arrived("w_attn_proj", on_ap, False, o)
    y_a, merged = _mm_nn("mm_attn_proj", o, wg_ap, F32, activation=_gated_merge,
                         beside=[(z, 5), (z, 7), (y_c, 0)], col_tile=d // 2)
    wg_out = arrived("w_out", on_out, False, merged)
    x1 = _mm_nn("mm_out", merged, wg_out, F32, residual=x0)
    h2 = _rms_fwd("rms_ffn", x1, norm_ffn)
    wg_ff1 = arrived("w_ff1", on_ff1, True, h2)
    f, act = _mm_nn("mm_ff1", h2, wg_ff1, F32, activation=_relu2)
    wg_ff2 = arrived("w_ff2", on_ff2, False, act)
    x2 = _mm_nn("mm_ff2", act, wg_ff2, F32, residual=x1)
    hp = _rms_fwd("rms_ple", x2, norm_ple)
    wg_pg = arrived("w_ple_gate", on_pg, False, hp)
    pre = _mm_nn("mm_ple_gate", hp, wg_pg, F32)
    wg_pp = arrived("w_ple_proj", on_pp, True, pre)
    pp = _mm_nn("mm_ple_proj", pe, wg_pp, F32)

    dx3, dpp, dpre, dg_final, loss_part = _head_loss("head_loss", x2, pre, pp, tgt, g_final)
    loss = lax.psum(loss_part[0, 0], ("x", "y", "c"))
    def send(name, gw):
        blocks = gw if gw.shape[0] == N_DEV else gw.reshape(N_DEV, gw.shape[1] // N_DEV, gw.shape[2])
        return _exchange_start("rs_start_" + name, blocks)

    sent_pp, tok_pp = send("w_ple_proj", _mm_tn("mm_d_ple_proj", pe, dpp, N_DEV))
    sent_pg, tok = send("w_ple_gate", _mm_tn("mm_d_ple_gate", hp, dpre, 1))
    dhp = _mm_nt("mm_dhp", dpre, wg_pg, F32, after=(tok_pp, tok))
    dx2, dg_ple = _rms_bwd("rms_ple_bwd", x2, norm_ple, dhp, dx3)
    sent_ff2, tok = send("w_ff2", _mm_tn("mm_d_ff2", act, dx2, 1))
    df = _mm_nt("mm_dact", dx2, wg_ff2, BF16, after=(tok,), through=_relu2_bwd, beside=[(f, 0)])
    sent_ff1, tok = send("w_ff1", _mm_tn("mm_d_ff1", h2, df, N_DEV))
    dh2 = _mm_nt("mm_dh2", df, wg_ff1, F32, after=(tok,))
    dx1, dg_ffn = _rms_bwd("rms_ffn_bwd", x1, norm_ffn, dh2, dx2)
    sent_out, tok = send("w_out", _mm_tn("mm_d_out", merged, dx1, 1))
    dgc, dga, dyc, dya = _mm_nt("mm_dmerged", dx1, wg_out, BF16, after=(tok,), through=_gated_merge_bwd,
                                beside=[(z, 5), (z, 7), (y_c, 0), (y_a, 0)], n_results=4, col_tile=d // 2)
    sent_cp, tok = send("w_conv_proj", _mm_tn("mm_d_conv_proj", s_c, dyc, N_DEV))
    ds_c = _mm_nt("mm_ds_c", dyc, wg_cp, F32, after=(tok,))
    dconv, dg_ln, db_ln = _conv_ln_bwd("conv_ln_bwd", ds_c, conv, conv_ln_g, conv_ln_b)
    dca, dcb, dg_dw = _conv_glu_bwd("conv_glu_bwd", dconv, z, cw, dw_full)
    sent_ap, tok = send("w_attn_proj", _mm_tn("mm_d_attn_proj", o, dya, 1))
    do, delta = _mm_nt("mm_do", dya, wg_ap, BF16, after=(tok,), head_dots=o)
    dqt, dkh, dvh = _flash_bwd(qr, kr, kr.T, z, col_v, do, lse, delta, n_heads)
    dq, dg_q = _head_norm_rope_bwd("q_prep_bwd", [(dqt, 0, 1)], z, col_q, n_heads, q_norm, cos, sin,
                                   heads_per_step=GROUP, transposed=True)
    dk, dg_k = _head_norm_rope_bwd("k_prep_bwd", [(dkh, g, GROUP) for g in range(GROUP)], z, col_k, n_kv, k_norm,
                                   cos, sin)
    dv = _group_sum("dv_sum", dvh, n_kv)
    dz = jnp.concatenate([dca, dcb, dq, dk, dv, dgc, dga], axis=1)
    sent_in, tok = send("w_in", _mm_tn("mm_d_in", h1, dz, N_DEV))
    dh1 = _mm_nt("mm_dh1", dz, wg_in, F32, after=(tok,))
    grad_x, dg_mix = _rms_bwd("rms_mix_bwd", x0, norm_mix, dh1, dx1)

    def pad(a):
        return jnp.pad(a, ((0, 0), (0, d - a.shape[1])))

    small = [dg_mix, dg_ffn, dg_ple, dg_final, dg_ln, db_ln, dg_q, dg_k, dg_dw]
    packed = _all_gather("ag_small", jnp.concatenate([pad(a) for a in small], axis=0))
    parts, row = [], 0
    for a in small:
        parts.append(packed[:, row:row + a.shape[0], :a.shape[1]])
        row += a.shape[0]
    p_mix, p_ffn, p_ple, p_final, p_lng, p_lnb, p_q, p_k, p_dw = parts
    p_dw = lax.dynamic_slice_in_dim(p_dw, me * (cw // N_DEV), cw // N_DEV, axis=2)

    def little(name, part, w, m, v):
        rows, n = (1 if w.ndim < 3 else w.shape[1]), w.shape[-1]
        padded = rows if rows == 1 else part.shape[1]

        def two_d(a):
            return jnp.pad(a.reshape(rows, n), ((0, padded - rows), (0, 0)))

        res = _adamw_small("adamw_" + name, part, two_d(w), two_d(m), two_d(v))
        return [r[:rows].reshape(w.shape) for r in res]

    done = {}
    after = packed
    for name, sent, w, m, v in [
            ("w_ple_proj", sent_pp, w_ple_proj, m_w_ple_proj, v_w_ple_proj),
            ("w_ple_gate", sent_pg, w_ple_gate, m_w_ple_gate, v_w_ple_gate),
            ("w_ff2", sent_ff2, w_ff2, m_w_ff2, v_w_ff2),
            ("w_ff1", sent_ff1, w_ff1, m_w_ff1, v_w_ff1),
            ("w_out", sent_out, w_out, m_w_out, v_w_out),
            ("w_conv_proj", sent_cp, w_conv_proj, m_w_conv_proj, v_w_conv_proj),
            ("w_attn_proj", sent_ap, w_attn_proj, m_w_attn_proj, v_w_attn_proj),
            ("w_in", sent_in, w_in, m_w_in, v_w_in)]:
        land = _exchange_wait("rs_wait_" + name, sent, after)
        res = _adamw_shard("adamw_" + name, land, w[0], m[0], v[0])
        after = res[0]
        done[name] = [r.reshape(w.shape) for r in res]

    results = [
        little("norm_mix", p_mix, norm_mix, m_norm_mix, v_norm_mix),
        done["w_in"],
        little("w_dw", p_dw, w_dw, m_w_dw, v_w_dw),
        little("conv_ln_g", p_lng, conv_ln_g, m_conv_ln_g, v_conv_ln_g),
        little("conv_ln_b", p_lnb, conv_ln_b, m_conv_ln_b, v_conv_ln_b),
        done["w_conv_proj"],
        little("q_norm", p_q, q_norm, m_q_norm, v_q_norm),
        little("k_norm", p_k, k_norm, m_k_norm, v_k_norm),
        done["w_attn_proj"],
        done["w_out"],
        little("norm_ffn", p_ffn, norm_ffn, m_norm_ffn, v_norm_ffn),
        done["w_ff1"],
        done["w_ff2"],
        little("norm_ple", p_ple, norm_ple, m_norm_ple, v_norm_ple),
        done["w_ple_gate"],
        done["w_ple_proj"],
        little("norm_final", p_final, norm_final, m_norm_final, v_norm_final),
    ]
    grads, deltas, new_m, new_v = zip(*results)
    return (loss, grad_x[None], *grads, *deltas, *new_m, *new_v)
```

```python
import functools

import jax
import jax.numpy as jnp
from jax import lax
from jax.experimental import pallas as pl
from jax.experimental.pallas import tpu as pltpu

F32 = jnp.float32
BF16 = jnp.bfloat16

EPS = 1e-6
HEAD_DIM = 128
GROUP = 4
CONV_KERNEL = 31
CONV_HALO = 16
CONV_TAPS_PAD = 32
GRID_W = 64
ROPE_THETA = 10000.0
ADAM_LR, ADAM_B1, ADAM_B2, ADAM_EPS, ADAM_WD, ADAM_STEP = 0.001, 0.9, 0.999, 1e-08, 0.01, 10

N_DEV = 8
MESH = pl.DeviceIdType.MESH
SUBLANES = 8
LANES = 128
VMEM_LIMIT = 48 * 1024 * 1024
ROW_TILE_ELEMS = 512 * 1024


def _div(n, pref, mult=1):
    if n <= pref:
        return n
    for d in range(pref, 0, -1):
        if n % d == 0 and d % mult == 0:
            return d
    return n


def _params(sem):
    return pltpu.CompilerParams(dimension_semantics=sem, vmem_limit_bytes=VMEM_LIMIT)


def _sum8(a):
    tm, w = a.shape
    return a.reshape(tm // SUBLANES, SUBLANES, w).sum(axis=0)


def _sigmoid(x):
    return jax.nn.sigmoid(x)


def _rowwise(name, fn, n_rows, tm, ins, outs, accs=(), ncol=1, refs_to_fn=False, after=()):
    nrow = n_rows // tm
    in_specs, arrays = [], []
    for spec in ins:
        kind, arr, w, c0 = spec[:4]
        if kind == "t":
            cs, r0 = spec[4], spec[5]
            in_specs.append(pl.BlockSpec((tm, w), lambda j, i, c0=c0, cs=cs, r0=r0: (r0 + i, c0 + cs * j)))
        elif kind == "p":
            h = spec[4]
            r = tm // h
            in_specs.append(pl.BlockSpec((h, w), lambda j, i, c0=c0, r=r: (jnp.maximum(i * r - 1, 0), c0 + j)))
        elif kind == "n":
            h = spec[4]
            r = tm // h
            last = n_rows // h - 1
            in_specs.append(
                pl.BlockSpec((h, w), lambda j, i, c0=c0, r=r, last=last: (jnp.minimum((i + 1) * r, last), c0 + j)))
        elif kind == "x":
            in_specs.append(pl.BlockSpec((w, tm), lambda j, i, c0=c0: (c0 + j, i)))
        else:
            cs = spec[4]
            in_specs.append(pl.BlockSpec((arr.shape[0], w), lambda j, i, c0=c0, cs=cs: (0, c0 + cs * j)))
        arrays.append(arr)
    out_shape, out_specs = [], []
    for dtype, total, w in outs:
        out_shape.append(jax.ShapeDtypeStruct((n_rows, total), dtype))
        out_specs.append(pl.BlockSpec((tm, w), lambda j, i: (i, j)))
    for rows, total, w, follow in accs:
        out_shape.append(jax.ShapeDtypeStruct((rows, total), F32))
        out_specs.append(pl.BlockSpec((rows, w), (lambda j, i: (0, j)) if follow else (lambda j, i: (0, 0))))
    n_in, n_out, n_acc = len(ins), len(outs), len(accs)
    for token in after:
        in_specs.append(pl.BlockSpec(memory_space=pl.ANY))
        arrays.append(token)
    out0 = n_in + len(after)

    def body(*refs):
        j, i = pl.program_id(0), pl.program_id(1)
        res = fn(i, nrow, *(refs[:n_in] if refs_to_fn else [r[...] for r in refs[:n_in]]))
        if not isinstance(res, (tuple, list)):
            res = (res,)
        for k in range(n_out):
            refs[out0 + k][...] = res[k].astype(outs[k][0])
        for k in range(n_acc):
            ref, term = refs[out0 + n_out + k], res[n_out + k]
            first = (i == 0) if accs[k][3] else jnp.logical_and(i == 0, j == 0)

            @pl.when(first)
            def _():
                ref[...] = term

            @pl.when(jnp.logical_not(first))
            def _():
                ref[...] += term

    res = pl.pallas_call(
        body, name=name, grid=(ncol, nrow), in_specs=in_specs, out_specs=out_specs, out_shape=out_shape,
        compiler_params=_params(("arbitrary", "arbitrary")))(*arrays)
    return res


def _row_tile(n_rows, width, mult=SUBLANES):
    return _div(n_rows, max(mult, ROW_TILE_ELEMS // width), mult)


def _t(arr, w=None, col0=0, cstride=1, row0=0):
    return ("t", arr, arr.shape[1] if w is None else w, col0, cstride, row0)


def _b(arr, w=None, col0=0, cstride=0):
    return ("b", arr, arr.shape[1] if w is None else w, col0, cstride)


def _rms_fwd(name, x, g, after=()):
    n, d = x.shape

    def fn(i, nrow, x, g):
        r = lax.rsqrt(jnp.mean(x * x, axis=-1, keepdims=True) + EPS)
        return x * r * g

    return _rowwise(name, fn, n, _row_tile(n, d // 2), [_t(x), _b(g)], [(BF16, d, d)], after=after)[0]


def _rms_bwd(name, x, g, dh, dres):
    n, d = x.shape

    def fn(i, nrow, x, g, dh, dres):
        r = lax.rsqrt(jnp.mean(x * x, axis=-1, keepdims=True) + EPS)
        w = dh * g
        dx = dres + r * w - x * (r * r * r) * jnp.mean(x * w, axis=-1, keepdims=True)
        return dx, _sum8(dh * x * r)

    return _rowwise(name, fn, n, _row_tile(n, d), [_t(x), _b(g), _t(dh), _t(dres)], [(F32, d, d)],
                    [(SUBLANES, d, d, True)])


def _swap_halves(x):
    lane = lax.broadcasted_iota(jnp.int32, x.shape, 1)
    return jnp.where(lane % 64 < 32, pltpu.roll(x, 96, axis=1), pltpu.roll(x, 32, axis=1))


def _head_norm_rope(name, z, col0, n_heads, g, cos, sin, out_scale=None, heads_per_step=1):
    n, hp = z.shape[0], heads_per_step
    assert col0 % hp == 0 and n_heads % hp == 0

    def fn(i, nrow, x, g, cos, sin):
        ys = []
        for h in range(hp):
            xh = x[:, h * HEAD_DIM:(h + 1) * HEAD_DIM]
            r = lax.rsqrt(jnp.mean(xh * xh, axis=-1, keepdims=True) + EPS)
            y = xh * r * g
            y = y * cos + _swap_halves(y) * sin
            ys.append(y if out_scale is None else y * out_scale)
        return ys[0] if hp == 1 else jnp.concatenate(ys, axis=1)

    tm = _div(n, 1024, SUBLANES)
    w = hp * HEAD_DIM
    return _rowwise(name, fn, n, tm, [_t(z, w, col0 // hp), _b(g), _t(cos, cstride=0), _t(sin, cstride=0)],
                    [(BF16, n_heads * HEAD_DIM, w)], ncol=n_heads // hp)[0]


def _head_norm_rope_bwd(name, douts, z, col0, n_heads, g, cos, sin, heads_per_step=1, transposed=False):
    n, hp = z.shape[0], heads_per_step
    n_d = len(douts)
    assert col0 % hp == 0 and n_heads % hp == 0 and (hp == 1 or all(c0 % hp == 0 and cs == 1 for _, c0, cs in douts))
    assert not transposed or n_d == 1

    def fn(i, nrow, *a):
        dy = a[0]
        for k in range(1, n_d):
            dy = dy + a[k]
        x, g, cos, sin = a[n_d:]
        dxs, dg = [], None
        for h in range(hp):
            xh = x[:, h * HEAD_DIM:(h + 1) * HEAD_DIM]
            if transposed:
                dyh = jnp.transpose(dy[h * HEAD_DIM:(h + 1) * HEAD_DIM, :])
            else:
                dyh = dy[:, h * HEAD_DIM:(h + 1) * HEAD_DIM]
            dn = dyh * cos + _swap_halves(dyh * sin)
            r = lax.rsqrt(jnp.mean(xh * xh, axis=-1, keepdims=True) + EPS)
            w = dn * g
            dxs.append(r * w - xh * (r * r * r) * jnp.mean(xh * w, axis=-1, keepdims=True))
            term = _sum8(dn * xh * r)
            dg = term if dg is None else dg + term
        return (dxs[0] if hp == 1 else jnp.concatenate(dxs, axis=1)), dg

    tm = _div(n, 1024, SUBLANES)
    w = hp * HEAD_DIM
    if transposed:
        ins = [("x", douts[0][0], w, douts[0][1] // hp)]
    else:
        ins = [_t(arr, w, c0 // hp, cs) for arr, c0, cs in douts]
    ins += [_t(z, w, col0 // hp), _b(g), _t(cos, cstride=0), _t(sin, cstride=0)]
    return _rowwise(name, fn, n, tm, ins, [(BF16, n_heads * HEAD_DIM, w)],
                    [(SUBLANES, HEAD_DIM, HEAD_DIM, False)], ncol=n_heads // hp)


def _halo(arr, w, col0):
    return [("p", arr, w, col0, CONV_HALO), _t(arr, w, col0), ("n", arr, w, col0, CONV_HALO)]


def _extend(i, nrow, prev, cur, nxt):
    prev = jnp.where(i > 0, prev, 0.0)
    nxt = jnp.where(i < nrow - 1, nxt, 0.0)
    return jnp.concatenate([prev, cur, nxt], axis=0)


def _shifted(ext, offset, tm):
    n = ext.shape[0]
    rolled = ext if offset == 0 else pltpu.roll(ext, (-offset) % n, axis=0)
    return rolled[CONV_HALO:CONV_HALO + tm]


def _conv_fwd(name, z, cw, w_dw, ln_g, ln_b):
    n = z.shape[0]
    tm = _row_tile(n, 4 * cw, CONV_HALO)

    def fn(i, nrow, ap, a, an, bp, b, bn, w, g, beta):
        a = a[...]
        ext = _extend(i, nrow, ap[...] * _sigmoid(bp[...]), a * _sigmoid(b[...]), an[...] * _sigmoid(bn[...]))
        conv = jnp.zeros_like(a)
        for k in range(CONV_KERNEL):
            conv = conv + _shifted(ext, k - CONV_KERNEL // 2, tm) * w[k:k + 1, :]
        xc = conv - jnp.mean(conv, axis=-1, keepdims=True)
        ln = xc * lax.rsqrt(jnp.mean(xc * xc, axis=-1, keepdims=True) + EPS) * g[...] + beta[...]
        return conv, ln * _sigmoid(ln)

    ins = _halo(z, cw, 0) + _halo(z, cw, 1) + [_b(w_dw), _b(ln_g), _b(ln_b)]
    return _rowwise(name, fn, n, tm, ins, [(F32, cw, cw), (BF16, cw, cw)], refs_to_fn=True)


def _conv_ln_bwd(name, ds, conv, ln_g, ln_b):
    n, cw = conv.shape

    def fn(i, nrow, ds, conv, g, beta):
        xc = conv - jnp.mean(conv, axis=-1, keepdims=True)
        rstd = lax.rsqrt(jnp.mean(xc * xc, axis=-1, keepdims=True) + EPS)
        xhat = xc * rstd
        ln = xhat * g + beta
        sg = _sigmoid(ln)
        dln = ds * (sg * (1.0 + ln * (1.0 - sg)))
        dxh = dln * g
        dconv = rstd * (dxh - jnp.mean(dxh, axis=-1, keepdims=True)
                        - xhat * jnp.mean(dxh * xhat, axis=-1, keepdims=True))
        return dconv, _sum8(dln * xhat), _sum8(dln)

    return _rowwise(name, fn, n, _row_tile(n, cw), [_t(ds), _t(conv), _b(ln_g), _b(ln_b)], [(F32, cw, cw)],
                    [(SUBLANES, cw, cw, True), (SUBLANES, cw, cw, True)])


def _conv_glu_bwd(name, dconv, z, cw, w_dw):
    n = z.shape[0]
    tm = _row_tile(n, 4 * cw, CONV_HALO)

    def fn(i, nrow, dp, d, dn, a, b, w):
        d, a = d[...], a[...]
        dext = _extend(i, nrow, dp[...], d, dn[...])
        sg = _sigmoid(b[...])
        u = a * sg
        tap = lax.broadcasted_iota(jnp.int32, (CONV_TAPS_PAD, cw), 0)
        du = jnp.zeros_like(d)
        dw = jnp.zeros((CONV_TAPS_PAD, cw), F32)
        for k in range(CONV_KERNEL):
            d_k = _shifted(dext, -(k - CONV_KERNEL // 2), tm)
            du = du + d_k * w[k:k + 1, :]
            dw = dw + jnp.where(tap == k, jnp.sum(d_k * u, axis=0, keepdims=True), 0.0)
        return du * sg, du * a * sg * (1.0 - sg), dw

    ins = _halo(dconv, cw, 0) + [_t(z, cw, 0), _t(z, cw, 1), _b(w_dw)]
    return _rowwise(name, fn, n, tm, ins, [(BF16, cw, cw), (BF16, cw, cw)], [(CONV_TAPS_PAD, cw, cw, True)],
                    refs_to_fn=True)


def _gated_merge(y_a, gc, ga, y_c):
    return _sigmoid(gc) * y_c + _sigmoid(ga) * y_a


def _gated_merge_bwd(dm, gc, ga, yc, ya):
    sc, sa = _sigmoid(gc), _sigmoid(ga)
    return dm * yc * sc * (1.0 - sc), dm * ya * sa * (1.0 - sa), dm * sc, dm * sa


def _relu2(f):
    r = jnp.maximum(f, 0.0)
    return r * r


def _relu2_bwd(da, f):
    return da * (2.0 * jnp.maximum(f, 0.0))


def _head_loss(name, x2, pre, pp, target, g):
    n, d = x2.shape

    def fn(i, nrow, x2, pre, pp, t, g):
        gate = _sigmoid(pre)
        x3 = x2 + gate * pp
        r = lax.rsqrt(jnp.mean(x3 * x3, axis=-1, keepdims=True) + EPS)
        xn = x3 * r
        e = xn * g - t
        part = 0.5 * jnp.sum(jnp.mean(e * e, axis=-1, keepdims=True), axis=0, keepdims=True)
        dy = e * (1.0 / d)
        w = dy * g
        dx3 = r * w - x3 * (r * r * r) * jnp.mean(x3 * w, axis=-1, keepdims=True)
        return (dx3, dx3 * gate, dx3 * pp * gate * (1.0 - gate), _sum8(dy * xn),
                jnp.broadcast_to(part, (SUBLANES, LANES)))

    ins = [_t(x2), _t(pre), _t(pp), _t(target), _b(g)]
    return _rowwise(name, fn, n, _row_tile(n, d), ins, [(F32, d, d), (BF16, d, d), (BF16, d, d)],
                    [(SUBLANES, d, d, True), (SUBLANES, LANES, LANES, True)])


def _group_sum(name, dvh, n_kv):
    n = dvh.shape[0]

    def fn(i, nrow, *a):
        s = a[0]
        for k in range(1, GROUP):
            s = s + a[k]
        return s

    ins = [_t(dvh, HEAD_DIM, k, GROUP) for k in range(GROUP)]
    return _rowwise(name, fn, n, _div(n, 1024, SUBLANES), ins, [(BF16, n_kv * HEAD_DIM, HEAD_DIM)], ncol=n_kv)[0]


MM_ROWS = 1024
MM_COLS = 1152
MM_REDUCE = 2048
MM_REDUCE_BLOCKS = 2304


def _rows_outermost(n_rows, n_cols, row_tile_bytes, weight_tile_bytes):
    rows_outer = n_rows * (row_tile_bytes + n_cols * weight_tile_bytes)
    weight_outer = n_cols * (weight_tile_bytes + n_rows * row_tile_bytes)
    return 5 * rows_outer < 4 * weight_outer


def _accumulate(step, n_steps, part, acc, finish):
    if n_steps == 1:
        finish(part)
        return

    @pl.when(step == 0)
    def _():
        acc[...] = part

    @pl.when(jnp.logical_and(step > 0, step < n_steps - 1))
    def _():
        acc[...] += part

    @pl.when(step == n_steps - 1)
    def _():
        finish(acc[...] + part)


MXU_WIDTH = 256


def _paired(nb, ns):
    return ns % MXU_WIDTH != 0 and nb % 2 == 0 and ns <= MM_COLS


def _mm_nn(name, a, w3, out_dtype, residual=None, activation=None, beside=(), col_tile=None):
    m, k = a.shape
    nb, _, ns = w3.shape
    jb = 2 if _paired(nb, ns) else 1
    tm = _div(m, MM_ROWS // max(jb, 1 + len(beside) // 2), SUBLANES)
    tn_w, tk = (ns if jb > 1 else _div(ns, col_tile or MM_COLS, LANES)), _div(k, MM_REDUCE, LANES)
    tn = jb * tn_w
    per, nk = ns // tn_w, k // tk
    has_res = residual is not None
    n_out = 1 if activation is None else 2
    first_beside = 3 if has_res else 2

    def body(*refs):
        a_ref, w_ref = refs[:2]
        acc = refs[-1]
        outs = refs[-1 - n_out:-1]
        w = w_ref[0] if jb == 1 else jnp.concatenate([w_ref[j] for j in range(jb)], axis=1)
        part = jnp.dot(a_ref[...].astype(BF16), w.astype(BF16), preferred_element_type=F32)

        def finish(total):
            if has_res:
                total = total + refs[2][...]
            outs[0][...] = total.astype(out_dtype)
            if activation is not None:
                tiles = [r[...] for r in refs[first_beside:first_beside + len(beside)]]
                outs[1][...] = activation(total, *tiles).astype(BF16)

        _accumulate(pl.program_id(2), nk, part, acc, finish)

    n_rows, n_cols = m // tm, (nb // jb) * per
    rows_outer = nk > 1 or _rows_outermost(n_rows, n_cols, tm * tk * a.dtype.itemsize,
                                           jb * tk * tn_w * w3.dtype.itemsize)

    def at(f):
        return f if rows_outer else (lambda n, i, kk: f(i, n, kk))

    in_specs = [pl.BlockSpec((tm, tk), at(lambda i, n, kk: (i, kk))),
                pl.BlockSpec((jb, tk, tn_w), at(lambda i, n, kk: (n // per, kk, n % per)))]
    args = [a, w3]
    if has_res:
        in_specs.append(pl.BlockSpec((tm, tn), at(lambda i, n, kk: (i, n))))
        args.append(residual)
    for arr, col0 in beside:
        in_specs.append(pl.BlockSpec((tm, tn), at(lambda i, n, kk, col0=col0: (i, col0 + n))))
        args.append(arr)
    out_dtypes = [out_dtype] + ([BF16] if activation is not None else [])
    res = pl.pallas_call(
        body, name=name, grid=(n_rows, n_cols, nk) if rows_outer else (n_cols, n_rows, nk), in_specs=in_specs,
        out_specs=[pl.BlockSpec((tm, tn), at(lambda i, n, kk: (i, n))) for _ in out_dtypes],
        out_shape=[jax.ShapeDtypeStruct((m, nb * ns), dt) for dt in out_dtypes],
        scratch_shapes=[pltpu.VMEM((tm, tn) if nk > 1 else (SUBLANES, LANES), F32)],
        compiler_params=_params(("parallel", "parallel", "arbitrary")))(*args)
    return res[0] if activation is None else res


def _mm_nt(name, dc, w3, out_dtype, after=(), through=None, beside=(), n_results=1, head_dots=None, col_tile=None):
    m = dc.shape[0]
    nb, k, ns = w3.shape
    tm = _div(m, MM_ROWS // (1 + len(beside) // 3), SUBLANES)
    to = _div(k, col_tile or MM_ROWS, LANES)
    if ns > MM_REDUCE_BLOCKS // 2:
        jb, tr = 1, _div(ns, MM_REDUCE, LANES)
    else:
        jb, tr = max(j for j in range(1, nb + 1) if nb % j == 0 and j * ns <= MM_REDUCE_BLOCKS), ns
    per = ns // tr
    nr = (nb // jb) * per
    tiles_in = list(beside) + ([(head_dots, 0)] if head_dots is not None else [])
    n_out = n_results + (1 if head_dots is not None else 0)
    heads = to // HEAD_DIM

    def body(dc_ref, w_ref, *rest):
        acc = rest[-1]
        outs = rest[-1 - n_out:-1]
        w = w_ref[0] if jb == 1 else jnp.concatenate([w_ref[j] for j in range(jb)], axis=1)
        part = lax.dot_general(dc_ref[...].astype(BF16), w.astype(BF16), (((1,), (1,)), ((), ())),
                               preferred_element_type=F32)

        def finish(total):
            results = (total,) if through is None else through(total, *[r[...] for r in rest[:len(beside)]])
            if not isinstance(results, (tuple, list)):
                results = (results,)
            for out, res in zip(outs, results):
                out[...] = res.astype(out_dtype)
            if head_dots is not None:
                prod = results[0].astype(out_dtype).astype(F32) * rest[len(beside)][...]
                for h in range(heads):
                    row = jnp.sum(jnp.transpose(prod[:, h * HEAD_DIM:(h + 1) * HEAD_DIM]), axis=0, keepdims=True)
                    outs[n_results][h] = jnp.broadcast_to(row, (SUBLANES, tm))

        _accumulate(pl.program_id(2), nr, part, acc, finish)

    rows_outer = nr > 1 or _rows_outermost(m // tm, k // to, tm * jb * tr * dc.dtype.itemsize,
                                           jb * to * tr * w3.dtype.itemsize)

    def at(f):
        return f if rows_outer else (lambda o, i, r: f(i, o, r))

    in_specs = [pl.BlockSpec((tm, jb * tr), at(lambda i, o, r: (i, r))),
                pl.BlockSpec((jb, to, tr), at(lambda i, o, r: (r // per, o, r % per)))]
    args = [dc, w3]
    for arr, col0 in tiles_in:
        in_specs.append(pl.BlockSpec((tm, to), at(lambda i, o, r, col0=col0: (i, col0 + o))))
        args.append(arr)
    for token in after:
        in_specs.append(pl.BlockSpec(memory_space=pl.ANY))
        args.append(token)
    out_specs = [pl.BlockSpec((tm, to), at(lambda i, o, r: (i, o))) for _ in range(n_results)]
    out_shape = [jax.ShapeDtypeStruct((m, k), out_dtype) for _ in range(n_results)]
    if head_dots is not None:
        out_specs.append(pl.BlockSpec((heads, SUBLANES, tm), at(lambda i, o, r: (o, 0, i))))
        out_shape.append(jax.ShapeDtypeStruct((k // HEAD_DIM, SUBLANES, m), F32))
    res = pl.pallas_call(
        body, name=name, grid=(m // tm, k // to, nr) if rows_outer else (k // to, m // tm, nr), in_specs=in_specs,
        out_specs=out_specs, out_shape=out_shape,
        scratch_shapes=[pltpu.VMEM((tm, to) if nr > 1 else (SUBLANES, LANES), F32)],
        compiler_params=_params(("parallel", "parallel", "arbitrary")))(*args)
    return res[0] if n_out == 1 else res


def _mm_tn(name, a, dc, nb):
    m, k = a.shape
    ns = dc.shape[1] // nb
    jb = 2 if _paired(nb, ns) else 1
    tr, to = _div(m, MM_REDUCE, LANES), _div(k, MM_ROWS // jb, LANES)
    tn_w = ns if jb > 1 else _div(ns, MM_COLS, LANES)
    tn = jb * tn_w
    per, nr = ns // tn_w, m // tr

    def body(a_ref, dc_ref, o_ref, acc):
        part = lax.dot_general(a_ref[...].astype(BF16), dc_ref[...].astype(BF16), (((0,), (0,)), ((), ())),
                               preferred_element_type=F32)

        def finish(total):
            for j in range(jb):
                o_ref[j] = total[:, j * tn_w:(j + 1) * tn_w].astype(BF16)

        _accumulate(pl.program_id(2), nr, part, acc, finish)

    return pl.pallas_call(
        body, name=name, grid=(k // to, (nb // jb) * per, nr),
        in_specs=[pl.BlockSpec((tr, to), lambda o, n, r: (r, o)),
                  pl.BlockSpec((tr, tn), lambda o, n, r: (r, n))],
        out_specs=pl.BlockSpec((jb, to, tn_w), lambda o, n, r: (n // per, o, n % per)),
        out_shape=jax.ShapeDtypeStruct((nb, k, ns), BF16),
        scratch_shapes=[pltpu.VMEM((to, tn) if nr > 1 else (SUBLANES, LANES), F32)],
        compiler_params=_params(("parallel", "parallel", "arbitrary")))(a, dc)


QK_SCALE = HEAD_DIM ** -0.5
LOG2_E = 1.4426950408889634
LN_2 = 0.6931471805599453


def _columns(row, n):
    return jnp.transpose(jnp.broadcast_to(row, (LANES, n)))


FLASH_TILE = 1024
ONES_ROWS = 16


def _flash_fwd(qr, kr, vt, n_heads):
    t = qr.shape[0]
    tq, tk = _div(t, 2 * FLASH_TILE, LANES), _div(t, FLASH_TILE, LANES)
    nk = t // tk
    rows = vt.shape[0] // (n_heads // GROUP)

    def body(q_ref, k_ref, vt_ref, o_ref, lse_ref, m_scr, acc_scr):
        kb = pl.program_id(2)

        @pl.when(kb == 0)
        def _():
            m_scr[...] = jnp.full_like(m_scr, -1e30)
            acc_scr[...] = jnp.zeros_like(acc_scr)

        s_t = lax.dot_general(k_ref[...], q_ref[...], (((1,), (1,)), ((), ())), preferred_element_type=F32)
        m_prev = m_scr[...]
        m_new = jnp.maximum(m_prev, jnp.max(s_t, axis=0, keepdims=True))
        alpha = jnp.exp2(m_prev - m_new)
        p_t = jnp.exp2(s_t - m_new).astype(BF16)
        acc_scr[...] = alpha * acc_scr[...] + jnp.dot(vt_ref[...], p_t, preferred_element_type=F32)
        m_scr[...] = m_new

        @pl.when(kb == nk - 1)
        def _():
            l = acc_scr[HEAD_DIM:HEAD_DIM + 1, :]
            o_ref[...] = jnp.transpose(acc_scr[0:HEAD_DIM, :] / l)
            lse_ref[...] = jnp.broadcast_to(m_scr[...] + jnp.log(l) * LOG2_E, (SUBLANES, tq))

    return pl.pallas_call(
        body, name="flash_fwd", grid=(n_heads, t // tq, nk),
        in_specs=[pl.BlockSpec((tq, HEAD_DIM), lambda h, qi, ki: (qi, h)),
                  pl.BlockSpec((tk, HEAD_DIM), lambda h, qi, ki: (ki, h // GROUP)),
                  pl.BlockSpec((rows, tk), lambda h, qi, ki: (h // GROUP, ki))],
        out_specs=[pl.BlockSpec((tq, HEAD_DIM), lambda h, qi, ki: (qi, h)),
                   pl.BlockSpec((None, SUBLANES, tq), lambda h, qi, ki: (h, 0, qi))],
        out_shape=[jax.ShapeDtypeStruct((t, n_heads * HEAD_DIM), F32),
                   jax.ShapeDtypeStruct((n_heads, SUBLANES, t), F32)],
        scratch_shapes=[pltpu.VMEM((1, tq), F32), pltpu.VMEM((rows, tq), F32)],
        compiler_params=_params(("parallel", "parallel", "arbitrary")))(qr, kr, vt)


def _flash_bwd(qr, kr, kt, z, v_col0, do, lse, delta, n_heads):
    t = qr.shape[0]
    tq, tk = _div(t, 2 * FLASH_TILE, LANES), _div(t, FLASH_TILE, LANES)
    nq = t // tq

    def body(q_ref, k_ref, kt_ref, v_ref, do_ref, lse_ref, dl_ref, dq_ref, dk_ref, dv_ref, dk_acc, dv_acc):
        kb, qb = pl.program_id(1), pl.program_id(2)
        q, k, do_ = q_ref[...], k_ref[...], do_ref[...]
        v = v_ref[...].astype(BF16)
        s_t = lax.dot_general(k, q, (((1,), (1,)), ((), ())), preferred_element_type=F32)
        p_t = jnp.exp2(s_t - lse_ref[0:1, :])
        dp_t = lax.dot_general(v, do_, (((1,), (1,)), ((), ())), preferred_element_type=F32)
        ds_t = (p_t * (dp_t - dl_ref[0:1, :])).astype(BF16)
        dv_c = jnp.dot(p_t.astype(BF16), do_, preferred_element_type=F32)
        dk_c = jnp.dot(ds_t, q, preferred_element_type=F32) * LN_2
        dq_c = jnp.dot(kt_ref[...], ds_t, preferred_element_type=F32) * QK_SCALE

        @pl.when(qb == 0)
        def _():
            dk_acc[...] = dk_c
            dv_acc[...] = dv_c

        @pl.when(qb > 0)
        def _():
            dk_acc[...] += dk_c
            dv_acc[...] += dv_c

        @pl.when(qb == nq - 1)
        def _():
            dk_ref[...] = dk_acc[...]
            dv_ref[...] = dv_acc[...]

        cols = pl.ds(pl.multiple_of(qb * tq, tq), tq)

        @pl.when(kb == 0)
        def _():
            dq_ref[:, cols] = dq_c

        @pl.when(kb > 0)
        def _():
            dq_ref[:, cols] += dq_c

    wide = jax.ShapeDtypeStruct((t, n_heads * HEAD_DIM), F32)
    return pl.pallas_call(
        body, name="flash_bwd", grid=(n_heads, t // tk, nq),
        in_specs=[pl.BlockSpec((tq, HEAD_DIM), lambda h, kb, qb: (qb, h)),
                  pl.BlockSpec((tk, HEAD_DIM), lambda h, kb, qb: (kb, h // GROUP)),
                  pl.BlockSpec((HEAD_DIM, tk), lambda h, kb, qb: (h // GROUP, kb)),
                  pl.BlockSpec((tk, HEAD_DIM), lambda h, kb, qb: (kb, v_col0 + h // GROUP)),
                  pl.BlockSpec((tq, HEAD_DIM), lambda h, kb, qb: (qb, h)),
                  pl.BlockSpec((None, SUBLANES, tq), lambda h, kb, qb: (h, 0, qb)),
                  pl.BlockSpec((None, SUBLANES, tq), lambda h, kb, qb: (h, 0, qb))],
        out_specs=[pl.BlockSpec((HEAD_DIM, t), lambda h, kb, qb: (h, 0)),
                   pl.BlockSpec((tk, HEAD_DIM), lambda h, kb, qb: (kb, h)),
                   pl.BlockSpec((tk, HEAD_DIM), lambda h, kb, qb: (kb, h))],
        out_shape=[jax.ShapeDtypeStruct((n_heads * HEAD_DIM, t), F32), wide, wide],
        scratch_shapes=[pltpu.VMEM((tk, HEAD_DIM), F32), pltpu.VMEM((tk, HEAD_DIM), F32)],
        compiler_params=_params(("parallel", "arbitrary", "arbitrary")))(qr, kr, kt, z, do, lse, delta)


def _position():
    x, y, c = lax.axis_index("x"), lax.axis_index("y"), lax.axis_index("c")
    return x, y, c


def _index(x, y, c):
    return 4 * x + 2 * y + c


def _all_gather(name, shard):
    a, b = shard.shape

    def body(x_ref, out_ref, send_sems, recv_sems, local_sem):
        x, y, c = _position()
        me, sibling = (x, y, c), (x, y, 1 - c)
        chips = [(1 - x, y), (x, 1 - y), (1 - x, 1 - y)]

        def block(px, py, pc):
            return out_ref.at[_index(px, py, pc)]

        def copy(k, blk, to, src=None):
            return pltpu.make_async_remote_copy(
                src_ref=block(*blk) if src is None else src, dst_ref=block(*blk),
                send_sem=send_sems.at[k], recv_sem=recv_sems.at[k], device_id=to, device_id_type=MESH)

        mine = pltpu.make_async_copy(x_ref, block(*me), local_sem)
        mine.start()
        first = [copy(0, me, sibling, src=x_ref)]
        first += [copy(1 + j, me, (*chip, c), src=x_ref) for j, chip in enumerate(chips)]
        for cp in first:
            cp.start()
        passed = [copy(4 + j, (*chip, c), sibling) for j, chip in enumerate(chips)]
        for j, chip in enumerate(chips):
            copy(1 + j, (*chip, c), me).wait_recv()
            passed[j].start()
        copy(0, sibling, me).wait_recv()
        for j, chip in enumerate(chips):
            copy(4 + j, (*chip, 1 - c), me).wait_recv()
        for cp in first + passed:
            cp.wait_send()
        mine.wait()

    return pl.pallas_call(
        body, name=name, out_shape=jax.ShapeDtypeStruct((N_DEV, a, b), shard.dtype),
        in_specs=[pl.BlockSpec(memory_space=pltpu.HBM)], out_specs=pl.BlockSpec(memory_space=pltpu.HBM),
        scratch_shapes=[pltpu.SemaphoreType.DMA((7,)), pltpu.SemaphoreType.DMA((7,)), pltpu.SemaphoreType.DMA],
    )(shard)


HBM_SPEC = pl.BlockSpec(memory_space=pltpu.HBM)
SEM_SPEC = pl.BlockSpec(memory_space=pltpu.SEMAPHORE)
N_PEERS = N_DEV - 1
PEER_FLIPS = [(fx, fy, fc) for fx in (0, 1) for fy in (0, 1) for fc in (0, 1)][1:]


def _peers(x, y, c):
    return [((1 - x) if fx else x, (1 - y) if fy else y, (1 - c) if fc else c) for fx, fy, fc in PEER_FLIPS]


def _exchange_copies(g_ref, land_ref, send_sems, recv_sems):
    x, y, c = _position()
    me = _index(x, y, c)
    peers = _peers(x, y, c)
    blocked = len(g_ref.shape) == 3

    def copy(k, src_block, dst_block):
        return pltpu.make_async_remote_copy(
            src_ref=g_ref.at[src_block] if blocked else g_ref, dst_ref=land_ref.at[dst_block],
            send_sem=send_sems.at[k], recv_sem=recv_sems.at[k], device_id=peers[k], device_id_type=MESH)

    sent = [copy(k, _index(*peer), me) for k, peer in enumerate(peers)]
    arriving = [copy(k, me, _index(*peer)) for k, peer in enumerate(peers)]
    own = pltpu.make_async_copy(g_ref.at[me] if blocked else g_ref, land_ref.at[me], send_sems.at[N_PEERS])
    return sent, arriving, own


def _exchange_start(name, g3, after=()):
    def body(g_ref, land_ref, *rest):
        send_sems, recv_sems, _, _, token = rest[len(after):]
        sent, _, own = _exchange_copies(g_ref, land_ref, send_sems, recv_sems)
        for cp in sent:
            cp.start()
        own.start()
        token[...] = jnp.zeros_like(token)

    land_shape = g3.shape if g3.ndim == 3 else (N_DEV,) + g3.shape
    land = lax.empty(land_shape, g3.dtype)
    send_sems, recv_sems, g_thru, land_thru, token = pl.pallas_call(
        body, name=name,
        out_shape=(pltpu.SemaphoreType.DMA((N_PEERS + 1,)), pltpu.SemaphoreType.DMA((N_PEERS,)),
                   pltpu.HBM(g3.shape, g3.dtype), pltpu.HBM(land_shape, g3.dtype),
                   jax.ShapeDtypeStruct((SUBLANES, LANES), F32)),
        in_specs=(HBM_SPEC, HBM_SPEC) + (pl.BlockSpec(memory_space=pl.ANY),) * len(after),
        out_specs=(SEM_SPEC, SEM_SPEC, HBM_SPEC, HBM_SPEC, pl.BlockSpec(memory_space=pltpu.VMEM)),
        input_output_aliases={0: 2, 1: 3},
        compiler_params=pltpu.CompilerParams(has_side_effects=pltpu.SideEffectType.DATAFLOW_SIDE_EFFECTING),
    )(pltpu.with_memory_space_constraint(g3, pltpu.HBM), pltpu.with_memory_space_constraint(land, pltpu.HBM), *after)
    return (send_sems, recv_sems, g_thru, land_thru), token


def _exchange_wait(name, handle, after):
    send_sems, recv_sems, g_thru, land_thru = handle

    def body(g_ref, land_ref, send_sems, recv_sems, after_ref, g_dead, land_out):
        sent, arriving, own = _exchange_copies(g_ref, land_ref, send_sems, recv_sems)
        for cp in sent:
            cp.wait_send()
        for cp in arriving:
            cp.wait_recv()
        own.wait()

    return pl.pallas_call(
        body, name=name,
        out_shape=(pltpu.HBM(g_thru.shape, g_thru.dtype), pltpu.HBM(land_thru.shape, land_thru.dtype)),
        in_specs=(HBM_SPEC, HBM_SPEC, SEM_SPEC, SEM_SPEC, pl.BlockSpec(memory_space=pl.ANY)),
        out_specs=(HBM_SPEC, HBM_SPEC), input_output_aliases={0: 0, 1: 1},
        compiler_params=pltpu.CompilerParams(has_side_effects=pltpu.SideEffectType.DATAFLOW_SIDE_EFFECTING),
    )(g_thru, land_thru, send_sems, recv_sems, after)[1]


def _adamw(w, g, m, v):
    m = ADAM_B1 * m + (1.0 - ADAM_B1) * g
    v = ADAM_B2 * v + (1.0 - ADAM_B2) * (g * g)
    m_hat = m / (1.0 - ADAM_B1 ** ADAM_STEP)
    v_hat = v / (1.0 - ADAM_B2 ** ADAM_STEP)
    delta = -ADAM_LR * (m_hat / (jnp.sqrt(v_hat) + ADAM_EPS) + ADAM_WD * w)
    return delta, m, v


def _adamw_shard(name, land, w, m, v):
    _, a, b = land.shape
    tm = _div(a, max(16, ROW_TILE_ELEMS // (2 * b)), 16)
    per = a // tm

    def fn(i, nrow, *t):
        g = t[0].astype(F32)
        for s in range(1, N_DEV):
            g = g + t[s].astype(F32)
        w, m, v = t[N_DEV:]
        return (g,) + _adamw(w, g, m, v)

    land2 = land.reshape(N_DEV * a, b)
    ins = [_t(land2, row0=s * per) for s in range(N_DEV)] + [_t(w), _t(m), _t(v)]
    return _rowwise(name, fn, a, tm, ins, [(F32, b, b)] * 4)


def _adamw_small(name, parts, w, m, v):
    rows, n = w.shape

    def body(p_ref, w_ref, m_ref, v_ref, g_out, d_out, m_out, v_out):
        g = p_ref[0]
        for s in range(1, parts.shape[0]):
            g = g + p_ref[s]
        g = jnp.sum(g, axis=0, keepdims=True) if rows == 1 else g[0:rows, :]
        delta, m_new, v_new = _adamw(w_ref[...], g, m_ref[...], v_ref[...])
        g_out[...] = g
        d_out[...] = delta
        m_out[...] = m_new
        v_out[...] = v_new

    return pl.pallas_call(body, name=name, out_shape=[jax.ShapeDtypeStruct((rows, n), F32)] * 4)(parts, w, m, v)


def _rope_tables(t):
    half = HEAD_DIM // 4
    n_rows = t // GRID_W
    inv_freq = ROPE_THETA ** (-jnp.arange(0, 2 * half, 2, dtype=F32) / (2 * half))
    ang_r = jnp.arange(n_rows, dtype=jnp.int32).astype(F32)[:, None] * inv_freq[None, :]
    ang_c = jnp.arange(GRID_W, dtype=jnp.int32).astype(F32)[:, None] * inv_freq[None, :]

    def by_row(a):
        return jnp.repeat(a, GRID_W, axis=0)

    def by_col(a):
        return jnp.tile(a, (n_rows, 1))

    cos_r, sin_r, cos_c, sin_c = by_row(jnp.cos(ang_r)), by_row(jnp.sin(ang_r)), by_col(jnp.cos(ang_c)), by_col(jnp.sin(ang_c))
    cos = jnp.concatenate([cos_r, cos_r, cos_c, cos_c], axis=-1)
    sin = jnp.concatenate([-sin_r, sin_r, -sin_c, sin_c], axis=-1)
    return cos, sin


def _gather_weight(name, w, cols):
    g = _all_gather(name, w[0].astype(BF16))
    return g if cols else g.reshape(1, N_DEV * g.shape[1], g.shape[2])


def kernel(x, p, norm_mix, w_in, w_dw, conv_ln_g, conv_ln_b, w_conv_proj, q_norm, k_norm, w_attn_proj, w_out, norm_ffn, w_ff1, w_ff2, norm_ple, w_ple_gate, w_ple_proj, norm_final, loss_target, m_norm_mix, m_w_in, m_w_dw, m_conv_ln_g, m_conv_ln_b, m_w_conv_proj, m_q_norm, m_k_norm, m_w_attn_proj, m_w_out, m_norm_ffn, m_w_ff1, m_w_ff2, m_norm_ple, m_w_ple_gate, m_w_ple_proj, m_norm_final, v_norm_mix, v_w_in, v_w_dw, v_conv_ln_g, v_conv_ln_b, v_w_conv_proj, v_q_norm, v_k_norm, v_w_attn_proj, v_w_out, v_norm_ffn, v_w_ff1, v_w_ff2, v_norm_ple, v_w_ple_gate, v_w_ple_proj, v_norm_final):
    t, d = x.shape[1], x.shape[2]
    cw = d // 2
    n_heads = d // HEAD_DIM
    n_kv = n_heads // GROUP
    col_q, col_k, col_v = d // HEAD_DIM, 2 * d // HEAD_DIM, (2 * d + d // 4) // HEAD_DIM
    x0, pe, tgt = x[0], p[0, 0], loss_target[0]
    g_final = norm_final.reshape(1, d)
    me = _index(*_position())

    wg_in = _gather_weight("ag_w_in", w_in, True)
    dw_pad = jnp.pad(w_dw[0], ((0, CONV_TAPS_PAD - CONV_KERNEL), (0, 0)))
    dw_all = _all_gather("ag_w_dw", dw_pad)
    dw_full = dw_all.transpose(1, 0, 2).reshape(CONV_TAPS_PAD, cw)
    cos, sin = _rope_tables(t)

    def fetch(name, w, prev):
        return _exchange_start("ag_start_" + name, w[0].astype(BF16), after=(wg_in, dw_all) if prev is None else (prev,))

    def arrived(name, handle, cols, after):
        g = _exchange_wait("ag_wait_" + name, handle, after)
        return g if cols else g.reshape(1, N_DEV * g.shape[1], g.shape[2])

    on_cp, tok = fetch("w_conv_proj", w_conv_proj, None)
    on_ap, tok = fetch("w_attn_proj", w_attn_proj, tok)
    on_out, tok = fetch("w_out", w_out, tok)
    on_ff1, tok = fetch("w_ff1", w_ff1, tok)
    on_ff2, tok = fetch("w_ff2", w_ff2, tok)
    on_pg, tok = fetch("w_ple_gate", w_ple_gate, tok)
    on_pp, tok = fetch("w_ple_proj", w_ple_proj, tok)

    h1 = _rms_fwd("rms_mix", x0, norm_mix, after=(tok,))
    z = _mm_nn("mm_in", h1, wg_in, F32)
    conv, s_c = _conv_fwd("conv_fwd", z, cw, dw_full, conv_ln_g, conv_ln_b)
    wg_cp = arrived("w_conv_proj", on_cp, True, s_c)
    y_c = _mm_nn("mm_conv_proj", s_c, wg_cp, F32)
    qr = _head_norm_rope("q_prep", z, col_q, n_heads, q_norm, cos, sin, out_scale=QK_SCALE * LOG2_E,
                         heads_per_step=GROUP)
    kr = _head_norm_rope("k_prep", z, col_k, n_kv, k_norm, cos, sin)
    vt = z[:, col_v * HEAD_DIM:(col_v + n_kv) * HEAD_DIM].T.astype(BF16).reshape(n_kv, HEAD_DIM, t)
    vt = jnp.concatenate([vt, jnp.ones((n_kv, ONES_ROWS, t), BF16)], axis=1).reshape(n_kv * (HEAD_DIM + ONES_ROWS), t)
    o, lse = _flash_fwd(qr, kr, vt, n_heads)
    wg_ap = arrived("w_attn_proj", on_ap, False, o)
    y_a, merged = _mm_nn("mm_attn_proj", o, wg_ap, F32, activation=_gated_merge,
                         beside=[(z, 5), (z, 7), (y_c, 0)], col_tile=d // 2)
    wg_out = arrived("w_out", on_out, False, merged)
    x1 = _mm_nn("mm_out", merged, wg_out, F32, residual=x0)
    h2 = _rms_fwd("rms_ffn", x1, norm_ffn)
    wg_ff1 = arrived("w_ff1", on_ff1, True, h2)
    f, act = _mm_nn("mm_ff1", h2, wg_ff1, F32, activation=_relu2)
    wg_ff2 = arrived("w_ff2", on_ff2, False, act)
    x2 = _mm_nn("mm_ff2", act, wg_ff2, F32, residual=x1)
    hp = _rms_fwd("rms_ple", x2, norm_ple)
    wg_pg = arrived("w_ple_gate", on_pg, False, hp)
    pre = _mm_nn("mm_ple_gate", hp, wg_pg, F32)
    wg_pp = arrived("w_ple_proj", on_pp, True, pre)
    pp = _mm_nn("mm_ple_proj", pe, wg_pp, F32)

    dx3, dpp, dpre, dg_final, loss_part = _head_loss("head_loss", x2, pre, pp, tgt, g_final)
    loss = lax.psum(loss_part[0, 0], ("x", "y", "c"))
    def send(name, gw):
        blocks = gw if gw.shape[0] == N_DEV else gw.reshape(N_DEV, gw.shape[1] // N_DEV, gw.shape[2])
        return _exchange_start("rs_start_" + name, blocks)

    sent_pp, tok_pp = send("w_ple_proj", _mm_tn("mm_d_ple_proj", pe, dpp, N_DEV))
    sent_pg, tok = send("w_ple_gate", _mm_tn("mm_d_ple_gate", hp, dpre, 1))
    dhp = _mm_nt("mm_dhp", dpre, wg_pg, F32, after=(tok_pp, tok))
    dx2, dg_ple = _rms_bwd("rms_ple_bwd", x2, norm_ple, dhp, dx3)
    sent_ff2, tok = send("w_ff2", _mm_tn("mm_d_ff2", act, dx2, 1))
    df = _mm_nt("mm_dact", dx2, wg_ff2, BF16, after=(tok,), through=_relu2_bwd, beside=[(f, 0)])
    sent_ff1, tok = send("w_ff1", _mm_tn("mm_d_ff1", h2, df, N_DEV))
    dh2 = _mm_nt("mm_dh2", df, wg_ff1, F32, after=(tok,))
    dx1, dg_ffn = _rms_bwd("rms_ffn_bwd", x1, norm_ffn, dh2, dx2)
    sent_out, tok = send("w_out", _mm_tn("mm_d_out", merged, dx1, 1))
    dgc, dga, dyc, dya = _mm_nt("mm_dmerged", dx1, wg_out, BF16, after=(tok,), through=_gated_merge_bwd,
                                beside=[(z, 5), (z, 7), (y_c, 0), (y_a, 0)], n_results=4, col_tile=d // 2)
    sent_cp, tok = send("w_conv_proj", _mm_tn("mm_d_conv_proj", s_c, dyc, N_DEV))
    ds_c = _mm_nt("mm_ds_c", dyc, wg_cp, F32, after=(tok,))
    dconv, dg_ln, db_ln = _conv_ln_bwd("conv_ln_bwd", ds_c, conv, conv_ln_g, conv_ln_b)
    dca, dcb, dg_dw = _conv_glu_bwd("conv_glu_bwd", dconv, z, cw, dw_full)
    sent_ap, tok = send("w_attn_proj", _mm_tn("mm_d_attn_proj", o, dya, 1))
    do, delta = _mm_nt("mm_do", dya, wg_ap, BF16, after=(tok,), head_dots=o)
    dqt, dkh, dvh = _flash_bwd(qr, kr, kr.T, z, col_v, do, lse, delta, n_heads)
    dq, dg_q = _head_norm_rope_bwd("q_prep_bwd", [(dqt, 0, 1)], z, col_q, n_heads, q_norm, cos, sin,
                                   heads_per_step=GROUP, transposed=True)
    dk, dg_k = _head_norm_rope_bwd("k_prep_bwd", [(dkh, g, GROUP) for g in range(GROUP)], z, col_k, n_kv, k_norm,
                                   cos, sin)
    dv = _group_sum("dv_sum", dvh, n_kv)
    dz = jnp.concatenate([dca, dcb, dq, dk, dv, dgc, dga], axis=1)
    sent_in, tok = send("w_in", _mm_tn("mm_d_in", h1, dz, N_DEV))
    dh1 = _mm_nt("mm_dh1", dz, wg_in, F32, after=(tok,))
    grad_x, dg_mix = _rms_bwd("rms_mix_bwd", x0, norm_mix, dh1, dx1)

    def pad(a):
        return jnp.pad(a, ((0, 0), (0, d - a.shape[1])))

    small = [dg_mix, dg_ffn, dg_ple, dg_final, dg_ln, db_ln, dg_q, dg_k, dg_dw]
    packed = _all_gather("ag_small", jnp.concatenate([pad(a) for a in small], axis=0))
    parts, row = [], 0
    for a in small:
        parts.append(packed[:, row:row + a.shape[0], :a.shape[1]])
        row += a.shape[0]
    p_mix, p_ffn, p_ple, p_final, p_lng, p_lnb, p_q, p_k, p_dw = parts
    p_dw = lax.dynamic_slice_in_dim(p_dw, me * (cw // N_DEV), cw // N_DEV, axis=2)

    def little(name, part, w, m, v):
        rows, n = (1 if w.ndim < 3 else w.shape[1]), w.shape[-1]
        padded = rows if rows == 1 else part.shape[1]

        def two_d(a):
            return jnp.pad(a.reshape(rows, n), ((0, padded - rows), (0, 0)))

        res = _adamw_small("adamw_" + name, part, two_d(w), two_d(m), two_d(v))
        return [r[:rows].reshape(w.shape) for r in res]

    done = {}
    after = packed
    for name, sent, w, m, v in [
            ("w_ple_proj", sent_pp, w_ple_proj, m_w_ple_proj, v_w_ple_proj),
            ("w_ple_gate", sent_pg, w_ple_gate, m_w_ple_gate, v_w_ple_gate),
            ("w_ff2", sent_ff2, w_ff2, m_w_ff2, v_w_ff2),
            ("w_ff1", sent_ff1, w_ff1, m_w_ff1, v_w_ff1),
            ("w_out", sent_out, w_out, m_w_out, v_w_out),
            ("w_conv_proj", sent_cp, w_conv_proj, m_w_conv_proj, v_w_conv_proj),
            ("w_attn_proj", sent_ap, w_attn_proj, m_w_attn_proj, v_w_attn_proj),
            ("w_in", sent_in, w_in, m_w_in, v_w_in)]:
        land = _exchange_wait("rs_wait_" + name, sent, after)
        res = _adamw_shard("adamw_" + name, land, w[0], m[0], v[0])
        after = res[0]
        done[name] = [r.reshape(w.shape) for r in res]

    results = [
        little("norm_mix", p_mix, norm_mix, m_norm_mix, v_norm_mix),
        done["w_in"],
        little("w_dw", p_dw, w_dw, m_w_dw, v_w_dw),
        little("conv_ln_g", p_lng, conv_ln_g, m_conv_ln_g, v_conv_ln_g),
        little("conv_ln_b", p_lnb, conv_ln_b, m_conv_ln_b, v_conv_ln_b),
        done["w_conv_proj"],
        little("q_norm", p_q, q_norm, m_q_norm, v_q_norm),
        little("k_norm", p_k, k_norm, m_k_norm, v_k_norm),
        done["w_attn_proj"],
        done["w_out"],
        little("norm_ffn", p_ffn, norm_ffn, m_norm_ffn, v_norm_ffn),
        done["w_ff1"],
        done["w_ff2"],
        little("norm_ple", p_ple, norm_ple, m_norm_ple, v_norm_ple),
        done["w_ple_gate"],
        done["w_ple_proj"],
        little("norm_final", p_final, norm_final, m_norm_final, v_norm_final),
    ]
    grads, deltas, new_m, new_v = zip(*results)
    return (loss, grad_x[None], *grads, *deltas, *new_m, *new_v)
```

```python
import functools

import jax
import jax.numpy as jnp
from jax import lax
from jax.experimental import pallas as pl
from jax.experimental.pallas import tpu as pltpu

F32 = jnp.float32
BF16 = jnp.bfloat16

EPS = 1e-6
HEAD_DIM = 128
GROUP = 4
CONV_KERNEL = 31
CONV_HALO = 16
CONV_TAPS_PAD = 32
GRID_W = 64
ROPE_THETA = 10000.0
ADAM_LR, ADAM_B1, ADAM_B2, ADAM_EPS, ADAM_WD, ADAM_STEP = 0.001, 0.9, 0.999, 1e-08, 0.01, 10

N_DEV = 8
MESH = pl.DeviceIdType.MESH
SUBLANES = 8
LANES = 128
VMEM_LIMIT = 48 * 1024 * 1024
ROW_TILE_ELEMS = 512 * 1024


def _div(n, pref, mult=1):
    if n <= pref:
        return n
    for d in range(pref, 0, -1):
        if n % d == 0 and d % mult == 0:
            return d
    return n


def _params(sem):
    return pltpu.CompilerParams(dimension_semantics=sem, vmem_limit_bytes=VMEM_LIMIT)


def _sum8(a):
    tm, w = a.shape
    return a.reshape(tm // SUBLANES, SUBLANES, w).sum(axis=0)


def _sigmoid(x):
    return jax.nn.sigmoid(x)


def _rowwise(name, fn, n_rows, tm, ins, outs, accs=(), ncol=1, refs_to_fn=False, after=()):
    nrow = n_rows // tm
    in_specs, arrays = [], []
    for spec in ins:
        kind, arr, w, c0 = spec[:4]
        if kind == "t":
            cs, r0 = spec[4], spec[5]
            in_specs.append(pl.BlockSpec((tm, w), lambda j, i, c0=c0, cs=cs, r0=r0: (r0 + i, c0 + cs * j)))
        elif kind == "p":
            h = spec[4]
            r = tm // h
            in_specs.append(pl.BlockSpec((h, w), lambda j, i, c0=c0, r=r: (jnp.maximum(i * r - 1, 0), c0 + j)))
        elif kind == "n":
            h = spec[4]
            r = tm // h
            last = n_rows // h - 1
            in_specs.append(
                pl.BlockSpec((h, w), lambda j, i, c0=c0, r=r, last=last: (jnp.minimum((i + 1) * r, last), c0 + j)))
        elif kind == "x":
            in_specs.append(pl.BlockSpec((w, tm), lambda j, i, c0=c0: (c0 + j, i)))
        else:
            cs = spec[4]
            in_specs.append(pl.BlockSpec((arr.shape[0], w), lambda j, i, c0=c0, cs=cs: (0, c0 + cs * j)))
        arrays.append(arr)
    out_shape, out_specs = [], []
    for dtype, total, w in outs:
        out_shape.append(jax.ShapeDtypeStruct((n_rows, total), dtype))
        out_specs.append(pl.BlockSpec((tm, w), lambda j, i: (i, j)))
    for rows, total, w, follow in accs:
        out_shape.append(jax.ShapeDtypeStruct((rows, total), F32))
        out_specs.append(pl.BlockSpec((rows, w), (lambda j, i: (0, j)) if follow else (lambda j, i: (0, 0))))
    n_in, n_out, n_acc = len(ins), len(outs), len(accs)
    for token in after:
        in_specs.append(pl.BlockSpec(memory_space=pl.ANY))
        arrays.append(token)
    out0 = n_in + len(after)

    def body(*refs):
        j, i = pl.program_id(0), pl.program_id(1)
        res = fn(i, nrow, *(refs[:n_in] if refs_to_fn else [r[...] for r in refs[:n_in]]))
        if not isinstance(res, (tuple, list)):
            res = (res,)
        for k in range(n_out):
            refs[out0 + k][...] = res[k].astype(outs[k][0])
        for k in range(n_acc):
            ref, term = refs[out0 + n_out + k], res[n_out + k]
            first = (i == 0) if accs[k][3] else jnp.logical_and(i == 0, j == 0)

            @pl.when(first)
            def _():
                ref[...] = term

            @pl.when(jnp.logical_not(first))
            def _():
                ref[...] += term

    res = pl.pallas_call(
        body, name=name, grid=(ncol, nrow), in_specs=in_specs, out_specs=out_specs, out_shape=out_shape,
        compiler_params=_params(("arbitrary", "arbitrary")))(*arrays)
    return res


def _row_tile(n_rows, width, mult=SUBLANES):
    return _div(n_rows, max(mult, ROW_TILE_ELEMS // width), mult)


def _t(arr, w=None, col0=0, cstride=1, row0=0):
    return ("t", arr, arr.shape[1] if w is None else w, col0, cstride, row0)


def _b(arr, w=None, col0=0, cstride=0):
    return ("b", arr, arr.shape[1] if w is None else w, col0, cstride)


def _rms_fwd(name, x, g, after=()):
    n, d = x.shape

    def fn(i, nrow, x, g):
        r = lax.rsqrt(jnp.mean(x * x, axis=-1, keepdims=True) + EPS)
        return x * r * g

    return _rowwise(name, fn, n, _row_tile(n, d // 4), [_t(x), _b(g)], [(BF16, d, d)], after=after)[0]


def _rms_bwd(name, x, g, dh, dres):
    n, d = x.shape

    def fn(i, nrow, x, g, dh, dres):
        r = lax.rsqrt(jnp.mean(x * x, axis=-1, keepdims=True) + EPS)
        w = dh * g
        dx = dres + r * w - x * (r * r * r) * jnp.mean(x * w, axis=-1, keepdims=True)
        return dx, _sum8(dh * x * r)

    return _rowwise(name, fn, n, _row_tile(n, d // 2), [_t(x), _b(g), _t(dh), _t(dres)], [(F32, d, d)],
                    [(SUBLANES, d, d, True)])


def _swap_halves(x):
    lane = lax.broadcasted_iota(jnp.int32, x.shape, 1)
    return jnp.where(lane % 64 < 32, pltpu.roll(x, 96, axis=1), pltpu.roll(x, 32, axis=1))


def _head_norm_rope(name, z, col0, n_heads, g, cos, sin, out_scale=None, heads_per_step=1):
    n, hp = z.shape[0], heads_per_step
    assert col0 % hp == 0 and n_heads % hp == 0

    def fn(i, nrow, x, g, cos, sin):
        ys = []
        for h in range(hp):
            xh = x[:, h * HEAD_DIM:(h + 1) * HEAD_DIM]
            r = lax.rsqrt(jnp.mean(xh * xh, axis=-1, keepdims=True) + EPS)
            y = xh * r * g
            y = y * cos + _swap_halves(y) * sin
            ys.append(y if out_scale is None else y * out_scale)
        return ys[0] if hp == 1 else jnp.concatenate(ys, axis=1)

    tm = _div(n, 1024, SUBLANES)
    w = hp * HEAD_DIM
    return _rowwise(name, fn, n, tm, [_t(z, w, col0 // hp), _b(g), _t(cos, cstride=0), _t(sin, cstride=0)],
                    [(BF16, n_heads * HEAD_DIM, w)], ncol=n_heads // hp)[0]


def _head_norm_rope_bwd(name, douts, z, col0, n_heads, g, cos, sin, heads_per_step=1, transposed=False):
    n, hp = z.shape[0], heads_per_step
    n_d = len(douts)
    assert col0 % hp == 0 and n_heads % hp == 0 and (hp == 1 or all(c0 % hp == 0 and cs == 1 for _, c0, cs in douts))
    assert not transposed or n_d == 1

    def fn(i, nrow, *a):
        dy = a[0]
        for k in range(1, n_d):
            dy = dy + a[k]
        x, g, cos, sin = a[n_d:]
        dxs, dg = [], None
        for h in range(hp):
            xh = x[:, h * HEAD_DIM:(h + 1) * HEAD_DIM]
            if transposed:
                dyh = jnp.transpose(dy[h * HEAD_DIM:(h + 1) * HEAD_DIM, :])
            else:
                dyh = dy[:, h * HEAD_DIM:(h + 1) * HEAD_DIM]
            dn = dyh * cos + _swap_halves(dyh * sin)
            r = lax.rsqrt(jnp.mean(xh * xh, axis=-1, keepdims=True) + EPS)
            w = dn * g
            dxs.append(r * w - xh * (r * r * r) * jnp.mean(xh * w, axis=-1, keepdims=True))
            term = _sum8(dn * xh * r)
            dg = term if dg is None else dg + term
        return (dxs[0] if hp == 1 else jnp.concatenate(dxs, axis=1)), dg

    tm = _div(n, 1024, SUBLANES)
    w = hp * HEAD_DIM
    if transposed:
        ins = [("x", douts[0][0], w, douts[0][1] // hp)]
    else:
        ins = [_t(arr, w, c0 // hp, cs) for arr, c0, cs in douts]
    ins += [_t(z, w, col0 // hp), _b(g), _t(cos, cstride=0), _t(sin, cstride=0)]
    return _rowwise(name, fn, n, tm, ins, [(BF16, n_heads * HEAD_DIM, w)],
                    [(SUBLANES, HEAD_DIM, HEAD_DIM, False)], ncol=n_heads // hp)


def _halo(arr, w, col0):
    return [("p", arr, w, col0, CONV_HALO), _t(arr, w, col0), ("n", arr, w, col0, CONV_HALO)]


def _extend(i, nrow, prev, cur, nxt):
    prev = jnp.where(i > 0, prev, 0.0)
    nxt = jnp.where(i < nrow - 1, nxt, 0.0)
    return jnp.concatenate([prev, cur, nxt], axis=0)


def _shifted(ext, offset, tm):
    n = ext.shape[0]
    rolled = ext if offset == 0 else pltpu.roll(ext, (-offset) % n, axis=0)
    return rolled[CONV_HALO:CONV_HALO + tm]


def _conv_fwd(name, z, cw, w_dw, ln_g, ln_b):
    n = z.shape[0]
    tm = _row_tile(n, 4 * cw, CONV_HALO)

    def fn(i, nrow, ap, a, an, bp, b, bn, w, g, beta):
        a = a[...]
        ext = _extend(i, nrow, ap[...] * _sigmoid(bp[...]), a * _sigmoid(b[...]), an[...] * _sigmoid(bn[...]))
        conv = jnp.zeros_like(a)
        for k in range(CONV_KERNEL):
            conv = conv + _shifted(ext, k - CONV_KERNEL // 2, tm) * w[k:k + 1, :]
        xc = conv - jnp.mean(conv, axis=-1, keepdims=True)
        ln = xc * lax.rsqrt(jnp.mean(xc * xc, axis=-1, keepdims=True) + EPS) * g[...] + beta[...]
        return conv, ln * _sigmoid(ln)

    ins = _halo(z, cw, 0) + _halo(z, cw, 1) + [_b(w_dw), _b(ln_g), _b(ln_b)]
    return _rowwise(name, fn, n, tm, ins, [(F32, cw, cw), (BF16, cw, cw)], refs_to_fn=True)


def _conv_ln_bwd(name, ds, conv, ln_g, ln_b):
    n, cw = conv.shape

    def fn(i, nrow, ds, conv, g, beta):
        xc = conv - jnp.mean(conv, axis=-1, keepdims=True)
        rstd = lax.rsqrt(jnp.mean(xc * xc, axis=-1, keepdims=True) + EPS)
        xhat = xc * rstd
        ln = xhat * g + beta
        sg = _sigmoid(ln)
        dln = ds * (sg * (1.0 + ln * (1.0 - sg)))
        dxh = dln * g
        dconv = rstd * (dxh - jnp.mean(dxh, axis=-1, keepdims=True)
                        - xhat * jnp.mean(dxh * xhat, axis=-1, keepdims=True))
        return dconv, _sum8(dln * xhat), _sum8(dln)

    return _rowwise(name, fn, n, _row_tile(n, cw // 2), [_t(ds), _t(conv), _b(ln_g), _b(ln_b)], [(F32, cw, cw)],
                    [(SUBLANES, cw, cw, True), (SUBLANES, cw, cw, True)])


def _conv_glu_bwd(name, dconv, z, cw, w_dw):
    n = z.shape[0]
    tm = _row_tile(n, 4 * cw, CONV_HALO)

    def fn(i, nrow, dp, d, dn, a, b, w):
        d, a = d[...], a[...]
        dext = _extend(i, nrow, dp[...], d, dn[...])
        sg = _sigmoid(b[...])
        u = a * sg
        tap = lax.broadcasted_iota(jnp.int32, (CONV_TAPS_PAD, cw), 0)
        du = jnp.zeros_like(d)
        dw = jnp.zeros((CONV_TAPS_PAD, cw), F32)
        for k in range(CONV_KERNEL):
            d_k = _shifted(dext, -(k - CONV_KERNEL // 2), tm)
            du = du + d_k * w[k:k + 1, :]
            dw = dw + jnp.where(tap == k, jnp.sum(d_k * u, axis=0, keepdims=True), 0.0)
        return du * sg, du * a * sg * (1.0 - sg), dw

    ins = _halo(dconv, cw, 0) + [_t(z, cw, 0), _t(z, cw, 1), _b(w_dw)]
    return _rowwise(name, fn, n, tm, ins, [(BF16, cw, cw), (BF16, cw, cw)], [(CONV_TAPS_PAD, cw, cw, True)],
                    refs_to_fn=True)


def _gated_merge(y_a, gc, ga, y_c):
    return _sigmoid(gc) * y_c + _sigmoid(ga) * y_a


def _gated_merge_bwd(dm, gc, ga, yc, ya):
    sc, sa = _sigmoid(gc), _sigmoid(ga)
    return dm * yc * sc * (1.0 - sc), dm * ya * sa * (1.0 - sa), dm * sc, dm * sa


def _relu2(f):
    r = jnp.maximum(f, 0.0)
    return r * r


def _relu2_bwd(da, f):
    return da * (2.0 * jnp.maximum(f, 0.0))


def _head_loss(name, x2, pre, pp, target, g):
    n, d = x2.shape

    def fn(i, nrow, x2, pre, pp, t, g):
        gate = _sigmoid(pre)
        x3 = x2 + gate * pp
        r = lax.rsqrt(jnp.mean(x3 * x3, axis=-1, keepdims=True) + EPS)
        xn = x3 * r
        e = xn * g - t
        part = 0.5 * jnp.sum(jnp.mean(e * e, axis=-1, keepdims=True), axis=0, keepdims=True)
        dy = e * (1.0 / d)
        w = dy * g
        dx3 = r * w - x3 * (r * r * r) * jnp.mean(x3 * w, axis=-1, keepdims=True)
        return (dx3, dx3 * gate, dx3 * pp * gate * (1.0 - gate), _sum8(dy * xn),
                jnp.broadcast_to(part, (SUBLANES, LANES)))

    ins = [_t(x2), _t(pre), _t(pp), _t(target), _b(g)]
    return _rowwise(name, fn, n, _row_tile(n, d), ins, [(F32, d, d), (BF16, d, d), (BF16, d, d)],
                    [(SUBLANES, d, d, True), (SUBLANES, LANES, LANES, True)])


def _group_sum(name, dvh, n_kv):
    n = dvh.shape[0]

    def fn(i, nrow, *a):
        s = a[0]
        for k in range(1, GROUP):
            s = s + a[k]
        return s

    ins = [_t(dvh, HEAD_DIM, k, GROUP) for k in range(GROUP)]
    return _rowwise(name, fn, n, _div(n, 1024, SUBLANES), ins, [(BF16, n_kv * HEAD_DIM, HEAD_DIM)], ncol=n_kv)[0]


MM_ROWS = 1024
MM_COLS = 1152
MM_REDUCE = 2048
MM_REDUCE_BLOCKS = 2304


def _rows_outermost(n_rows, n_cols, row_tile_bytes, weight_tile_bytes):
    rows_outer = n_rows * (row_tile_bytes + n_cols * weight_tile_bytes)
    weight_outer = n_cols * (weight_tile_bytes + n_rows * row_tile_bytes)
    return 5 * rows_outer < 4 * weight_outer


def _accumulate(step, n_steps, part, acc, finish):
    if n_steps == 1:
        finish(part)
        return

    @pl.when(step == 0)
    def _():
        acc[...] = part

    @pl.when(jnp.logical_and(step > 0, step < n_steps - 1))
    def _():
        acc[...] += part

    @pl.when(step == n_steps - 1)
    def _():
        finish(acc[...] + part)


MXU_WIDTH = 256


def _paired(nb, ns):
    return ns % MXU_WIDTH != 0 and nb % 2 == 0 and ns <= MM_COLS


def _mm_nn(name, a, w3, out_dtype, residual=None, activation=None, beside=(), col_tile=None):
    m, k = a.shape
    nb, _, ns = w3.shape
    jb = 2 if _paired(nb, ns) else 1
    tm = _div(m, MM_ROWS // max(jb, 1 + len(beside) // 2), SUBLANES)
    tn_w, tk = (ns if jb > 1 else _div(ns, col_tile or MM_COLS, LANES)), _div(k, MM_REDUCE, LANES)
    tn = jb * tn_w
    per, nk = ns // tn_w, k // tk
    has_res = residual is not None
    n_out = 1 if activation is None else 2
    first_beside = 3 if has_res else 2

    def body(*refs):
        a_ref, w_ref = refs[:2]
        acc = refs[-1]
        outs = refs[-1 - n_out:-1]
        w = w_ref[0] if jb == 1 else jnp.concatenate([w_ref[j] for j in range(jb)], axis=1)
        part = jnp.dot(a_ref[...].astype(BF16), w.astype(BF16), preferred_element_type=F32)

        def finish(total):
            if has_res:
                total = total + refs[2][...]
            outs[0][...] = total.astype(out_dtype)
            if activation is not None:
                tiles = [r[...] for r in refs[first_beside:first_beside + len(beside)]]
                outs[1][...] = activation(total, *tiles).astype(BF16)

        _accumulate(pl.program_id(2), nk, part, acc, finish)

    n_rows, n_cols = m // tm, (nb // jb) * per
    rows_outer = nk > 1 or _rows_outermost(n_rows, n_cols, tm * tk * a.dtype.itemsize,
                                           jb * tk * tn_w * w3.dtype.itemsize)

    def at(f):
        return f if rows_outer else (lambda n, i, kk: f(i, n, kk))

    in_specs = [pl.BlockSpec((tm, tk), at(lambda i, n, kk: (i, kk))),
                pl.BlockSpec((jb, tk, tn_w), at(lambda i, n, kk: (n // per, kk, n % per)))]
    args = [a, w3]
    if has_res:
        in_specs.append(pl.BlockSpec((tm, tn), at(lambda i, n, kk: (i, n))))
        args.append(residual)
    for arr, col0 in beside:
        in_specs.append(pl.BlockSpec((tm, tn), at(lambda i, n, kk, col0=col0: (i, col0 + n))))
        args.append(arr)
    out_dtypes = [out_dtype] + ([BF16] if activation is not None else [])
    res = pl.pallas_call(
        body, name=name, grid=(n_rows, n_cols, nk) if rows_outer else (n_cols, n_rows, nk), in_specs=in_specs,
        out_specs=[pl.BlockSpec((tm, tn), at(lambda i, n, kk: (i, n))) for _ in out_dtypes],
        out_shape=[jax.ShapeDtypeStruct((m, nb * ns), dt) for dt in out_dtypes],
        scratch_shapes=[pltpu.VMEM((tm, tn) if nk > 1 else (SUBLANES, LANES), F32)],
        compiler_params=_params(("parallel", "parallel", "arbitrary")))(*args)
    return res[0] if activation is None else res


def _mm_nt(name, dc, w3, out_dtype, after=(), through=None, beside=(), n_results=1, head_dots=None, col_tile=None):
    m = dc.shape[0]
    nb, k, ns = w3.shape
    tm = _div(m, MM_ROWS // (1 + len(beside) // 3), SUBLANES)
    to = _div(k, col_tile or MM_ROWS, LANES)
    if ns > MM_REDUCE_BLOCKS // 2:
        jb, tr = 1, _div(ns, MM_REDUCE, LANES)
    else:
        jb, tr = max(j for j in range(1, nb + 1) if nb % j == 0 and j * ns <= MM_REDUCE_BLOCKS), ns
    per = ns // tr
    nr = (nb // jb) * per
    tiles_in = list(beside) + ([(head_dots, 0)] if head_dots is not None else [])
    n_out = n_results + (1 if head_dots is not None else 0)
    heads = to // HEAD_DIM

    def body(dc_ref, w_ref, *rest):
        acc = rest[-1]
        outs = rest[-1 - n_out:-1]
        w = w_ref[0] if jb == 1 else jnp.concatenate([w_ref[j] for j in range(jb)], axis=1)
        part = lax.dot_general(dc_ref[...].astype(BF16), w.astype(BF16), (((1,), (1,)), ((), ())),
                               preferred_element_type=F32)

        def finish(total):
            results = (total,) if through is None else through(total, *[r[...] for r in rest[:len(beside)]])
            if not isinstance(results, (tuple, list)):
                results = (results,)
            for out, res in zip(outs, results):
                out[...] = res.astype(out_dtype)
            if head_dots is not None:
                prod = results[0].astype(out_dtype).astype(F32) * rest[len(beside)][...]
                for h in range(heads):
                    row = jnp.sum(jnp.transpose(prod[:, h * HEAD_DIM:(h + 1) * HEAD_DIM]), axis=0, keepdims=True)
                    outs[n_results][h] = jnp.broadcast_to(row, (SUBLANES, tm))

        _accumulate(pl.program_id(2), nr, part, acc, finish)

    rows_outer = nr > 1 or _rows_outermost(m // tm, k // to, tm * jb * tr * dc.dtype.itemsize,
                                           jb * to * tr * w3.dtype.itemsize)

    def at(f):
        return f if rows_outer else (lambda o, i, r: f(i, o, r))

    in_specs = [pl.BlockSpec((tm, jb * tr), at(lambda i, o, r: (i, r))),
                pl.BlockSpec((jb, to, tr), at(lambda i, o, r: (r // per, o, r % per)))]
    args = [dc, w3]
    for arr, col0 in tiles_in:
        in_specs.append(pl.BlockSpec((tm, to), at(lambda i, o, r, col0=col0: (i, col0 + o))))
        args.append(arr)
    for token in after:
        in_specs.append(pl.BlockSpec(memory_space=pl.ANY))
        args.append(token)
    out_specs = [pl.BlockSpec((tm, to), at(lambda i, o, r: (i, o))) for _ in range(n_results)]
    out_shape = [jax.ShapeDtypeStruct((m, k), out_dtype) for _ in range(n_results)]
    if head_dots is not None:
        out_specs.append(pl.BlockSpec((heads, SUBLANES, tm), at(lambda i, o, r: (o, 0, i))))
        out_shape.append(jax.ShapeDtypeStruct((k // HEAD_DIM, SUBLANES, m), F32))
    res = pl.pallas_call(
        body, name=name, grid=(m // tm, k // to, nr) if rows_outer else (k // to, m // tm, nr), in_specs=in_specs,
        out_specs=out_specs, out_shape=out_shape,
        scratch_shapes=[pltpu.VMEM((tm, to) if nr > 1 else (SUBLANES, LANES), F32)],
        compiler_params=_params(("parallel", "parallel", "arbitrary")))(*args)
    return res[0] if n_out == 1 else res


def _mm_tn(name, a, dc, nb):
    m, k = a.shape
    ns = dc.shape[1] // nb
    jb = 2 if _paired(nb, ns) else 1
    tr, to = _div(m, MM_REDUCE, LANES), _div(k, MM_ROWS // jb, LANES)
    tn_w = ns if jb > 1 else _div(ns, MM_COLS, LANES)
    tn = jb * tn_w
    per, nr = ns // tn_w, m // tr

    def body(a_ref, dc_ref, o_ref, acc):
        part = lax.dot_general(a_ref[...].astype(BF16), dc_ref[...].astype(BF16), (((0,), (0,)), ((), ())),
                               preferred_element_type=F32)

        def finish(total):
            for j in range(jb):
                o_ref[j] = total[:, j * tn_w:(j + 1) * tn_w].astype(BF16)

        _accumulate(pl.program_id(2), nr, part, acc, finish)

    return pl.pallas_call(
        body, name=name, grid=(k // to, (nb // jb) * per, nr),
        in_specs=[pl.BlockSpec((tr, to), lambda o, n, r: (r, o)),
                  pl.BlockSpec((tr, tn), lambda o, n, r: (r, n))],
        out_specs=pl.BlockSpec((jb, to, tn_w), lambda o, n, r: (n // per, o, n % per)),
        out_shape=jax.ShapeDtypeStruct((nb, k, ns), BF16),
        scratch_shapes=[pltpu.VMEM((to, tn) if nr > 1 else (SUBLANES, LANES), F32)],
        compiler_params=_params(("parallel", "parallel", "arbitrary")))(a, dc)


QK_SCALE = HEAD_DIM ** -0.5
LOG2_E = 1.4426950408889634
LN_2 = 0.6931471805599453


def _columns(row, n):
    return jnp.transpose(jnp.broadcast_to(row, (LANES, n)))


FLASH_TILE = 1024
ONES_ROWS = 16


def _flash_fwd(qr, kr, vt, n_heads):
    t = qr.shape[0]
    tq, tk = _div(t, 2 * FLASH_TILE, LANES), _div(t, FLASH_TILE, LANES)
    nk = t // tk
    rows = vt.shape[0] // (n_heads // GROUP)

    def body(q_ref, k_ref, vt_ref, o_ref, lse_ref, m_scr, acc_scr):
        kb = pl.program_id(2)

        @pl.when(kb == 0)
        def _():
            m_scr[...] = jnp.full_like(m_scr, -1e30)
            acc_scr[...] = jnp.zeros_like(acc_scr)

        s_t = lax.dot_general(k_ref[...], q_ref[...], (((1,), (1,)), ((), ())), preferred_element_type=F32)
        m_prev = m_scr[...]
        m_new = jnp.maximum(m_prev, jnp.max(s_t, axis=0, keepdims=True))
        alpha = jnp.exp2(m_prev - m_new)
        p_t = jnp.exp2(s_t - m_new).astype(BF16)
        acc_scr[...] = alpha * acc_scr[...] + jnp.dot(vt_ref[...], p_t, preferred_element_type=F32)
        m_scr[...] = m_new

        @pl.when(kb == nk - 1)
        def _():
            l = acc_scr[HEAD_DIM:HEAD_DIM + 1, :]
            o_ref[...] = jnp.transpose(acc_scr[0:HEAD_DIM, :] / l)
            lse_ref[...] = jnp.broadcast_to(m_scr[...] + jnp.log(l) * LOG2_E, (SUBLANES, tq))

    return pl.pallas_call(
        body, name="flash_fwd", grid=(n_heads, t // tq, nk),
        in_specs=[pl.BlockSpec((tq, HEAD_DIM), lambda h, qi, ki: (qi, h)),
                  pl.BlockSpec((tk, HEAD_DIM), lambda h, qi, ki: (ki, h // GROUP)),
                  pl.BlockSpec((rows, tk), lambda h, qi, ki: (h // GROUP, ki))],
        out_specs=[pl.BlockSpec((tq, HEAD_DIM), lambda h, qi, ki: (qi, h)),
                   pl.BlockSpec((None, SUBLANES, tq), lambda h, qi, ki: (h, 0, qi))],
        out_shape=[jax.ShapeDtypeStruct((t, n_heads * HEAD_DIM), F32),
                   jax.ShapeDtypeStruct((n_heads, SUBLANES, t), F32)],
        scratch_shapes=[pltpu.VMEM((1, tq), F32), pltpu.VMEM((rows, tq), F32)],
        compiler_params=_params(("parallel", "parallel", "arbitrary")))(qr, kr, vt)


def _flash_bwd(qr, kr, kt, z, v_col0, do, lse, delta, n_heads):
    t = qr.shape[0]
    tq, tk = _div(t, 2 * FLASH_TILE, LANES), _div(t, FLASH_TILE, LANES)
    nq = t // tq

    def body(q_ref, k_ref, kt_ref, v_ref, do_ref, lse_ref, dl_ref, dq_ref, dk_ref, dv_ref, dk_acc, dv_acc):
        kb, qb = pl.program_id(1), pl.program_id(2)
        q, k, do_ = q_ref[...], k_ref[...], do_ref[...]
        v = v_ref[...].astype(BF16)
        s_t = lax.dot_general(k, q, (((1,), (1,)), ((), ())), preferred_element_type=F32)
        p_t = jnp.exp2(s_t - lse_ref[0:1, :])
        dp_t = lax.dot_general(v, do_, (((1,), (1,)), ((), ())), preferred_element_type=F32)
        ds_t = (p_t * (dp_t - dl_ref[0:1, :])).astype(BF16)
        dv_c = jnp.dot(p_t.astype(BF16), do_, preferred_element_type=F32)
        dk_c = jnp.dot(ds_t, q, preferred_element_type=F32) * LN_2
        dq_c = jnp.dot(kt_ref[...], ds_t, preferred_element_type=F32) * QK_SCALE

        @pl.when(qb == 0)
        def _():
            dk_acc[...] = dk_c
            dv_acc[...] = dv_c

        @pl.when(qb > 0)
        def _():
            dk_acc[...] += dk_c
            dv_acc[...] += dv_c

        @pl.when(qb == nq - 1)
        def _():
            dk_ref[...] = dk_acc[...]
            dv_ref[...] = dv_acc[...]

        cols = pl.ds(pl.multiple_of(qb * tq, tq), tq)

        @pl.when(kb == 0)
        def _():
            dq_ref[:, cols] = dq_c

        @pl.when(kb > 0)
        def _():
            dq_ref[:, cols] += dq_c

    wide = jax.ShapeDtypeStruct((t, n_heads * HEAD_DIM), F32)
    return pl.pallas_call(
        body, name="flash_bwd", grid=(n_heads, t // tk, nq),
        in_specs=[pl.BlockSpec((tq, HEAD_DIM), lambda h, kb, qb: (qb, h)),
                  pl.BlockSpec((tk, HEAD_DIM), lambda h, kb, qb: (kb, h // GROUP)),
                  pl.BlockSpec((HEAD_DIM, tk), lambda h, kb, qb: (h // GROUP, kb)),
                  pl.BlockSpec((tk, HEAD_DIM), lambda h, kb, qb: (kb, v_col0 + h // GROUP)),
                  pl.BlockSpec((tq, HEAD_DIM), lambda h, kb, qb: (qb, h)),
                  pl.BlockSpec((None, SUBLANES, tq), lambda h, kb, qb: (h, 0, qb)),
                  pl.BlockSpec((None, SUBLANES, tq), lambda h, kb, qb: (h, 0, qb))],
        out_specs=[pl.BlockSpec((HEAD_DIM, t), lambda h, kb, qb: (h, 0)),
                   pl.BlockSpec((tk, HEAD_DIM), lambda h, kb, qb: (kb, h)),
                   pl.BlockSpec((tk, HEAD_DIM), lambda h, kb, qb: (kb, h))],
        out_shape=[jax.ShapeDtypeStruct((n_heads * HEAD_DIM, t), F32), wide, wide],
        scratch_shapes=[pltpu.VMEM((tk, HEAD_DIM), F32), pltpu.VMEM((tk, HEAD_DIM), F32)],
        compiler_params=_params(("parallel", "arbitrary", "arbitrary")))(qr, kr, kt, z, do, lse, delta)


def _position():
    x, y, c = lax.axis_index("x"), lax.axis_index("y"), lax.axis_index("c")
    return x, y, c


def _index(x, y, c):
    return 4 * x + 2 * y + c


def _all_gather(name, shard):
    a, b = shard.shape

    def body(x_ref, out_ref, send_sems, recv_sems, local_sem):
        x, y, c = _position()
        me, sibling = (x, y, c), (x, y, 1 - c)
        chips = [(1 - x, y), (x, 1 - y), (1 - x, 1 - y)]

        def block(px, py, pc):
            return out_ref.at[_index(px, py, pc)]

        def copy(k, blk, to, src=None):
            return pltpu.make_async_remote_copy(
                src_ref=block(*blk) if src is None else src, dst_ref=block(*blk),
                send_sem=send_sems.at[k], recv_sem=recv_sems.at[k], device_id=to, device_id_type=MESH)

        mine = pltpu.make_async_copy(x_ref, block(*me), local_sem)
        mine.start()
        first = [copy(0, me, sibling, src=x_ref)]
        first += [copy(1 + j, me, (*chip, c), src=x_ref) for j, chip in enumerate(chips)]
        for cp in first:
            cp.start()
        passed = [copy(4 + j, (*chip, c), sibling) for j, chip in enumerate(chips)]
        for j, chip in enumerate(chips):
            copy(1 + j, (*chip, c), me).wait_recv()
            passed[j].start()
        copy(0, sibling, me).wait_recv()
        for j, chip in enumerate(chips):
            copy(4 + j, (*chip, 1 - c), me).wait_recv()
        for cp in first + passed:
            cp.wait_send()
        mine.wait()

    return pl.pallas_call(
        body, name=name, out_shape=jax.ShapeDtypeStruct((N_DEV, a, b), shard.dtype),
        in_specs=[pl.BlockSpec(memory_space=pltpu.HBM)], out_specs=pl.BlockSpec(memory_space=pltpu.HBM),
        scratch_shapes=[pltpu.SemaphoreType.DMA((7,)), pltpu.SemaphoreType.DMA((7,)), pltpu.SemaphoreType.DMA],
    )(shard)


HBM_SPEC = pl.BlockSpec(memory_space=pltpu.HBM)
SEM_SPEC = pl.BlockSpec(memory_space=pltpu.SEMAPHORE)
N_PEERS = N_DEV - 1
PEER_FLIPS = [(fx, fy, fc) for fx in (0, 1) for fy in (0, 1) for fc in (0, 1)][1:]


def _peers(x, y, c):
    return [((1 - x) if fx else x, (1 - y) if fy else y, (1 - c) if fc else c) for fx, fy, fc in PEER_FLIPS]


def _exchange_copies(g_ref, land_ref, send_sems, recv_sems):
    x, y, c = _position()
    me = _index(x, y, c)
    peers = _peers(x, y, c)
    blocked = len(g_ref.shape) == 3

    def copy(k, src_block, dst_block):
        return pltpu.make_async_remote_copy(
            src_ref=g_ref.at[src_block] if blocked else g_ref, dst_ref=land_ref.at[dst_block],
            send_sem=send_sems.at[k], recv_sem=recv_sems.at[k], device_id=peers[k], device_id_type=MESH)

    sent = [copy(k, _index(*peer), me) for k, peer in enumerate(peers)]
    arriving = [copy(k, me, _index(*peer)) for k, peer in enumerate(peers)]
    own = pltpu.make_async_copy(g_ref.at[me] if blocked else g_ref, land_ref.at[me], send_sems.at[N_PEERS])
    return sent, arriving, own


def _exchange_start(name, g3, after=()):
    def body(g_ref, land_ref, *rest):
        send_sems, recv_sems, _, _, token = rest[len(after):]
        sent, _, own = _exchange_copies(g_ref, land_ref, send_sems, recv_sems)
        for cp in sent:
            cp.start()
        own.start()
        token[...] = jnp.zeros_like(token)

    land_shape = g3.shape if g3.ndim == 3 else (N_DEV,) + g3.shape
    land = lax.empty(land_shape, g3.dtype)
    send_sems, recv_sems, g_thru, land_thru, token = pl.pallas_call(
        body, name=name,
        out_shape=(pltpu.SemaphoreType.DMA((N_PEERS + 1,)), pltpu.SemaphoreType.DMA((N_PEERS,)),
                   pltpu.HBM(g3.shape, g3.dtype), pltpu.HBM(land_shape, g3.dtype),
                   jax.ShapeDtypeStruct((SUBLANES, LANES), F32)),
        in_specs=(HBM_SPEC, HBM_SPEC) + (pl.BlockSpec(memory_space=pl.ANY),) * len(after),
        out_specs=(SEM_SPEC, SEM_SPEC, HBM_SPEC, HBM_SPEC, pl.BlockSpec(memory_space=pltpu.VMEM)),
        input_output_aliases={0: 2, 1: 3},
        compiler_params=pltpu.CompilerParams(has_side_effects=pltpu.SideEffectType.DATAFLOW_SIDE_EFFECTING),
    )(pltpu.with_memory_space_constraint(g3, pltpu.HBM), pltpu.with_memory_space_constraint(land, pltpu.HBM), *after)
    return (send_sems, recv_sems, g_thru, land_thru), token


def _exchange_wait(name, handle, after):
    send_sems, recv_sems, g_thru, land_thru = handle

    def body(g_ref, land_ref, send_sems, recv_sems, after_ref, g_dead, land_out):
        sent, arriving, own = _exchange_copies(g_ref, land_ref, send_sems, recv_sems)
        for cp in sent:
            cp.wait_send()
        for cp in arriving:
            cp.wait_recv()
        own.wait()

    return pl.pallas_call(
        body, name=name,
        out_shape=(pltpu.HBM(g_thru.shape, g_thru.dtype), pltpu.HBM(land_thru.shape, land_thru.dtype)),
        in_specs=(HBM_SPEC, HBM_SPEC, SEM_SPEC, SEM_SPEC, pl.BlockSpec(memory_space=pl.ANY)),
        out_specs=(HBM_SPEC, HBM_SPEC), input_output_aliases={0: 0, 1: 1},
        compiler_params=pltpu.CompilerParams(has_side_effects=pltpu.SideEffectType.DATAFLOW_SIDE_EFFECTING),
    )(g_thru, land_thru, send_sems, recv_sems, after)[1]


def _adamw(w, g, m, v):
    m = ADAM_B1 * m + (1.0 - ADAM_B1) * g
    v = ADAM_B2 * v + (1.0 - ADAM_B2) * (g * g)
    m_hat = m / (1.0 - ADAM_B1 ** ADAM_STEP)
    v_hat = v / (1.0 - ADAM_B2 ** ADAM_STEP)
    delta = -ADAM_LR * (m_hat / (jnp.sqrt(v_hat) + ADAM_EPS) + ADAM_WD * w)
    return delta, m, v


def _adamw_shard(name, land, w, m, v):
    _, a, b = land.shape
    tm = _div(a, max(16, ROW_TILE_ELEMS // (2 * b)), 16)
    per = a // tm

    def fn(i, nrow, *t):
        g = t[0].astype(F32)
        for s in range(1, N_DEV):
            g = g + t[s].astype(F32)
        w, m, v = t[N_DEV:]
        return (g,) + _adamw(w, g, m, v)

    land2 = land.reshape(N_DEV * a, b)
    ins = [_t(land2, row0=s * per) for s in range(N_DEV)] + [_t(w), _t(m), _t(v)]
    return _rowwise(name, fn, a, tm, ins, [(F32, b, b)] * 4)


def _adamw_small(name, parts, w, m, v):
    rows, n = w.shape

    def body(p_ref, w_ref, m_ref, v_ref, g_out, d_out, m_out, v_out):
        g = p_ref[0]
        for s in range(1, parts.shape[0]):
            g = g + p_ref[s]
        g = jnp.sum(g, axis=0, keepdims=True) if rows == 1 else g[0:rows, :]
        delta, m_new, v_new = _adamw(w_ref[...], g, m_ref[...], v_ref[...])
        g_out[...] = g
        d_out[...] = delta
        m_out[...] = m_new
        v_out[...] = v_new

    return pl.pallas_call(body, name=name, out_shape=[jax.ShapeDtypeStruct((rows, n), F32)] * 4)(parts, w, m, v)


def _rope_tables(t):
    half = HEAD_DIM // 4
    n_rows = t // GRID_W
    inv_freq = ROPE_THETA ** (-jnp.arange(0, 2 * half, 2, dtype=F32) / (2 * half))
    ang_r = jnp.arange(n_rows, dtype=jnp.int32).astype(F32)[:, None] * inv_freq[None, :]
    ang_c = jnp.arange(GRID_W, dtype=jnp.int32).astype(F32)[:, None] * inv_freq[None, :]

    def by_row(a):
        return jnp.repeat(a, GRID_W, axis=0)

    def by_col(a):
        return jnp.tile(a, (n_rows, 1))

    cos_r, sin_r, cos_c, sin_c = by_row(jnp.cos(ang_r)), by_row(jnp.sin(ang_r)), by_col(jnp.cos(ang_c)), by_col(jnp.sin(ang_c))
    cos = jnp.concatenate([cos_r, cos_r, cos_c, cos_c], axis=-1)
    sin = jnp.concatenate([-sin_r, sin_r, -sin_c, sin_c], axis=-1)
    return cos, sin


def _gather_weight(name, w, cols):
    g = _all_gather(name, w[0].astype(BF16))
    return g if cols else g.reshape(1, N_DEV * g.shape[1], g.shape[2])


def kernel(x, p, norm_mix, w_in, w_dw, conv_ln_g, conv_ln_b, w_conv_proj, q_norm, k_norm, w_attn_proj, w_out, norm_ffn, w_ff1, w_ff2, norm_ple, w_ple_gate, w_ple_proj, norm_final, loss_target, m_norm_mix, m_w_in, m_w_dw, m_conv_ln_g, m_conv_ln_b, m_w_conv_proj, m_q_norm, m_k_norm, m_w_attn_proj, m_w_out, m_norm_ffn, m_w_ff1, m_w_ff2, m_norm_ple, m_w_ple_gate, m_w_ple_proj, m_norm_final, v_norm_mix, v_w_in, v_w_dw, v_conv_ln_g, v_conv_ln_b, v_w_conv_proj, v_q_norm, v_k_norm, v_w_attn_proj, v_w_out, v_norm_ffn, v_w_ff1, v_w_ff2, v_norm_ple, v_w_ple_gate, v_w_ple_proj, v_norm_final):
    t, d = x.shape[1], x.shape[2]
    cw = d // 2
    n_heads = d // HEAD_DIM
    n_kv = n_heads // GROUP
    col_q, col_k, col_v = d // HEAD_DIM, 2 * d // HEAD_DIM, (2 * d + d // 4) // HEAD_DIM
    x0, pe, tgt = x[0], p[0, 0], loss_target[0]
    g_final = norm_final.reshape(1, d)
    me = _index(*_position())

    wg_in = _gather_weight("ag_w_in", w_in, True)
    dw_pad = jnp.pad(w_dw[0], ((0, CONV_TAPS_PAD - CONV_KERNEL), (0, 0)))
    dw_all = _all_gather("ag_w_dw", dw_pad)
    dw_full = dw_all.transpose(1, 0, 2).reshape(CONV_TAPS_PAD, cw)
    cos, sin = _rope_tables(t)

    def fetch(name, w, prev):
        return _exchange_start("ag_start_" + name, w[0].astype(BF16), after=(wg_in, dw_all) if prev is None else (prev,))

    def arrived(name, handle, cols, after):
        g = _exchange_wait("ag_wait_" + name, handle, after)
        return g if cols else g.reshape(1, N_DEV * g.shape[1], g.shape[2])

    on_cp, tok = fetch("w_conv_proj", w_conv_proj, None)
    on_ap, tok = fetch("w_attn_proj", w_attn_proj, tok)
    on_out, tok = fetch("w_out", w_out, tok)
    on_ff1, tok = fetch("w_ff1", w_ff1, tok)
    on_ff2, tok = fetch("w_ff2", w_ff2, tok)
    on_pg, tok = fetch("w_ple_gate", w_ple_gate, tok)
    on_pp, tok = fetch("w_ple_proj", w_ple_proj, tok)

    h1 = _rms_fwd("rms_mix", x0, norm_mix, after=(tok,))
    z = _mm_nn("mm_in", h1, wg_in, F32)
    conv, s_c = _conv_fwd("conv_fwd", z, cw, dw_full, conv_ln_g, conv_ln_b)
    wg_cp = arrived("w_conv_proj", on_cp, True, s_c)
    y_c = _mm_nn("mm_conv_proj", s_c, wg_cp, F32)
    qr = _head_norm_rope("q_prep", z, col_q, n_heads, q_norm, cos, sin, out_scale=QK_SCALE * LOG2_E,
                         heads_per_step=GROUP)
    kr = _head_norm_rope("k_prep", z, col_k, n_kv, k_norm, cos, sin)
    vt = z[:, col_v * HEAD_DIM:(col_v + n_kv) * HEAD_DIM].T.astype(BF16).reshape(n_kv, HEAD_DIM, t)
    vt = jnp.concatenate([vt, jnp.ones((n_kv, ONES_ROWS, t), BF16)], axis=1).reshape(n_kv * (HEAD_DIM + ONES_ROWS), t)
    o, lse = _flash_fwd(qr, kr, vt, n_heads)
    wg_ap = arrived("w_attn_proj", on_ap, False, o)
    y_a, merged = _mm_nn("mm_attn_proj", o, wg_ap, F32, activation=_gated_merge,
                         beside=[(z, 5), (z, 7), (y_c, 0)], col_tile=d // 2)
    wg_out = arrived("w_out", on_out, False, merged)
    x1 = _mm_nn("mm_out", merged, wg_out, F32, residual=x0)
    h2 = _rms_fwd("rms_ffn", x1, norm_ffn)
    wg_ff1 = arrived("w_ff1", on_ff1, True, h2)
    f, act = _mm_nn("mm_ff1", h2, wg_ff1, F32, activation=_relu2)
    wg_ff2 = arrived("w_ff2", on_ff2, False, act)
    x2 = _mm_nn("mm_ff2", act, wg_ff2, F32, residual=x1)
    hp = _rms_fwd("rms_ple", x2, norm_ple)
    wg_pg = arrived("w_ple_gate", on_pg, False, hp)
    pre = _mm_nn("mm_ple_gate", hp, wg_pg, F32)
    wg_pp = arrived("w_ple_proj", on_pp, True, pre)
    pp = _mm_nn("mm_ple_proj", pe, wg_pp, F32)

    dx3, dpp, dpre, dg_final, loss_part = _head_loss("head_loss", x2, pre, pp, tgt, g_final)
    loss = lax.psum(loss_part[0, 0], ("x", "y", "c"))
    def send(name, gw):
        blocks = gw if gw.shape[0] == N_DEV else gw.reshape(N_DEV, gw.shape[1] // N_DEV, gw.shape[2])
        return _exchange_start("rs_start_" + name, blocks)

    sent_pp, tok_pp = send("w_ple_proj", _mm_tn("mm_d_ple_proj", pe, dpp, N_DEV))
    sent_pg, tok = send("w_ple_gate", _mm_tn("mm_d_ple_gate", hp, dpre, 1))
    dhp = _mm_nt("mm_dhp", dpre, wg_pg, F32, after=(tok_pp, tok))
    dx2, dg_ple = _rms_bwd("rms_ple_bwd", x2, norm_ple, dhp, dx3)
    sent_ff2, tok = send("w_ff2", _mm_tn("mm_d_ff2", act, dx2, 1))
    df = _mm_nt("mm_dact", dx2, wg_ff2, BF16, after=(tok,), through=_relu2_bwd, beside=[(f, 0)])
    sent_ff1, tok = send("w_ff1", _mm_tn("mm_d_ff1", h2, df, N_DEV))
    dh2 = _mm_nt("mm_dh2", df, wg_ff1, F32, after=(tok,))
    dx1, dg_ffn = _rms_bwd("rms_ffn_bwd", x1, norm_ffn, dh2, dx2)
    sent_out, tok = send("w_out", _mm_tn("mm_d_out", merged, dx1, 1))
    dgc, dga, dyc, dya = _mm_nt("mm_dmerged", dx1, wg_out, BF16, after=(tok,), through=_gated_merge_bwd,
                                beside=[(z, 5), (z, 7), (y_c, 0), (y_a, 0)], n_results=4, col_tile=d // 2)
    sent_cp, tok = send("w_conv_proj", _mm_tn("mm_d_conv_proj", s_c, dyc, N_DEV))
    ds_c = _mm_nt("mm_ds_c", dyc, wg_cp, F32, after=(tok,))
    dconv, dg_ln, db_ln = _conv_ln_bwd("conv_ln_bwd", ds_c, conv, conv_ln_g, conv_ln_b)
    dca, dcb, dg_dw = _conv_glu_bwd("conv_glu_bwd", dconv, z, cw, dw_full)
    sent_ap, tok = send("w_attn_proj", _mm_tn("mm_d_attn_proj", o, dya, 1))
    do, delta = _mm_nt("mm_do", dya, wg_ap, BF16, after=(tok,), head_dots=o)
    dqt, dkh, dvh = _flash_bwd(qr, kr, kr.T, z, col_v, do, lse, delta, n_heads)
    dq, dg_q = _head_norm_rope_bwd("q_prep_bwd", [(dqt, 0, 1)], z, col_q, n_heads, q_norm, cos, sin,
                                   heads_per_step=GROUP, transposed=True)
    dk, dg_k = _head_norm_rope_bwd("k_prep_bwd", [(dkh, g, GROUP) for g in range(GROUP)], z, col_k, n_kv, k_norm,
                                   cos, sin)
    dv = _group_sum("dv_sum", dvh, n_kv)
    dz = jnp.concatenate([dca, dcb, dq, dk, dv, dgc, dga], axis=1)
    sent_in, tok = send("w_in", _mm_tn("mm_d_in", h1, dz, N_DEV))
    dh1 = _mm_nt("mm_dh1", dz, wg_in, F32, after=(tok,))
    grad_x, dg_mix = _rms_bwd("rms_mix_bwd", x0, norm_mix, dh1, dx1)

    def pad(a):
        return jnp.pad(a, ((0, 0), (0, d - a.shape[1])))

    small = [dg_mix, dg_ffn, dg_ple, dg_final, dg_ln, db_ln, dg_q, dg_k, dg_dw]
    packed = _all_gather("ag_small", jnp.concatenate([pad(a) for a in small], axis=0))
    parts, row = [], 0
    for a in small:
        parts.append(packed[:, row:row + a.shape[0], :a.shape[1]])
        row += a.shape[0]
    p_mix, p_ffn, p_ple, p_final, p_lng, p_lnb, p_q, p_k, p_dw = parts
    p_dw = lax.dynamic_slice_in_dim(p_dw, me * (cw // N_DEV), cw // N_DEV, axis=2)

    def little(name, part, w, m, v):
        rows, n = (1 if w.ndim < 3 else w.shape[1]), w.shape[-1]
        padded = rows if rows == 1 else part.shape[1]

        def two_d(a):
            return jnp.pad(a.reshape(rows, n), ((0, padded - rows), (0, 0)))

        res = _adamw_small("adamw_" + name, part, two_d(w), two_d(m), two_d(v))
        return [r[:rows].reshape(w.shape) for r in res]

    done = {}
    after = packed
    for name, sent, w, m, v in [
            ("w_ple_proj", sent_pp, w_ple_proj, m_w_ple_proj, v_w_ple_proj),
            ("w_ple_gate", sent_pg, w_ple_gate, m_w_ple_gate, v_w_ple_gate),
            ("w_ff2", sent_ff2, w_ff2, m_w_ff2, v_w_ff2),
            ("w_ff1", sent_ff1, w_ff1, m_w_ff1, v_w_ff1),
            ("w_out", sent_out, w_out, m_w_out, v_w_out),
            ("w_conv_proj", sent_cp, w_conv_proj, m_w_conv_proj, v_w_conv_proj),
            ("w_attn_proj", sent_ap, w_attn_proj, m_w_attn_proj, v_w_attn_proj),
            ("w_in", sent_in, w_in, m_w_in, v_w_in)]:
        land = _exchange_wait("rs_wait_" + name, sent, after)
        res = _adamw_shard("adamw_" + name, land, w[0], m[0], v[0])
        after = res[0]
        done[name] = [r.reshape(w.shape) for r in res]

    results = [
        little("norm_mix", p_mix, norm_mix, m_norm_mix, v_norm_mix),
        done["w_in"],
        little("w_dw", p_dw, w_dw, m_w_dw, v_w_dw),
        little("conv_ln_g", p_lng, conv_ln_g, m_conv_ln_g, v_conv_ln_g),
        little("conv_ln_b", p_lnb, conv_ln_b, m_conv_ln_b, v_conv_ln_b),
        done["w_conv_proj"],
        little("q_norm", p_q, q_norm, m_q_norm, v_q_norm),
        little("k_norm", p_k, k_norm, m_k_norm, v_k_norm),
        done["w_attn_proj"],
        done["w_out"],
        little("norm_ffn", p_ffn, norm_ffn, m_norm_ffn, v_norm_ffn),
        done["w_ff1"],
        done["w_ff2"],
        little("norm_ple", p_ple, norm_ple, m_norm_ple, v_norm_ple),
        done["w_ple_gate"],
        done["w_ple_proj"],
        little("norm_final", p_final, norm_final, m_norm_final, v_norm_final),
    ]
    grads, deltas, new_m, new_v = zip(*results)
    return (loss, grad_x[None], *grads, *deltas, *new_m, *new_v)
```

```python
import functools

import jax
import jax.numpy as jnp
from jax import lax
from jax.experimental import pallas as pl
from jax.experimental.pallas import tpu as pltpu

F32 = jnp.float32
BF16 = jnp.bfloat16

EPS = 1e-6
HEAD_DIM = 128
GROUP = 4
CONV_KERNEL = 31
CONV_HALO = 16
CONV_TAPS_PAD = 32
GRID_W = 64
ROPE_THETA = 10000.0
ADAM_LR, ADAM_B1, ADAM_B2, ADAM_EPS, ADAM_WD, ADAM_STEP = 0.001, 0.9, 0.999, 1e-08, 0.01, 10

N_DEV = 8
MESH = pl.DeviceIdType.MESH
SUBLANES = 8
LANES = 128
VMEM_LIMIT = 48 * 1024 * 1024
ROW_TILE_ELEMS = 512 * 1024


def _div(n, pref, mult=1):
    if n <= pref:
        return n
    for d in range(pref, 0, -1):
        if n % d == 0 and d % mult == 0:
            return d
    return n


def _params(sem):
    return pltpu.CompilerParams(dimension_semantics=sem, vmem_limit_bytes=VMEM_LIMIT)


def _sum8(a):
    tm, w = a.shape
    return a.reshape(tm // SUBLANES, SUBLANES, w).sum(axis=0)


def _sigmoid(x):
    return jax.nn.sigmoid(x)


def _rowwise(name, fn, n_rows, tm, ins, outs, accs=(), ncol=1, refs_to_fn=False, after=()):
    nrow = n_rows // tm
    in_specs, arrays = [], []
    for spec in ins:
        kind, arr, w, c0 = spec[:4]
        if kind == "t":
            cs, r0 = spec[4], spec[5]
            in_specs.append(pl.BlockSpec((tm, w), lambda j, i, c0=c0, cs=cs, r0=r0: (r0 + i, c0 + cs * j)))
        elif kind == "p":
            h = spec[4]
            r = tm // h
            in_specs.append(pl.BlockSpec((h, w), lambda j, i, c0=c0, r=r: (jnp.maximum(i * r - 1, 0), c0 + j)))
        elif kind == "n":
            h = spec[4]
            r = tm // h
            last = n_rows // h - 1
            in_specs.append(
                pl.BlockSpec((h, w), lambda j, i, c0=c0, r=r, last=last: (jnp.minimum((i + 1) * r, last), c0 + j)))
        elif kind == "x":
            in_specs.append(pl.BlockSpec((w, tm), lambda j, i, c0=c0: (c0 + j, i)))
        else:
            cs = spec[4]
            in_specs.append(pl.BlockSpec((arr.shape[0], w), lambda j, i, c0=c0, cs=cs: (0, c0 + cs * j)))
        arrays.append(arr)
    out_shape, out_specs = [], []
    for dtype, total, w in outs:
        out_shape.append(jax.ShapeDtypeStruct((n_rows, total), dtype))
        out_specs.append(pl.BlockSpec((tm, w), lambda j, i: (i, j)))
    for rows, total, w, follow in accs:
        out_shape.append(jax.ShapeDtypeStruct((rows, total), F32))
        out_specs.append(pl.BlockSpec((rows, w), (lambda j, i: (0, j)) if follow else (lambda j, i: (0, 0))))
    n_in, n_out, n_acc = len(ins), len(outs), len(accs)
    for token in after:
        in_specs.append(pl.BlockSpec(memory_space=pl.ANY))
        arrays.append(token)
    out0 = n_in + len(after)

    def body(*refs):
        j, i = pl.program_id(0), pl.program_id(1)
        res = fn(i, nrow, *(refs[:n_in] if refs_to_fn else [r[...] for r in refs[:n_in]]))
        if not isinstance(res, (tuple, list)):
            res = (res,)
        for k in range(n_out):
            refs[out0 + k][...] = res[k].astype(outs[k][0])
        for k in range(n_acc):
            ref, term = refs[out0 + n_out + k], res[n_out + k]
            first = (i == 0) if accs[k][3] else jnp.logical_and(i == 0, j == 0)

            @pl.when(first)
            def _():
                ref[...] = term

            @pl.when(jnp.logical_not(first))
            def _():
                ref[...] += term

    res = pl.pallas_call(
        body, name=name, grid=(ncol, nrow), in_specs=in_specs, out_specs=out_specs, out_shape=out_shape,
        compiler_params=_params(("arbitrary", "arbitrary")))(*arrays)
    return res


def _row_tile(n_rows, width, mult=SUBLANES):
    return _div(n_rows, max(mult, ROW_TILE_ELEMS // width), mult)


def _t(arr, w=None, col0=0, cstride=1, row0=0):
    return ("t", arr, arr.shape[1] if w is None else w, col0, cstride, row0)


def _b(arr, w=None, col0=0, cstride=0):
    return ("b", arr, arr.shape[1] if w is None else w, col0, cstride)


def _rms_fwd(name, x, g, after=()):
    n, d = x.shape

    def fn(i, nrow, x, g):
        r = lax.rsqrt(jnp.mean(x * x, axis=-1, keepdims=True) + EPS)
        return x * r * g

    return _rowwise(name, fn, n, _row_tile(n, d // 4), [_t(x), _b(g)], [(BF16, d, d)], after=after)[0]


def _rms_bwd(name, x, g, dh, dres):
    n, d = x.shape

    def fn(i, nrow, x, g, dh, dres):
        r = lax.rsqrt(jnp.mean(x * x, axis=-1, keepdims=True) + EPS)
        w = dh * g
        dx = dres + r * w - x * (r * r * r) * jnp.mean(x * w, axis=-1, keepdims=True)
        return dx, _sum8(dh * x * r)

    return _rowwise(name, fn, n, _row_tile(n, d // 2), [_t(x), _b(g), _t(dh), _t(dres)], [(F32, d, d)],
                    [(SUBLANES, d, d, True)])


def _swap_halves(x):
    lane = lax.broadcasted_iota(jnp.int32, x.shape, 1)
    return jnp.where(lane % 64 < 32, pltpu.roll(x, 96, axis=1), pltpu.roll(x, 32, axis=1))


def _head_norm_rope(name, z, col0, n_heads, g, cos, sin, out_scale=None, heads_per_step=1):
    n, hp = z.shape[0], heads_per_step
    assert col0 % hp == 0 and n_heads % hp == 0

    def fn(i, nrow, x, g, cos, sin):
        ys = []
        for h in range(hp):
            xh = x[:, h * HEAD_DIM:(h + 1) * HEAD_DIM]
            r = lax.rsqrt(jnp.mean(xh * xh, axis=-1, keepdims=True) + EPS)
            y = xh * r * g
            y = y * cos + _swap_halves(y) * sin
            ys.append(y if out_scale is None else y * out_scale)
        return ys[0] if hp == 1 else jnp.concatenate(ys, axis=1)

    tm = _div(n, 1024, SUBLANES)
    w = hp * HEAD_DIM
    return _rowwise(name, fn, n, tm, [_t(z, w, col0 // hp), _b(g), _t(cos, cstride=0), _t(sin, cstride=0)],
                    [(BF16, n_heads * HEAD_DIM, w)], ncol=n_heads // hp)[0]


def _head_norm_rope_bwd(name, douts, z, col0, n_heads, g, cos, sin, heads_per_step=1, transposed=False):
    n, hp = z.shape[0], heads_per_step
    n_d = len(douts)
    assert col0 % hp == 0 and n_heads % hp == 0 and (hp == 1 or all(c0 % hp == 0 and cs == 1 for _, c0, cs in douts))
    assert not transposed or n_d == 1

    def fn(i, nrow, *a):
        dy = a[0]
        for k in range(1, n_d):
            dy = dy + a[k]
        x, g, cos, sin = a[n_d:]
        dxs, dg = [], None
        for h in range(hp):
            xh = x[:, h * HEAD_DIM:(h + 1) * HEAD_DIM]
            if transposed:
                dyh = jnp.transpose(dy[h * HEAD_DIM:(h + 1) * HEAD_DIM, :])
            else:
                dyh = dy[:, h * HEAD_DIM:(h + 1) * HEAD_DIM]
            dn = dyh * cos + _swap_halves(dyh * sin)
            r = lax.rsqrt(jnp.mean(xh * xh, axis=-1, keepdims=True) + EPS)
            w = dn * g
            dxs.append(r * w - xh * (r * r * r) * jnp.mean(xh * w, axis=-1, keepdims=True))
            term = _sum8(dn * xh * r)
            dg = term if dg is None else dg + term
        return (dxs[0] if hp == 1 else jnp.concatenate(dxs, axis=1)), dg

    tm = _div(n, 1024, SUBLANES)
    w = hp * HEAD_DIM
    if transposed:
        ins = [("x", douts[0][0], w, douts[0][1] // hp)]
    else:
        ins = [_t(arr, w, c0 // hp, cs) for arr, c0, cs in douts]
    ins += [_t(z, w, col0 // hp), _b(g), _t(cos, cstride=0), _t(sin, cstride=0)]
    return _rowwise(name, fn, n, tm, ins, [(BF16, n_heads * HEAD_DIM, w)],
                    [(SUBLANES, HEAD_DIM, HEAD_DIM, False)], ncol=n_heads // hp)


def _halo(arr, w, col0):
    return [("p", arr, w, col0, CONV_HALO), _t(arr, w, col0), ("n", arr, w, col0, CONV_HALO)]


def _extend(i, nrow, prev, cur, nxt):
    prev = jnp.where(i > 0, prev, 0.0)
    nxt = jnp.where(i < nrow - 1, nxt, 0.0)
    return jnp.concatenate([prev, cur, nxt], axis=0)


def _shifted(ext, offset, tm):
    n = ext.shape[0]
    rolled = ext if offset == 0 else pltpu.roll(ext, (-offset) % n, axis=0)
    return rolled[CONV_HALO:CONV_HALO + tm]


def _conv_fwd(name, z, cw, w_dw, ln_g, ln_b):
    n = z.shape[0]
    tm = _row_tile(n, 4 * cw, CONV_HALO)

    def fn(i, nrow, ap, a, an, bp, b, bn, w, g, beta):
        a = a[...]
        ext = _extend(i, nrow, ap[...] * _sigmoid(bp[...]), a * _sigmoid(b[...]), an[...] * _sigmoid(bn[...]))
        conv = jnp.zeros_like(a)
        for k in range(CONV_KERNEL):
            conv = conv + _shifted(ext, k - CONV_KERNEL // 2, tm) * w[k:k + 1, :]
        xc = conv - jnp.mean(conv, axis=-1, keepdims=True)
        ln = xc * lax.rsqrt(jnp.mean(xc * xc, axis=-1, keepdims=True) + EPS) * g[...] + beta[...]
        return conv, ln * _sigmoid(ln)

    ins = _halo(z, cw, 0) + _halo(z, cw, 1) + [_b(w_dw), _b(ln_g), _b(ln_b)]
    return _rowwise(name, fn, n, tm, ins, [(F32, cw, cw), (BF16, cw, cw)], refs_to_fn=True)


def _conv_ln_bwd(name, ds, conv, ln_g, ln_b):
    n, cw = conv.shape

    def fn(i, nrow, ds, conv, g, beta):
        xc = conv - jnp.mean(conv, axis=-1, keepdims=True)
        rstd = lax.rsqrt(jnp.mean(xc * xc, axis=-1, keepdims=True) + EPS)
        xhat = xc * rstd
        ln = xhat * g + beta
        sg = _sigmoid(ln)
        dln = ds * (sg * (1.0 + ln * (1.0 - sg)))
        dxh = dln * g
        dconv = rstd * (dxh - jnp.mean(dxh, axis=-1, keepdims=True)
                        - xhat * jnp.mean(dxh * xhat, axis=-1, keepdims=True))
        return dconv, _sum8(dln * xhat), _sum8(dln)

    return _rowwise(name, fn, n, _row_tile(n, cw // 2), [_t(ds), _t(conv), _b(ln_g), _b(ln_b)], [(F32, cw, cw)],
                    [(SUBLANES, cw, cw, True), (SUBLANES, cw, cw, True)])


def _conv_glu_bwd(name, dconv, z, cw, w_dw):
    n = z.shape[0]
    tm = _row_tile(n, 4 * cw, CONV_HALO)

    def fn(i, nrow, dp, d, dn, a, b, w):
        d, a = d[...], a[...]
        dext = _extend(i, nrow, dp[...], d, dn[...])
        sg = _sigmoid(b[...])
        u = a * sg
        tap = lax.broadcasted_iota(jnp.int32, (CONV_TAPS_PAD, cw), 0)
        du = jnp.zeros_like(d)
        dw = jnp.zeros((CONV_TAPS_PAD, cw), F32)
        for k in range(CONV_KERNEL):
            d_k = _shifted(dext, -(k - CONV_KERNEL // 2), tm)
            du = du + d_k * w[k:k + 1, :]
            dw = dw + jnp.where(tap == k, jnp.sum(d_k * u, axis=0, keepdims=True), 0.0)
        return du * sg, du * a * sg * (1.0 - sg), dw

    ins = _halo(dconv, cw, 0) + [_t(z, cw, 0), _t(z, cw, 1), _b(w_dw)]
    return _rowwise(name, fn, n, tm, ins, [(BF16, cw, cw), (BF16, cw, cw)], [(CONV_TAPS_PAD, cw, cw, True)],
                    refs_to_fn=True)


def _gated_merge(y_a, gc, ga, y_c):
    return _sigmoid(gc) * y_c + _sigmoid(ga) * y_a


def _gated_merge_bwd(dm, gc, ga, yc, ya):
    sc, sa = _sigmoid(gc), _sigmoid(ga)
    return dm * yc * sc * (1.0 - sc), dm * ya * sa * (1.0 - sa), dm * sc, dm * sa


def _relu2(f):
    r = jnp.maximum(f, 0.0)
    return r * r


def _relu2_bwd(da, f):
    return da * (2.0 * jnp.maximum(f, 0.0))


def _head_loss(name, x2, pre, pp, target, g):
    n, d = x2.shape

    def fn(i, nrow, x2, pre, pp, t, g):
        gate = _sigmoid(pre)
        x3 = x2 + gate * pp
        r = lax.rsqrt(jnp.mean(x3 * x3, axis=-1, keepdims=True) + EPS)
        xn = x3 * r
        e = xn * g - t
        part = 0.5 * jnp.sum(jnp.mean(e * e, axis=-1, keepdims=True), axis=0, keepdims=True)
        dy = e * (1.0 / d)
        w = dy * g
        dx3 = r * w - x3 * (r * r * r) * jnp.mean(x3 * w, axis=-1, keepdims=True)
        return (dx3, dx3 * gate, dx3 * pp * gate * (1.0 - gate), _sum8(dy * xn),
                jnp.broadcast_to(part, (SUBLANES, LANES)))

    ins = [_t(x2), _t(pre), _t(pp), _t(target), _b(g)]
    return _rowwise(name, fn, n, _row_tile(n, d), ins, [(F32, d, d), (BF16, d, d), (BF16, d, d)],
                    [(SUBLANES, d, d, True), (SUBLANES, LANES, LANES, True)])


def _group_sum(name, dvh, n_kv):
    n = dvh.shape[0]

    def fn(i, nrow, *a):
        s = a[0]
        for k in range(1, GROUP):
            s = s + a[k]
        return s

    ins = [_t(dvh, HEAD_DIM, k, GROUP) for k in range(GROUP)]
    return _rowwise(name, fn, n, _div(n, 1024, SUBLANES), ins, [(BF16, n_kv * HEAD_DIM, HEAD_DIM)], ncol=n_kv)[0]


MM_ROWS = 1024
MM_COLS = 1152
MM_REDUCE = 2048
MM_REDUCE_BLOCKS = 2304


def _rows_outermost(n_rows, n_cols, row_tile_bytes, weight_tile_bytes):
    rows_outer = n_rows * (row_tile_bytes + n_cols * weight_tile_bytes)
    weight_outer = n_cols * (weight_tile_bytes + n_rows * row_tile_bytes)
    return 5 * rows_outer < 4 * weight_outer


def _accumulate(step, n_steps, part, acc, finish):
    if n_steps == 1:
        finish(part)
        return

    @pl.when(step == 0)
    def _():
        acc[...] = part

    @pl.when(jnp.logical_and(step > 0, step < n_steps - 1))
    def _():
        acc[...] += part

    @pl.when(step == n_steps - 1)
    def _():
        finish(acc[...] + part)


MXU_WIDTH = 256


def _paired(nb, ns):
    return ns % MXU_WIDTH != 0 and nb % 2 == 0 and ns <= MM_COLS


def _mm_nn(name, a, w3, out_dtype, residual=None, activation=None, beside=(), col_tile=None):
    m, k = a.shape
    nb, _, ns = w3.shape
    jb = 2 if _paired(nb, ns) else 1
    tm = _div(m, MM_ROWS // max(jb, 1 + len(beside) // 2), SUBLANES)
    tn_w, tk = (ns if jb > 1 else _div(ns, col_tile or MM_COLS, LANES)), _div(k, MM_REDUCE, LANES)
    tn = jb * tn_w
    per, nk = ns // tn_w, k // tk
    has_res = residual is not None
    n_out = 1 if activation is None else 2
    first_beside = 3 if has_res else 2

    def body(*refs):
        a_ref, w_ref = refs[:2]
        acc = refs[-1]
        outs = refs[-1 - n_out:-1]
        w = w_ref[0] if jb == 1 else jnp.concatenate([w_ref[j] for j in range(jb)], axis=1)
        part = jnp.dot(a_ref[...].astype(BF16), w.astype(BF16), preferred_element_type=F32)

        def finish(total):
            if has_res:
                total = total + refs[2][...]
            outs[0][...] = total.astype(out_dtype)
            if activation is not None:
                tiles = [r[...] for r in refs[first_beside:first_beside + len(beside)]]
                outs[1][...] = activation(total, *tiles).astype(BF16)

        _accumulate(pl.program_id(2), nk, part, acc, finish)

    n_rows, n_cols = m // tm, (nb // jb) * per
    rows_outer = nk > 1 or _rows_outermost(n_rows, n_cols, tm * tk * a.dtype.itemsize,
                                           jb * tk * tn_w * w3.dtype.itemsize)

    def at(f):
        return f if rows_outer else (lambda n, i, kk: f(i, n, kk))

    in_specs = [pl.BlockSpec((tm, tk), at(lambda i, n, kk: (i, kk))),
                pl.BlockSpec((jb, tk, tn_w), at(lambda i, n, kk: (n // per, kk, n % per)))]
    args = [a, w3]
    if has_res:
        in_specs.append(pl.BlockSpec((tm, tn), at(lambda i, n, kk: (i, n))))
        args.append(residual)
    for arr, col0 in beside:
        in_specs.append(pl.BlockSpec((tm, tn), at(lambda i, n, kk, col0=col0: (i, col0 + n))))
        args.append(arr)
    out_dtypes = [out_dtype] + ([BF16] if activation is not None else [])
    res = pl.pallas_call(
        body, name=name, grid=(n_rows, n_cols, nk) if rows_outer else (n_cols, n_rows, nk), in_specs=in_specs,
        out_specs=[pl.BlockSpec((tm, tn), at(lambda i, n, kk: (i, n))) for _ in out_dtypes],
        out_shape=[jax.ShapeDtypeStruct((m, nb * ns), dt) for dt in out_dtypes],
        scratch_shapes=[pltpu.VMEM((tm, tn) if nk > 1 else (SUBLANES, LANES), F32)],
        compiler_params=_params(("parallel", "parallel", "arbitrary")))(*args)
    return res[0] if activation is None else res


def _mm_nt(name, dc, w3, out_dtype, after=(), through=None, beside=(), n_results=1, head_dots=None, col_tile=None):
    m = dc.shape[0]
    nb, k, ns = w3.shape
    tm = _div(m, MM_ROWS // (1 + len(beside) // 3), SUBLANES)
    to = _div(k, col_tile or MM_ROWS, LANES)
    if ns > MM_REDUCE_BLOCKS // 2:
        jb, tr = 1, _div(ns, MM_REDUCE, LANES)
    else:
        jb, tr = max(j for j in range(1, nb + 1) if nb % j == 0 and j * ns <= MM_REDUCE_BLOCKS), ns
    per = ns // tr
    nr = (nb // jb) * per
    tiles_in = list(beside) + ([(head_dots, 0)] if head_dots is not None else [])
    n_out = n_results + (1 if head_dots is not None else 0)
    heads = to // HEAD_DIM

    def body(dc_ref, w_ref, *rest):
        acc = rest[-1]
        outs = rest[-1 - n_out:-1]
        w = w_ref[0] if jb == 1 else jnp.concatenate([w_ref[j] for j in range(jb)], axis=1)
        part = lax.dot_general(dc_ref[...].astype(BF16), w.astype(BF16), (((1,), (1,)), ((), ())),
                               preferred_element_type=F32)

        def finish(total):
            results = (total,) if through is None else through(total, *[r[...] for r in rest[:len(beside)]])
            if not isinstance(results, (tuple, list)):
                results = (results,)
            for out, res in zip(outs, results):
                out[...] = res.astype(out_dtype)
            if head_dots is not None:
                prod = results[0].astype(out_dtype).astype(F32) * rest[len(beside)][...]
                for h in range(heads):
                    row = jnp.sum(jnp.transpose(prod[:, h * HEAD_DIM:(h + 1) * HEAD_DIM]), axis=0, keepdims=True)
                    outs[n_results][h] = jnp.broadcast_to(row, (SUBLANES, tm))

        _accumulate(pl.program_id(2), nr, part, acc, finish)

    rows_outer = nr > 1 or _rows_outermost(m // tm, k // to, tm * jb * tr * dc.dtype.itemsize,
                                           jb * to * tr * w3.dtype.itemsize)

    def at(f):
        return f if rows_outer else (lambda o, i, r: f(i, o, r))

    in_specs = [pl.BlockSpec((tm, jb * tr), at(lambda i, o, r: (i, r))),
                pl.BlockSpec((jb, to, tr), at(lambda i, o, r: (r // per, o, r % per)))]
    args = [dc, w3]
    for arr, col0 in tiles_in:
        in_specs.append(pl.BlockSpec((tm, to), at(lambda i, o, r, col0=col0: (i, col0 + o))))
        args.append(arr)
    for token in after:
        in_specs.append(pl.BlockSpec(memory_space=pl.ANY))
        args.append(token)
    out_specs = [pl.BlockSpec((tm, to), at(lambda i, o, r: (i, o))) for _ in range(n_results)]
    out_shape = [jax.ShapeDtypeStruct((m, k), out_dtype) for _ in range(n_results)]
    if head_dots is not None:
        out_specs.append(pl.BlockSpec((heads, SUBLANES, tm), at(lambda i, o, r: (o, 0, i))))
        out_shape.append(jax.ShapeDtypeStruct((k // HEAD_DIM, SUBLANES, m), F32))
    res = pl.pallas_call(
        body, name=name, grid=(m // tm, k // to, nr) if rows_outer else (k // to, m // tm, nr), in_specs=in_specs,
        out_specs=out_specs, out_shape=out_shape,
        scratch_shapes=[pltpu.VMEM((tm, to) if nr > 1 else (SUBLANES, LANES), F32)],
        compiler_params=_params(("parallel", "parallel", "arbitrary")))(*args)
    return res[0] if n_out == 1 else res


def _mm_tn(name, a, dc, nb):
    m, k = a.shape
    ns = dc.shape[1] // nb
    jb = 2 if _paired(nb, ns) else 1
    tr, to = _div(m, MM_REDUCE, LANES), _div(k, MM_ROWS // jb, LANES)
    tn_w = ns if jb > 1 else _div(ns, MM_COLS, LANES)
    tn = jb * tn_w
    per, nr = ns // tn_w, m // tr

    def body(a_ref, dc_ref, o_ref, acc):
        part = lax.dot_general(a_ref[...].astype(BF16), dc_ref[...].astype(BF16), (((0,), (0,)), ((), ())),
                               preferred_element_type=F32)

        def finish(total):
            for j in range(jb):
                o_ref[j] = total[:, j * tn_w:(j + 1) * tn_w].astype(BF16)

        _accumulate(pl.program_id(2), nr, part, acc, finish)

    return pl.pallas_call(
        body, name=name, grid=(k // to, (nb // jb) * per, nr),
        in_specs=[pl.BlockSpec((tr, to), lambda o, n, r: (r, o)),
                  pl.BlockSpec((tr, tn), lambda o, n, r: (r, n))],
        out_specs=pl.BlockSpec((jb, to, tn_w), lambda o, n, r: (n // per, o, n % per)),
        out_shape=jax.ShapeDtypeStruct((nb, k, ns), BF16),
        scratch_shapes=[pltpu.VMEM((to, tn) if nr > 1 else (SUBLANES, LANES), F32)],
        compiler_params=_params(("parallel", "parallel", "arbitrary")))(a, dc)


QK_SCALE = HEAD_DIM ** -0.5
LOG2_E = 1.4426950408889634
LN_2 = 0.6931471805599453


FLASH_TILE = 1024
ONES_ROWS = 16


def _flash_fwd(qr, kr, vt, n_heads):
    t = qr.shape[0]
    tq, tk = _div(t, 2 * FLASH_TILE, LANES), _div(t, FLASH_TILE, LANES)
    nk = t // tk
    rows = vt.shape[0] // (n_heads // GROUP)

    def body(q_ref, k_ref, vt_ref, o_ref, lse_ref, m_scr, acc_scr):
        kb = pl.program_id(2)

        @pl.when(kb == 0)
        def _():
            m_scr[...] = jnp.full_like(m_scr, -1e30)
            acc_scr[...] = jnp.zeros_like(acc_scr)

        s_t = lax.dot_general(k_ref[...], q_ref[...], (((1,), (1,)), ((), ())), preferred_element_type=F32)
        m_prev = m_scr[...]
        m_new = jnp.maximum(m_prev, jnp.max(s_t, axis=0, keepdims=True))
        alpha = jnp.exp2(m_prev - m_new)
        p_t = jnp.exp2(s_t - m_new).astype(BF16)
        acc_scr[...] = alpha * acc_scr[...] + jnp.dot(vt_ref[...], p_t, preferred_element_type=F32)
        m_scr[...] = m_new

        @pl.when(kb == nk - 1)
        def _():
            l = acc_scr[HEAD_DIM:HEAD_DIM + 1, :]
            o_ref[...] = jnp.transpose(acc_scr[0:HEAD_DIM, :] / l)
            lse_ref[...] = jnp.broadcast_to(m_scr[...] + jnp.log(l) * LOG2_E, (SUBLANES, tq))

    return pl.pallas_call(
        body, name="flash_fwd", grid=(n_heads, t // tq, nk),
        in_specs=[pl.BlockSpec((tq, HEAD_DIM), lambda h, qi, ki: (qi, h)),
                  pl.BlockSpec((tk, HEAD_DIM), lambda h, qi, ki: (ki, h // GROUP)),
                  pl.BlockSpec((rows, tk), lambda h, qi, ki: (h // GROUP, ki))],
        out_specs=[pl.BlockSpec((tq, HEAD_DIM), lambda h, qi, ki: (qi, h)),
                   pl.BlockSpec((None, SUBLANES, tq), lambda h, qi, ki: (h, 0, qi))],
        out_shape=[jax.ShapeDtypeStruct((t, n_heads * HEAD_DIM), F32),
                   jax.ShapeDtypeStruct((n_heads, SUBLANES, t), F32)],
        scratch_shapes=[pltpu.VMEM((1, tq), F32), pltpu.VMEM((rows, tq), F32)],
        compiler_params=_params(("parallel", "parallel", "arbitrary")))(qr, kr, vt)


def _flash_bwd(qr, kr, kt, z, v_col0, do, lse, delta, n_heads):
    t = qr.shape[0]
    tq, tk = _div(t, 2 * FLASH_TILE, LANES), _div(t, FLASH_TILE, LANES)
    nq = t // tq

    def body(q_ref, k_ref, kt_ref, v_ref, do_ref, lse_ref, dl_ref, dq_ref, dk_ref, dv_ref, dk_acc, dv_acc):
        kb, qb = pl.program_id(1), pl.program_id(2)
        q, k, do_ = q_ref[...], k_ref[...], do_ref[...]
        v = v_ref[...].astype(BF16)
        s_t = lax.dot_general(k, q, (((1,), (1,)), ((), ())), preferred_element_type=F32)
        p_t = jnp.exp2(s_t - lse_ref[0:1, :])
        dp_t = lax.dot_general(v, do_, (((1,), (1,)), ((), ())), preferred_element_type=F32)
        ds_t = (p_t * (dp_t - dl_ref[0:1, :])).astype(BF16)
        dv_c = jnp.dot(p_t.astype(BF16), do_, preferred_element_type=F32)
        dk_c = jnp.dot(ds_t, q, preferred_element_type=F32) * LN_2
        dq_c = jnp.dot(kt_ref[...], ds_t, preferred_element_type=F32) * QK_SCALE

        @pl.when(qb == 0)
        def _():
            dk_acc[...] = dk_c
            dv_acc[...] = dv_c

        @pl.when(qb > 0)
        def _():
            dk_acc[...] += dk_c
            dv_acc[...] += dv_c

        @pl.when(qb == nq - 1)
        def _():
            dk_ref[...] = dk_acc[...]
            dv_ref[...] = dv_acc[...]

        cols = pl.ds(pl.multiple_of(qb * tq, tq), tq)

        @pl.when(kb == 0)
        def _():
            dq_ref[:, cols] = dq_c

        @pl.when(kb > 0)
        def _():
            dq_ref[:, cols] += dq_c

    wide = jax.ShapeDtypeStruct((t, n_heads * HEAD_DIM), F32)
    return pl.pallas_call(
        body, name="flash_bwd", grid=(n_heads, t // tk, nq),
        in_specs=[pl.BlockSpec((tq, HEAD_DIM), lambda h, kb, qb: (qb, h)),
                  pl.BlockSpec((tk, HEAD_DIM), lambda h, kb, qb: (kb, h // GROUP)),
                  pl.BlockSpec((HEAD_DIM, tk), lambda h, kb, qb: (h // GROUP, kb)),
                  pl.BlockSpec((tk, HEAD_DIM), lambda h, kb, qb: (kb, v_col0 + h // GROUP)),
                  pl.BlockSpec((tq, HEAD_DIM), lambda h, kb, qb: (qb, h)),
                  pl.BlockSpec((None, SUBLANES, tq), lambda h, kb, qb: (h, 0, qb)),
                  pl.BlockSpec((None, SUBLANES, tq), lambda h, kb, qb: (h, 0, qb))],
        out_specs=[pl.BlockSpec((HEAD_DIM, t), lambda h, kb, qb: (h, 0)),
                   pl.BlockSpec((tk, HEAD_DIM), lambda h, kb, qb: (kb, h)),
                   pl.BlockSpec((tk, HEAD_DIM), lambda h, kb, qb: (kb, h))],
        out_shape=[jax.ShapeDtypeStruct((n_heads * HEAD_DIM, t), F32), wide, wide],
        scratch_shapes=[pltpu.VMEM((tk, HEAD_DIM), F32), pltpu.VMEM((tk, HEAD_DIM), F32)],
        compiler_params=_params(("parallel", "arbitrary", "arbitrary")))(qr, kr, kt, z, do, lse, delta)


def _position():
    x, y, c = lax.axis_index("x"), lax.axis_index("y"), lax.axis_index("c")
    return x, y, c


def _index(x, y, c):
    return 4 * x + 2 * y + c


def _all_gather(name, shard):
    a, b = shard.shape

    def body(x_ref, out_ref, send_sems, recv_sems, local_sem):
        x, y, c = _position()
        me, sibling = (x, y, c), (x, y, 1 - c)
        chips = [(1 - x, y), (x, 1 - y), (1 - x, 1 - y)]

        def block(px, py, pc):
            return out_ref.at[_index(px, py, pc)]

        def copy(k, blk, to, src=None):
            return pltpu.make_async_remote_copy(
                src_ref=block(*blk) if src is None else src, dst_ref=block(*blk),
                send_sem=send_sems.at[k], recv_sem=recv_sems.at[k], device_id=to, device_id_type=MESH)

        mine = pltpu.make_async_copy(x_ref, block(*me), local_sem)
        mine.start()
        first = [copy(0, me, sibling, src=x_ref)]
        first += [copy(1 + j, me, (*chip, c), src=x_ref) for j, chip in enumerate(chips)]
        for cp in first:
            cp.start()
        passed = [copy(4 + j, (*chip, c), sibling) for j, chip in enumerate(chips)]
        for j, chip in enumerate(chips):
            copy(1 + j, (*chip, c), me).wait_recv()
            passed[j].start()
        copy(0, sibling, me).wait_recv()
        for j, chip in enumerate(chips):
            copy(4 + j, (*chip, 1 - c), me).wait_recv()
        for cp in first + passed:
            cp.wait_send()
        mine.wait()

    return pl.pallas_call(
        body, name=name, out_shape=jax.ShapeDtypeStruct((N_DEV, a, b), shard.dtype),
        in_specs=[pl.BlockSpec(memory_space=pltpu.HBM)], out_specs=pl.BlockSpec(memory_space=pltpu.HBM),
        scratch_shapes=[pltpu.SemaphoreType.DMA((7,)), pltpu.SemaphoreType.DMA((7,)), pltpu.SemaphoreType.DMA],
    )(shard)


HBM_SPEC = pl.BlockSpec(memory_space=pltpu.HBM)
SEM_SPEC = pl.BlockSpec(memory_space=pltpu.SEMAPHORE)
N_PEERS = N_DEV - 1
PEER_FLIPS = [(fx, fy, fc) for fx in (0, 1) for fy in (0, 1) for fc in (0, 1)][1:]


def _peers(x, y, c):
    return [((1 - x) if fx else x, (1 - y) if fy else y, (1 - c) if fc else c) for fx, fy, fc in PEER_FLIPS]


def _exchange_copies(g_ref, land_ref, send_sems, recv_sems):
    x, y, c = _position()
    me = _index(x, y, c)
    peers = _peers(x, y, c)
    blocked = len(g_ref.shape) == 3

    def copy(k, src_block, dst_block):
        return pltpu.make_async_remote_copy(
            src_ref=g_ref.at[src_block] if blocked else g_ref, dst_ref=land_ref.at[dst_block],
            send_sem=send_sems.at[k], recv_sem=recv_sems.at[k], device_id=peers[k], device_id_type=MESH)

    sent = [copy(k, _index(*peer), me) for k, peer in enumerate(peers)]
    arriving = [copy(k, me, _index(*peer)) for k, peer in enumerate(peers)]
    own = pltpu.make_async_copy(g_ref.at[me] if blocked else g_ref, land_ref.at[me], send_sems.at[N_PEERS])
    return sent, arriving, own


def _exchange_start(name, g3, after=()):
    def body(g_ref, land_ref, *rest):
        send_sems, recv_sems, _, _, token = rest[len(after):]
        sent, _, own = _exchange_copies(g_ref, land_ref, send_sems, recv_sems)
        for cp in sent:
            cp.start()
        own.start()
        token[...] = jnp.zeros_like(token)

    land_shape = g3.shape if g3.ndim == 3 else (N_DEV,) + g3.shape
    land = lax.empty(land_shape, g3.dtype)
    send_sems, recv_sems, g_thru, land_thru, token = pl.pallas_call(
        body, name=name,
        out_shape=(pltpu.SemaphoreType.DMA((N_PEERS + 1,)), pltpu.SemaphoreType.DMA((N_PEERS,)),
                   pltpu.HBM(g3.shape, g3.dtype), pltpu.HBM(land_shape, g3.dtype),
                   jax.ShapeDtypeStruct((SUBLANES, LANES), F32)),
        in_specs=(HBM_SPEC, HBM_SPEC) + (pl.BlockSpec(memory_space=pl.ANY),) * len(after),
        out_specs=(SEM_SPEC, SEM_SPEC, HBM_SPEC, HBM_SPEC, pl.BlockSpec(memory_space=pltpu.VMEM)),
        input_output_aliases={0: 2, 1: 3},
        compiler_params=pltpu.CompilerParams(has_side_effects=pltpu.SideEffectType.DATAFLOW_SIDE_EFFECTING),
    )(pltpu.with_memory_space_constraint(g3, pltpu.HBM), pltpu.with_memory_space_constraint(land, pltpu.HBM), *after)
    return (send_sems, recv_sems, g_thru, land_thru), token


def _exchange_wait(name, handle, after):
    send_sems, recv_sems, g_thru, land_thru = handle

    def body(g_ref, land_ref, send_sems, recv_sems, after_ref, g_dead, land_out):
        sent, arriving, own = _exchange_copies(g_ref, land_ref, send_sems, recv_sems)
        for cp in sent:
            cp.wait_send()
        for cp in arriving:
            cp.wait_recv()
        own.wait()

    return pl.pallas_call(
        body, name=name,
        out_shape=(pltpu.HBM(g_thru.shape, g_thru.dtype), pltpu.HBM(land_thru.shape, land_thru.dtype)),
        in_specs=(HBM_SPEC, HBM_SPEC, SEM_SPEC, SEM_SPEC, pl.BlockSpec(memory_space=pl.ANY)),
        out_specs=(HBM_SPEC, HBM_SPEC), input_output_aliases={0: 0, 1: 1},
        compiler_params=pltpu.CompilerParams(has_side_effects=pltpu.SideEffectType.DATAFLOW_SIDE_EFFECTING),
    )(g_thru, land_thru, send_sems, recv_sems, after)[1]


def _adamw(w, g, m, v):
    m = ADAM_B1 * m + (1.0 - ADAM_B1) * g
    v = ADAM_B2 * v + (1.0 - ADAM_B2) * (g * g)
    m_hat = m / (1.0 - ADAM_B1 ** ADAM_STEP)
    v_hat = v / (1.0 - ADAM_B2 ** ADAM_STEP)
    delta = -ADAM_LR * (m_hat / (jnp.sqrt(v_hat) + ADAM_EPS) + ADAM_WD * w)
    return delta, m, v


def _adamw_shard(name, land, w, m, v):
    _, a, b = land.shape
    tm = _div(a, max(16, ROW_TILE_ELEMS // (2 * b)), 16)
    per = a // tm

    def fn(i, nrow, *t):
        g = t[0].astype(F32)
        for s in range(1, N_DEV):
            g = g + t[s].astype(F32)
        w, m, v = t[N_DEV:]
        return (g,) + _adamw(w, g, m, v)

    land2 = land.reshape(N_DEV * a, b)
    ins = [_t(land2, row0=s * per) for s in range(N_DEV)] + [_t(w), _t(m), _t(v)]
    return _rowwise(name, fn, a, tm, ins, [(F32, b, b)] * 4)


def _adamw_small(name, parts, w, m, v):
    rows, n = w.shape

    def body(p_ref, w_ref, m_ref, v_ref, g_out, d_out, m_out, v_out):
        g = p_ref[0]
        for s in range(1, parts.shape[0]):
            g = g + p_ref[s]
        g = jnp.sum(g, axis=0, keepdims=True) if rows == 1 else g[0:rows, :]
        delta, m_new, v_new = _adamw(w_ref[...], g, m_ref[...], v_ref[...])
        g_out[...] = g
        d_out[...] = delta
        m_out[...] = m_new
        v_out[...] = v_new

    return pl.pallas_call(body, name=name, out_shape=[jax.ShapeDtypeStruct((rows, n), F32)] * 4)(parts, w, m, v)


def _rope_tables(t):
    half = HEAD_DIM // 4
    n_rows = t // GRID_W
    inv_freq = ROPE_THETA ** (-jnp.arange(0, 2 * half, 2, dtype=F32) / (2 * half))
    ang_r = jnp.arange(n_rows, dtype=jnp.int32).astype(F32)[:, None] * inv_freq[None, :]
    ang_c = jnp.arange(GRID_W, dtype=jnp.int32).astype(F32)[:, None] * inv_freq[None, :]

    def by_row(a):
        return jnp.repeat(a, GRID_W, axis=0)

    def by_col(a):
        return jnp.tile(a, (n_rows, 1))

    cos_r, sin_r, cos_c, sin_c = by_row(jnp.cos(ang_r)), by_row(jnp.sin(ang_r)), by_col(jnp.cos(ang_c)), by_col(jnp.sin(ang_c))
    cos = jnp.concatenate([cos_r, cos_r, cos_c, cos_c], axis=-1)
    sin = jnp.concatenate([-sin_r, sin_r, -sin_c, sin_c], axis=-1)
    return cos, sin


def _gather_weight(name, w, cols):
    g = _all_gather(name, w[0].astype(BF16))
    return g if cols else g.reshape(1, N_DEV * g.shape[1], g.shape[2])


def kernel(x, p, norm_mix, w_in, w_dw, conv_ln_g, conv_ln_b, w_conv_proj, q_norm, k_norm, w_attn_proj, w_out, norm_ffn, w_ff1, w_ff2, norm_ple, w_ple_gate, w_ple_proj, norm_final, loss_target, m_norm_mix, m_w_in, m_w_dw, m_conv_ln_g, m_conv_ln_b, m_w_conv_proj, m_q_norm, m_k_norm, m_w_attn_proj, m_w_out, m_norm_ffn, m_w_ff1, m_w_ff2, m_norm_ple, m_w_ple_gate, m_w_ple_proj, m_norm_final, v_norm_mix, v_w_in, v_w_dw, v_conv_ln_g, v_conv_ln_b, v_w_conv_proj, v_q_norm, v_k_norm, v_w_attn_proj, v_w_out, v_norm_ffn, v_w_ff1, v_w_ff2, v_norm_ple, v_w_ple_gate, v_w_ple_proj, v_norm_final):
    t, d = x.shape[1], x.shape[2]
    cw = d // 2
    n_heads = d // HEAD_DIM
    n_kv = n_heads // GROUP
    col_q, col_k, col_v = d // HEAD_DIM, 2 * d // HEAD_DIM, (2 * d + d // 4) // HEAD_DIM
    x0, pe, tgt = x[0], p[0, 0], loss_target[0]
    g_final = norm_final.reshape(1, d)
    me = _index(*_position())

    wg_in = _gather_weight("ag_w_in", w_in, True)
    dw_pad = jnp.pad(w_dw[0], ((0, CONV_TAPS_PAD - CONV_KERNEL), (0, 0)))
    dw_all = _all_gather("ag_w_dw", dw_pad)
    dw_full = dw_all.transpose(1, 0, 2).reshape(CONV_TAPS_PAD, cw)
    cos, sin = _rope_tables(t)

    def fetch(name, w, prev):
        return _exchange_start("ag_start_" + name, w[0].astype(BF16), after=(wg_in, dw_all) if prev is None else (prev,))

    def arrived(name, handle, cols, after):
        g = _exchange_wait("ag_wait_" + name, handle, after)
        return g if cols else g.reshape(1, N_DEV * g.shape[1], g.shape[2])

    on_cp, tok = fetch("w_conv_proj", w_conv_proj, None)
    on_ap, tok = fetch("w_attn_proj", w_attn_proj, tok)
    on_out, tok = fetch("w_out", w_out, tok)
    on_ff1, tok = fetch("w_ff1", w_ff1, tok)
    on_ff2, tok = fetch("w_ff2", w_ff2, tok)
    on_pg, tok = fetch("w_ple_gate", w_ple_gate, tok)
    on_pp, tok = fetch("w_ple_proj", w_ple_proj, tok)

    h1 = _rms_fwd("rms_mix", x0, norm_mix, after=(tok,))
    z = _mm_nn("mm_in", h1, wg_in, F32)
    conv, s_c = _conv_fwd("conv_fwd", z, cw, dw_full, conv_ln_g, conv_ln_b)
    wg_cp = arrived("w_conv_proj", on_cp, True, s_c)
    y_c = _mm_nn("mm_conv_proj", s_c, wg_cp, F32)
    qr = _head_norm_rope("q_prep", z, col_q, n_heads, q_norm, cos, sin, out_scale=QK_SCALE * LOG2_E,
                         heads_per_step=GROUP)
    kr = _head_norm_rope("k_prep", z, col_k, n_kv, k_norm, cos, sin, heads_per_step=n_kv)
    vt = z[:, col_v * HEAD_DIM:(col_v + n_kv) * HEAD_DIM].T.astype(BF16).reshape(n_kv, HEAD_DIM, t)
    vt = jnp.concatenate([vt, jnp.ones((n_kv, ONES_ROWS, t), BF16)], axis=1).reshape(n_kv * (HEAD_DIM + ONES_ROWS), t)
    o, lse = _flash_fwd(qr, kr, vt, n_heads)
    wg_ap = arrived("w_attn_proj", on_ap, False, o)
    y_a, merged = _mm_nn("mm_attn_proj", o, wg_ap, F32, activation=_gated_merge,
                         beside=[(z, 5), (z, 7), (y_c, 0)], col_tile=d // 2)
    wg_out = arrived("w_out", on_out, False, merged)
    x1 = _mm_nn("mm_out", merged, wg_out, F32, residual=x0)
    h2 = _rms_fwd("rms_ffn", x1, norm_ffn)
    wg_ff1 = arrived("w_ff1", on_ff1, True, h2)
    f, act = _mm_nn("mm_ff1", h2, wg_ff1, F32, activation=_relu2)
    wg_ff2 = arrived("w_ff2", on_ff2, False, act)
    x2 = _mm_nn("mm_ff2", act, wg_ff2, F32, residual=x1)
    hp = _rms_fwd("rms_ple", x2, norm_ple)
    wg_pg = arrived("w_ple_gate", on_pg, False, hp)
    pre = _mm_nn("mm_ple_gate", hp, wg_pg, F32)
    wg_pp = arrived("w_ple_proj", on_pp, True, pre)
    pp = _mm_nn("mm_ple_proj", pe, wg_pp, F32)

    dx3, dpp, dpre, dg_final, loss_part = _head_loss("head_loss", x2, pre, pp, tgt, g_final)
    loss = lax.psum(loss_part[0, 0], ("x", "y", "c"))
    def send(name, gw):
        blocks = gw if gw.shape[0] == N_DEV else gw.reshape(N_DEV, gw.shape[1] // N_DEV, gw.shape[2])
        return _exchange_start("rs_start_" + name, blocks)

    sent_pp, tok_pp = send("w_ple_proj", _mm_tn("mm_d_ple_proj", pe, dpp, N_DEV))
    sent_pg, tok = send("w_ple_gate", _mm_tn("mm_d_ple_gate", hp, dpre, 1))
    dhp = _mm_nt("mm_dhp", dpre, wg_pg, F32, after=(tok_pp, tok))
    dx2, dg_ple = _rms_bwd("rms_ple_bwd", x2, norm_ple, dhp, dx3)
    sent_ff2, tok = send("w_ff2", _mm_tn("mm_d_ff2", act, dx2, 1))
    df = _mm_nt("mm_dact", dx2, wg_ff2, BF16, after=(tok,), through=_relu2_bwd, beside=[(f, 0)])
    sent_ff1, tok = send("w_ff1", _mm_tn("mm_d_ff1", h2, df, N_DEV))
    dh2 = _mm_nt("mm_dh2", df, wg_ff1, F32, after=(tok,))
    dx1, dg_ffn = _rms_bwd("rms_ffn_bwd", x1, norm_ffn, dh2, dx2)
    sent_out, tok = send("w_out", _mm_tn("mm_d_out", merged, dx1, 1))
    dgc, dga, dyc, dya = _mm_nt("mm_dmerged", dx1, wg_out, BF16, after=(tok,), through=_gated_merge_bwd,
                                beside=[(z, 5), (z, 7), (y_c, 0), (y_a, 0)], n_results=4, col_tile=d // 2)
    sent_cp, tok = send("w_conv_proj", _mm_tn("mm_d_conv_proj", s_c, dyc, N_DEV))
    ds_c = _mm_nt("mm_ds_c", dyc, wg_cp, F32, after=(tok,))
    dconv, dg_ln, db_ln = _conv_ln_bwd("conv_ln_bwd", ds_c, conv, conv_ln_g, conv_ln_b)
    dca, dcb, dg_dw = _conv_glu_bwd("conv_glu_bwd", dconv, z, cw, dw_full)
    sent_ap, tok = send("w_attn_proj", _mm_tn("mm_d_attn_proj", o, dya, 1))
    do, delta = _mm_nt("mm_do", dya, wg_ap, BF16, after=(tok,), head_dots=o)
    dqt, dkh, dvh = _flash_bwd(qr, kr, kr.T, z, col_v, do, lse, delta, n_heads)
    dq, dg_q = _head_norm_rope_bwd("q_prep_bwd", [(dqt, 0, 1)], z, col_q, n_heads, q_norm, cos, sin,
                                   heads_per_step=GROUP, transposed=True)
    dk, dg_k = _head_norm_rope_bwd("k_prep_bwd", [(dkh, g, GROUP) for g in range(GROUP)], z, col_k, n_kv, k_norm,
                                   cos, sin)
    dv = _group_sum("dv_sum", dvh, n_kv)
    dz = jnp.concatenate([dca, dcb, dq, dk, dv, dgc, dga], axis=1)
    sent_in, tok = send("w_in", _mm_tn("mm_d_in", h1, dz, N_DEV))
    dh1 = _mm_nt("mm_dh1", dz, wg_in, F32, after=(tok,))
    grad_x, dg_mix = _rms_bwd("rms_mix_bwd", x0, norm_mix, dh1, dx1)

    def pad(a):
        return jnp.pad(a, ((0, 0), (0, d - a.shape[1])))

    small = [dg_mix, dg_ffn, dg_ple, dg_final, dg_ln, db_ln, dg_q, dg_k, dg_dw]
    packed = _all_gather("ag_small", jnp.concatenate([pad(a) for a in small], axis=0))
    parts, row = [], 0
    for a in small:
        parts.append(packed[:, row:row + a.shape[0], :a.shape[1]])
        row += a.shape[0]
    p_mix, p_ffn, p_ple, p_final, p_lng, p_lnb, p_q, p_k, p_dw = parts
    p_dw = lax.dynamic_slice_in_dim(p_dw, me * (cw // N_DEV), cw // N_DEV, axis=2)

    def little(name, part, w, m, v):
        rows, n = (1 if w.ndim < 3 else w.shape[1]), w.shape[-1]
        padded = rows if rows == 1 else part.shape[1]

        def two_d(a):
            return jnp.pad(a.reshape(rows, n), ((0, padded - rows), (0, 0)))

        res = _adamw_small("adamw_" + name, part, two_d(w), two_d(m), two_d(v))
        return [r[:rows].reshape(w.shape) for r in res]

    done = {}
    after = packed
    for name, sent, w, m, v in [
            ("w_ple_proj", sent_pp, w_ple_proj, m_w_ple_proj, v_w_ple_proj),
            ("w_ple_gate", sent_pg, w_ple_gate, m_w_ple_gate, v_w_ple_gate),
            ("w_ff2", sent_ff2, w_ff2, m_w_ff2, v_w_ff2),
            ("w_ff1", sent_ff1, w_ff1, m_w_ff1, v_w_ff1),
            ("w_out", sent_out, w_out, m_w_out, v_w_out),
            ("w_conv_proj", sent_cp, w_conv_proj, m_w_conv_proj, v_w_conv_proj),
            ("w_attn_proj", sent_ap, w_attn_proj, m_w_attn_proj, v_w_attn_proj),
            ("w_in", sent_in, w_in, m_w_in, v_w_in)]:
        land = _exchange_wait("rs_wait_" + name, sent, after)
        res = _adamw_shard("adamw_" + name, land, w[0], m[0], v[0])
        after = res[0]
        done[name] = [r.reshape(w.shape) for r in res]

    results = [
        little("norm_mix", p_mix, norm_mix, m_norm_mix, v_norm_mix),
        done["w_in"],
        little("w_dw", p_dw, w_dw, m_w_dw, v_w_dw),
        little("conv_ln_g", p_lng, conv_ln_g, m_conv_ln_g, v_conv_ln_g),
        little("conv_ln_b", p_lnb, conv_ln_b, m_conv_ln_b, v_conv_ln_b),
        done["w_conv_proj"],
        little("q_norm", p_q, q_norm, m_q_norm, v_q_norm),
        little("k_norm", p_k, k_norm, m_k_norm, v_k_norm),
        done["w_attn_proj"],
        done["w_out"],
        little("norm_ffn", p_ffn, norm_ffn, m_norm_ffn, v_norm_ffn),
        done["w_ff1"],
        done["w_ff2"],
        little("norm_ple", p_ple, norm_ple, m_norm_ple, v_norm_ple),
        done["w_ple_gate"],
        done["w_ple_proj"],
        little("norm_final", p_final, norm_final, m_norm_final, v_norm_final),
    ]
    grads, deltas, new_m, new_v = zip(*results)
    return (loss, grad_x[None], *grads, *deltas, *new_m, *new_v)
```
